```python
import math
import jax, jax.numpy as jnp
from jax import lax
import numpy as np

D_MODEL = 1024
BATCH = 4
SEQ = 8192
DEPTH = 1
DEC_BATCH = 8
DEC_SEQ = 16
PAST_LEN = 2048

CHUNK = 64
Q_BLOCK = 128
HEAD_DIM = 64
FOX_HEADS = 8
DIFF_HEADS = 4
DIFF_VDIM = 2 * HEAD_DIM
FOX_WIDTH = FOX_HEADS * HEAD_DIM
DIFF_QK_WIDTH = DIFF_HEADS * 2 * HEAD_DIM
DIFF_WIDTH = DIFF_HEADS * DIFF_VDIM
MIX_WIDTH = FOX_WIDTH + DIFF_WIDTH
SPLIT_SIZES = (FOX_WIDTH, FOX_WIDTH, FOX_WIDTH, FOX_HEADS, FOX_WIDTH,
               DIFF_QK_WIDTH, DIFF_QK_WIDTH, DIFF_WIDTH, DIFF_WIDTH)
SPLIT_POINTS = tuple(sum(SPLIT_SIZES[:i + 1]) for i in range(len(SPLIT_SIZES) - 1))
IN_WIDTH = sum(SPLIT_SIZES)
DEEPNORM_ALPHA = (2 * DEPTH) ** 0.25
DEEPNORM_BETA = (8 * DEPTH) ** -0.25
FORGET_BIAS_INIT = 3.0
LN_EPS = 1e-5
RMS_EPS = 1e-5
ADA_SCALE = 0.5

kernel_name = "hybrid_fox_diffattn_streaming_step"

F32 = jnp.float32


def layer_norm(x, g, b):
    xf = x.astype(F32)
    mu = jnp.mean(xf, axis=-1, keepdims=True)
    xc = xf - mu
    var = jnp.mean(xc * xc, axis=-1, keepdims=True)
    return (xc * lax.rsqrt(var + LN_EPS) * g.astype(F32) + b.astype(F32)).astype(x.dtype)


def ada_modulation(c, w_ada, b_ada):
    m = jax.nn.silu(c) @ w_ada + b_ada
    shift, scale, gate = jnp.split(m, 3, axis=-1)
    return shift, scale, gate


def in_project(h, w_in, b_f):
    B, T, _ = h.shape
    z = h @ w_in
    fq, fk, fv, ff, fg, dq, dk, dv, dg = jnp.split(z, list(SPLIT_POINTS), axis=-1)

    def heads(t, n, d):
        return t.reshape(B, T, n, d).transpose(0, 2, 1, 3)

    fq = heads(fq, FOX_HEADS, HEAD_DIM)
    fk = heads(fk, FOX_HEADS, HEAD_DIM)
    fv = heads(fv, FOX_HEADS, HEAD_DIM)
    logf = jax.nn.log_sigmoid(ff.astype(F32) + b_f.astype(F32)).transpose(0, 2, 1)
    dq = dq.reshape(B, T, DIFF_HEADS, 2, HEAD_DIM).transpose(0, 2, 1, 3, 4)
    dk = dk.reshape(B, T, DIFF_HEADS, 2, HEAD_DIM).transpose(0, 2, 1, 3, 4)
    dv = heads(dv, DIFF_HEADS, DIFF_VDIM)
    return fq, fk, fv, logf, fg, dq, dk, dv, dg


def fox_attend(q, k, v, cum_q, cum_k, pos_q, pos_k):
    s = jnp.einsum('bhqd,bhkd->bhqk', q, k).astype(F32) * (HEAD_DIM ** -0.5)
    s = s + cum_q[..., :, None] - cum_k[..., None, :]
    mask = pos_k[None, :] <= pos_q[:, None]
    s = jnp.where(mask, s, -jnp.inf)
    p = jax.nn.softmax(s, axis=-1)
    return jnp.einsum('bhqk,bhkd->bhqd', p.astype(v.dtype), v)


def diff_attend(q, k, v, pos_q, pos_k, lam):
    s = jnp.einsum('bhqcd,bhkcd->bhcqk', q, k).astype(F32) * (HEAD_DIM ** -0.5)
    slopes = 2.0 ** (-8.0 * jnp.arange(1, DIFF_HEADS + 1, dtype=F32) / DIFF_HEADS)
    dist = jnp.abs(pos_q[:, None] - pos_k[None, :]).astype(F32)
    s = s - slopes[None, :, None, None, None] * dist
    mask = (pos_k[None, :] // CHUNK) <= (pos_q[:, None] // CHUNK)
    s = jnp.where(mask, s, -jnp.inf)
    p = jax.nn.softmax(s, axis=-1)
    a = p[:, :, 0] - lam * p[:, :, 1]
    return jnp.einsum('bhqk,bhkv->bhqv', a.astype(v.dtype), v)


def diff_lambda(lq1, lk1, lq2, lk2, lam_init):
    return (jnp.exp(jnp.sum(lq1.astype(F32) * lk1.astype(F32)))
            - jnp.exp(jnp.sum(lq2.astype(F32) * lk2.astype(F32))) + lam_init)


def mix_out(fox_o, diff_o, fg, dg, subln_g, lam_init, w_out):
    B, _, T, _ = fox_o.shape
    fox = fox_o.transpose(0, 2, 1, 3).reshape(B, T, FOX_WIDTH)
    df = diff_o.astype(F32)
    df = df * lax.rsqrt(jnp.mean(df * df, axis=-1, keepdims=True) + RMS_EPS)
    df = df * subln_g.astype(F32) * (1.0 - lam_init)
    diff = df.astype(fox.dtype).transpose(0, 2, 1, 3).reshape(B, T, DIFF_WIDTH)
    u = jnp.concatenate([fox * jax.nn.silu(fg), diff * jax.nn.silu(dg)], axis=-1)
    return u @ w_out


def trunk_layer(x, c, p, lam_init, attend):
    shift, scale, gate = ada_modulation(c, p['w_ada'], p['b_ada'])
    h = x * (1.0 + scale[:, None, :]) + shift[:, None, :]
    fq, fk, fv, logf, fg, dq, dk, dv, dg = in_project(h, p['w_in'], p['b_f'])
    lam = diff_lambda(p['lq1'], p['lk1'], p['lq2'], p['lk2'], lam_init)
    fox_o, diff_o = attend(fq, fk, fv, logf, dq, dk, dv, lam)
    branch = mix_out(fox_o, diff_o, fg, dg, p['subln_g'], lam_init, p['w_out'])
    y = layer_norm(DEEPNORM_ALPHA * x + gate[:, None, :] * branch, p['ln_g'], p['ln_b'])
    return y, (fk, fv, logf, dk, dv)


def prompt_attend(fq, fk, fv, logf, dq, dk, dv, lam):
    B, _, S, _ = fq.shape
    nb = S // Q_BLOCK
    pos = jnp.arange(S, dtype=jnp.int32)
    cum = jnp.cumsum(logf, axis=-1)

    def blocks(t):
        t = t.reshape(t.shape[:2] + (nb, Q_BLOCK) + t.shape[3:])
        return jnp.moveaxis(t, 2, 0)

    def unblocks(o):
        o = jnp.moveaxis(o, 0, 2)
        return o.reshape(o.shape[:2] + (S,) + o.shape[4:])

    pos_b = pos.reshape(nb, Q_BLOCK)
    fox_o = lax.map(lambda a: fox_attend(a[0], fk, fv, a[1], cum, a[2], pos),
                    (blocks(fq), blocks(cum), pos_b))
    diff_o = lax.map(lambda a: diff_attend(a[0], dk, dv, a[1], pos, lam),
                     (blocks(dq), pos_b))
    return unblocks(fox_o), unblocks(diff_o)


def make_sample_attend(ck_f, cv_f, clogf, ck_d, cv_d):
    def attend(fq, fk, fv, logf, dq, dk, dv, lam):
        P = ck_f.shape[2]
        T = fq.shape[2]
        fk_all = jnp.concatenate([ck_f.astype(fk.dtype), fk], axis=2)
        fv_all = jnp.concatenate([cv_f.astype(fv.dtype), fv], axis=2)
        cum = jnp.cumsum(jnp.concatenate([clogf.astype(F32), logf], axis=-1), axis=-1)
        dk_all = jnp.concatenate([ck_d.astype(dk.dtype), dk], axis=2)
        dv_all = jnp.concatenate([cv_d.astype(dv.dtype), dv], axis=2)
        pos_k = jnp.arange(P + T, dtype=jnp.int32)
        pos_q = P + jnp.arange(T, dtype=jnp.int32)
        fox_o = fox_attend(fq, fk_all, fv_all, cum[..., P:], cum, pos_q, pos_k)
        diff_o = diff_attend(dq, dk_all, dv_all, pos_q, pos_k, lam)
        return fox_o, diff_o
    return attend


def setup_inputs(seed: int = 0) -> dict:
    key = jax.random.key(seed)
    ks = jax.random.split(key, 24)
    nrm = jax.random.normal
    x_prompt = nrm(ks[0], (BATCH, SEQ, D_MODEL), F32)
    x_sample = nrm(ks[1], (DEC_BATCH, DEC_SEQ, D_MODEL), F32)
    cache_fox_k = nrm(ks[2], (DEPTH, DEC_BATCH, FOX_HEADS, PAST_LEN, HEAD_DIM), F32)
    cache_fox_v = nrm(ks[3], (DEPTH, DEC_BATCH, FOX_HEADS, PAST_LEN, HEAD_DIM), F32)
    cache_fox_logf = jax.nn.log_sigmoid(
        FORGET_BIAS_INIT + nrm(ks[4], (DEPTH, DEC_BATCH, FOX_HEADS, PAST_LEN), F32))
    cache_diff_k = nrm(ks[5], (DEPTH, DEC_BATCH, DIFF_HEADS, PAST_LEN, 2, HEAD_DIM), F32)
    cache_diff_v = nrm(ks[6], (DEPTH, DEC_BATCH, DIFF_HEADS, PAST_LEN, DIFF_VDIM), F32) * DEEPNORM_BETA
    c_prompt = nrm(ks[7], (BATCH, D_MODEL), F32)
    c_sample = nrm(ks[8], (DEC_BATCH, D_MODEL), F32)
    w_ada = nrm(ks[9], (DEPTH, D_MODEL, 3 * D_MODEL), F32) * (ADA_SCALE * D_MODEL ** -0.5)
    b_ada = nrm(ks[10], (DEPTH, 3 * D_MODEL), F32) * 0.02
    col_scale = jnp.concatenate([
        jnp.ones((2 * FOX_WIDTH,), F32),
        jnp.full((FOX_WIDTH,), DEEPNORM_BETA, F32),
        jnp.ones((FOX_HEADS + FOX_WIDTH + 2 * DIFF_QK_WIDTH,), F32),
        jnp.full((DIFF_WIDTH,), DEEPNORM_BETA, F32),
        jnp.ones((DIFF_WIDTH,), F32)])
    w_in = nrm(ks[11], (DEPTH, D_MODEL, IN_WIDTH), F32) * (D_MODEL ** -0.5) * col_scale
    b_f = FORGET_BIAS_INIT + 0.1 * nrm(ks[12], (DEPTH, FOX_HEADS), F32)
    lambda_q1 = 0.1 * nrm(ks[13], (DEPTH, HEAD_DIM), F32)
    lambda_k1 = 0.1 * nrm(ks[14], (DEPTH, HEAD_DIM), F32)
    lambda_q2 = 0.1 * nrm(ks[15], (DEPTH, HEAD_DIM), F32)
    lambda_k2 = 0.1 * nrm(ks[16], (DEPTH, HEAD_DIM), F32)
    subln_g = 1.0 + 0.02 * nrm(ks[17], (DEPTH, DIFF_VDIM), F32)
    w_out = nrm(ks[18], (DEPTH, MIX_WIDTH, D_MODEL), F32) * (MIX_WIDTH ** -0.5) * DEEPNORM_BETA
    ln_g = 1.0 + 0.02 * nrm(ks[19], (DEPTH, D_MODEL), F32)
    ln_b = 0.02 * nrm(ks[20], (DEPTH, D_MODEL), F32)
    return {"x_prompt": x_prompt, "x_sample": x_sample,
            "cache_fox_k": cache_fox_k, "cache_fox_v": cache_fox_v, "cache_fox_logf": cache_fox_logf,
            "cache_diff_k": cache_diff_k, "cache_diff_v": cache_diff_v,
            "c_prompt": c_prompt, "c_sample": c_sample,
            "w_ada": w_ada, "b_ada": b_ada, "w_in": w_in, "b_f": b_f,
            "lambda_q1": lambda_q1, "lambda_k1": lambda_k1, "lambda_q2": lambda_q2, "lambda_k2": lambda_k2,
            "subln_g": subln_g, "w_out": w_out, "ln_g": ln_g, "ln_b": ln_b}


def reference(x_prompt, x_sample, cache_fox_k, cache_fox_v, cache_fox_logf, cache_diff_k, cache_diff_v,
              c_prompt, c_sample, w_ada, b_ada, w_in, b_f, lambda_q1, lambda_k1, lambda_q2, lambda_k2,
              subln_g, w_out, ln_g, ln_b):
    yp = x_prompt
    ys = x_sample
    p_states = ([], [], [], [], [])
    s_states = ([], [], [], [], [])
    for l in range(DEPTH):
        lam_init = 0.8 - 0.6 * math.exp(-0.3 * l)
        p = {'w_ada': w_ada[l], 'b_ada': b_ada[l], 'w_in': w_in[l], 'b_f': b_f[l],
             'lq1': lambda_q1[l], 'lk1': lambda_k1[l], 'lq2': lambda_q2[l], 'lk2': lambda_k2[l],
             'subln_g': subln_g[l], 'w_out': w_out[l], 'ln_g': ln_g[l], 'ln_b': ln_b[l]}
        yp, new_p = trunk_layer(yp, c_prompt, p, lam_init, prompt_attend)
        attend_s = make_sample_attend(cache_fox_k[l], cache_fox_v[l], cache_fox_logf[l],
                                      cache_diff_k[l], cache_diff_v[l])
        ys, new_s = trunk_layer(ys, c_sample, p, lam_init, attend_s)
        for i in range(5):
            p_states[i].append(new_p[i])
            s_states[i].append(new_s[i])
    pk_f, pv_f, plogf, pk_d, pv_d = [jnp.stack(t, axis=0) for t in p_states]
    sk_f, sv_f, slogf, sk_d, sv_d = [jnp.stack(t, axis=0) for t in s_states]
    return (yp, ys, pk_f, pv_f, plogf, pk_d, pv_d, sk_f, sv_f, slogf, sk_d, sv_d)
```

```python
import functools
import math

import jax
import jax.numpy as jnp
import numpy as np
from jax import lax
from jax.experimental import pallas as pl
from jax.experimental.pallas import tpu as pltpu

F32 = jnp.float32
BF16 = jnp.bfloat16

HEAD_DIM = 64
FOX_HEADS = 8
DIFF_HEADS = 4
DIFF_VDIM = 2 * HEAD_DIM
FOX_WIDTH = FOX_HEADS * HEAD_DIM
DIFF_WIDTH = DIFF_HEADS * DIFF_VDIM
CHUNK = 64
LN_EPS = 1e-5
RMS_EPS = 1e-5
LANES = 128
NEG_INF = float("-inf")

_SEC = {name: i * FOX_WIDTH for i, name in enumerate(("fq", "fk", "fv", "fg", "dq", "dk", "dv", "dg"))}
_FF_OFF = 8 * FOX_WIDTH
_W_COLS = _FF_OFF + LANES

_VMEM_LIMIT = 56 * 1024 * 1024


def _params(sem):
    return pltpu.CompilerParams(dimension_semantics=sem, vmem_limit_bytes=_VMEM_LIMIT)


def _ada_kernel(c_ref, w_ref, b_ref, o_ref):
    c = c_ref[...]
    s = c * jax.nn.sigmoid(c)
    o_ref[...] = jnp.dot(s, w_ref[...], preferred_element_type=F32) + b_ref[...]


def _ada(c_all, w_ada, b_ada):
    rows, d = c_all.shape
    n = w_ada.shape[1]
    tn = 512
    return pl.pallas_call(
        _ada_kernel,
        grid=(n // tn,),
        in_specs=[pl.BlockSpec((rows, d), lambda j: (0, 0)),
                  pl.BlockSpec((d, tn), lambda j: (0, j)),
                  pl.BlockSpec((1, tn), lambda j: (0, j))],
        out_specs=pl.BlockSpec((rows, tn), lambda j: (0, j)),
        out_shape=jax.ShapeDtypeStruct((rows, n), F32),
        compiler_params=_params(("arbitrary",)),
        name="ada",
    )(c_all, w_ada, b_ada.reshape(1, n))


def _split3(x):
    hi = x.astype(BF16)
    r1 = x - hi.astype(F32)
    mid = r1.astype(BF16)
    lo = (r1 - mid.astype(F32)).astype(BF16)
    return hi, mid, lo


def _fox_placement():
    pq = np.zeros((LANES, FOX_WIDTH), np.float32)
    pk = np.zeros((LANES, FOX_WIDTH), np.float32)
    oq = np.zeros((1, FOX_WIDTH), np.float32)
    ok = np.zeros((1, FOX_WIDTH), np.float32)
    for h in range(FOX_HEADS):
        base = (h // 2) * LANES + (HEAD_DIM if h % 2 == 0 else 0)
        for p in range(3):
            pq[8 * p + h, base + p] = 1.0
            ok[0, base + p] = 1.0
            oq[0, base + 3 + p] = 1.0
            pk[8 * p + h, base + 3 + p] = -1.0
    return pq, pk, oq, ok


def _in_proj_kernel(x_ref, mod_ref, w_ref, bf_ref, pq_ref, pk_ref, oq_ref, ok_ref,
                    fk_ref, fv_ref, dk_ref, dv_ref, logf_ref,
                    fqa_ref, fka_ref, fva_ref, fg_ref, dqa_ref, dka_ref, dvb_ref, dg_ref,
                    carry_ref, *, tm, pos_offset):
    ti = pl.program_id(1)
    x = x_ref[0]
    shift = mod_ref[0, 0:1, :]
    scale = mod_ref[0, 1:2, :]
    h = (x * (1.0 + scale) + shift).astype(BF16)

    def proj(name, width=FOX_WIDTH):
        off = _SEC[name] if name in _SEC else _FF_OFF
        return jnp.dot(h, w_ref[:, off:off + width], preferred_element_type=F32)

    lane = lax.broadcasted_iota(jnp.int32, (tm, LANES), 1)
    row = lax.broadcasted_iota(jnp.int32, (tm, LANES), 0)
    lower = lane < HEAD_DIM
    grp = (lane // 8) % 4

    def by_group(a, b, c):
        return jnp.where(grp == 0, a, jnp.where(grp == 1, b, jnp.where(grp == 2, c, jnp.zeros_like(a))))

    logf = jax.nn.log_sigmoid(proj("ff", LANES) + bf_ref[...])
    logf_ref[0] = logf
    tri_r = lax.broadcasted_iota(jnp.int32, (tm, tm), 0)
    tri_c = lax.broadcasted_iota(jnp.int32, (tm, tm), 1)
    tri = (tri_c <= tri_r).astype(BF16)
    part = jnp.dot(tri, by_group(*_split3(logf)), preferred_element_type=F32)
    local = part + pltpu.roll(part, 8, 1) + pltpu.roll(part, 16, 1) + pltpu.roll(part, 24, 1)

    @pl.when(ti == 0)
    def _():
        carry_ref[...] = jnp.zeros_like(carry_ref)

    cum = local + carry_ref[...]
    carry_ref[...] = cum[tm - 1:tm, :]
    cum_cat = by_group(*_split3(cum))
    eq = jnp.dot(cum_cat, pq_ref[...], preferred_element_type=F32) + oq_ref[...]
    ek = jnp.dot(cum_cat, pk_ref[...], preferred_element_type=F32) + ok_ref[...]

    zq, zk, zv = proj("fq"), proj("fk"), proj("fv")
    e_even = (lane == HEAD_DIM).astype(F32)
    e_odd = (lane == 0).astype(F32)
    for j in range(FOX_HEADS // 2):
        sl = slice(j * LANES, (j + 1) * LANES)
        fqa_ref[0, 2 * j] = jnp.where(lower, zq[:, sl], eq[:, sl]).astype(BF16)
        fqa_ref[0, 2 * j + 1] = jnp.where(lower, eq[:, sl], zq[:, sl]).astype(BF16)
        fka_ref[0, 2 * j] = jnp.where(lower, zk[:, sl], ek[:, sl]).astype(BF16)
        fka_ref[0, 2 * j + 1] = jnp.where(lower, ek[:, sl], zk[:, sl]).astype(BF16)
        fva_ref[0, 2 * j] = jnp.where(lower, zv[:, sl], e_even).astype(BF16)
        fva_ref[0, 2 * j + 1] = jnp.where(lower, e_odd, zv[:, sl]).astype(BF16)
    for hd in range(FOX_HEADS):
        fk_ref[0, hd] = zk[:, hd * HEAD_DIM:(hd + 1) * HEAD_DIM]
        fv_ref[0, hd] = zv[:, hd * HEAD_DIM:(hd + 1) * HEAD_DIM]
    fg_ref[0] = proj("fg").astype(BF16)

    zq, zk, zv = proj("dq"), proj("dk"), proj("dv")
    pos = (row + (ti * tm + pos_offset)).astype(F32)
    pos_hi = pos.astype(BF16).astype(F32)
    pos_lo = pos - pos_hi
    sub = lane % HEAD_DIM
    for hd in range(DIFF_HEADS):
        slope = 2.0 ** (-8.0 * (hd + 1) / DIFF_HEADS)
        sl = slice(hd * LANES, (hd + 1) * LANES)
        ones = ((sub == 2) | (sub == 3)).astype(F32)
        ext_q = jnp.where(sub == 0, -slope * pos_hi, jnp.where(sub == 1, -slope * pos_lo, ones))
        ones = ((sub == 0) | (sub == 1)).astype(F32)
        ext_k = jnp.where(sub == 2, slope * pos_hi, jnp.where(sub == 3, slope * pos_lo, ones))
        dqa_ref[0, 2 * hd] = jnp.where(lower, zq[:, sl], ext_q).astype(BF16)
        dqa_ref[0, 2 * hd + 1] = jnp.where(lower, ext_q, zq[:, sl]).astype(BF16)
        dka_ref[0, 2 * hd] = jnp.where(lower, zk[:, sl], ext_k).astype(BF16)
        dka_ref[0, 2 * hd + 1] = jnp.where(lower, ext_k, zk[:, sl]).astype(BF16)
        dk_ref[0, hd] = zk[:, sl]
        dv_ref[0, hd] = zv[:, sl]
        dvb_ref[0, hd] = zv[:, sl].astype(BF16)
    dg_ref[0] = proj("dg").astype(BF16)


def _in_proj(x, mod, w, bf16x, consts, pos_offset):
    b, t, d = x.shape
    tm = min(t, 512)
    nt = t // tm
    pq, pk, oq, ok = consts

    def full(a):
        return pl.BlockSpec(a.shape, lambda i, j: (0,) * a.ndim)

    def heads(n, width):
        return pl.BlockSpec((1, n, tm, width), lambda i, j: (i, 0, j, 0))

    rows = pl.BlockSpec((1, tm, FOX_WIDTH), lambda i, j: (i, j, 0))
    out_shape = (
        jax.ShapeDtypeStruct((b, FOX_HEADS, t, HEAD_DIM), F32),
        jax.ShapeDtypeStruct((b, FOX_HEADS, t, HEAD_DIM), F32),
        jax.ShapeDtypeStruct((b, DIFF_HEADS, t, DIFF_VDIM), F32),
        jax.ShapeDtypeStruct((b, DIFF_HEADS, t, DIFF_VDIM), F32),
        jax.ShapeDtypeStruct((b, t, LANES), F32),
        jax.ShapeDtypeStruct((b, FOX_HEADS, t, LANES), BF16),
        jax.ShapeDtypeStruct((b, FOX_HEADS, t, LANES), BF16),
        jax.ShapeDtypeStruct((b, FOX_HEADS, t, LANES), BF16),
        jax.ShapeDtypeStruct((b, t, FOX_WIDTH), BF16),
        jax.ShapeDtypeStruct((b, 2 * DIFF_HEADS, t, LANES), BF16),
        jax.ShapeDtypeStruct((b, 2 * DIFF_HEADS, t, LANES), BF16),
        jax.ShapeDtypeStruct((b, DIFF_HEADS, t, DIFF_VDIM), BF16),
        jax.ShapeDtypeStruct((b, t, DIFF_WIDTH), BF16),
    )
    out_specs = (
        heads(FOX_HEADS, HEAD_DIM), heads(FOX_HEADS, HEAD_DIM),
        heads(DIFF_HEADS, DIFF_VDIM), heads(DIFF_HEADS, DIFF_VDIM),
        pl.BlockSpec((1, tm, LANES), lambda i, j: (i, j, 0)),
        heads(FOX_HEADS, LANES), heads(FOX_HEADS, LANES), heads(FOX_HEADS, LANES), rows,
        heads(2 * DIFF_HEADS, LANES), heads(2 * DIFF_HEADS, LANES), heads(DIFF_HEADS, DIFF_VDIM), rows,
    )
    return pl.pallas_call(
        functools.partial(_in_proj_kernel, tm=tm, pos_offset=pos_offset),
        grid=(b, nt),
        in_specs=[pl.BlockSpec((1, tm, d), lambda i, j: (i, j, 0)),
                  pl.BlockSpec((1, 3, d), lambda i, j: (i, 0, 0)),
                  full(w), full(bf16x), full(pq), full(pk), full(oq), full(ok)],
        out_specs=out_specs,
        out_shape=out_shape,
        scratch_shapes=[pltpu.VMEM((1, LANES), F32)],
        compiler_params=_params(("arbitrary", "arbitrary")),
        name="in_proj",
    )(x, mod, w, bf16x, pq, pk, oq, ok)


_NT = (((1,), (1,)), ((), ()))


def _fox_kernel(q_ref, k_ref, v_ref, o_ref, m_ref, acc_ref, *, tq):
    qi = pl.program_id(2)
    row = lax.broadcasted_iota(jnp.int32, (tq, tq), 0)
    col = lax.broadcasted_iota(jnp.int32, (tq, tq), 1)
    lane = lax.broadcasted_iota(jnp.int32, (tq, LANES), 1)
    outs = []
    for hh in range(2):
        q = q_ref[0, hh]
        m_ref[...] = jnp.full_like(m_ref, NEG_INF)
        acc_ref[...] = jnp.zeros_like(acc_ref)

        def step(ki, diagonal, hh=hh, q=q):
            start = pl.multiple_of(ki * tq, tq)
            k = k_ref[0, hh, pl.ds(start, tq), :]
            v = v_ref[0, hh, pl.ds(start, tq), :]
            s = lax.dot_general(q, k, _NT, preferred_element_type=F32)
            if diagonal:
                s = jnp.where(col <= row, s, NEG_INF)
            m_prev = m_ref[...]
            m_new = jnp.maximum(m_prev, jnp.max(s, axis=1, keepdims=True))
            p = jnp.exp(s - m_new)
            alpha = jnp.exp(m_prev - m_new)
            acc_ref[...] = alpha * acc_ref[...] + jnp.dot(p.astype(BF16), v, preferred_element_type=F32)
            m_ref[...] = m_new

        def body(ki, c):
            step(ki, False)
            return c

        lax.fori_loop(0, qi, body, 0)
        step(qi, True)
        acc = acc_ref[...]
        denom = acc[:, HEAD_DIM:HEAD_DIM + 1] if hh == 0 else acc[:, 0:1]
        outs.append(acc / denom)
    o_ref[0] = jnp.where(lane < HEAD_DIM, outs[0], outs[1]).astype(BF16)


def _fox_attention(fqa, fka, fva):
    b, nh, t, _ = fqa.shape
    tq = min(t, 512)
    return pl.pallas_call(
        functools.partial(_fox_kernel, tq=tq),
        grid=(b, nh // 2, t // tq),
        in_specs=[pl.BlockSpec((1, 2, tq, LANES), lambda i, j, q: (i, j, q, 0)),
                  pl.BlockSpec((1, 2, t, LANES), lambda i, j, q: (i, j, 0, 0)),
                  pl.BlockSpec((1, 2, t, LANES), lambda i, j, q: (i, j, 0, 0))],
        out_specs=pl.BlockSpec((1, tq, LANES), lambda i, j, q: (i, q, j)),
        out_shape=jax.ShapeDtypeStruct((b, t, FOX_WIDTH), BF16),
        scratch_shapes=[pltpu.VMEM((tq, 1), F32), pltpu.VMEM((tq, LANES), F32)],
        compiler_params=_params(("arbitrary", "arbitrary", "arbitrary")),
        name="fox_attn",
    )(fqa, fka, fva)


def _alibi_slope(head):
    return jnp.exp2(jnp.full((1, 1), -8.0 / DIFF_HEADS, F32) * (head + 1).astype(F32))


def _lambda(lam_ref, lam_init):
    lv = lam_ref[...]
    e1 = jnp.exp(jnp.sum(lv[0:1] * lv[1:2], axis=1, keepdims=True))
    e2 = jnp.exp(jnp.sum(lv[2:3] * lv[3:4], axis=1, keepdims=True))
    return e1 - e2 + lam_init


def _diff_finish(o0, o1, lam, g, lam_init):
    df = o0 - lam * o1
    df = df * lax.rsqrt(jnp.mean(df * df, axis=-1, keepdims=True) + RMS_EPS)
    return df * g * (1.0 - lam_init)


def _diff_kernel(q_ref, k_ref, v_ref, lam_ref, g_ref, o_ref, m_ref, l_ref, acc_ref, *, tq, lam_init):
    hd = pl.program_id(1)
    qi = pl.program_id(2)
    slope = _alibi_slope(hd)
    row = lax.broadcasted_iota(jnp.int32, (tq, tq), 0)
    col = lax.broadcasted_iota(jnp.int32, (tq, tq), 1)
    visible = (col // CHUNK) <= (row // CHUNK)
    ahead = jnp.maximum(col - row, 0).astype(F32)
    m_ref[...] = jnp.full_like(m_ref, NEG_INF)
    l_ref[...] = jnp.zeros_like(l_ref)
    acc_ref[...] = jnp.zeros_like(acc_ref)

    def step(ki, diagonal):
        start = pl.multiple_of(ki * tq, tq)
        v = v_ref[0, 0, pl.ds(start, tq), :]
        for c in range(2):
            k = k_ref[0, c, pl.ds(start, tq), :]
            s = lax.dot_general(q_ref[0, c], k, _NT, preferred_element_type=F32)
            if diagonal:
                s = jnp.where(visible, s - (2.0 * slope) * ahead, NEG_INF)
            m_prev = m_ref[c]
            m_new = jnp.maximum(m_prev, jnp.max(s, axis=1, keepdims=True))
            p = jnp.exp(s - m_new)
            alpha = jnp.exp(m_prev - m_new)
            l_ref[c] = alpha * l_ref[c] + jnp.sum(p, axis=1, keepdims=True)
            acc_ref[c] = alpha * acc_ref[c] + jnp.dot(p.astype(BF16), v, preferred_element_type=F32)
            m_ref[c] = m_new

    def body(ki, c):
        step(ki, False)
        return c

    lax.fori_loop(0, qi, body, 0)
    step(qi, True)
    lam = _lambda(lam_ref, lam_init)
    o_ref[0] = _diff_finish(acc_ref[0] / l_ref[0], acc_ref[1] / l_ref[1], lam, g_ref[...],
                            lam_init).astype(BF16)


def _diff_attention(dqa, dka, dvb, lam_vecs, subln_g, lam_init):
    b, nm, t, _ = dqa.shape
    tq = min(t, 512)
    return pl.pallas_call(
        functools.partial(_diff_kernel, tq=tq, lam_init=lam_init),
        grid=(b, nm // 2, t // tq),
        in_specs=[pl.BlockSpec((1, 2, tq, LANES), lambda i, j, q: (i, j, q, 0)),
                  pl.BlockSpec((1, 2, t, LANES), lambda i, j, q: (i, j, 0, 0)),
                  pl.BlockSpec((1, 1, t, DIFF_VDIM), lambda i, j, q: (i, j, 0, 0)),
                  pl.BlockSpec(lam_vecs.shape, lambda i, j, q: (0, 0)),
                  pl.BlockSpec(subln_g.shape, lambda i, j, q: (0, 0))],
        out_specs=pl.BlockSpec((1, tq, DIFF_VDIM), lambda i, j, q: (i, q, j)),
        out_shape=jax.ShapeDtypeStruct((b, t, DIFF_WIDTH), BF16),
        scratch_shapes=[pltpu.VMEM((2, tq, 1), F32), pltpu.VMEM((2, tq, 1), F32),
                        pltpu.VMEM((2, tq, DIFF_VDIM), F32)],
        compiler_params=_params(("arbitrary", "arbitrary", "arbitrary")),
        name="diff_attn",
    )(dqa, dka, dvb, lam_vecs, subln_g)


def _prefix_sum_lanes(x):
    n = x.shape[-1]
    lane = lax.broadcasted_iota(jnp.int32, x.shape, x.ndim - 1)
    sh = 1
    while sh < n:
        x = x + jnp.where(lane >= sh, pltpu.roll(x, sh, x.ndim - 1), 0.0)
        sh *= 2
    return x


def _fox_sample_kernel(q_ref, k_ref, v_ref, ck_ref, cv_ref, clf_ref, o_ref, *, t, past):
    r = lax.broadcasted_iota(jnp.int32, (t, t), 0)
    c = lax.broadcasted_iota(jnp.int32, (t, t), 1)
    for hh in range(2):
        data = slice(0, HEAD_DIM) if hh == 0 else slice(HEAD_DIM, LANES)
        ext = HEAD_DIM if hh == 0 else 0
        qa = q_ref[0, hh]
        qf = qa.astype(F32)
        cum_q = qf[:, ext:ext + 1] + qf[:, ext + 1:ext + 2] + qf[:, ext + 2:ext + 3]
        cum_c = _prefix_sum_lanes(clf_ref[0, hh])
        suffix = cum_c[:, past - 1:past] - cum_c
        s_c = lax.dot_general(qa[:, data], ck_ref[0, 0, hh].astype(BF16), _NT, preferred_element_type=F32)
        s_c = s_c + cum_q + suffix
        s_n = lax.dot_general(qa, k_ref[0, hh], _NT, preferred_element_type=F32)
        s_n = jnp.where(c <= r, s_n, NEG_INF)
        m = jnp.maximum(jnp.max(s_c, axis=1, keepdims=True), jnp.max(s_n, axis=1, keepdims=True))
        p_c = jnp.exp(s_c - m)
        p_n = jnp.exp(s_n - m)
        denom = jnp.sum(p_c, axis=1, keepdims=True) + jnp.sum(p_n, axis=1, keepdims=True)
        o = jnp.dot(p_c.astype(BF16), cv_ref[0, 0, hh].astype(BF16), preferred_element_type=F32)
        o = o + jnp.dot(p_n.astype(BF16), v_ref[0, hh][:, data], preferred_element_type=F32)
        o_ref[0, :, data] = (o / denom).astype(BF16)


def _fox_sample(fqa, fka, fva, cache_k, cache_v, cache_logf):
    b, nh, t, _ = fqa.shape
    past = cache_k.shape[3]
    clf = cache_logf.reshape(b, nh, 1, past)
    new = pl.BlockSpec((1, 2, t, LANES), lambda i, j: (i, j, 0, 0))
    cache = pl.BlockSpec((1, 1, 2, past, HEAD_DIM), lambda i, j: (0, i, j, 0, 0))
    return pl.pallas_call(
        functools.partial(_fox_sample_kernel, t=t, past=past),
        grid=(b, nh // 2),
        in_specs=[new, new, new, cache, cache,
                  pl.BlockSpec((1, 2, 1, past), lambda i, j: (i, j, 0, 0))],
        out_specs=pl.BlockSpec((1, t, LANES), lambda i, j: (i, 0, j)),
        out_shape=jax.ShapeDtypeStruct((b, t, FOX_WIDTH), BF16),
        compiler_params=_params(("arbitrary", "arbitrary")),
        name="fox_sample",
    )(fqa, fka, fva, cache_k, cache_v, clf)


def _diff_sample_kernel(q_ref, k_ref, v_ref, ck_ref, cv_ref, lam_ref, g_ref, o_ref, *, t, past, lam_init):
    hd = pl.program_id(1)
    slope = _alibi_slope(hd)
    lane = lax.broadcasted_iota(jnp.int32, (t, LANES), 1)
    pos_q = past + lax.broadcasted_iota(jnp.int32, (t, past), 0)
    pos_c = lax.broadcasted_iota(jnp.int32, (t, past), 1)
    bias_c = -slope * jnp.abs(pos_q - pos_c).astype(F32)
    vis_c = (pos_c // CHUNK) <= (pos_q // CHUNK)
    pq_n = past + lax.broadcasted_iota(jnp.int32, (t, t), 0)
    pk_n = past + lax.broadcasted_iota(jnp.int32, (t, t), 1)
    bias_n = -slope * jnp.abs(pq_n - pk_n).astype(F32)
    vis_n = (pk_n // CHUNK) <= (pq_n // CHUNK)
    ck = ck_ref[0, 0, 0].astype(BF16)
    cv = cv_ref[0, 0, 0].astype(BF16)
    v = v_ref[0, 0]
    outs = []
    for c in range(2):
        half = (lane < HEAD_DIM) if c == 0 else (lane >= HEAD_DIM)
        qz = jnp.where(half, q_ref[0, c], jnp.zeros((), BF16))
        kz = jnp.where(half, k_ref[0, c], jnp.zeros((), BF16))
        s_c = lax.dot_general(qz, ck, _NT, preferred_element_type=F32)
        s_c = jnp.where(vis_c, s_c + bias_c, NEG_INF)
        s_n = lax.dot_general(qz, kz, _NT, preferred_element_type=F32)
        s_n = jnp.where(vis_n, s_n + bias_n, NEG_INF)
        m = jnp.maximum(jnp.max(s_c, axis=1, keepdims=True), jnp.max(s_n, axis=1, keepdims=True))
        p_c = jnp.exp(s_c - m)
        p_n = jnp.exp(s_n - m)
        denom = jnp.sum(p_c, axis=1, keepdims=True) + jnp.sum(p_n, axis=1, keepdims=True)
        o = jnp.dot(p_c.astype(BF16), cv, preferred_element_type=F32)
        o = o + jnp.dot(p_n.astype(BF16), v, preferred_element_type=F32)
        outs.append(o / denom)
    lam = _lambda(lam_ref, lam_init)
    o_ref[0] = _diff_finish(outs[0], outs[1], lam, g_ref[...], lam_init).astype(BF16)


def _diff_sample(dqa, dka, dvb, cache_k, cache_v, lam_vecs, subln_g, lam_init):
    b, nm, t, _ = dqa.shape
    past = cache_k.shape[3]
    ck = cache_k.reshape(cache_k.shape[:4] + (DIFF_VDIM,))
    new = pl.BlockSpec((1, 2, t, LANES), lambda i, j: (i, j, 0, 0))
    cache = pl.BlockSpec((1, 1, 1, past, DIFF_VDIM), lambda i, j: (0, i, j, 0, 0))
    return pl.pallas_call(
        functools.partial(_diff_sample_kernel, t=t, past=past, lam_init=lam_init),
        grid=(b, nm // 2),
        in_specs=[new, new, pl.BlockSpec((1, 1, t, DIFF_VDIM), lambda i, j: (i, j, 0, 0)),
                  cache, cache,
                  pl.BlockSpec(lam_vecs.shape, lambda i, j: (0, 0)),
                  pl.BlockSpec(subln_g.shape, lambda i, j: (0, 0))],
        out_specs=pl.BlockSpec((1, t, DIFF_VDIM), lambda i, j: (i, 0, j)),
        out_shape=jax.ShapeDtypeStruct((b, t, DIFF_WIDTH), BF16),
        compiler_params=_params(("arbitrary", "arbitrary")),
        name="diff_sample",
    )(dqa, dka, dvb, ck, cache_v, lam_vecs, subln_g)


def _out_proj_kernel(fo_ref, do_ref, fg_ref, dg_ref, x_ref, mod_ref, w_ref, g_ref, b_ref, y_ref, *, alpha):
    def gated(o_ref, gate_ref):
        g = gate_ref[0].astype(F32)
        return (o_ref[0].astype(F32) * (g * jax.nn.sigmoid(g))).astype(BF16)

    branch = jnp.dot(gated(fo_ref, fg_ref), w_ref[0:FOX_WIDTH, :], preferred_element_type=F32)
    branch = branch + jnp.dot(gated(do_ref, dg_ref), w_ref[FOX_WIDTH:, :], preferred_element_type=F32)
    r = alpha * x_ref[0] + mod_ref[0, 2:3, :] * branch
    mu = jnp.mean(r, axis=-1, keepdims=True)
    rc = r - mu
    var = jnp.mean(rc * rc, axis=-1, keepdims=True)
    y_ref[0] = rc * lax.rsqrt(var + LN_EPS) * g_ref[...] + b_ref[...]


def _out_proj(fox_o, diff_o, fg, dg, x, mod, w_out, ln_g, ln_b, alpha):
    b, t, d = x.shape
    tm = min(t, 512)
    rows = pl.BlockSpec((1, tm, FOX_WIDTH), lambda i, j: (i, j, 0))
    wide = pl.BlockSpec((1, tm, d), lambda i, j: (i, j, 0))

    def full(a):
        return pl.BlockSpec(a.shape, lambda i, j: (0,) * a.ndim)

    return pl.pallas_call(
        functools.partial(_out_proj_kernel, alpha=alpha),
        grid=(b, t // tm),
        in_specs=[rows, rows, rows, rows, wide,
                  pl.BlockSpec((1, 3, d), lambda i, j: (i, 0, 0)),
                  full(w_out), full(ln_g), full(ln_b)],
        out_specs=wide,
        out_shape=jax.ShapeDtypeStruct((b, t, d), F32),
        compiler_params=_params(("arbitrary", "arbitrary")),
        name="out_proj",
    )(fox_o, diff_o, fg, dg, x, mod, w_out, ln_g, ln_b)


def _pack_w_in(w_in):
    fw = FOX_WIDTH
    o = 0
    fq = w_in[:, o:o + fw]; o += fw
    fk = w_in[:, o:o + fw]; o += fw
    fv = w_in[:, o:o + fw]; o += fw
    ff = w_in[:, o:o + FOX_HEADS]; o += FOX_HEADS
    fg = w_in[:, o:o + fw]; o += fw
    dq = w_in[:, o:o + fw]; o += fw
    dk = w_in[:, o:o + fw]; o += fw
    dv = w_in[:, o:o + fw]; o += fw
    dg = w_in[:, o:o + fw]
    qs = HEAD_DIM ** -0.5
    return jnp.concatenate([fq * qs, fk, fv, fg, dq * qs, dk, dv, dg,
                            jnp.tile(ff, (1, LANES // FOX_HEADS))], axis=1).astype(BF16)


def kernel(x_prompt, x_sample, cache_fox_k, cache_fox_v, cache_fox_logf, cache_diff_k, cache_diff_v,
           c_prompt, c_sample, w_ada, b_ada, w_in, b_f, lambda_q1, lambda_k1, lambda_q2, lambda_k2,
           subln_g, w_out, ln_g, ln_b):
    depth = w_in.shape[0]
    assert depth == 1, "single-layer step"
    layer = 0
    lam_init = 0.8 - 0.6 * math.exp(-0.3 * layer)
    alpha = (2 * depth) ** 0.25
    bp, bs = c_prompt.shape[0], c_sample.shape[0]
    d = x_prompt.shape[-1]
    past = cache_fox_k.shape[3]

    c_all = jnp.concatenate([c_prompt, c_sample, jnp.zeros((-(bp + bs) % 8, d), F32)], axis=0)
    mod = _ada(c_all, w_ada[layer], b_ada[layer]).reshape(c_all.shape[0], 3, d)
    mod_p, mod_s = mod[:bp], mod[bp:bp + bs]

    w = _pack_w_in(w_in[layer])
    bf16x = jnp.tile(b_f[layer], LANES // FOX_HEADS).reshape(1, LANES)
    consts = tuple(jnp.asarray(a, BF16 if a.shape[0] > 1 else F32) for a in _fox_placement())
    lam_vecs = jnp.stack([lambda_q1[layer], lambda_k1[layer], lambda_q2[layer], lambda_k2[layer]])
    g = subln_g[layer].reshape(1, DIFF_VDIM)
    w_o = w_out[layer].astype(BF16)
    lg, lb = ln_g[layer].reshape(1, d), ln_b[layer].reshape(1, d)

    def run(x, mod_x, pos_offset, attend):
        (fk, fv, dk, dv, logf, fqa, fka, fva, fg, dqa, dka, dvb, dg) = _in_proj(
            x, mod_x, w, bf16x, consts, pos_offset)
        fox_o, diff_o = attend(fqa, fka, fva, dqa, dka, dvb)
        y = _out_proj(fox_o, diff_o, fg, dg, x, mod_x, w_o, lg, lb, alpha)
        b, t = x.shape[:2]
        states = (fk[None], fv[None], jnp.swapaxes(logf[:, :, :FOX_HEADS], 1, 2)[None],
                  dk.reshape(1, b, DIFF_HEADS, t, 2, HEAD_DIM), dv[None])
        return y, states

    def prompt_attend(fqa, fka, fva, dqa, dka, dvb):
        return _fox_attention(fqa, fka, fva), _diff_attention(dqa, dka, dvb, lam_vecs, g, lam_init)

    def sample_attend(fqa, fka, fva, dqa, dka, dvb):
        return (_fox_sample(fqa, fka, fva, cache_fox_k, cache_fox_v, cache_fox_logf[layer]),
                _diff_sample(dqa, dka, dvb, cache_diff_k, cache_diff_v, lam_vecs, g, lam_init))

    yp, sp = run(x_prompt, mod_p, 0, prompt_attend)
    ys, ss = run(x_sample, mod_s, past, sample_attend)
    return (yp, ys) + sp + ss
```

```python
import functools
import math

import jax
import jax.numpy as jnp
import numpy as np
from jax import lax
from jax.experimental import pallas as pl
from jax.experimental.pallas import tpu as pltpu

F32 = jnp.float32
BF16 = jnp.bfloat16

HEAD_DIM = 64
FOX_HEADS = 8
DIFF_HEADS = 4
DIFF_VDIM = 2 * HEAD_DIM
FOX_WIDTH = FOX_HEADS * HEAD_DIM
DIFF_WIDTH = DIFF_HEADS * DIFF_VDIM
CHUNK = 64
LN_EPS = 1e-5
RMS_EPS = 1e-5
LANES = 128
NEG_INF = float("-inf")

_SEC = {name: i * FOX_WIDTH for i, name in enumerate(("fq", "fk", "fv", "fg", "dq", "dk", "dv", "dg"))}
_FF_OFF = 8 * FOX_WIDTH
_W_COLS = _FF_OFF + LANES

_VMEM_LIMIT = 56 * 1024 * 1024


def _params(sem):
    return pltpu.CompilerParams(dimension_semantics=sem, vmem_limit_bytes=_VMEM_LIMIT)


def _ada_kernel(c_ref, w_ref, b_ref, o_ref):
    c = c_ref[...]
    s = c * jax.nn.sigmoid(c)
    o_ref[...] = jnp.dot(s, w_ref[...], preferred_element_type=F32) + b_ref[...]


def _ada(c_all, w_ada, b_ada):
    rows, d = c_all.shape
    n = w_ada.shape[1]
    tn = 512
    return pl.pallas_call(
        _ada_kernel,
        grid=(n // tn,),
        in_specs=[pl.BlockSpec((rows, d), lambda j: (0, 0)),
                  pl.BlockSpec((d, tn), lambda j: (0, j)),
                  pl.BlockSpec((1, tn), lambda j: (0, j))],
        out_specs=pl.BlockSpec((rows, tn), lambda j: (0, j)),
        out_shape=jax.ShapeDtypeStruct((rows, n), F32),
        compiler_params=_params(("arbitrary",)),
        name="ada",
    )(c_all, w_ada, b_ada.reshape(1, n))


def _split3(x):
    hi = x.astype(BF16)
    r1 = x - hi.astype(F32)
    mid = r1.astype(BF16)
    lo = (r1 - mid.astype(F32)).astype(BF16)
    return hi, mid, lo


def _fox_placement():
    pq = np.zeros((LANES, FOX_WIDTH), np.float32)
    pk = np.zeros((LANES, FOX_WIDTH), np.float32)
    oq = np.zeros((1, FOX_WIDTH), np.float32)
    ok = np.zeros((1, FOX_WIDTH), np.float32)
    for h in range(FOX_HEADS):
        base = (h // 2) * LANES + (HEAD_DIM if h % 2 == 0 else 0)
        for p in range(3):
            pq[8 * p + h, base + p] = 1.0
            ok[0, base + p] = 1.0
            oq[0, base + 3 + p] = 1.0
            pk[8 * p + h, base + 3 + p] = -1.0
    return pq, pk, oq, ok


def _in_proj_kernel(x_ref, mod_ref, w_ref, bf_ref, pq_ref, pk_ref, oq_ref, ok_ref,
                    fk_ref, fv_ref, dk_ref, dv_ref, logf_ref,
                    fqa_ref, fka_ref, fva_ref, fg_ref, dqa_ref, dka_ref, dvb_ref, dg_ref,
                    carry_ref, *, tm, pos_offset):
    ti = pl.program_id(1)
    x = x_ref[0]
    shift = mod_ref[0, 0:1, :]
    scale = mod_ref[0, 1:2, :]
    h = (x * (1.0 + scale) + shift).astype(BF16)

    def proj(name, width=FOX_WIDTH):
        off = _SEC[name] if name in _SEC else _FF_OFF
        return jnp.dot(h, w_ref[:, off:off + width], preferred_element_type=F32)

    lane = lax.broadcasted_iota(jnp.int32, (tm, LANES), 1)
    row = lax.broadcasted_iota(jnp.int32, (tm, LANES), 0)
    lower = lane < HEAD_DIM
    grp = (lane // 8) % 4

    def by_group(a, b, c):
        return jnp.where(grp == 0, a, jnp.where(grp == 1, b, jnp.where(grp == 2, c, jnp.zeros_like(a))))

    logf = jax.nn.log_sigmoid(proj("ff", LANES) + bf_ref[...])
    logf_ref[0] = logf
    tri_r = lax.broadcasted_iota(jnp.int32, (tm, tm), 0)
    tri_c = lax.broadcasted_iota(jnp.int32, (tm, tm), 1)
    tri = (tri_c <= tri_r).astype(BF16)
    part = jnp.dot(tri, by_group(*_split3(logf)), preferred_element_type=F32)
    local = part + pltpu.roll(part, 8, 1) + pltpu.roll(part, 16, 1) + pltpu.roll(part, 24, 1)

    @pl.when(ti == 0)
    def _():
        carry_ref[...] = jnp.zeros_like(carry_ref)

    cum = local + carry_ref[...]
    carry_ref[...] = cum[tm - 1:tm, :]
    cum_cat = by_group(*_split3(cum))
    eq = jnp.dot(cum_cat, pq_ref[...], preferred_element_type=F32) + oq_ref[...]
    ek = jnp.dot(cum_cat, pk_ref[...], preferred_element_type=F32) + ok_ref[...]

    zq, zk, zv = proj("fq"), proj("fk"), proj("fv")
    e_even = (lane == HEAD_DIM).astype(F32)
    e_odd = (lane == 0).astype(F32)
    for j in range(FOX_HEADS // 2):
        sl = slice(j * LANES, (j + 1) * LANES)
        fqa_ref[0, 2 * j] = jnp.where(lower, zq[:, sl], eq[:, sl]).astype(BF16)
        fqa_ref[0, 2 * j + 1] = jnp.where(lower, eq[:, sl], zq[:, sl]).astype(BF16)
        fka_ref[0, 2 * j] = jnp.where(lower, zk[:, sl], ek[:, sl]).astype(BF16)
        fka_ref[0, 2 * j + 1] = jnp.where(lower, ek[:, sl], zk[:, sl]).astype(BF16)
        fva_ref[0, 2 * j] = jnp.where(lower, zv[:, sl], e_even).astype(BF16)
        fva_ref[0, 2 * j + 1] = jnp.where(lower, e_odd, zv[:, sl]).astype(BF16)
    for hd in range(FOX_HEADS):
        fk_ref[0, hd] = zk[:, hd * HEAD_DIM:(hd + 1) * HEAD_DIM]
        fv_ref[0, hd] = zv[:, hd * HEAD_DIM:(hd + 1) * HEAD_DIM]
    fg_ref[0] = proj("fg").astype(BF16)

    zq, zk, zv = proj("dq"), proj("dk"), proj("dv")
    pos = (row + (ti * tm + pos_offset)).astype(F32)
    pos_hi = pos.astype(BF16).astype(F32)
    pos_lo = pos - pos_hi
    sub = lane % HEAD_DIM
    for hd in range(DIFF_HEADS):
        slope = 2.0 ** (-8.0 * (hd + 1) / DIFF_HEADS)
        sl = slice(hd * LANES, (hd + 1) * LANES)
        ones = ((sub == 2) | (sub == 3)).astype(F32)
        ext_q = jnp.where(sub == 0, -slope * pos_hi, jnp.where(sub == 1, -slope * pos_lo, ones))
        ones = ((sub == 0) | (sub == 1)).astype(F32)
        ext_k = jnp.where(sub == 2, slope * pos_hi, jnp.where(sub == 3, slope * pos_lo, ones))
        dqa_ref[0, 2 * hd] = jnp.where(lower, zq[:, sl], ext_q).astype(BF16)
        dqa_ref[0, 2 * hd + 1] = jnp.where(lower, ext_q, zq[:, sl]).astype(BF16)
        dka_ref[0, 2 * hd] = jnp.where(lower, zk[:, sl], ext_k).astype(BF16)
        dka_ref[0, 2 * hd + 1] = jnp.where(lower, ext_k, zk[:, sl]).astype(BF16)
        dk_ref[0, hd] = zk[:, sl]
        dv_ref[0, hd] = zv[:, sl]
        dvb_ref[0, hd] = zv[:, sl].astype(BF16)
    dg_ref[0] = proj("dg").astype(BF16)


def _in_proj(x, mod, w, bf16x, consts, pos_offset):
    b, t, d = x.shape
    tm = min(t, 512)
    nt = t // tm
    pq, pk, oq, ok = consts

    def full(a):
        return pl.BlockSpec(a.shape, lambda i, j: (0,) * a.ndim)

    def heads(n, width):
        return pl.BlockSpec((1, n, tm, width), lambda i, j: (i, 0, j, 0))

    rows = pl.BlockSpec((1, tm, FOX_WIDTH), lambda i, j: (i, j, 0))
    out_shape = (
        jax.ShapeDtypeStruct((b, FOX_HEADS, t, HEAD_DIM), F32),
        jax.ShapeDtypeStruct((b, FOX_HEADS, t, HEAD_DIM), F32),
        jax.ShapeDtypeStruct((b, DIFF_HEADS, t, DIFF_VDIM), F32),
        jax.ShapeDtypeStruct((b, DIFF_HEADS, t, DIFF_VDIM), F32),
        jax.ShapeDtypeStruct((b, t, LANES), F32),
        jax.ShapeDtypeStruct((b, FOX_HEADS, t, LANES), BF16),
        jax.ShapeDtypeStruct((b, FOX_HEADS, t, LANES), BF16),
        jax.ShapeDtypeStruct((b, FOX_HEADS, t, LANES), BF16),
        jax.ShapeDtypeStruct((b, t, FOX_WIDTH), BF16),
        jax.ShapeDtypeStruct((b, 2 * DIFF_HEADS, t, LANES), BF16),
        jax.ShapeDtypeStruct((b, 2 * DIFF_HEADS, t, LANES), BF16),
        jax.ShapeDtypeStruct((b, DIFF_HEADS, t, DIFF_VDIM), BF16),
        jax.ShapeDtypeStruct((b, t, DIFF_WIDTH), BF16),
    )
    out_specs = (
        heads(FOX_HEADS, HEAD_DIM), heads(FOX_HEADS, HEAD_DIM),
        heads(DIFF_HEADS, DIFF_VDIM), heads(DIFF_HEADS, DIFF_VDIM),
        pl.BlockSpec((1, tm, LANES), lambda i, j: (i, j, 0)),
        heads(FOX_HEADS, LANES), heads(FOX_HEADS, LANES), heads(FOX_HEADS, LANES), rows,
        heads(2 * DIFF_HEADS, LANES), heads(2 * DIFF_HEADS, LANES), heads(DIFF_HEADS, DIFF_VDIM), rows,
    )
    return pl.pallas_call(
        functools.partial(_in_proj_kernel, tm=tm, pos_offset=pos_offset),
        grid=(b, nt),
        in_specs=[pl.BlockSpec((1, tm, d), lambda i, j: (i, j, 0)),
                  pl.BlockSpec((1, 3, d), lambda i, j: (i, 0, 0)),
                  full(w), full(bf16x), full(pq), full(pk), full(oq), full(ok)],
        out_specs=out_specs,
        out_shape=out_shape,
        scratch_shapes=[pltpu.VMEM((1, LANES), F32)],
        compiler_params=_params(("arbitrary", "arbitrary")),
        name="in_proj",
    )(x, mod, w, bf16x, pq, pk, oq, ok)


_NT = (((1,), (1,)), ((), ()))
_TN = (((0,), (0,)), ((), ()))


def _pipelined_tiles(n_full, scores, absorb, sa_ref, sb_ref):
    @pl.when(n_full > 0)
    def _():
        scores(0, sa_ref)

        def pair(i, c):
            k0 = 2 * i
            scores(k0 + 1, sb_ref)
            absorb(sa_ref, k0)
            scores(jnp.minimum(k0 + 2, n_full - 1), sa_ref)
            absorb(sb_ref, k0 + 1)
            return c

        lax.fori_loop(0, n_full // 2, pair, 0)

        @pl.when(n_full % 2 == 1)
        def _():
            absorb(sa_ref, n_full - 1)


def _fox_kernel(q_ref, k_ref, v_ref, o_ref, sa_ref, sb_ref, m_ref, acc_ref, *, tq):
    qi = pl.program_id(2)
    lane = lax.broadcasted_iota(jnp.int32, (tq, LANES), 1)
    m_ref[...] = jnp.full_like(m_ref, NEG_INF)
    acc_ref[...] = jnp.zeros_like(acc_ref)

    def scores(ki, dst):
        start = pl.multiple_of(ki * tq, tq)
        for hh in range(2):
            k = k_ref[0, hh, pl.ds(start, tq), :]
            dst[hh] = lax.dot_general(k, q_ref[0, hh], _NT, preferred_element_type=F32)

    def absorb(src, ki, diagonal=False):
        start = pl.multiple_of(ki * tq, tq)
        for hh in range(2):
            s = src[hh]
            if diagonal:
                key = lax.broadcasted_iota(jnp.int32, (tq, tq), 0)
                qry = lax.broadcasted_iota(jnp.int32, (tq, tq), 1)
                s = jnp.where(key <= qry, s, NEG_INF)
            m_prev = m_ref[hh]
            m_new = jnp.maximum(m_prev, jnp.max(s, axis=0, keepdims=True))
            p = jnp.exp(s - m_new)
            alpha = jnp.exp(m_prev - m_new)
            v = v_ref[0, hh, pl.ds(start, tq), :]
            pv = lax.dot_general(v, p.astype(BF16), _TN, preferred_element_type=F32)
            acc_ref[hh] = alpha * acc_ref[hh] + pv
            m_ref[hh] = m_new

    scores(qi, sa_ref)
    absorb(sa_ref, qi, diagonal=True)
    _pipelined_tiles(qi, scores, absorb, sa_ref, sb_ref)
    acc0, acc1 = acc_ref[0], acc_ref[1]
    out0 = (acc0 / acc0[HEAD_DIM:HEAD_DIM + 1, :]).T
    out1 = (acc1 / acc1[0:1, :]).T
    o_ref[0] = jnp.where(lane < HEAD_DIM, out0, out1).astype(BF16)


def _fox_attention(fqa, fka, fva):
    b, nh, t, _ = fqa.shape
    tq = min(t, 512)
    return pl.pallas_call(
        functools.partial(_fox_kernel, tq=tq),
        grid=(b, nh // 2, t // tq),
        in_specs=[pl.BlockSpec((1, 2, tq, LANES), lambda i, j, q: (i, j, q, 0)),
                  pl.BlockSpec((1, 2, t, LANES), lambda i, j, q: (i, j, 0, 0)),
                  pl.BlockSpec((1, 2, t, LANES), lambda i, j, q: (i, j, 0, 0))],
        out_specs=pl.BlockSpec((1, tq, LANES), lambda i, j, q: (i, q, j)),
        out_shape=jax.ShapeDtypeStruct((b, t, FOX_WIDTH), BF16),
        scratch_shapes=[pltpu.VMEM((2, tq, tq), F32), pltpu.VMEM((2, tq, tq), F32),
                        pltpu.VMEM((2, 1, tq), F32), pltpu.VMEM((2, LANES, tq), F32)],
        compiler_params=_params(("arbitrary", "arbitrary", "arbitrary")),
        name="fox_attn",
    )(fqa, fka, fva)


def _alibi_slope(head):
    return jnp.exp2(jnp.full((1, 1), -8.0 / DIFF_HEADS, F32) * (head + 1).astype(F32))


def _lambda(lam_ref, lam_init):
    lv = lam_ref[...]
    e1 = jnp.exp(jnp.sum(lv[0:1] * lv[1:2], axis=1, keepdims=True))
    e2 = jnp.exp(jnp.sum(lv[2:3] * lv[3:4], axis=1, keepdims=True))
    return e1 - e2 + lam_init


def _diff_normed(o0, o1, lam, axis):
    df = o0 - lam * o1
    return df * lax.rsqrt(jnp.mean(df * df, axis=axis, keepdims=True) + RMS_EPS)


_ONES_ROWS = 16


def _diff_kernel(q_ref, k_ref, v_ref, lam_ref, g_ref, o_ref, sa_ref, sb_ref, m_ref, acc_ref, *, tq, lam_init):
    hd = pl.program_id(1)
    qi = pl.program_id(2)
    slope = _alibi_slope(hd)
    m_ref[...] = jnp.full_like(m_ref, NEG_INF)
    acc_ref[...] = jnp.zeros_like(acc_ref)

    def scores(ki, dst):
        start = pl.multiple_of(ki * tq, tq)
        for c in range(2):
            k = k_ref[0, c, pl.ds(start, tq), :]
            dst[c] = lax.dot_general(k, q_ref[0, c], _NT, preferred_element_type=F32)

    def absorb(src, ki, diagonal=False):
        start = pl.multiple_of(ki * tq, tq)
        v = v_ref[0, 0, pl.ds(start, tq), :]
        vt = jnp.concatenate([v.T, jnp.ones((_ONES_ROWS, tq), BF16)], axis=0)
        for c in range(2):
            s = src[c]
            if diagonal:
                key = lax.broadcasted_iota(jnp.int32, (tq, tq), 0)
                qry = lax.broadcasted_iota(jnp.int32, (tq, tq), 1)
                ahead = jnp.maximum(key - qry, 0).astype(F32)
                s = jnp.where((key // CHUNK) <= (qry // CHUNK), s - (2.0 * slope) * ahead, NEG_INF)
            m_prev = m_ref[c]
            m_new = jnp.maximum(m_prev, jnp.max(s, axis=0, keepdims=True))
            p = jnp.exp(s - m_new)
            alpha = jnp.exp(m_prev - m_new)
            acc_ref[c] = alpha * acc_ref[c] + jnp.dot(vt, p.astype(BF16), preferred_element_type=F32)
            m_ref[c] = m_new

    scores(qi, sa_ref)
    absorb(sa_ref, qi, diagonal=True)
    _pipelined_tiles(qi, scores, absorb, sa_ref, sb_ref)
    lam = _lambda(lam_ref, lam_init)
    acc0, acc1 = acc_ref[0], acc_ref[1]
    o0 = acc0[:DIFF_VDIM] / acc0[DIFF_VDIM:DIFF_VDIM + 1]
    o1 = acc1[:DIFF_VDIM] / acc1[DIFF_VDIM:DIFF_VDIM + 1]
    o_ref[0] = (_diff_normed(o0, o1, lam, 0).T * g_ref[...] * (1.0 - lam_init)).astype(BF16)


def _diff_attention(dqa, dka, dvb, lam_vecs, subln_g, lam_init):
    b, nm, t, _ = dqa.shape
    tq = min(t, 512)
    return pl.pallas_call(
        functools.partial(_diff_kernel, tq=tq, lam_init=lam_init),
        grid=(b, nm // 2, t // tq),
        in_specs=[pl.BlockSpec((1, 2, tq, LANES), lambda i, j, q: (i, j, q, 0)),
                  pl.BlockSpec((1, 2, t, LANES), lambda i, j, q: (i, j, 0, 0)),
                  pl.BlockSpec((1, 1, t, DIFF_VDIM), lambda i, j, q: (i, j, 0, 0)),
                  pl.BlockSpec(lam_vecs.shape, lambda i, j, q: (0, 0)),
                  pl.BlockSpec(subln_g.shape, lambda i, j, q: (0, 0))],
        out_specs=pl.BlockSpec((1, tq, DIFF_VDIM), lambda i, j, q: (i, q, j)),
        out_shape=jax.ShapeDtypeStruct((b, t, DIFF_WIDTH), BF16),
        scratch_shapes=[pltpu.VMEM((2, tq, tq), F32), pltpu.VMEM((2, tq, tq), F32),
                        pltpu.VMEM((2, 1, tq), F32), pltpu.VMEM((2, DIFF_VDIM + _ONES_ROWS, tq), F32)],
        compiler_params=_params(("arbitrary", "arbitrary", "arbitrary")),
        name="diff_attn",
    )(dqa, dka, dvb, lam_vecs, subln_g)


def _prefix_sum_lanes(x):
    n = x.shape[-1]
    lane = lax.broadcasted_iota(jnp.int32, x.shape, x.ndim - 1)
    sh = 1
    while sh < n:
        x = x + jnp.where(lane >= sh, pltpu.roll(x, sh, x.ndim - 1), 0.0)
        sh *= 2
    return x


def _fox_sample_kernel(q_ref, k_ref, v_ref, ck_ref, cv_ref, clf_ref, o_ref, *, t, past):
    r = lax.broadcasted_iota(jnp.int32, (t, t), 0)
    c = lax.broadcasted_iota(jnp.int32, (t, t), 1)
    for hh in range(2):
        data = slice(0, HEAD_DIM) if hh == 0 else slice(HEAD_DIM, LANES)
        ext = HEAD_DIM if hh == 0 else 0
        qa = q_ref[0, hh]
        qf = qa.astype(F32)
        cum_q = qf[:, ext:ext + 1] + qf[:, ext + 1:ext + 2] + qf[:, ext + 2:ext + 3]
        cum_c = _prefix_sum_lanes(clf_ref[0, hh])
        suffix = cum_c[:, past - 1:past] - cum_c
        s_c = lax.dot_general(qa[:, data], ck_ref[0, 0, hh].astype(BF16), _NT, preferred_element_type=F32)
        s_c = s_c + cum_q + suffix
        s_n = lax.dot_general(qa, k_ref[0, hh], _NT, preferred_element_type=F32)
        s_n = jnp.where(c <= r, s_n, NEG_INF)
        m = jnp.maximum(jnp.max(s_c, axis=1, keepdims=True), jnp.max(s_n, axis=1, keepdims=True))
        p_c = jnp.exp(s_c - m)
        p_n = jnp.exp(s_n - m)
        denom = jnp.sum(p_c, axis=1, keepdims=True) + jnp.sum(p_n, axis=1, keepdims=True)
        o = jnp.dot(p_c.astype(BF16), cv_ref[0, 0, hh].astype(BF16), preferred_element_type=F32)
        o = o + jnp.dot(p_n.astype(BF16), v_ref[0, hh][:, data], preferred_element_type=F32)
        o_ref[0, :, data] = (o / denom).astype(BF16)


def _fox_sample(fqa, fka, fva, cache_k, cache_v, cache_logf):
    b, nh, t, _ = fqa.shape
    past = cache_k.shape[3]
    clf = cache_logf.reshape(b, nh, 1, past)
    new = pl.BlockSpec((1, 2, t, LANES), lambda i, j: (i, j, 0, 0))
    cache = pl.BlockSpec((1, 1, 2, past, HEAD_DIM), lambda i, j: (0, i, j, 0, 0))
    return pl.pallas_call(
        functools.partial(_fox_sample_kernel, t=t, past=past),
        grid=(b, nh // 2),
        in_specs=[new, new, new, cache, cache,
                  pl.BlockSpec((1, 2, 1, past), lambda i, j: (i, j, 0, 0))],
        out_specs=pl.BlockSpec((1, t, LANES), lambda i, j: (i, 0, j)),
        out_shape=jax.ShapeDtypeStruct((b, t, FOX_WIDTH), BF16),
        compiler_params=_params(("arbitrary", "arbitrary")),
        name="fox_sample",
    )(fqa, fka, fva, cache_k, cache_v, clf)


def _diff_sample_kernel(q_ref, k_ref, v_ref, ck_ref, cv_ref, lam_ref, g_ref, o_ref, *, t, past, lam_init):
    hd = pl.program_id(1)
    slope = _alibi_slope(hd)
    lane = lax.broadcasted_iota(jnp.int32, (t, LANES), 1)
    pos_q = past + lax.broadcasted_iota(jnp.int32, (t, past), 0)
    pos_c = lax.broadcasted_iota(jnp.int32, (t, past), 1)
    bias_c = -slope * jnp.abs(pos_q - pos_c).astype(F32)
    vis_c = (pos_c // CHUNK) <= (pos_q // CHUNK)
    pq_n = past + lax.broadcasted_iota(jnp.int32, (t, t), 0)
    pk_n = past + lax.broadcasted_iota(jnp.int32, (t, t), 1)
    bias_n = -slope * jnp.abs(pq_n - pk_n).astype(F32)
    vis_n = (pk_n // CHUNK) <= (pq_n // CHUNK)
    ck = ck_ref[0, 0, 0].astype(BF16)
    cv = cv_ref[0, 0, 0].astype(BF16)
    v = v_ref[0, 0]
    outs = []
    for c in range(2):
        half = (lane < HEAD_DIM) if c == 0 else (lane >= HEAD_DIM)
        qz = jnp.where(half, q_ref[0, c], jnp.zeros((), BF16))
        kz = jnp.where(half, k_ref[0, c], jnp.zeros((), BF16))
        s_c = lax.dot_general(qz, ck, _NT, preferred_element_type=F32)
        s_c = jnp.where(vis_c, s_c + bias_c, NEG_INF)
        s_n = lax.dot_general(qz, kz, _NT, preferred_element_type=F32)
        s_n = jnp.where(vis_n, s_n + bias_n, NEG_INF)
        m = jnp.maximum(jnp.max(s_c, axis=1, keepdims=True), jnp.max(s_n, axis=1, keepdims=True))
        p_c = jnp.exp(s_c - m)
        p_n = jnp.exp(s_n - m)
        denom = jnp.sum(p_c, axis=1, keepdims=True) + jnp.sum(p_n, axis=1, keepdims=True)
        o = jnp.dot(p_c.astype(BF16), cv, preferred_element_type=F32)
        o = o + jnp.dot(p_n.astype(BF16), v, preferred_element_type=F32)
        outs.append(o / denom)
    lam = _lambda(lam_ref, lam_init)
    o_ref[0] = (_diff_normed(outs[0], outs[1], lam, 1) * g_ref[...] * (1.0 - lam_init)).astype(BF16)


def _diff_sample(dqa, dka, dvb, cache_k, cache_v, lam_vecs, subln_g, lam_init):
    b, nm, t, _ = dqa.shape
    past = cache_k.shape[3]
    ck = cache_k.reshape(cache_k.shape[:4] + (DIFF_VDIM,))
    new = pl.BlockSpec((1, 2, t, LANES), lambda i, j: (i, j, 0, 0))
    cache = pl.BlockSpec((1, 1, 1, past, DIFF_VDIM), lambda i, j: (0, i, j, 0, 0))
    return pl.pallas_call(
        functools.partial(_diff_sample_kernel, t=t, past=past, lam_init=lam_init),
        grid=(b, nm // 2),
        in_specs=[new, new, pl.BlockSpec((1, 1, t, DIFF_VDIM), lambda i, j: (i, j, 0, 0)),
                  cache, cache,
                  pl.BlockSpec(lam_vecs.shape, lambda i, j: (0, 0)),
                  pl.BlockSpec(subln_g.shape, lambda i, j: (0, 0))],
        out_specs=pl.BlockSpec((1, t, DIFF_VDIM), lambda i, j: (i, 0, j)),
        out_shape=jax.ShapeDtypeStruct((b, t, DIFF_WIDTH), BF16),
        compiler_params=_params(("arbitrary", "arbitrary")),
        name="diff_sample",
    )(dqa, dka, dvb, ck, cache_v, lam_vecs, subln_g)


def _out_proj_kernel(fo_ref, do_ref, fg_ref, dg_ref, x_ref, mod_ref, w_ref, g_ref, b_ref, y_ref, *, alpha):
    def gated(o_ref, gate_ref):
        g = gate_ref[0].astype(F32)
        return (o_ref[0].astype(F32) * (g * jax.nn.sigmoid(g))).astype(BF16)

    branch = jnp.dot(gated(fo_ref, fg_ref), w_ref[0:FOX_WIDTH, :], preferred_element_type=F32)
    branch = branch + jnp.dot(gated(do_ref, dg_ref), w_ref[FOX_WIDTH:, :], preferred_element_type=F32)
    r = alpha * x_ref[0] + mod_ref[0, 2:3, :] * branch
    mu = jnp.mean(r, axis=-1, keepdims=True)
    rc = r - mu
    var = jnp.mean(rc * rc, axis=-1, keepdims=True)
    y_ref[0] = rc * lax.rsqrt(var + LN_EPS) * g_ref[...] + b_ref[...]


def _out_proj(fox_o, diff_o, fg, dg, x, mod, w_out, ln_g, ln_b, alpha):
    b, t, d = x.shape
    tm = min(t, 512)
    rows = pl.BlockSpec((1, tm, FOX_WIDTH), lambda i, j: (i, j, 0))
    wide = pl.BlockSpec((1, tm, d), lambda i, j: (i, j, 0))

    def full(a):
        return pl.BlockSpec(a.shape, lambda i, j: (0,) * a.ndim)

    return pl.pallas_call(
        functools.partial(_out_proj_kernel, alpha=alpha),
        grid=(b, t // tm),
        in_specs=[rows, rows, rows, rows, wide,
                  pl.BlockSpec((1, 3, d), lambda i, j: (i, 0, 0)),
                  full(w_out), full(ln_g), full(ln_b)],
        out_specs=wide,
        out_shape=jax.ShapeDtypeStruct((b, t, d), F32),
        compiler_params=_params(("arbitrary", "arbitrary")),
        name="out_proj",
    )(fox_o, diff_o, fg, dg, x, mod, w_out, ln_g, ln_b)


def _pack_w_in(w_in):
    fw = FOX_WIDTH
    o = 0
    fq = w_in[:, o:o + fw]; o += fw
    fk = w_in[:, o:o + fw]; o += fw
    fv = w_in[:, o:o + fw]; o += fw
    ff = w_in[:, o:o + FOX_HEADS]; o += FOX_HEADS
    fg = w_in[:, o:o + fw]; o += fw
    dq = w_in[:, o:o + fw]; o += fw
    dk = w_in[:, o:o + fw]; o += fw
    dv = w_in[:, o:o + fw]; o += fw
    dg = w_in[:, o:o + fw]
    qs = HEAD_DIM ** -0.5
    return jnp.concatenate([fq * qs, fk, fv, fg, dq * qs, dk, dv, dg,
                            jnp.tile(ff, (1, LANES // FOX_HEADS))], axis=1).astype(BF16)


def kernel(x_prompt, x_sample, cache_fox_k, cache_fox_v, cache_fox_logf, cache_diff_k, cache_diff_v,
           c_prompt, c_sample, w_ada, b_ada, w_in, b_f, lambda_q1, lambda_k1, lambda_q2, lambda_k2,
           subln_g, w_out, ln_g, ln_b):
    depth = w_in.shape[0]
    assert depth == 1, "single-layer step"
    layer = 0
    lam_init = 0.8 - 0.6 * math.exp(-0.3 * layer)
    alpha = (2 * depth) ** 0.25
    bp, bs = c_prompt.shape[0], c_sample.shape[0]
    d = x_prompt.shape[-1]
    past = cache_fox_k.shape[3]

    c_all = jnp.concatenate([c_prompt, c_sample, jnp.zeros((-(bp + bs) % 8, d), F32)], axis=0)
    mod = _ada(c_all, w_ada[layer], b_ada[layer]).reshape(c_all.shape[0], 3, d)
    mod_p, mod_s = mod[:bp], mod[bp:bp + bs]

    w = _pack_w_in(w_in[layer])
    bf16x = jnp.tile(b_f[layer], LANES // FOX_HEADS).reshape(1, LANES)
    consts = tuple(jnp.asarray(a, BF16 if a.shape[0] > 1 else F32) for a in _fox_placement())
    lam_vecs = jnp.stack([lambda_q1[layer], lambda_k1[layer], lambda_q2[layer], lambda_k2[layer]])
    g = subln_g[layer].reshape(1, DIFF_VDIM)
    w_o = w_out[layer].astype(BF16)
    lg, lb = ln_g[layer].reshape(1, d), ln_b[layer].reshape(1, d)

    def run(x, mod_x, pos_offset, attend):
        (fk, fv, dk, dv, logf, fqa, fka, fva, fg, dqa, dka, dvb, dg) = _in_proj(
            x, mod_x, w, bf16x, consts, pos_offset)
        fox_o, diff_o = attend(fqa, fka, fva, dqa, dka, dvb)
        y = _out_proj(fox_o, diff_o, fg, dg, x, mod_x, w_o, lg, lb, alpha)
        b, t = x.shape[:2]
        states = (fk[None], fv[None], jnp.swapaxes(logf[:, :, :FOX_HEADS], 1, 2)[None],
                  dk.reshape(1, b, DIFF_HEADS, t, 2, HEAD_DIM), dv[None])
        return y, states

    def prompt_attend(fqa, fka, fva, dqa, dka, dvb):
        return _fox_attention(fqa, fka, fva), _diff_attention(dqa, dka, dvb, lam_vecs, g, lam_init)

    def sample_attend(fqa, fka, fva, dqa, dka, dvb):
        return (_fox_sample(fqa, fka, fva, cache_fox_k, cache_fox_v, cache_fox_logf[layer]),
                _diff_sample(dqa, dka, dvb, cache_diff_k, cache_diff_v, lam_vecs, g, lam_init))

    yp, sp = run(x_prompt, mod_p, 0, prompt_attend)
    ys, ss = run(x_sample, mod_s, past, sample_attend)
    return (yp, ys) + sp + ss
```

```python
import functools
import math

import jax
import jax.numpy as jnp
import numpy as np
from jax import lax
from jax.experimental import pallas as pl
from jax.experimental.pallas import tpu as pltpu

F32 = jnp.float32
BF16 = jnp.bfloat16

HEAD_DIM = 64
FOX_HEADS = 8
DIFF_HEADS = 4
DIFF_VDIM = 2 * HEAD_DIM
FOX_WIDTH = FOX_HEADS * HEAD_DIM
DIFF_WIDTH = DIFF_HEADS * DIFF_VDIM
CHUNK = 64
LN_EPS = 1e-5
RMS_EPS = 1e-5
LANES = 128
NEG_INF = float("-inf")
LOG2E = math.log2(math.e)

_SEC = {name: i * FOX_WIDTH for i, name in enumerate(("fq", "fk", "fv", "fg", "dq", "dk", "dv", "dg"))}
_FF_OFF = 8 * FOX_WIDTH
_W_COLS = _FF_OFF + LANES

_VMEM_LIMIT = 56 * 1024 * 1024


def _params(sem):
    return pltpu.CompilerParams(dimension_semantics=sem, vmem_limit_bytes=_VMEM_LIMIT)


def _ada_kernel(c_ref, w_ref, b_ref, o_ref):
    c = c_ref[...]
    s = c * jax.nn.sigmoid(c)
    o_ref[...] = jnp.dot(s, w_ref[...], preferred_element_type=F32) + b_ref[...]


def _ada(c_all, w_ada, b_ada):
    rows, d = c_all.shape
    n = w_ada.shape[1]
    tn = 512
    return pl.pallas_call(
        _ada_kernel,
        grid=(n // tn,),
        in_specs=[pl.BlockSpec((rows, d), lambda j: (0, 0)),
                  pl.BlockSpec((d, tn), lambda j: (0, j)),
                  pl.BlockSpec((1, tn), lambda j: (0, j))],
        out_specs=pl.BlockSpec((rows, tn), lambda j: (0, j)),
        out_shape=jax.ShapeDtypeStruct((rows, n), F32),
        compiler_params=_params(("arbitrary",)),
        name="ada",
    )(c_all, w_ada, b_ada.reshape(1, n))


def _split3(x):
    hi = x.astype(BF16)
    r1 = x - hi.astype(F32)
    mid = r1.astype(BF16)
    lo = (r1 - mid.astype(F32)).astype(BF16)
    return hi, mid, lo


def _fox_placement():
    pq = np.zeros((LANES, FOX_WIDTH), np.float32)
    pk = np.zeros((LANES, FOX_WIDTH), np.float32)
    oq = np.zeros((1, FOX_WIDTH), np.float32)
    ok = np.zeros((1, FOX_WIDTH), np.float32)
    for h in range(FOX_HEADS):
        base = (h // 2) * LANES + (HEAD_DIM if h % 2 == 0 else 0)
        for p in range(3):
            pq[8 * p + h, base + p] = 1.0
            ok[0, base + p] = 1.0
            oq[0, base + 3 + p] = 1.0
            pk[8 * p + h, base + 3 + p] = -1.0
    return pq, pk, oq, ok


def _in_proj_kernel(x_ref, mod_ref, w_ref, bf_ref, pq_ref, pk_ref, oq_ref, ok_ref,
                    fk_ref, fv_ref, dk_ref, dv_ref, logf_ref,
                    fqa_ref, fka_ref, fva_ref, fg_ref, dqa_ref, dka_ref, dvb_ref, dg_ref,
                    carry_ref, *, tm, pos_offset):
    ti = pl.program_id(1)
    x = x_ref[0]
    shift = mod_ref[0, 0:1, :]
    scale = mod_ref[0, 1:2, :]
    h = (x * (1.0 + scale) + shift).astype(BF16)

    def proj(name, width=FOX_WIDTH):
        off = _SEC[name] if name in _SEC else _FF_OFF
        return jnp.dot(h, w_ref[:, off:off + width], preferred_element_type=F32)

    lane = lax.broadcasted_iota(jnp.int32, (tm, LANES), 1)
    row = lax.broadcasted_iota(jnp.int32, (tm, LANES), 0)
    lower = lane < HEAD_DIM
    grp = (lane // 8) % 4

    def by_group(a, b, c):
        return jnp.where(grp == 0, a, jnp.where(grp == 1, b, jnp.where(grp == 2, c, jnp.zeros_like(a))))

    logf = jax.nn.log_sigmoid(proj("ff", LANES) + bf_ref[...])
    logf_ref[0] = logf
    tri_r = lax.broadcasted_iota(jnp.int32, (tm, tm), 0)
    tri_c = lax.broadcasted_iota(jnp.int32, (tm, tm), 1)
    tri = (tri_c <= tri_r).astype(BF16)
    part = jnp.dot(tri, by_group(*_split3(logf)), preferred_element_type=F32)
    local = part + pltpu.roll(part, 8, 1) + pltpu.roll(part, 16, 1) + pltpu.roll(part, 24, 1)

    @pl.when(ti == 0)
    def _():
        carry_ref[...] = jnp.zeros_like(carry_ref)

    cum = local + carry_ref[...]
    carry_ref[...] = cum[tm - 1:tm, :]
    cum_cat = by_group(*_split3(cum * LOG2E))
    eq = jnp.dot(cum_cat, pq_ref[...], preferred_element_type=F32) + oq_ref[...]
    ek = jnp.dot(cum_cat, pk_ref[...], preferred_element_type=F32) + ok_ref[...]

    zq, zk, zv = proj("fq"), proj("fk"), proj("fv")
    e_even = (lane == HEAD_DIM).astype(F32)
    e_odd = (lane == 0).astype(F32)
    for j in range(FOX_HEADS // 2):
        sl = slice(j * LANES, (j + 1) * LANES)
        fqa_ref[0, 2 * j] = jnp.where(lower, zq[:, sl], eq[:, sl]).astype(BF16)
        fqa_ref[0, 2 * j + 1] = jnp.where(lower, eq[:, sl], zq[:, sl]).astype(BF16)
        fka_ref[0, 2 * j] = jnp.where(lower, zk[:, sl], ek[:, sl]).astype(BF16)
        fka_ref[0, 2 * j + 1] = jnp.where(lower, ek[:, sl], zk[:, sl]).astype(BF16)
        fva_ref[0, 2 * j] = jnp.where(lower, zv[:, sl], e_even).astype(BF16)
        fva_ref[0, 2 * j + 1] = jnp.where(lower, e_odd, zv[:, sl]).astype(BF16)
    for hd in range(FOX_HEADS):
        fk_ref[0, hd] = zk[:, hd * HEAD_DIM:(hd + 1) * HEAD_DIM]
        fv_ref[0, hd] = zv[:, hd * HEAD_DIM:(hd + 1) * HEAD_DIM]
    fg_ref[0] = proj("fg").astype(BF16)

    zq, zk, zv = proj("dq"), proj("dk"), proj("dv")
    pos = (row + (ti * tm + pos_offset)).astype(F32)
    sub = lane % HEAD_DIM
    for hd in range(DIFF_HEADS):
        slope = 2.0 ** (-8.0 * (hd + 1) / DIFF_HEADS)
        sl = slice(hd * LANES, (hd + 1) * LANES)
        p0, p1, p2 = (a.astype(F32) for a in _split3((slope * LOG2E) * pos))
        pieces = jnp.where(sub % 3 == 0, p0, jnp.where(sub % 3 == 1, p1, p2))
        ext_q = jnp.where(sub < 3, -pieces, (sub < 6).astype(F32))
        ext_k = jnp.where(sub < 3, 1.0, jnp.where(sub < 6, pieces, 0.0))
        dqa_ref[0, 2 * hd] = jnp.where(lower, zq[:, sl], ext_q).astype(BF16)
        dqa_ref[0, 2 * hd + 1] = jnp.where(lower, ext_q, zq[:, sl]).astype(BF16)
        dka_ref[0, 2 * hd] = jnp.where(lower, zk[:, sl], ext_k).astype(BF16)
        dka_ref[0, 2 * hd + 1] = jnp.where(lower, ext_k, zk[:, sl]).astype(BF16)
        dk_ref[0, hd] = zk[:, sl]
        dv_ref[0, hd] = zv[:, sl]
        dvb_ref[0, hd] = zv[:, sl].astype(BF16)
    dg_ref[0] = proj("dg").astype(BF16)


def _in_proj(x, mod, w, bf16x, consts, pos_offset):
    b, t, d = x.shape
    tm = min(t, 512)
    nt = t // tm
    pq, pk, oq, ok = consts

    def full(a):
        return pl.BlockSpec(a.shape, lambda i, j: (0,) * a.ndim)

    def heads(n, width):
        return pl.BlockSpec((1, n, tm, width), lambda i, j: (i, 0, j, 0))

    rows = pl.BlockSpec((1, tm, FOX_WIDTH), lambda i, j: (i, j, 0))
    out_shape = (
        jax.ShapeDtypeStruct((b, FOX_HEADS, t, HEAD_DIM), F32),
        jax.ShapeDtypeStruct((b, FOX_HEADS, t, HEAD_DIM), F32),
        jax.ShapeDtypeStruct((b, DIFF_HEADS, t, DIFF_VDIM), F32),
        jax.ShapeDtypeStruct((b, DIFF_HEADS, t, DIFF_VDIM), F32),
        jax.ShapeDtypeStruct((b, t, LANES), F32),
        jax.ShapeDtypeStruct((b, FOX_HEADS, t, LANES), BF16),
        jax.ShapeDtypeStruct((b, FOX_HEADS, t, LANES), BF16),
        jax.ShapeDtypeStruct((b, FOX_HEADS, t, LANES), BF16),
        jax.ShapeDtypeStruct((b, t, FOX_WIDTH), BF16),
        jax.ShapeDtypeStruct((b, 2 * DIFF_HEADS, t, LANES), BF16),
        jax.ShapeDtypeStruct((b, 2 * DIFF_HEADS, t, LANES), BF16),
        jax.ShapeDtypeStruct((b, DIFF_HEADS, t, DIFF_VDIM), BF16),
        jax.ShapeDtypeStruct((b, t, DIFF_WIDTH), BF16),
    )
    out_specs = (
        heads(FOX_HEADS, HEAD_DIM), heads(FOX_HEADS, HEAD_DIM),
        heads(DIFF_HEADS, DIFF_VDIM), heads(DIFF_HEADS, DIFF_VDIM),
        pl.BlockSpec((1, tm, LANES), lambda i, j: (i, j, 0)),
        heads(FOX_HEADS, LANES), heads(FOX_HEADS, LANES), heads(FOX_HEADS, LANES), rows,
        heads(2 * DIFF_HEADS, LANES), heads(2 * DIFF_HEADS, LANES), heads(DIFF_HEADS, DIFF_VDIM), rows,
    )
    return pl.pallas_call(
        functools.partial(_in_proj_kernel, tm=tm, pos_offset=pos_offset),
        grid=(b, nt),
        in_specs=[pl.BlockSpec((1, tm, d), lambda i, j: (i, j, 0)),
                  pl.BlockSpec((1, 3, d), lambda i, j: (i, 0, 0)),
                  full(w), full(bf16x), full(pq), full(pk), full(oq), full(ok)],
        out_specs=out_specs,
        out_shape=out_shape,
        scratch_shapes=[pltpu.VMEM((1, LANES), F32)],
        compiler_params=_params(("arbitrary", "arbitrary")),
        name="in_proj",
    )(x, mod, w, bf16x, pq, pk, oq, ok)


_NT = (((1,), (1,)), ((), ()))
_TN = (((0,), (0,)), ((), ()))


def _pipelined_tiles(n_full, scores, absorb, sa_ref, sb_ref):
    chains = range(2)
    for c in chains:
        scores(0, sa_ref, c)

    def pair(i, carry):
        k0 = 2 * i
        for c in chains:
            scores(k0 + 1, sb_ref, c)
            absorb(sa_ref, k0, c, False)
        for c in chains:
            scores(k0 + 2, sa_ref, c)
            absorb(sb_ref, k0 + 1, c, False)
        return carry

    lax.fori_loop(0, n_full // 2, pair, 0)

    @pl.when(n_full % 2 == 0)
    def _():
        for c in chains:
            absorb(sa_ref, n_full, c, True)

    @pl.when(n_full % 2 == 1)
    def _():
        for c in chains:
            scores(n_full, sb_ref, c)
            absorb(sa_ref, n_full - 1, c, False)
        for c in chains:
            absorb(sb_ref, n_full, c, True)


def _fox_kernel(q_ref, k_ref, v_ref, o_ref, sa_ref, sb_ref, m_ref, acc_ref, *, tq):
    qi = pl.program_id(2)
    lane = lax.broadcasted_iota(jnp.int32, (tq, LANES), 1)
    m_ref[...] = jnp.full_like(m_ref, NEG_INF)
    acc_ref[...] = jnp.zeros_like(acc_ref)

    def scores(ki, dst, hh):
        k = k_ref[0, hh, pl.ds(pl.multiple_of(ki * tq, tq), tq), :]
        dst[hh] = lax.dot_general(k, q_ref[0, hh], _NT, preferred_element_type=F32)

    def absorb(src, ki, hh, diagonal):
        s = src[hh]
        if diagonal:
            key = lax.broadcasted_iota(jnp.int32, (tq, tq), 0)
            qry = lax.broadcasted_iota(jnp.int32, (tq, tq), 1)
            s = jnp.where(key <= qry, s, NEG_INF)
        m_prev = m_ref[hh]
        m_new = jnp.maximum(m_prev, jnp.max(s, axis=0, keepdims=True))
        p = jnp.exp2(s - m_new)
        alpha = jnp.exp2(m_prev - m_new)
        v = v_ref[0, hh, pl.ds(pl.multiple_of(ki * tq, tq), tq), :]
        pv = lax.dot_general(v, p.astype(BF16), _TN, preferred_element_type=F32)
        acc_ref[hh] = alpha * acc_ref[hh] + pv
        m_ref[hh] = m_new

    _pipelined_tiles(qi, scores, absorb, sa_ref, sb_ref)
    acc0, acc1 = acc_ref[0], acc_ref[1]
    out0 = (acc0 / acc0[HEAD_DIM:HEAD_DIM + 1, :]).T
    out1 = (acc1 / acc1[0:1, :]).T
    o_ref[0] = jnp.where(lane < HEAD_DIM, out0, out1).astype(BF16)


def _fox_attention(fqa, fka, fva):
    b, nh, t, _ = fqa.shape
    tq = min(t, 512)
    return pl.pallas_call(
        functools.partial(_fox_kernel, tq=tq),
        grid=(b, nh // 2, t // tq),
        in_specs=[pl.BlockSpec((1, 2, tq, LANES), lambda i, j, q: (i, j, q, 0)),
                  pl.BlockSpec((1, 2, t, LANES), lambda i, j, q: (i, j, 0, 0)),
                  pl.BlockSpec((1, 2, t, LANES), lambda i, j, q: (i, j, 0, 0))],
        out_specs=pl.BlockSpec((1, tq, LANES), lambda i, j, q: (i, q, j)),
        out_shape=jax.ShapeDtypeStruct((b, t, FOX_WIDTH), BF16),
        scratch_shapes=[pltpu.VMEM((2, tq, tq), F32), pltpu.VMEM((2, tq, tq), F32),
                        pltpu.VMEM((2, 1, tq), F32), pltpu.VMEM((2, LANES, tq), F32)],
        compiler_params=_params(("arbitrary", "arbitrary", "arbitrary")),
        name="fox_attn",
    )(fqa, fka, fva)


def _alibi_slope(head):
    return jnp.exp2(jnp.full((1, 1), -8.0 / DIFF_HEADS, F32) * (head + 1).astype(F32))


def _lambda(lam_ref, lam_init):
    lv = lam_ref[...]
    e1 = jnp.exp(jnp.sum(lv[0:1] * lv[1:2], axis=1, keepdims=True))
    e2 = jnp.exp(jnp.sum(lv[2:3] * lv[3:4], axis=1, keepdims=True))
    return e1 - e2 + lam_init


def _diff_normed(o0, o1, lam, axis):
    df = o0 - lam * o1
    return df * lax.rsqrt(jnp.mean(df * df, axis=axis, keepdims=True) + RMS_EPS)


_ONES_ROWS = 16


def _diff_kernel(q_ref, k_ref, v_ref, lam_ref, g_ref, o_ref, sa_ref, sb_ref, m_ref, acc_ref, *, tq, lam_init):
    hd = pl.program_id(1)
    qi = pl.program_id(2)
    slope = _alibi_slope(hd)
    m_ref[...] = jnp.full_like(m_ref, NEG_INF)
    acc_ref[...] = jnp.zeros_like(acc_ref)

    def scores(ki, dst, c):
        k = k_ref[0, c, pl.ds(pl.multiple_of(ki * tq, tq), tq), :]
        dst[c] = lax.dot_general(k, q_ref[0, c], _NT, preferred_element_type=F32)

    def absorb(src, ki, c, diagonal):
        s = src[c]
        if diagonal:
            key = lax.broadcasted_iota(jnp.int32, (tq, tq), 0)
            qry = lax.broadcasted_iota(jnp.int32, (tq, tq), 1)
            ahead = jnp.maximum(key - qry, 0).astype(F32)
            s = jnp.where((key // CHUNK) <= (qry // CHUNK), s - (2.0 * LOG2E * slope) * ahead, NEG_INF)
        m_prev = m_ref[c]
        m_new = jnp.maximum(m_prev, jnp.max(s, axis=0, keepdims=True))
        p = jnp.exp2(s - m_new)
        alpha = jnp.exp2(m_prev - m_new)
        v = v_ref[0, 0, pl.ds(pl.multiple_of(ki * tq, tq), tq), :]
        vt = jnp.concatenate([v.T, jnp.ones((_ONES_ROWS, tq), BF16)], axis=0)
        acc_ref[c] = alpha * acc_ref[c] + jnp.dot(vt, p.astype(BF16), preferred_element_type=F32)
        m_ref[c] = m_new

    _pipelined_tiles(qi, scores, absorb, sa_ref, sb_ref)
    lam = _lambda(lam_ref, lam_init)
    acc0, acc1 = acc_ref[0], acc_ref[1]
    o0 = acc0[:DIFF_VDIM] / acc0[DIFF_VDIM:DIFF_VDIM + 1]
    o1 = acc1[:DIFF_VDIM] / acc1[DIFF_VDIM:DIFF_VDIM + 1]
    o_ref[0] = (_diff_normed(o0, o1, lam, 0).T * g_ref[...] * (1.0 - lam_init)).astype(BF16)


def _diff_attention(dqa, dka, dvb, lam_vecs, subln_g, lam_init):
    b, nm, t, _ = dqa.shape
    tq = min(t, 512)
    return pl.pallas_call(
        functools.partial(_diff_kernel, tq=tq, lam_init=lam_init),
        grid=(b, nm // 2, t // tq),
        in_specs=[pl.BlockSpec((1, 2, tq, LANES), lambda i, j, q: (i, j, q, 0)),
                  pl.BlockSpec((1, 2, t, LANES), lambda i, j, q: (i, j, 0, 0)),
                  pl.BlockSpec((1, 1, t, DIFF_VDIM), lambda i, j, q: (i, j, 0, 0)),
                  pl.BlockSpec(lam_vecs.shape, lambda i, j, q: (0, 0)),
                  pl.BlockSpec(subln_g.shape, lambda i, j, q: (0, 0))],
        out_specs=pl.BlockSpec((1, tq, DIFF_VDIM), lambda i, j, q: (i, q, j)),
        out_shape=jax.ShapeDtypeStruct((b, t, DIFF_WIDTH), BF16),
        scratch_shapes=[pltpu.VMEM((2, tq, tq), F32), pltpu.VMEM((2, tq, tq), F32),
                        pltpu.VMEM((2, 1, tq), F32), pltpu.VMEM((2, DIFF_VDIM + _ONES_ROWS, tq), F32)],
        compiler_params=_params(("arbitrary", "arbitrary", "arbitrary")),
        name="diff_attn",
    )(dqa, dka, dvb, lam_vecs, subln_g)


def _prefix_sum_lanes(x):
    n = x.shape[-1]
    lane = lax.broadcasted_iota(jnp.int32, x.shape, x.ndim - 1)
    sh = 1
    while sh < n:
        x = x + jnp.where(lane >= sh, pltpu.roll(x, sh, x.ndim - 1), 0.0)
        sh *= 2
    return x


def _fox_sample_kernel(q_ref, k_ref, v_ref, ck_ref, cv_ref, clf_ref, o_ref, *, t, past):
    r = lax.broadcasted_iota(jnp.int32, (t, t), 0)
    c = lax.broadcasted_iota(jnp.int32, (t, t), 1)
    for hh in range(2):
        data = slice(0, HEAD_DIM) if hh == 0 else slice(HEAD_DIM, LANES)
        ext = HEAD_DIM if hh == 0 else 0
        qa = q_ref[0, hh]
        qf = qa.astype(F32)
        cum_q = qf[:, ext:ext + 1] + qf[:, ext + 1:ext + 2] + qf[:, ext + 2:ext + 3]
        cum_c = _prefix_sum_lanes(clf_ref[0, hh])
        suffix = cum_c[:, past - 1:past] - cum_c
        s_c = jnp.dot(qa[:, data], ck_ref[0, 0, hh].astype(BF16), preferred_element_type=F32)
        s_c = s_c + cum_q + suffix * LOG2E
        s_n = lax.dot_general(qa, k_ref[0, hh], _NT, preferred_element_type=F32)
        s_n = jnp.where(c <= r, s_n, NEG_INF)
        m = jnp.maximum(jnp.max(s_c, axis=1, keepdims=True), jnp.max(s_n, axis=1, keepdims=True))
        p_c = jnp.exp2(s_c - m)
        p_n = jnp.exp2(s_n - m)
        denom = jnp.sum(p_c, axis=1, keepdims=True) + jnp.sum(p_n, axis=1, keepdims=True)
        o = lax.dot_general(p_c.astype(BF16), cv_ref[0, 0, hh].astype(BF16), _NT, preferred_element_type=F32)
        o = o + jnp.dot(p_n.astype(BF16), v_ref[0, hh][:, data], preferred_element_type=F32)
        o_ref[0, :, data] = (o / denom).astype(BF16)


def _fox_sample(fqa, fka, fva, cache_k, cache_v, cache_logf):
    b, nh, t, _ = fqa.shape
    past = cache_k.shape[3]
    clf = cache_logf.reshape(b, nh, 1, past)
    cache_k, cache_v = jnp.swapaxes(cache_k, 3, 4), jnp.swapaxes(cache_v, 3, 4)
    new = pl.BlockSpec((1, 2, t, LANES), lambda i, j: (i, j, 0, 0))
    cache = pl.BlockSpec((1, 1, 2, HEAD_DIM, past), lambda i, j: (0, i, j, 0, 0))
    return pl.pallas_call(
        functools.partial(_fox_sample_kernel, t=t, past=past),
        grid=(b, nh // 2),
        in_specs=[new, new, new, cache, cache,
                  pl.BlockSpec((1, 2, 1, past), lambda i, j: (i, j, 0, 0))],
        out_specs=pl.BlockSpec((1, t, LANES), lambda i, j: (i, 0, j)),
        out_shape=jax.ShapeDtypeStruct((b, t, FOX_WIDTH), BF16),
        compiler_params=_params(("arbitrary", "arbitrary")),
        name="fox_sample",
    )(fqa, fka, fva, cache_k, cache_v, clf)


def _diff_sample_kernel(q_ref, k_ref, v_ref, ck_ref, cv_ref, lam_ref, g_ref, o_ref, *, t, past, lam_init):
    hd = pl.program_id(1)
    slope = _alibi_slope(hd)
    lane = lax.broadcasted_iota(jnp.int32, (t, LANES), 1)
    pos_q = past + lax.broadcasted_iota(jnp.int32, (t, past), 0)
    pos_c = lax.broadcasted_iota(jnp.int32, (t, past), 1)
    bias_c = (-LOG2E * slope) * jnp.abs(pos_q - pos_c).astype(F32)
    vis_c = (pos_c // CHUNK) <= (pos_q // CHUNK)
    pq_n = past + lax.broadcasted_iota(jnp.int32, (t, t), 0)
    pk_n = past + lax.broadcasted_iota(jnp.int32, (t, t), 1)
    bias_n = (-LOG2E * slope) * jnp.abs(pq_n - pk_n).astype(F32)
    vis_n = (pk_n // CHUNK) <= (pq_n // CHUNK)
    ck = ck_ref[0, 0, 0].astype(BF16)
    cv = cv_ref[0, 0, 0].astype(BF16)
    v = v_ref[0, 0]
    outs = []
    for c in range(2):
        half = (lane < HEAD_DIM) if c == 0 else (lane >= HEAD_DIM)
        qz = jnp.where(half, q_ref[0, c], jnp.zeros((), BF16))
        kz = jnp.where(half, k_ref[0, c], jnp.zeros((), BF16))
        s_c = jnp.dot(qz, ck, preferred_element_type=F32)
        s_c = jnp.where(vis_c, s_c + bias_c, NEG_INF)
        s_n = lax.dot_general(qz, kz, _NT, preferred_element_type=F32)
        s_n = jnp.where(vis_n, s_n + bias_n, NEG_INF)
        m = jnp.maximum(jnp.max(s_c, axis=1, keepdims=True), jnp.max(s_n, axis=1, keepdims=True))
        p_c = jnp.exp2(s_c - m)
        p_n = jnp.exp2(s_n - m)
        denom = jnp.sum(p_c, axis=1, keepdims=True) + jnp.sum(p_n, axis=1, keepdims=True)
        o = jnp.dot(p_c.astype(BF16), cv, preferred_element_type=F32)
        o = o + jnp.dot(p_n.astype(BF16), v, preferred_element_type=F32)
        outs.append(o / denom)
    lam = _lambda(lam_ref, lam_init)
    o_ref[0] = (_diff_normed(outs[0], outs[1], lam, 1) * g_ref[...] * (1.0 - lam_init)).astype(BF16)


def _diff_sample(dqa, dka, dvb, cache_k, cache_v, lam_vecs, subln_g, lam_init):
    b, nm, t, _ = dqa.shape
    past = cache_k.shape[3]
    ck = jnp.transpose(cache_k, (0, 1, 2, 4, 5, 3)).reshape(cache_k.shape[:3] + (DIFF_VDIM, past))
    new = pl.BlockSpec((1, 2, t, LANES), lambda i, j: (i, j, 0, 0))
    cache = pl.BlockSpec((1, 1, 1, past, DIFF_VDIM), lambda i, j: (0, i, j, 0, 0))
    cache_t = pl.BlockSpec((1, 1, 1, DIFF_VDIM, past), lambda i, j: (0, i, j, 0, 0))
    return pl.pallas_call(
        functools.partial(_diff_sample_kernel, t=t, past=past, lam_init=lam_init),
        grid=(b, nm // 2),
        in_specs=[new, new, pl.BlockSpec((1, 1, t, DIFF_VDIM), lambda i, j: (i, j, 0, 0)),
                  cache_t, cache,
                  pl.BlockSpec(lam_vecs.shape, lambda i, j: (0, 0)),
                  pl.BlockSpec(subln_g.shape, lambda i, j: (0, 0))],
        out_specs=pl.BlockSpec((1, t, DIFF_VDIM), lambda i, j: (i, 0, j)),
        out_shape=jax.ShapeDtypeStruct((b, t, DIFF_WIDTH), BF16),
        compiler_params=_params(("arbitrary", "arbitrary")),
        name="diff_sample",
    )(dqa, dka, dvb, ck, cache_v, lam_vecs, subln_g)


def _out_proj_kernel(fo_ref, do_ref, fg_ref, dg_ref, x_ref, mod_ref, w_ref, g_ref, b_ref, y_ref, *, alpha):
    def gated(o_ref, gate_ref):
        g = gate_ref[0].astype(F32)
        return (o_ref[0].astype(F32) * (g * jax.nn.sigmoid(g))).astype(BF16)

    branch = jnp.dot(gated(fo_ref, fg_ref), w_ref[0:FOX_WIDTH, :], preferred_element_type=F32)
    branch = branch + jnp.dot(gated(do_ref, dg_ref), w_ref[FOX_WIDTH:, :], preferred_element_type=F32)
    r = alpha * x_ref[0] + mod_ref[0, 2:3, :] * branch
    mu = jnp.mean(r, axis=-1, keepdims=True)
    rc = r - mu
    var = jnp.mean(rc * rc, axis=-1, keepdims=True)
    y_ref[0] = rc * lax.rsqrt(var + LN_EPS) * g_ref[...] + b_ref[...]


def _out_proj(fox_o, diff_o, fg, dg, x, mod, w_out, ln_g, ln_b, alpha):
    b, t, d = x.shape
    tm = min(t, 512)
    rows = pl.BlockSpec((1, tm, FOX_WIDTH), lambda i, j: (i, j, 0))
    wide = pl.BlockSpec((1, tm, d), lambda i, j: (i, j, 0))

    def full(a):
        return pl.BlockSpec(a.shape, lambda i, j: (0,) * a.ndim)

    return pl.pallas_call(
        functools.partial(_out_proj_kernel, alpha=alpha),
        grid=(b, t // tm),
        in_specs=[rows, rows, rows, rows, wide,
                  pl.BlockSpec((1, 3, d), lambda i, j: (i, 0, 0)),
                  full(w_out), full(ln_g), full(ln_b)],
        out_specs=wide,
        out_shape=jax.ShapeDtypeStruct((b, t, d), F32),
        compiler_params=_params(("arbitrary", "arbitrary")),
        name="out_proj",
    )(fox_o, diff_o, fg, dg, x, mod, w_out, ln_g, ln_b)


def _pack_w_in(w_in):
    fw = FOX_WIDTH
    o = 0
    fq = w_in[:, o:o + fw]; o += fw
    fk = w_in[:, o:o + fw]; o += fw
    fv = w_in[:, o:o + fw]; o += fw
    ff = w_in[:, o:o + FOX_HEADS]; o += FOX_HEADS
    fg = w_in[:, o:o + fw]; o += fw
    dq = w_in[:, o:o + fw]; o += fw
    dk = w_in[:, o:o + fw]; o += fw
    dv = w_in[:, o:o + fw]; o += fw
    dg = w_in[:, o:o + fw]
    qs = HEAD_DIM ** -0.5 * LOG2E
    return jnp.concatenate([fq * qs, fk, fv, fg, dq * qs, dk, dv, dg,
                            jnp.tile(ff, (1, LANES // FOX_HEADS))], axis=1).astype(BF16)


def kernel(x_prompt, x_sample, cache_fox_k, cache_fox_v, cache_fox_logf, cache_diff_k, cache_diff_v,
           c_prompt, c_sample, w_ada, b_ada, w_in, b_f, lambda_q1, lambda_k1, lambda_q2, lambda_k2,
           subln_g, w_out, ln_g, ln_b):
    depth = w_in.shape[0]
    assert depth == 1, "single-layer step"
    layer = 0
    lam_init = 0.8 - 0.6 * math.exp(-0.3 * layer)
    alpha = (2 * depth) ** 0.25
    bp, bs = c_prompt.shape[0], c_sample.shape[0]
    d = x_prompt.shape[-1]
    past = cache_fox_k.shape[3]

    c_all = jnp.concatenate([c_prompt, c_sample, jnp.zeros((-(bp + bs) % 8, d), F32)], axis=0)
    mod = _ada(c_all, w_ada[layer], b_ada[layer]).reshape(c_all.shape[0], 3, d)
    mod_p, mod_s = mod[:bp], mod[bp:bp + bs]

    w = _pack_w_in(w_in[layer])
    bf16x = jnp.tile(b_f[layer], LANES // FOX_HEADS).reshape(1, LANES)
    consts = tuple(jnp.asarray(a, BF16 if a.shape[0] > 1 else F32) for a in _fox_placement())
    lam_vecs = jnp.stack([lambda_q1[layer], lambda_k1[layer], lambda_q2[layer], lambda_k2[layer]])
    g = subln_g[layer].reshape(1, DIFF_VDIM)
    w_o = w_out[layer].astype(BF16)
    lg, lb = ln_g[layer].reshape(1, d), ln_b[layer].reshape(1, d)

    def run(x, mod_x, pos_offset, attend):
        (fk, fv, dk, dv, logf, fqa, fka, fva, fg, dqa, dka, dvb, dg) = _in_proj(
            x, mod_x, w, bf16x, consts, pos_offset)
        fox_o, diff_o = attend(fqa, fka, fva, dqa, dka, dvb)
        y = _out_proj(fox_o, diff_o, fg, dg, x, mod_x, w_o, lg, lb, alpha)
        b, t = x.shape[:2]
        states = (fk[None], fv[None], jnp.swapaxes(logf[:, :, :FOX_HEADS], 1, 2)[None],
                  dk.reshape(1, b, DIFF_HEADS, t, 2, HEAD_DIM), dv[None])
        return y, states

    def prompt_attend(fqa, fka, fva, dqa, dka, dvb):
        return _fox_attention(fqa, fka, fva), _diff_attention(dqa, dka, dvb, lam_vecs, g, lam_init)

    def sample_attend(fqa, fka, fva, dqa, dka, dvb):
        return (_fox_sample(fqa, fka, fva, cache_fox_k, cache_fox_v, cache_fox_logf[layer]),
                _diff_sample(dqa, dka, dvb, cache_diff_k, cache_diff_v, lam_vecs, g, lam_init))

    yp, sp = run(x_prompt, mod_p, 0, prompt_attend)
    ys, ss = run(x_sample, mod_s, past, sample_attend)
    return (yp, ys) + sp + ss
```

```python
import functools
import math

import jax
import jax.numpy as jnp
import numpy as np
from jax import lax
from jax.experimental import pallas as pl
from jax.experimental.pallas import tpu as pltpu

F32 = jnp.float32
BF16 = jnp.bfloat16

HEAD_DIM = 64
FOX_HEADS = 8
DIFF_HEADS = 4
DIFF_VDIM = 2 * HEAD_DIM
FOX_WIDTH = FOX_HEADS * HEAD_DIM
DIFF_WIDTH = DIFF_HEADS * DIFF_VDIM
CHUNK = 64
LN_EPS = 1e-5
RMS_EPS = 1e-5
LANES = 128
NEG_INF = float("-inf")
LOG2E = math.log2(math.e)

_SEC = {name: i * FOX_WIDTH for i, name in enumerate(("fq", "fk", "fv", "fg", "dq", "dk", "dv", "dg"))}
_FF_OFF = 8 * FOX_WIDTH
_W_COLS = _FF_OFF + LANES

_VMEM_LIMIT = 56 * 1024 * 1024


def _params(sem):
    return pltpu.CompilerParams(dimension_semantics=sem, vmem_limit_bytes=_VMEM_LIMIT)


def _ada_kernel(c_ref, w_ref, b_ref, o_ref):
    c = c_ref[...]
    s = c * jax.nn.sigmoid(c)
    o_ref[...] = jnp.dot(s, w_ref[...], preferred_element_type=F32) + b_ref[...]


def _ada(c_all, w_ada, b_ada):
    rows, d = c_all.shape
    n = w_ada.shape[1]
    tn = 512
    return pl.pallas_call(
        _ada_kernel,
        grid=(n // tn,),
        in_specs=[pl.BlockSpec((rows, d), lambda j: (0, 0)),
                  pl.BlockSpec((d, tn), lambda j: (0, j)),
                  pl.BlockSpec((1, tn), lambda j: (0, j))],
        out_specs=pl.BlockSpec((rows, tn), lambda j: (0, j)),
        out_shape=jax.ShapeDtypeStruct((rows, n), F32),
        compiler_params=_params(("arbitrary",)),
        name="ada",
    )(c_all, w_ada, b_ada.reshape(1, n))


def _split3(x):
    hi = x.astype(BF16)
    r1 = x - hi.astype(F32)
    mid = r1.astype(BF16)
    lo = (r1 - mid.astype(F32)).astype(BF16)
    return hi, mid, lo


def _fox_placement():
    pq = np.zeros((LANES, FOX_WIDTH), np.float32)
    pk = np.zeros((LANES, FOX_WIDTH), np.float32)
    oq = np.zeros((1, FOX_WIDTH), np.float32)
    ok = np.zeros((1, FOX_WIDTH), np.float32)
    for h in range(FOX_HEADS):
        base = (h // 2) * LANES + (HEAD_DIM if h % 2 == 0 else 0)
        for p in range(3):
            pq[8 * p + h, base + p] = 1.0
            ok[0, base + p] = 1.0
            oq[0, base + 3 + p] = 1.0
            pk[8 * p + h, base + 3 + p] = -1.0
    return pq, pk, oq, ok


def _in_proj_kernel(x_ref, mod_ref, w_ref, bf_ref, pq_ref, pk_ref, oq_ref, ok_ref,
                    fk_ref, fv_ref, dk_ref, dv_ref, logf_ref,
                    fqa_ref, fka_ref, fva_ref, fg_ref, dqa_ref, dka_ref, dvb_ref, dg_ref,
                    carry_ref, *, tm, pos_offset):
    ti = pl.program_id(1)
    x = x_ref[0]
    shift = mod_ref[0, 0:1, :]
    scale = mod_ref[0, 1:2, :]
    h = (x * (1.0 + scale) + shift).astype(BF16)

    def proj(name, width=FOX_WIDTH):
        off = _SEC[name] if name in _SEC else _FF_OFF
        return jnp.dot(h, w_ref[:, off:off + width], preferred_element_type=F32)

    lane = lax.broadcasted_iota(jnp.int32, (tm, LANES), 1)
    row = lax.broadcasted_iota(jnp.int32, (tm, LANES), 0)
    lower = lane < HEAD_DIM
    grp = (lane // 8) % 4

    def by_group(a, b, c):
        return jnp.where(grp == 0, a, jnp.where(grp == 1, b, jnp.where(grp == 2, c, jnp.zeros_like(a))))

    logf = jax.nn.log_sigmoid(proj("ff", LANES) + bf_ref[...])
    logf_ref[0] = logf
    tri_r = lax.broadcasted_iota(jnp.int32, (tm, tm), 0)
    tri_c = lax.broadcasted_iota(jnp.int32, (tm, tm), 1)
    tri = (tri_c <= tri_r).astype(BF16)
    part = jnp.dot(tri, by_group(*_split3(logf)), preferred_element_type=F32)
    local = part + pltpu.roll(part, 8, 1) + pltpu.roll(part, 16, 1) + pltpu.roll(part, 24, 1)

    @pl.when(ti == 0)
    def _():
        carry_ref[...] = jnp.zeros_like(carry_ref)

    cum = local + carry_ref[...]
    carry_ref[...] = cum[tm - 1:tm, :]
    cum_cat = by_group(*_split3(cum * LOG2E))
    eq = jnp.dot(cum_cat, pq_ref[...], preferred_element_type=F32) + oq_ref[...]
    ek = jnp.dot(cum_cat, pk_ref[...], preferred_element_type=F32) + ok_ref[...]

    zq, zk, zv = proj("fq"), proj("fk"), proj("fv")
    e_even = (lane == HEAD_DIM).astype(F32)
    for j in range(FOX_HEADS // 2):
        sl = slice(j * LANES, (j + 1) * LANES)
        fqa_ref[0, 2 * j] = jnp.where(lower, zq[:, sl], eq[:, sl]).astype(BF16)
        fqa_ref[0, 2 * j + 1] = jnp.where(lower, eq[:, sl], zq[:, sl]).astype(BF16)
        fka_ref[0, 2 * j] = jnp.where(lower, zk[:, sl], ek[:, sl]).astype(BF16)
        fka_ref[0, 2 * j + 1] = jnp.where(lower, ek[:, sl], zk[:, sl]).astype(BF16)
        fva_ref[0, 2 * j] = jnp.where(lower, zv[:, sl], e_even).astype(BF16)
        fva_ref[0, 2 * j + 1] = jnp.where(lower, pltpu.roll(zv[:, sl], HEAD_DIM, 1), e_even).astype(BF16)
    for hd in range(FOX_HEADS):
        fk_ref[0, hd] = zk[:, hd * HEAD_DIM:(hd + 1) * HEAD_DIM]
        fv_ref[0, hd] = zv[:, hd * HEAD_DIM:(hd + 1) * HEAD_DIM]
    fg_ref[0] = proj("fg").astype(BF16)

    zq, zk, zv = proj("dq"), proj("dk"), proj("dv")
    pos = (row + (ti * tm + pos_offset)).astype(F32)
    sub = lane % HEAD_DIM
    for hd in range(DIFF_HEADS):
        slope = 2.0 ** (-8.0 * (hd + 1) / DIFF_HEADS)
        sl = slice(hd * LANES, (hd + 1) * LANES)
        p0, p1, p2 = (a.astype(F32) for a in _split3((slope * LOG2E) * pos))
        pieces = jnp.where(sub % 3 == 0, p0, jnp.where(sub % 3 == 1, p1, p2))
        ext_q = jnp.where(sub < 3, -pieces, (sub < 6).astype(F32))
        ext_k = jnp.where(sub < 3, 1.0, jnp.where(sub < 6, pieces, 0.0))
        dqa_ref[0, 2 * hd] = jnp.where(lower, zq[:, sl], ext_q).astype(BF16)
        dqa_ref[0, 2 * hd + 1] = jnp.where(lower, ext_q, zq[:, sl]).astype(BF16)
        dka_ref[0, 2 * hd] = jnp.where(lower, zk[:, sl], ext_k).astype(BF16)
        dka_ref[0, 2 * hd + 1] = jnp.where(lower, ext_k, zk[:, sl]).astype(BF16)
        dk_ref[0, hd] = zk[:, sl]
        dv_ref[0, hd] = zv[:, sl]
        dvb_ref[0, hd] = zv[:, sl].astype(BF16)
    dg_ref[0] = proj("dg").astype(BF16)


def _in_proj(x, mod, w, bf16x, consts, pos_offset):
    b, t, d = x.shape
    tm = min(t, 512)
    nt = t // tm
    pq, pk, oq, ok = consts

    def full(a):
        return pl.BlockSpec(a.shape, lambda i, j: (0,) * a.ndim)

    def heads(n, width):
        return pl.BlockSpec((1, n, tm, width), lambda i, j: (i, 0, j, 0))

    rows = pl.BlockSpec((1, tm, FOX_WIDTH), lambda i, j: (i, j, 0))
    out_shape = (
        jax.ShapeDtypeStruct((b, FOX_HEADS, t, HEAD_DIM), F32),
        jax.ShapeDtypeStruct((b, FOX_HEADS, t, HEAD_DIM), F32),
        jax.ShapeDtypeStruct((b, DIFF_HEADS, t, DIFF_VDIM), F32),
        jax.ShapeDtypeStruct((b, DIFF_HEADS, t, DIFF_VDIM), F32),
        jax.ShapeDtypeStruct((b, t, LANES), F32),
        jax.ShapeDtypeStruct((b, FOX_HEADS, t, LANES), BF16),
        jax.ShapeDtypeStruct((b, FOX_HEADS, t, LANES), BF16),
        jax.ShapeDtypeStruct((b, FOX_HEADS, t, LANES), BF16),
        jax.ShapeDtypeStruct((b, t, FOX_WIDTH), BF16),
        jax.ShapeDtypeStruct((b, 2 * DIFF_HEADS, t, LANES), BF16),
        jax.ShapeDtypeStruct((b, 2 * DIFF_HEADS, t, LANES), BF16),
        jax.ShapeDtypeStruct((b, DIFF_HEADS, t, DIFF_VDIM), BF16),
        jax.ShapeDtypeStruct((b, t, DIFF_WIDTH), BF16),
    )
    out_specs = (
        heads(FOX_HEADS, HEAD_DIM), heads(FOX_HEADS, HEAD_DIM),
        heads(DIFF_HEADS, DIFF_VDIM), heads(DIFF_HEADS, DIFF_VDIM),
        pl.BlockSpec((1, tm, LANES), lambda i, j: (i, j, 0)),
        heads(FOX_HEADS, LANES), heads(FOX_HEADS, LANES), heads(FOX_HEADS, LANES), rows,
        heads(2 * DIFF_HEADS, LANES), heads(2 * DIFF_HEADS, LANES), heads(DIFF_HEADS, DIFF_VDIM), rows,
    )
    return pl.pallas_call(
        functools.partial(_in_proj_kernel, tm=tm, pos_offset=pos_offset),
        grid=(b, nt),
        in_specs=[pl.BlockSpec((1, tm, d), lambda i, j: (i, j, 0)),
                  pl.BlockSpec((1, 3, d), lambda i, j: (i, 0, 0)),
                  full(w), full(bf16x), full(pq), full(pk), full(oq), full(ok)],
        out_specs=out_specs,
        out_shape=out_shape,
        scratch_shapes=[pltpu.VMEM((1, LANES), F32)],
        compiler_params=_params(("arbitrary", "arbitrary")),
        name="in_proj",
    )(x, mod, w, bf16x, pq, pk, oq, ok)


_NT = (((1,), (1,)), ((), ()))
_TN = (((0,), (0,)), ((), ()))
_STAGES = 4
_FOX_V_ROWS = HEAD_DIM + 16


def _pipelined_tiles(n_full, scores, absorb, sa_ref, sb_ref):
    chains = range(2)
    bufs = (sa_ref, sb_ref)
    for c in chains:
        scores(0, sa_ref, c)

    def stages(first, count, ends_on_diagonal):
        for s in range(count):
            last = s == count - 1
            for c in chains:
                if not (last and ends_on_diagonal):
                    scores(first + s + 1, bufs[(s + 1) % 2], c)
                absorb(bufs[s % 2], first + s, c, last and ends_on_diagonal)

    def body(i, carry):
        stages(_STAGES * i, _STAGES, False)
        return carry

    lax.fori_loop(0, n_full // _STAGES, body, 0)
    for rem in range(_STAGES):
        @pl.when(n_full % _STAGES == rem)
        def _(rem=rem):
            stages(n_full - rem, rem + 1, True)


def _fox_kernel(q_ref, k_ref, v_ref, o_ref, sa_ref, sb_ref, m_ref, acc_ref, *, tq):
    qi = pl.program_id(2)
    m_ref[...] = jnp.full_like(m_ref, NEG_INF)
    acc_ref[...] = jnp.zeros_like(acc_ref)

    def scores(ki, dst, hh):
        k = k_ref[0, hh, pl.ds(pl.multiple_of(ki * tq, tq), tq), :]
        dst[hh] = lax.dot_general(k, q_ref[0, hh], _NT, preferred_element_type=F32)

    def absorb(src, ki, hh, diagonal):
        s = src[hh]
        if diagonal:
            key = lax.broadcasted_iota(jnp.int32, (tq, tq), 0)
            qry = lax.broadcasted_iota(jnp.int32, (tq, tq), 1)
            s = jnp.where(key <= qry, s, NEG_INF)
        m_prev = m_ref[hh]
        m_new = jnp.maximum(m_prev, jnp.max(s, axis=0, keepdims=True))
        p = jnp.exp2(s - m_new)
        alpha = jnp.exp2(m_prev - m_new)
        v = v_ref[0, hh, pl.ds(pl.multiple_of(ki * tq, tq), tq), 0:_FOX_V_ROWS]
        pv = lax.dot_general(v, p.astype(BF16), _TN, preferred_element_type=F32)
        acc_ref[hh] = alpha * acc_ref[hh] + pv
        m_ref[hh] = m_new

    _pipelined_tiles(qi, scores, absorb, sa_ref, sb_ref)
    outs = [acc_ref[hh, 0:HEAD_DIM] / acc_ref[hh, HEAD_DIM:HEAD_DIM + 1] for hh in range(2)]
    o_ref[0] = jnp.concatenate(outs, axis=0).T.astype(BF16)


def _fox_attention(fqa, fka, fva):
    b, nh, t, _ = fqa.shape
    tq = min(t, 512)
    return pl.pallas_call(
        functools.partial(_fox_kernel, tq=tq),
        grid=(b, nh // 2, t // tq),
        in_specs=[pl.BlockSpec((1, 2, tq, LANES), lambda i, j, q: (i, j, q, 0)),
                  pl.BlockSpec((1, 2, t, LANES), lambda i, j, q: (i, j, 0, 0)),
                  pl.BlockSpec((1, 2, t, LANES), lambda i, j, q: (i, j, 0, 0))],
        out_specs=pl.BlockSpec((1, tq, LANES), lambda i, j, q: (i, q, j)),
        out_shape=jax.ShapeDtypeStruct((b, t, FOX_WIDTH), BF16),
        scratch_shapes=[pltpu.VMEM((2, tq, tq), F32), pltpu.VMEM((2, tq, tq), F32),
                        pltpu.VMEM((2, 1, tq), F32), pltpu.VMEM((2, _FOX_V_ROWS, tq), F32)],
        compiler_params=_params(("arbitrary", "arbitrary", "arbitrary")),
        name="fox_attn",
    )(fqa, fka, fva)


def _alibi_slope(head):
    return jnp.exp2(jnp.full((1, 1), -8.0 / DIFF_HEADS, F32) * (head + 1).astype(F32))


def _lambda(lam_ref, lam_init):
    lv = lam_ref[...]
    e1 = jnp.exp(jnp.sum(lv[0:1] * lv[1:2], axis=1, keepdims=True))
    e2 = jnp.exp(jnp.sum(lv[2:3] * lv[3:4], axis=1, keepdims=True))
    return e1 - e2 + lam_init


def _diff_normed(o0, o1, lam, axis):
    df = o0 - lam * o1
    return df * lax.rsqrt(jnp.mean(df * df, axis=axis, keepdims=True) + RMS_EPS)


_ONES_ROWS = 16


def _diff_kernel(q_ref, k_ref, v_ref, lam_ref, g_ref, o_ref, sa_ref, sb_ref, m_ref, acc_ref, *, tq, lam_init):
    hd = pl.program_id(1)
    qi = pl.program_id(2)
    slope = _alibi_slope(hd)
    m_ref[...] = jnp.full_like(m_ref, NEG_INF)
    acc_ref[...] = jnp.zeros_like(acc_ref)

    def scores(ki, dst, c):
        k = k_ref[0, c, pl.ds(pl.multiple_of(ki * tq, tq), tq), :]
        dst[c] = lax.dot_general(k, q_ref[0, c], _NT, preferred_element_type=F32)

    def absorb(src, ki, c, diagonal):
        s = src[c]
        if diagonal:
            key = lax.broadcasted_iota(jnp.int32, (tq, tq), 0)
            qry = lax.broadcasted_iota(jnp.int32, (tq, tq), 1)
            ahead = jnp.maximum(key - qry, 0).astype(F32)
            s = jnp.where((key // CHUNK) <= (qry // CHUNK), s - (2.0 * LOG2E * slope) * ahead, NEG_INF)
        m_prev = m_ref[c]
        m_new = jnp.maximum(m_prev, jnp.max(s, axis=0, keepdims=True))
        p = jnp.exp2(s - m_new)
        alpha = jnp.exp2(m_prev - m_new)
        v = v_ref[0, 0, pl.ds(pl.multiple_of(ki * tq, tq), tq), :]
        vt = jnp.concatenate([v.T, jnp.ones((_ONES_ROWS, tq), BF16)], axis=0)
        acc_ref[c] = alpha * acc_ref[c] + jnp.dot(vt, p.astype(BF16), preferred_element_type=F32)
        m_ref[c] = m_new

    _pipelined_tiles(qi, scores, absorb, sa_ref, sb_ref)
    lam = _lambda(lam_ref, lam_init)
    acc0, acc1 = acc_ref[0], acc_ref[1]
    o0 = acc0[:DIFF_VDIM] / acc0[DIFF_VDIM:DIFF_VDIM + 1]
    o1 = acc1[:DIFF_VDIM] / acc1[DIFF_VDIM:DIFF_VDIM + 1]
    o_ref[0] = (_diff_normed(o0, o1, lam, 0).T * g_ref[...] * (1.0 - lam_init)).astype(BF16)


def _diff_attention(dqa, dka, dvb, lam_vecs, subln_g, lam_init):
    b, nm, t, _ = dqa.shape
    tq = min(t, 512)
    return pl.pallas_call(
        functools.partial(_diff_kernel, tq=tq, lam_init=lam_init),
        grid=(b, nm // 2, t // tq),
        in_specs=[pl.BlockSpec((1, 2, tq, LANES), lambda i, j, q: (i, j, q, 0)),
                  pl.BlockSpec((1, 2, t, LANES), lambda i, j, q: (i, j, 0, 0)),
                  pl.BlockSpec((1, 1, t, DIFF_VDIM), lambda i, j, q: (i, j, 0, 0)),
                  pl.BlockSpec(lam_vecs.shape, lambda i, j, q: (0, 0)),
                  pl.BlockSpec(subln_g.shape, lambda i, j, q: (0, 0))],
        out_specs=pl.BlockSpec((1, tq, DIFF_VDIM), lambda i, j, q: (i, q, j)),
        out_shape=jax.ShapeDtypeStruct((b, t, DIFF_WIDTH), BF16),
        scratch_shapes=[pltpu.VMEM((2, tq, tq), F32), pltpu.VMEM((2, tq, tq), F32),
                        pltpu.VMEM((2, 1, tq), F32), pltpu.VMEM((2, DIFF_VDIM + _ONES_ROWS, tq), F32)],
        compiler_params=_params(("arbitrary", "arbitrary", "arbitrary")),
        name="diff_attn",
    )(dqa, dka, dvb, lam_vecs, subln_g)


def _prefix_sum_lanes(x):
    n = x.shape[-1]
    lane = lax.broadcasted_iota(jnp.int32, x.shape, x.ndim - 1)
    sh = 1
    while sh < n:
        x = x + jnp.where(lane >= sh, pltpu.roll(x, sh, x.ndim - 1), 0.0)
        sh *= 2
    return x


def _fox_sample_kernel(q_ref, k_ref, v_ref, ck_ref, cv_ref, clf_ref, o_ref, *, t, past):
    r = lax.broadcasted_iota(jnp.int32, (t, t), 0)
    c = lax.broadcasted_iota(jnp.int32, (t, t), 1)
    for hh in range(2):
        data = slice(0, HEAD_DIM) if hh == 0 else slice(HEAD_DIM, LANES)
        ext = HEAD_DIM if hh == 0 else 0
        qa = q_ref[0, hh]
        qf = qa.astype(F32)
        cum_q = qf[:, ext:ext + 1] + qf[:, ext + 1:ext + 2] + qf[:, ext + 2:ext + 3]
        cum_c = _prefix_sum_lanes(clf_ref[0, hh])
        suffix = cum_c[:, past - 1:past] - cum_c
        s_c = jnp.dot(qa[:, data], ck_ref[0, 0, hh].astype(BF16), preferred_element_type=F32)
        s_c = s_c + cum_q + suffix * LOG2E
        s_n = lax.dot_general(qa, k_ref[0, hh], _NT, preferred_element_type=F32)
        s_n = jnp.where(c <= r, s_n, NEG_INF)
        m = jnp.maximum(jnp.max(s_c, axis=1, keepdims=True), jnp.max(s_n, axis=1, keepdims=True))
        p_c = jnp.exp2(s_c - m)
        p_n = jnp.exp2(s_n - m)
        denom = jnp.sum(p_c, axis=1, keepdims=True) + jnp.sum(p_n, axis=1, keepdims=True)
        o = lax.dot_general(p_c.astype(BF16), cv_ref[0, 0, hh].astype(BF16), _NT, preferred_element_type=F32)
        o = o + jnp.dot(p_n.astype(BF16), v_ref[0, hh][:, 0:HEAD_DIM], preferred_element_type=F32)
        o_ref[0, :, data] = (o / denom).astype(BF16)


def _fox_sample(fqa, fka, fva, cache_k, cache_v, cache_logf):
    b, nh, t, _ = fqa.shape
    past = cache_k.shape[3]
    clf = cache_logf.reshape(b, nh, 1, past)
    cache_k, cache_v = jnp.swapaxes(cache_k, 3, 4), jnp.swapaxes(cache_v, 3, 4)
    new = pl.BlockSpec((1, 2, t, LANES), lambda i, j: (i, j, 0, 0))
    cache = pl.BlockSpec((1, 1, 2, HEAD_DIM, past), lambda i, j: (0, i, j, 0, 0))
    return pl.pallas_call(
        functools.partial(_fox_sample_kernel, t=t, past=past),
        grid=(b, nh // 2),
        in_specs=[new, new, new, cache, cache,
                  pl.BlockSpec((1, 2, 1, past), lambda i, j: (i, j, 0, 0))],
        out_specs=pl.BlockSpec((1, t, LANES), lambda i, j: (i, 0, j)),
        out_shape=jax.ShapeDtypeStruct((b, t, FOX_WIDTH), BF16),
        compiler_params=_params(("arbitrary", "arbitrary")),
        name="fox_sample",
    )(fqa, fka, fva, cache_k, cache_v, clf)


def _diff_sample_kernel(q_ref, k_ref, v_ref, ck_ref, cv_ref, lam_ref, g_ref, o_ref, *, t, past, lam_init):
    hd = pl.program_id(1)
    slope = _alibi_slope(hd)
    lane = lax.broadcasted_iota(jnp.int32, (t, LANES), 1)
    pos_q = past + lax.broadcasted_iota(jnp.int32, (t, past), 0)
    pos_c = lax.broadcasted_iota(jnp.int32, (t, past), 1)
    bias_c = (-LOG2E * slope) * jnp.abs(pos_q - pos_c).astype(F32)
    vis_c = (pos_c // CHUNK) <= (pos_q // CHUNK)
    pq_n = past + lax.broadcasted_iota(jnp.int32, (t, t), 0)
    pk_n = past + lax.broadcasted_iota(jnp.int32, (t, t), 1)
    bias_n = (-LOG2E * slope) * jnp.abs(pq_n - pk_n).astype(F32)
    vis_n = (pk_n // CHUNK) <= (pq_n // CHUNK)
    ck = ck_ref[0, 0, 0].astype(BF16)
    cv = cv_ref[0, 0, 0].astype(BF16)
    v = v_ref[0, 0]
    outs = []
    for c in range(2):
        half = (lane < HEAD_DIM) if c == 0 else (lane >= HEAD_DIM)
        qz = jnp.where(half, q_ref[0, c], jnp.zeros((), BF16))
        kz = jnp.where(half, k_ref[0, c], jnp.zeros((), BF16))
        s_c = jnp.dot(qz, ck, preferred_element_type=F32)
        s_c = jnp.where(vis_c, s_c + bias_c, NEG_INF)
        s_n = lax.dot_general(qz, kz, _NT, preferred_element_type=F32)
        s_n = jnp.where(vis_n, s_n + bias_n, NEG_INF)
        m = jnp.maximum(jnp.max(s_c, axis=1, keepdims=True), jnp.max(s_n, axis=1, keepdims=True))
        p_c = jnp.exp2(s_c - m)
        p_n = jnp.exp2(s_n - m)
        denom = jnp.sum(p_c, axis=1, keepdims=True) + jnp.sum(p_n, axis=1, keepdims=True)
        o = jnp.dot(p_c.astype(BF16), cv, preferred_element_type=F32)
        o = o + jnp.dot(p_n.astype(BF16), v, preferred_element_type=F32)
        outs.append(o / denom)
    lam = _lambda(lam_ref, lam_init)
    o_ref[0] = (_diff_normed(outs[0], outs[1], lam, 1) * g_ref[...] * (1.0 - lam_init)).astype(BF16)


def _diff_sample(dqa, dka, dvb, cache_k, cache_v, lam_vecs, subln_g, lam_init):
    b, nm, t, _ = dqa.shape
    past = cache_k.shape[3]
    ck = jnp.transpose(cache_k, (0, 1, 2, 4, 5, 3)).reshape(cache_k.shape[:3] + (DIFF_VDIM, past))
    new = pl.BlockSpec((1, 2, t, LANES), lambda i, j: (i, j, 0, 0))
    cache = pl.BlockSpec((1, 1, 1, past, DIFF_VDIM), lambda i, j: (0, i, j, 0, 0))
    cache_t = pl.BlockSpec((1, 1, 1, DIFF_VDIM, past), lambda i, j: (0, i, j, 0, 0))
    return pl.pallas_call(
        functools.partial(_diff_sample_kernel, t=t, past=past, lam_init=lam_init),
        grid=(b, nm // 2),
        in_specs=[new, new, pl.BlockSpec((1, 1, t, DIFF_VDIM), lambda i, j: (i, j, 0, 0)),
                  cache_t, cache,
                  pl.BlockSpec(lam_vecs.shape, lambda i, j: (0, 0)),
                  pl.BlockSpec(subln_g.shape, lambda i, j: (0, 0))],
        out_specs=pl.BlockSpec((1, t, DIFF_VDIM), lambda i, j: (i, 0, j)),
        out_shape=jax.ShapeDtypeStruct((b, t, DIFF_WIDTH), BF16),
        compiler_params=_params(("arbitrary", "arbitrary")),
        name="diff_sample",
    )(dqa, dka, dvb, ck, cache_v, lam_vecs, subln_g)


def _out_proj_kernel(fo_ref, do_ref, fg_ref, dg_ref, x_ref, mod_ref, w_ref, g_ref, b_ref, y_ref, *, alpha):
    def gated(o_ref, gate_ref):
        g = gate_ref[0].astype(F32)
        return (o_ref[0].astype(F32) * (g * jax.nn.sigmoid(g))).astype(BF16)

    branch = jnp.dot(gated(fo_ref, fg_ref), w_ref[0:FOX_WIDTH, :], preferred_element_type=F32)
    branch = branch + jnp.dot(gated(do_ref, dg_ref), w_ref[FOX_WIDTH:, :], preferred_element_type=F32)
    r = alpha * x_ref[0] + mod_ref[0, 2:3, :] * branch
    mu = jnp.mean(r, axis=-1, keepdims=True)
    rc = r - mu
    var = jnp.mean(rc * rc, axis=-1, keepdims=True)
    y_ref[0] = rc * lax.rsqrt(var + LN_EPS) * g_ref[...] + b_ref[...]


def _out_proj(fox_o, diff_o, fg, dg, x, mod, w_out, ln_g, ln_b, alpha):
    b, t, d = x.shape
    tm = min(t, 512)
    rows = pl.BlockSpec((1, tm, FOX_WIDTH), lambda i, j: (i, j, 0))
    wide = pl.BlockSpec((1, tm, d), lambda i, j: (i, j, 0))

    def full(a):
        return pl.BlockSpec(a.shape, lambda i, j: (0,) * a.ndim)

    return pl.pallas_call(
        functools.partial(_out_proj_kernel, alpha=alpha),
        grid=(b, t // tm),
        in_specs=[rows, rows, rows, rows, wide,
                  pl.BlockSpec((1, 3, d), lambda i, j: (i, 0, 0)),
                  full(w_out), full(ln_g), full(ln_b)],
        out_specs=wide,
        out_shape=jax.ShapeDtypeStruct((b, t, d), F32),
        compiler_params=_params(("arbitrary", "arbitrary")),
        name="out_proj",
    )(fox_o, diff_o, fg, dg, x, mod, w_out, ln_g, ln_b)


def _pack_w_in(w_in):
    fw = FOX_WIDTH
    o = 0
    fq = w_in[:, o:o + fw]; o += fw
    fk = w_in[:, o:o + fw]; o += fw
    fv = w_in[:, o:o + fw]; o += fw
    ff = w_in[:, o:o + FOX_HEADS]; o += FOX_HEADS
    fg = w_in[:, o:o + fw]; o += fw
    dq = w_in[:, o:o + fw]; o += fw
    dk = w_in[:, o:o + fw]; o += fw
    dv = w_in[:, o:o + fw]; o += fw
    dg = w_in[:, o:o + fw]
    qs = HEAD_DIM ** -0.5 * LOG2E
    return jnp.concatenate([fq * qs, fk, fv, fg, dq * qs, dk, dv, dg,
                            jnp.tile(ff, (1, LANES // FOX_HEADS))], axis=1).astype(BF16)


def kernel(x_prompt, x_sample, cache_fox_k, cache_fox_v, cache_fox_logf, cache_diff_k, cache_diff_v,
           c_prompt, c_sample, w_ada, b_ada, w_in, b_f, lambda_q1, lambda_k1, lambda_q2, lambda_k2,
           subln_g, w_out, ln_g, ln_b):
    depth = w_in.shape[0]
    assert depth == 1, "single-layer step"
    layer = 0
    lam_init = 0.8 - 0.6 * math.exp(-0.3 * layer)
    alpha = (2 * depth) ** 0.25
    bp, bs = c_prompt.shape[0], c_sample.shape[0]
    d = x_prompt.shape[-1]
    past = cache_fox_k.shape[3]

    c_all = jnp.concatenate([c_prompt, c_sample, jnp.zeros((-(bp + bs) % 8, d), F32)], axis=0)
    mod = _ada(c_all, w_ada[layer], b_ada[layer]).reshape(c_all.shape[0], 3, d)
    mod_p, mod_s = mod[:bp], mod[bp:bp + bs]

    w = _pack_w_in(w_in[layer])
    bf16x = jnp.tile(b_f[layer], LANES // FOX_HEADS).reshape(1, LANES)
    consts = tuple(jnp.asarray(a, BF16 if a.shape[0] > 1 else F32) for a in _fox_placement())
    lam_vecs = jnp.stack([lambda_q1[layer], lambda_k1[layer], lambda_q2[layer], lambda_k2[layer]])
    g = subln_g[layer].reshape(1, DIFF_VDIM)
    w_o = w_out[layer].astype(BF16)
    lg, lb = ln_g[layer].reshape(1, d), ln_b[layer].reshape(1, d)

    def run(x, mod_x, pos_offset, attend):
        (fk, fv, dk, dv, logf, fqa, fka, fva, fg, dqa, dka, dvb, dg) = _in_proj(
            x, mod_x, w, bf16x, consts, pos_offset)
        fox_o, diff_o = attend(fqa, fka, fva, dqa, dka, dvb)
        y = _out_proj(fox_o, diff_o, fg, dg, x, mod_x, w_o, lg, lb, alpha)
        b, t = x.shape[:2]
        states = (fk[None], fv[None], jnp.swapaxes(logf[:, :, :FOX_HEADS], 1, 2)[None],
                  dk.reshape(1, b, DIFF_HEADS, t, 2, HEAD_DIM), dv[None])
        return y, states

    def prompt_attend(fqa, fka, fva, dqa, dka, dvb):
        return _fox_attention(fqa, fka, fva), _diff_attention(dqa, dka, dvb, lam_vecs, g, lam_init)

    def sample_attend(fqa, fka, fva, dqa, dka, dvb):
        return (_fox_sample(fqa, fka, fva, cache_fox_k, cache_fox_v, cache_fox_logf[layer]),
                _diff_sample(dqa, dka, dvb, cache_diff_k, cache_diff_v, lam_vecs, g, lam_init))

    yp, sp = run(x_prompt, mod_p, 0, prompt_attend)
    ys, ss = run(x_sample, mod_s, past, sample_attend)
    return (yp, ys) + sp + ss
```

```python
import functools
import math

import jax
import jax.numpy as jnp
import numpy as np
from jax import lax
from jax.experimental import pallas as pl
from jax.experimental.pallas import tpu as pltpu

F32 = jnp.float32
BF16 = jnp.bfloat16

HEAD_DIM = 64
FOX_HEADS = 8
DIFF_HEADS = 4
DIFF_VDIM = 2 * HEAD_DIM
FOX_WIDTH = FOX_HEADS * HEAD_DIM
DIFF_WIDTH = DIFF_HEADS * DIFF_VDIM
CHUNK = 64
LN_EPS = 1e-5
RMS_EPS = 1e-5
LANES = 128
NEG_INF = float("-inf")
LOG2E = math.log2(math.e)

_SEC = {name: i * FOX_WIDTH for i, name in enumerate(("fq", "fk", "fv", "fg", "dq", "dk", "dv", "dg"))}
_FF_OFF = 8 * FOX_WIDTH
_W_COLS = _FF_OFF + LANES

_VMEM_LIMIT = 56 * 1024 * 1024


def _params(sem):
    return pltpu.CompilerParams(dimension_semantics=sem, vmem_limit_bytes=_VMEM_LIMIT)


def _ada_kernel(c_ref, w_ref, b_ref, o_ref):
    c = c_ref[...]
    s = c * jax.nn.sigmoid(c)
    o_ref[...] = jnp.dot(s, w_ref[...], preferred_element_type=F32) + b_ref[...]


def _ada(c_all, w_ada, b_ada):
    rows, d = c_all.shape
    n = w_ada.shape[1]
    tn = 512
    return pl.pallas_call(
        _ada_kernel,
        grid=(n // tn,),
        in_specs=[pl.BlockSpec((rows, d), lambda j: (0, 0)),
                  pl.BlockSpec((d, tn), lambda j: (0, j)),
                  pl.BlockSpec((1, tn), lambda j: (0, j))],
        out_specs=pl.BlockSpec((rows, tn), lambda j: (0, j)),
        out_shape=jax.ShapeDtypeStruct((rows, n), F32),
        compiler_params=_params(("arbitrary",)),
        name="ada",
    )(c_all, w_ada, b_ada.reshape(1, n))


def _split3(x):
    hi = x.astype(BF16)
    r1 = x - hi.astype(F32)
    mid = r1.astype(BF16)
    lo = (r1 - mid.astype(F32)).astype(BF16)
    return hi, mid, lo


def _fox_placement():
    pq = np.zeros((LANES, FOX_WIDTH), np.float32)
    pk = np.zeros((LANES, FOX_WIDTH), np.float32)
    oq = np.zeros((1, FOX_WIDTH), np.float32)
    ok = np.zeros((1, FOX_WIDTH), np.float32)
    for h in range(FOX_HEADS):
        base = (h // 2) * LANES + (HEAD_DIM if h % 2 == 0 else 0)
        for p in range(3):
            pq[8 * p + h, base + p] = 1.0
            ok[0, base + p] = 1.0
            oq[0, base + 3 + p] = 1.0
            pk[8 * p + h, base + 3 + p] = -1.0
    return pq, pk, oq, ok


def _in_proj_kernel(x_ref, mod_ref, w_ref, bf_ref, pq_ref, pk_ref, oq_ref, ok_ref, seg_ref,
                    fk_ref, fv_ref, dk_ref, dv_ref, logf_ref,
                    fqa_ref, fka_ref, fva_ref, fg_ref, dqa_ref, dka_ref, dvb_ref, dg_ref, stat_ref,
                    carry_ref, *, tm, pos_offset):
    ti = pl.program_id(1)
    x = x_ref[0]
    shift = mod_ref[0, 0:1, :]
    scale = mod_ref[0, 1:2, :]
    h = (x * (1.0 + scale) + shift).astype(BF16)

    def proj(name, width=FOX_WIDTH):
        off = _SEC[name] if name in _SEC else _FF_OFF
        return jnp.dot(h, w_ref[:, off:off + width], preferred_element_type=F32)

    lane = lax.broadcasted_iota(jnp.int32, (tm, LANES), 1)
    row = lax.broadcasted_iota(jnp.int32, (tm, LANES), 0)
    lower = lane < HEAD_DIM
    grp = (lane // 8) % 4

    def by_group(a, b, c):
        return jnp.where(grp == 0, a, jnp.where(grp == 1, b, jnp.where(grp == 2, c, jnp.zeros_like(a))))

    def max_sq_norm(z):
        zr = z.astype(BF16).astype(F32)
        sq = jnp.dot((zr * zr).astype(BF16), seg_ref[...], preferred_element_type=F32)
        return jnp.max(sq, axis=0, keepdims=True)

    logf = jax.nn.log_sigmoid(proj("ff", LANES) + bf_ref[...])
    logf_ref[0] = logf
    tri_r = lax.broadcasted_iota(jnp.int32, (tm, tm), 0)
    tri_c = lax.broadcasted_iota(jnp.int32, (tm, tm), 1)
    tri = (tri_c <= tri_r).astype(BF16)
    part = jnp.dot(tri, by_group(*_split3(logf)), preferred_element_type=F32)
    local = part + pltpu.roll(part, 8, 1) + pltpu.roll(part, 16, 1) + pltpu.roll(part, 24, 1)

    @pl.when(ti == 0)
    def _():
        carry_ref[...] = jnp.zeros_like(carry_ref)

    cum = local + carry_ref[...]
    carry_ref[...] = cum[tm - 1:tm, :]
    cum2 = cum * LOG2E
    cum_cat = by_group(*_split3(cum2))
    eq = jnp.dot(cum_cat, pq_ref[...], preferred_element_type=F32) + oq_ref[...]
    ek = jnp.dot(cum_cat, pk_ref[...], preferred_element_type=F32) + ok_ref[...]

    zq, zk, zv = proj("fq"), proj("fk"), proj("fv")
    fox_norms = [max_sq_norm(zq), max_sq_norm(zk)]
    e_even = (lane == HEAD_DIM).astype(F32)
    for j in range(FOX_HEADS // 2):
        sl = slice(j * LANES, (j + 1) * LANES)
        fqa_ref[0, 2 * j] = jnp.where(lower, zq[:, sl], eq[:, sl]).astype(BF16)
        fqa_ref[0, 2 * j + 1] = jnp.where(lower, eq[:, sl], zq[:, sl]).astype(BF16)
        fka_ref[0, 2 * j] = jnp.where(lower, zk[:, sl], ek[:, sl]).astype(BF16)
        fka_ref[0, 2 * j + 1] = jnp.where(lower, ek[:, sl], zk[:, sl]).astype(BF16)
        fva_ref[0, 2 * j] = jnp.where(lower, zv[:, sl], e_even).astype(BF16)
        fva_ref[0, 2 * j + 1] = jnp.where(lower, pltpu.roll(zv[:, sl], HEAD_DIM, 1), e_even).astype(BF16)
    for hd in range(FOX_HEADS):
        fk_ref[0, hd] = zk[:, hd * HEAD_DIM:(hd + 1) * HEAD_DIM]
        fv_ref[0, hd] = zv[:, hd * HEAD_DIM:(hd + 1) * HEAD_DIM]
    fg_ref[0] = proj("fg").astype(BF16)

    zq, zk, zv = proj("dq"), proj("dk"), proj("dv")
    pos = (row + (ti * tm + pos_offset)).astype(F32)
    sub = lane % HEAD_DIM
    for hd in range(DIFF_HEADS):
        slope = 2.0 ** (-8.0 * (hd + 1) / DIFF_HEADS)
        sl = slice(hd * LANES, (hd + 1) * LANES)
        p0, p1, p2 = (a.astype(F32) for a in _split3((slope * LOG2E) * pos))
        pieces = jnp.where(sub % 3 == 0, p0, jnp.where(sub % 3 == 1, p1, p2))
        ext_q = jnp.where(sub < 3, -pieces, (sub < 6).astype(F32))
        ext_k = jnp.where(sub < 3, 1.0, jnp.where(sub < 6, pieces, 0.0))
        dqa_ref[0, 2 * hd] = jnp.where(lower, zq[:, sl], ext_q).astype(BF16)
        dqa_ref[0, 2 * hd + 1] = jnp.where(lower, ext_q, zq[:, sl]).astype(BF16)
        dka_ref[0, 2 * hd] = jnp.where(lower, zk[:, sl], ext_k).astype(BF16)
        dka_ref[0, 2 * hd + 1] = jnp.where(lower, ext_k, zk[:, sl]).astype(BF16)
        dk_ref[0, hd] = zk[:, sl]
        dv_ref[0, hd] = zv[:, sl]
        dvb_ref[0, hd] = zv[:, sl].astype(BF16)
    dg_ref[0] = proj("dg").astype(BF16)
    stat_ref[0, 0] = jnp.concatenate(
        fox_norms + [max_sq_norm(zq), max_sq_norm(zk), cum2[0:1], cum2[tm - 1:tm], jnp.zeros((2, LANES), F32)],
        axis=0)


def _in_proj(x, mod, w, bf16x, consts, pos_offset):
    b, t, d = x.shape
    tm = min(t, 512)
    nt = t // tm
    pq, pk, oq, ok, seg = consts

    def full(a):
        return pl.BlockSpec(a.shape, lambda i, j: (0,) * a.ndim)

    def heads(n, width):
        return pl.BlockSpec((1, n, tm, width), lambda i, j: (i, 0, j, 0))

    rows = pl.BlockSpec((1, tm, FOX_WIDTH), lambda i, j: (i, j, 0))
    out_shape = (
        jax.ShapeDtypeStruct((b, FOX_HEADS, t, HEAD_DIM), F32),
        jax.ShapeDtypeStruct((b, FOX_HEADS, t, HEAD_DIM), F32),
        jax.ShapeDtypeStruct((b, DIFF_HEADS, t, DIFF_VDIM), F32),
        jax.ShapeDtypeStruct((b, DIFF_HEADS, t, DIFF_VDIM), F32),
        jax.ShapeDtypeStruct((b, t, LANES), F32),
        jax.ShapeDtypeStruct((b, FOX_HEADS, t, LANES), BF16),
        jax.ShapeDtypeStruct((b, FOX_HEADS, t, LANES), BF16),
        jax.ShapeDtypeStruct((b, FOX_HEADS, t, LANES), BF16),
        jax.ShapeDtypeStruct((b, t, FOX_WIDTH), BF16),
        jax.ShapeDtypeStruct((b, 2 * DIFF_HEADS, t, LANES), BF16),
        jax.ShapeDtypeStruct((b, 2 * DIFF_HEADS, t, LANES), BF16),
        jax.ShapeDtypeStruct((b, DIFF_HEADS, t, DIFF_VDIM), BF16),
        jax.ShapeDtypeStruct((b, t, DIFF_WIDTH), BF16),
        jax.ShapeDtypeStruct((b, nt, 8, LANES), F32),
    )
    out_specs = (
        heads(FOX_HEADS, HEAD_DIM), heads(FOX_HEADS, HEAD_DIM),
        heads(DIFF_HEADS, DIFF_VDIM), heads(DIFF_HEADS, DIFF_VDIM),
        pl.BlockSpec((1, tm, LANES), lambda i, j: (i, j, 0)),
        heads(FOX_HEADS, LANES), heads(FOX_HEADS, LANES), heads(FOX_HEADS, LANES), rows,
        heads(2 * DIFF_HEADS, LANES), heads(2 * DIFF_HEADS, LANES), heads(DIFF_HEADS, DIFF_VDIM), rows,
        pl.BlockSpec((1, 1, 8, LANES), lambda i, j: (i, j, 0, 0)),
    )
    return pl.pallas_call(
        functools.partial(_in_proj_kernel, tm=tm, pos_offset=pos_offset),
        grid=(b, nt),
        in_specs=[pl.BlockSpec((1, tm, d), lambda i, j: (i, j, 0)),
                  pl.BlockSpec((1, 3, d), lambda i, j: (i, 0, 0)),
                  full(w), full(bf16x), full(pq), full(pk), full(oq), full(ok), full(seg)],
        out_specs=out_specs,
        out_shape=out_shape,
        scratch_shapes=[pltpu.VMEM((1, LANES), F32)],
        compiler_params=_params(("arbitrary", "arbitrary")),
        name="in_proj",
    )(x, mod, w, bf16x, pq, pk, oq, ok, seg)


_NT = (((1,), (1,)), ((), ()))
_TN = (((0,), (0,)), ((), ()))
_STAGES = 4
_FOX_V_ROWS = HEAD_DIM + 16


_SKIP_MARGIN = 152.0
_STAT_FOX_Q, _STAT_FOX_K, _STAT_DIFF_Q, _STAT_DIFF_K, _STAT_CUM_FIRST, _STAT_CUM_LAST = range(6)


def _dot_bound(stat, q_row, k_row, qi):
    tile = lax.broadcasted_iota(jnp.int32, (1, stat.shape[-1]), 1)
    qn2 = jnp.sum(jnp.where(tile == qi, stat[q_row:q_row + 1], 0.0))
    return jnp.sqrt(qn2 * stat[k_row:k_row + 1]) * 1.01 + 1.0


def _first_needed_tile(qi, bounds, m_ref):
    first = qi
    for c, bound in enumerate(bounds):
        tile = lax.broadcasted_iota(jnp.int32, bound.shape, 1)
        needed = (bound >= jnp.min(m_ref[c]) - _SKIP_MARGIN) & (tile < qi)
        first = jnp.minimum(first, jnp.min(jnp.where(needed, tile, qi)))
    return first


def _pipelined_tiles(qi, scores, absorb, first_needed, sa_ref, sb_ref):
    chains = range(2)
    bufs = (sa_ref, sb_ref)
    for c in chains:
        scores(qi, sa_ref, c)
        scores(jnp.maximum(qi - 1, 0), sb_ref, c)
        absorb(sa_ref, qi, c, True)
    n = qi - first_needed()

    def stages(top, count, more):
        for s in range(count):
            for c in chains:
                if more or s + 1 < count:
                    scores(jnp.maximum(top - s - 1, 0), bufs[s % 2], c)
                absorb(bufs[(s + 1) % 2], top - s, c, False)

    def body(i, carry):
        stages(qi - 1 - _STAGES * i, _STAGES, True)
        return carry

    lax.fori_loop(0, n // _STAGES, body, 0)
    for rem in range(1, _STAGES):
        @pl.when(n % _STAGES == rem)
        def _(rem=rem):
            stages(qi - 1 - (n - rem), rem, False)


def _fox_kernel(q_ref, k_ref, v_ref, stat_ref, o_ref, sa_ref, sb_ref, m_ref, acc_ref, *, tq):
    qi = pl.program_id(2)
    m_ref[...] = jnp.full_like(m_ref, NEG_INF)
    acc_ref[...] = jnp.zeros_like(acc_ref)

    def scores(ki, dst, hh):
        k = k_ref[0, hh, pl.ds(pl.multiple_of(ki * tq, tq), tq), :]
        dst[hh] = lax.dot_general(k, q_ref[0, hh], _NT, preferred_element_type=F32)

    def absorb(src, ki, hh, diagonal):
        s = src[hh]
        if diagonal:
            key = lax.broadcasted_iota(jnp.int32, (tq, tq), 0)
            qry = lax.broadcasted_iota(jnp.int32, (tq, tq), 1)
            s = jnp.where(key <= qry, s, NEG_INF)
        m_prev = m_ref[hh]
        m_new = jnp.maximum(m_prev, jnp.max(s, axis=0, keepdims=True))
        p = jnp.exp2(s - m_new)
        alpha = jnp.exp2(m_prev - m_new)
        v = v_ref[0, hh, pl.ds(pl.multiple_of(ki * tq, tq), tq), 0:_FOX_V_ROWS]
        pv = lax.dot_general(v, p.astype(BF16), _TN, preferred_element_type=F32)
        acc_ref[hh] = alpha * acc_ref[hh] + pv
        m_ref[hh] = m_new

    def first_needed():
        bounds = []
        for hh in range(2):
            stat = stat_ref[0, hh]
            tile = lax.broadcasted_iota(jnp.int32, (1, stat.shape[-1]), 1)
            cum_q = jnp.sum(jnp.where(tile == qi, stat[_STAT_CUM_FIRST:_STAT_CUM_FIRST + 1], 0.0))
            decay = cum_q - stat[_STAT_CUM_LAST:_STAT_CUM_LAST + 1]
            bounds.append(_dot_bound(stat, _STAT_FOX_Q, _STAT_FOX_K, qi) + decay)
        return _first_needed_tile(qi, bounds, m_ref)

    _pipelined_tiles(qi, scores, absorb, first_needed, sa_ref, sb_ref)
    outs = [acc_ref[hh, 0:HEAD_DIM] / acc_ref[hh, HEAD_DIM:HEAD_DIM + 1] for hh in range(2)]
    o_ref[0] = jnp.concatenate(outs, axis=0).T.astype(BF16)


def _fox_attention(fqa, fka, fva, stats):
    b, nh, t, _ = fqa.shape
    tq = min(t, 512)
    return pl.pallas_call(
        functools.partial(_fox_kernel, tq=tq),
        grid=(b, nh // 2, t // tq),
        in_specs=[pl.BlockSpec((1, 2, tq, LANES), lambda i, j, q: (i, j, q, 0)),
                  pl.BlockSpec((1, 2, t, LANES), lambda i, j, q: (i, j, 0, 0)),
                  pl.BlockSpec((1, 2, t, LANES), lambda i, j, q: (i, j, 0, 0)),
                  pl.BlockSpec((1, 2) + stats.shape[2:], lambda i, j, q: (i, j, 0, 0))],
        out_specs=pl.BlockSpec((1, tq, LANES), lambda i, j, q: (i, q, j)),
        out_shape=jax.ShapeDtypeStruct((b, t, FOX_WIDTH), BF16),
        scratch_shapes=[pltpu.VMEM((2, tq, tq), F32), pltpu.VMEM((2, tq, tq), F32),
                        pltpu.VMEM((2, 1, tq), F32), pltpu.VMEM((2, _FOX_V_ROWS, tq), F32)],
        compiler_params=_params(("arbitrary", "arbitrary", "arbitrary")),
        name="fox_attn",
    )(fqa, fka, fva, stats)


def _alibi_slope(head):
    return jnp.exp2(jnp.full((1, 1), -8.0 / DIFF_HEADS, F32) * (head + 1).astype(F32))


def _lambda(lam_ref, lam_init):
    lv = lam_ref[...]
    e1 = jnp.exp(jnp.sum(lv[0:1] * lv[1:2], axis=1, keepdims=True))
    e2 = jnp.exp(jnp.sum(lv[2:3] * lv[3:4], axis=1, keepdims=True))
    return e1 - e2 + lam_init


def _diff_normed(o0, o1, lam, axis):
    df = o0 - lam * o1
    return df * lax.rsqrt(jnp.mean(df * df, axis=axis, keepdims=True) + RMS_EPS)


_ONES_ROWS = 16


def _diff_kernel(q_ref, k_ref, v_ref, stat_ref, lam_ref, g_ref, o_ref, sa_ref, sb_ref, m_ref, acc_ref, *, tq,
                 lam_init):
    hd = pl.program_id(1)
    qi = pl.program_id(2)
    slope = _alibi_slope(hd)
    m_ref[...] = jnp.full_like(m_ref, NEG_INF)
    acc_ref[...] = jnp.zeros_like(acc_ref)

    def scores(ki, dst, c):
        k = k_ref[0, c, pl.ds(pl.multiple_of(ki * tq, tq), tq), :]
        dst[c] = lax.dot_general(k, q_ref[0, c], _NT, preferred_element_type=F32)

    def absorb(src, ki, c, diagonal):
        s = src[c]
        if diagonal:
            key = lax.broadcasted_iota(jnp.int32, (tq, tq), 0)
            qry = lax.broadcasted_iota(jnp.int32, (tq, tq), 1)
            ahead = jnp.maximum(key - qry, 0).astype(F32)
            s = jnp.where((key // CHUNK) <= (qry // CHUNK), s - (2.0 * LOG2E * slope) * ahead, NEG_INF)
        m_prev = m_ref[c]
        m_new = jnp.maximum(m_prev, jnp.max(s, axis=0, keepdims=True))
        p = jnp.exp2(s - m_new)
        alpha = jnp.exp2(m_prev - m_new)
        v = v_ref[0, 0, pl.ds(pl.multiple_of(ki * tq, tq), tq), :]
        vt = jnp.concatenate([v.T, jnp.ones((_ONES_ROWS, tq), BF16)], axis=0)
        acc_ref[c] = alpha * acc_ref[c] + jnp.dot(vt, p.astype(BF16), preferred_element_type=F32)
        m_ref[c] = m_new

    def first_needed():
        bounds = []
        for c in range(2):
            stat = stat_ref[0, c]
            tile = lax.broadcasted_iota(jnp.int32, (1, stat.shape[-1]), 1)
            gap = ((qi - 1 - tile) * tq + 1).astype(F32)
            bounds.append(_dot_bound(stat, _STAT_DIFF_Q, _STAT_DIFF_K, qi) - (LOG2E * slope) * gap)
        return _first_needed_tile(qi, bounds, m_ref)

    _pipelined_tiles(qi, scores, absorb, first_needed, sa_ref, sb_ref)
    lam = _lambda(lam_ref, lam_init)
    acc0, acc1 = acc_ref[0], acc_ref[1]
    o0 = acc0[:DIFF_VDIM] / acc0[DIFF_VDIM:DIFF_VDIM + 1]
    o1 = acc1[:DIFF_VDIM] / acc1[DIFF_VDIM:DIFF_VDIM + 1]
    o_ref[0] = (_diff_normed(o0, o1, lam, 0).T * g_ref[...] * (1.0 - lam_init)).astype(BF16)


def _diff_attention(dqa, dka, dvb, stats, lam_vecs, subln_g, lam_init):
    b, nm, t, _ = dqa.shape
    tq = min(t, 512)
    return pl.pallas_call(
        functools.partial(_diff_kernel, tq=tq, lam_init=lam_init),
        grid=(b, nm // 2, t // tq),
        in_specs=[pl.BlockSpec((1, 2, tq, LANES), lambda i, j, q: (i, j, q, 0)),
                  pl.BlockSpec((1, 2, t, LANES), lambda i, j, q: (i, j, 0, 0)),
                  pl.BlockSpec((1, 1, t, DIFF_VDIM), lambda i, j, q: (i, j, 0, 0)),
                  pl.BlockSpec((1, 2) + stats.shape[2:], lambda i, j, q: (i, j, 0, 0)),
                  pl.BlockSpec(lam_vecs.shape, lambda i, j, q: (0, 0)),
                  pl.BlockSpec(subln_g.shape, lambda i, j, q: (0, 0))],
        out_specs=pl.BlockSpec((1, tq, DIFF_VDIM), lambda i, j, q: (i, q, j)),
        out_shape=jax.ShapeDtypeStruct((b, t, DIFF_WIDTH), BF16),
        scratch_shapes=[pltpu.VMEM((2, tq, tq), F32), pltpu.VMEM((2, tq, tq), F32),
                        pltpu.VMEM((2, 1, tq), F32), pltpu.VMEM((2, DIFF_VDIM + _ONES_ROWS, tq), F32)],
        compiler_params=_params(("arbitrary", "arbitrary", "arbitrary")),
        name="diff_attn",
    )(dqa, dka, dvb, stats, lam_vecs, subln_g)


def _prefix_sum_lanes(x):
    n = x.shape[-1]
    lane = lax.broadcasted_iota(jnp.int32, x.shape, x.ndim - 1)
    sh = 1
    while sh < n:
        x = x + jnp.where(lane >= sh, pltpu.roll(x, sh, x.ndim - 1), 0.0)
        sh *= 2
    return x


def _fox_sample_kernel(q_ref, k_ref, v_ref, ck_ref, cv_ref, clf_ref, o_ref, *, t, past):
    r = lax.broadcasted_iota(jnp.int32, (t, t), 0)
    c = lax.broadcasted_iota(jnp.int32, (t, t), 1)
    for hh in range(2):
        data = slice(0, HEAD_DIM) if hh == 0 else slice(HEAD_DIM, LANES)
        ext = HEAD_DIM if hh == 0 else 0
        qa = q_ref[0, hh]
        qf = qa.astype(F32)
        cum_q = qf[:, ext:ext + 1] + qf[:, ext + 1:ext + 2] + qf[:, ext + 2:ext + 3]
        cum_c = _prefix_sum_lanes(clf_ref[0, hh])
        suffix = cum_c[:, past - 1:past] - cum_c
        s_c = jnp.dot(qa[:, data], ck_ref[0, 0, hh].astype(BF16), preferred_element_type=F32)
        s_c = s_c + cum_q + suffix * LOG2E
        s_n = lax.dot_general(qa, k_ref[0, hh], _NT, preferred_element_type=F32)
        s_n = jnp.where(c <= r, s_n, NEG_INF)
        m = jnp.maximum(jnp.max(s_c, axis=1, keepdims=True), jnp.max(s_n, axis=1, keepdims=True))
        p_c = jnp.exp2(s_c - m)
        p_n = jnp.exp2(s_n - m)
        denom = jnp.sum(p_c, axis=1, keepdims=True) + jnp.sum(p_n, axis=1, keepdims=True)
        o = lax.dot_general(p_c.astype(BF16), cv_ref[0, 0, hh].astype(BF16), _NT, preferred_element_type=F32)
        o = o + jnp.dot(p_n.astype(BF16), v_ref[0, hh][:, 0:HEAD_DIM], preferred_element_type=F32)
        o_ref[0, :, data] = (o / denom).astype(BF16)


def _fox_sample(fqa, fka, fva, cache_k, cache_v, cache_logf):
    b, nh, t, _ = fqa.shape
    past = cache_k.shape[3]
    clf = cache_logf.reshape(b, nh, 1, past)
    cache_k, cache_v = jnp.swapaxes(cache_k, 3, 4), jnp.swapaxes(cache_v, 3, 4)
    new = pl.BlockSpec((1, 2, t, LANES), lambda i, j: (i, j, 0, 0))
    cache = pl.BlockSpec((1, 1, 2, HEAD_DIM, past), lambda i, j: (0, i, j, 0, 0))
    return pl.pallas_call(
        functools.partial(_fox_sample_kernel, t=t, past=past),
        grid=(b, nh // 2),
        in_specs=[new, new, new, cache, cache,
                  pl.BlockSpec((1, 2, 1, past), lambda i, j: (i, j, 0, 0))],
        out_specs=pl.BlockSpec((1, t, LANES), lambda i, j: (i, 0, j)),
        out_shape=jax.ShapeDtypeStruct((b, t, FOX_WIDTH), BF16),
        compiler_params=_params(("arbitrary", "arbitrary")),
        name="fox_sample",
    )(fqa, fka, fva, cache_k, cache_v, clf)


def _diff_sample_kernel(q_ref, k_ref, v_ref, ck_ref, cv_ref, lam_ref, g_ref, o_ref, *, t, past, lam_init):
    hd = pl.program_id(1)
    slope = _alibi_slope(hd)
    lane = lax.broadcasted_iota(jnp.int32, (t, LANES), 1)
    pos_q = past + lax.broadcasted_iota(jnp.int32, (t, past), 0)
    pos_c = lax.broadcasted_iota(jnp.int32, (t, past), 1)
    bias_c = (-LOG2E * slope) * jnp.abs(pos_q - pos_c).astype(F32)
    vis_c = (pos_c // CHUNK) <= (pos_q // CHUNK)
    pq_n = past + lax.broadcasted_iota(jnp.int32, (t, t), 0)
    pk_n = past + lax.broadcasted_iota(jnp.int32, (t, t), 1)
    bias_n = (-LOG2E * slope) * jnp.abs(pq_n - pk_n).astype(F32)
    vis_n = (pk_n // CHUNK) <= (pq_n // CHUNK)
    ck = ck_ref[0, 0, 0].astype(BF16)
    cv = cv_ref[0, 0, 0].astype(BF16)
    v = v_ref[0, 0]
    outs = []
    for c in range(2):
        half = (lane < HEAD_DIM) if c == 0 else (lane >= HEAD_DIM)
        qz = jnp.where(half, q_ref[0, c], jnp.zeros((), BF16))
        kz = jnp.where(half, k_ref[0, c], jnp.zeros((), BF16))
        s_c = jnp.dot(qz, ck, preferred_element_type=F32)
        s_c = jnp.where(vis_c, s_c + bias_c, NEG_INF)
        s_n = lax.dot_general(qz, kz, _NT, preferred_element_type=F32)
        s_n = jnp.where(vis_n, s_n + bias_n, NEG_INF)
        m = jnp.maximum(jnp.max(s_c, axis=1, keepdims=True), jnp.max(s_n, axis=1, keepdims=True))
        p_c = jnp.exp2(s_c - m)
        p_n = jnp.exp2(s_n - m)
        denom = jnp.sum(p_c, axis=1, keepdims=True) + jnp.sum(p_n, axis=1, keepdims=True)
        o = jnp.dot(p_c.astype(BF16), cv, preferred_element_type=F32)
        o = o + jnp.dot(p_n.astype(BF16), v, preferred_element_type=F32)
        outs.append(o / denom)
    lam = _lambda(lam_ref, lam_init)
    o_ref[0] = (_diff_normed(outs[0], outs[1], lam, 1) * g_ref[...] * (1.0 - lam_init)).astype(BF16)


def _diff_sample(dqa, dka, dvb, cache_k, cache_v, lam_vecs, subln_g, lam_init):
    b, nm, t, _ = dqa.shape
    past = cache_k.shape[3]
    ck = jnp.transpose(cache_k, (0, 1, 2, 4, 5, 3)).reshape(cache_k.shape[:3] + (DIFF_VDIM, past))
    new = pl.BlockSpec((1, 2, t, LANES), lambda i, j: (i, j, 0, 0))
    cache = pl.BlockSpec((1, 1, 1, past, DIFF_VDIM), lambda i, j: (0, i, j, 0, 0))
    cache_t = pl.BlockSpec((1, 1, 1, DIFF_VDIM, past), lambda i, j: (0, i, j, 0, 0))
    return pl.pallas_call(
        functools.partial(_diff_sample_kernel, t=t, past=past, lam_init=lam_init),
        grid=(b, nm // 2),
        in_specs=[new, new, pl.BlockSpec((1, 1, t, DIFF_VDIM), lambda i, j: (i, j, 0, 0)),
                  cache_t, cache,
                  pl.BlockSpec(lam_vecs.shape, lambda i, j: (0, 0)),
                  pl.BlockSpec(subln_g.shape, lambda i, j: (0, 0))],
        out_specs=pl.BlockSpec((1, t, DIFF_VDIM), lambda i, j: (i, 0, j)),
        out_shape=jax.ShapeDtypeStruct((b, t, DIFF_WIDTH), BF16),
        compiler_params=_params(("arbitrary", "arbitrary")),
        name="diff_sample",
    )(dqa, dka, dvb, ck, cache_v, lam_vecs, subln_g)


def _out_proj_kernel(fo_ref, do_ref, fg_ref, dg_ref, x_ref, mod_ref, w_ref, g_ref, b_ref, y_ref, *, alpha):
    def gated(o_ref, gate_ref):
        g = gate_ref[0].astype(F32)
        return (o_ref[0].astype(F32) * (g * jax.nn.sigmoid(g))).astype(BF16)

    branch = jnp.dot(gated(fo_ref, fg_ref), w_ref[0:FOX_WIDTH, :], preferred_element_type=F32)
    branch = branch + jnp.dot(gated(do_ref, dg_ref), w_ref[FOX_WIDTH:, :], preferred_element_type=F32)
    r = alpha * x_ref[0] + mod_ref[0, 2:3, :] * branch
    mu = jnp.mean(r, axis=-1, keepdims=True)
    rc = r - mu
    var = jnp.mean(rc * rc, axis=-1, keepdims=True)
    y_ref[0] = rc * lax.rsqrt(var + LN_EPS) * g_ref[...] + b_ref[...]


def _out_proj(fox_o, diff_o, fg, dg, x, mod, w_out, ln_g, ln_b, alpha):
    b, t, d = x.shape
    tm = min(t, 512)
    rows = pl.BlockSpec((1, tm, FOX_WIDTH), lambda i, j: (i, j, 0))
    wide = pl.BlockSpec((1, tm, d), lambda i, j: (i, j, 0))

    def full(a):
        return pl.BlockSpec(a.shape, lambda i, j: (0,) * a.ndim)

    return pl.pallas_call(
        functools.partial(_out_proj_kernel, alpha=alpha),
        grid=(b, t // tm),
        in_specs=[rows, rows, rows, rows, wide,
                  pl.BlockSpec((1, 3, d), lambda i, j: (i, 0, 0)),
                  full(w_out), full(ln_g), full(ln_b)],
        out_specs=wide,
        out_shape=jax.ShapeDtypeStruct((b, t, d), F32),
        compiler_params=_params(("arbitrary", "arbitrary")),
        name="out_proj",
    )(fox_o, diff_o, fg, dg, x, mod, w_out, ln_g, ln_b)


def _pack_w_in(w_in):
    fw = FOX_WIDTH
    o = 0
    fq = w_in[:, o:o + fw]; o += fw
    fk = w_in[:, o:o + fw]; o += fw
    fv = w_in[:, o:o + fw]; o += fw
    ff = w_in[:, o:o + FOX_HEADS]; o += FOX_HEADS
    fg = w_in[:, o:o + fw]; o += fw
    dq = w_in[:, o:o + fw]; o += fw
    dk = w_in[:, o:o + fw]; o += fw
    dv = w_in[:, o:o + fw]; o += fw
    dg = w_in[:, o:o + fw]
    qs = HEAD_DIM ** -0.5 * LOG2E
    return jnp.concatenate([fq * qs, fk, fv, fg, dq * qs, dk, dv, dg,
                            jnp.tile(ff, (1, LANES // FOX_HEADS))], axis=1).astype(BF16)


def kernel(x_prompt, x_sample, cache_fox_k, cache_fox_v, cache_fox_logf, cache_diff_k, cache_diff_v,
           c_prompt, c_sample, w_ada, b_ada, w_in, b_f, lambda_q1, lambda_k1, lambda_q2, lambda_k2,
           subln_g, w_out, ln_g, ln_b):
    depth = w_in.shape[0]
    assert depth == 1, "single-layer step"
    layer = 0
    lam_init = 0.8 - 0.6 * math.exp(-0.3 * layer)
    alpha = (2 * depth) ** 0.25
    bp, bs = c_prompt.shape[0], c_sample.shape[0]
    d = x_prompt.shape[-1]
    past = cache_fox_k.shape[3]

    c_all = jnp.concatenate([c_prompt, c_sample, jnp.zeros((-(bp + bs) % 8, d), F32)], axis=0)
    mod = _ada(c_all, w_ada[layer], b_ada[layer]).reshape(c_all.shape[0], 3, d)
    mod_p, mod_s = mod[:bp], mod[bp:bp + bs]

    w = _pack_w_in(w_in[layer])
    bf16x = jnp.tile(b_f[layer], LANES // FOX_HEADS).reshape(1, LANES)
    seg = np.repeat(np.eye(FOX_WIDTH // HEAD_DIM, LANES, dtype=np.float32), HEAD_DIM, axis=0)
    consts = tuple(jnp.asarray(a, BF16 if a.shape[0] > 1 else F32) for a in _fox_placement() + (seg,))
    lam_vecs = jnp.stack([lambda_q1[layer], lambda_k1[layer], lambda_q2[layer], lambda_k2[layer]])
    g = subln_g[layer].reshape(1, DIFF_VDIM)
    w_o = w_out[layer].astype(BF16)
    lg, lb = ln_g[layer].reshape(1, d), ln_b[layer].reshape(1, d)

    def run(x, mod_x, pos_offset, attend):
        (fk, fv, dk, dv, logf, fqa, fka, fva, fg, dqa, dka, dvb, dg, stats) = _in_proj(
            x, mod_x, w, bf16x, consts, pos_offset)
        stats = jnp.transpose(stats[..., :FOX_WIDTH // HEAD_DIM], (0, 3, 2, 1))
        fox_o, diff_o = attend(fqa, fka, fva, dqa, dka, dvb, stats)
        y = _out_proj(fox_o, diff_o, fg, dg, x, mod_x, w_o, lg, lb, alpha)
        b, t = x.shape[:2]
        states = (fk[None], fv[None], jnp.swapaxes(logf[:, :, :FOX_HEADS], 1, 2)[None],
                  dk.reshape(1, b, DIFF_HEADS, t, 2, HEAD_DIM), dv[None])
        return y, states

    def prompt_attend(fqa, fka, fva, dqa, dka, dvb, stats):
        return (_fox_attention(fqa, fka, fva, stats),
                _diff_attention(dqa, dka, dvb, stats, lam_vecs, g, lam_init))

    def sample_attend(fqa, fka, fva, dqa, dka, dvb, stats):
        del stats
        return (_fox_sample(fqa, fka, fva, cache_fox_k, cache_fox_v, cache_fox_logf[layer]),
                _diff_sample(dqa, dka, dvb, cache_diff_k, cache_diff_v, lam_vecs, g, lam_init))

    yp, sp = run(x_prompt, mod_p, 0, prompt_attend)
    ys, ss = run(x_sample, mod_s, past, sample_attend)
    return (yp, ys) + sp + ss
```

```python
import functools
import math

import jax
import jax.numpy as jnp
import numpy as np
from jax import lax
from jax.experimental import pallas as pl
from jax.experimental.pallas import tpu as pltpu

F32 = jnp.float32
BF16 = jnp.bfloat16

HEAD_DIM = 64
FOX_HEADS = 8
DIFF_HEADS = 4
DIFF_VDIM = 2 * HEAD_DIM
FOX_WIDTH = FOX_HEADS * HEAD_DIM
DIFF_WIDTH = DIFF_HEADS * DIFF_VDIM
CHUNK = 64
LN_EPS = 1e-5
RMS_EPS = 1e-5
LANES = 128
NEG_INF = float("-inf")
LOG2E = math.log2(math.e)

_SEC = {name: i * FOX_WIDTH for i, name in enumerate(("fq", "fk", "fv", "fg", "dq", "dk", "dv", "dg"))}
_FF_OFF = 8 * FOX_WIDTH
_W_COLS = _FF_OFF + LANES

_VMEM_LIMIT = 56 * 1024 * 1024


def _params(sem):
    return pltpu.CompilerParams(dimension_semantics=sem, vmem_limit_bytes=_VMEM_LIMIT)


def _ada_kernel(c_ref, w_ref, b_ref, o_ref):
    c = c_ref[...]
    s = c * jax.nn.sigmoid(c)
    o_ref[...] = jnp.dot(s, w_ref[...], preferred_element_type=F32) + b_ref[...]


def _ada(c_all, w_ada, b_ada):
    rows, d = c_all.shape
    n = w_ada.shape[1]
    tn = 512
    return pl.pallas_call(
        _ada_kernel,
        grid=(n // tn,),
        in_specs=[pl.BlockSpec((rows, d), lambda j: (0, 0)),
                  pl.BlockSpec((d, tn), lambda j: (0, j)),
                  pl.BlockSpec((1, tn), lambda j: (0, j))],
        out_specs=pl.BlockSpec((rows, tn), lambda j: (0, j)),
        out_shape=jax.ShapeDtypeStruct((rows, n), F32),
        compiler_params=_params(("arbitrary",)),
        name="ada",
    )(c_all, w_ada, b_ada.reshape(1, n))


def _split3(x):
    hi = x.astype(BF16)
    r1 = x - hi.astype(F32)
    mid = r1.astype(BF16)
    lo = (r1 - mid.astype(F32)).astype(BF16)
    return hi, mid, lo


def _fox_placement():
    pq = np.zeros((LANES, FOX_WIDTH), np.float32)
    pk = np.zeros((LANES, FOX_WIDTH), np.float32)
    oq = np.zeros((1, FOX_WIDTH), np.float32)
    ok = np.zeros((1, FOX_WIDTH), np.float32)
    for h in range(FOX_HEADS):
        base = (h // 2) * LANES + (HEAD_DIM if h % 2 == 0 else 0)
        for p in range(3):
            pq[8 * p + h, base + p] = 1.0
            ok[0, base + p] = 1.0
            oq[0, base + 3 + p] = 1.0
            pk[8 * p + h, base + 3 + p] = -1.0
    return pq, pk, oq, ok


def _in_proj_kernel(x_ref, mod_ref, w_ref, bf_ref, pq_ref, pk_ref, oq_ref, ok_ref, seg_ref,
                    fk_ref, fv_ref, dk_ref, dv_ref, logf_ref,
                    fqa_ref, fka_ref, fva_ref, fg_ref, dqa_ref, dka_ref, dvb_ref, dg_ref, stat_ref,
                    carry_ref, *, tm, pos_offset):
    ti = pl.program_id(1)
    x = x_ref[0]
    shift = mod_ref[0, 0:1, :]
    scale = mod_ref[0, 1:2, :]
    h = (x * (1.0 + scale) + shift).astype(BF16)

    def proj(name, width=FOX_WIDTH):
        off = _SEC[name] if name in _SEC else _FF_OFF
        return jnp.dot(h, w_ref[:, off:off + width], preferred_element_type=F32)

    lane = lax.broadcasted_iota(jnp.int32, (tm, LANES), 1)
    row = lax.broadcasted_iota(jnp.int32, (tm, LANES), 0)
    lower = lane < HEAD_DIM
    grp = (lane // 8) % 4

    def by_group(a, b, c):
        return jnp.where(grp == 0, a, jnp.where(grp == 1, b, jnp.where(grp == 2, c, jnp.zeros_like(a))))

    def max_sq_norm(z):
        zr = z.astype(BF16).astype(F32)
        sq = jnp.dot((zr * zr).astype(BF16), seg_ref[...], preferred_element_type=F32)
        return jnp.max(sq, axis=0, keepdims=True)

    logf = jax.nn.log_sigmoid(proj("ff", LANES) + bf_ref[...])
    logf_ref[0] = logf
    tri_r = lax.broadcasted_iota(jnp.int32, (tm, tm), 0)
    tri_c = lax.broadcasted_iota(jnp.int32, (tm, tm), 1)
    tri = (tri_c <= tri_r).astype(BF16)
    part = jnp.dot(tri, by_group(*_split3(logf)), preferred_element_type=F32)
    local = part + pltpu.roll(part, 8, 1) + pltpu.roll(part, 16, 1) + pltpu.roll(part, 24, 1)

    @pl.when(ti == 0)
    def _():
        carry_ref[...] = jnp.zeros_like(carry_ref)

    cum = local + carry_ref[...]
    carry_ref[...] = cum[tm - 1:tm, :]
    cum2 = cum * LOG2E
    cum_cat = by_group(*_split3(cum2))
    eq = jnp.dot(cum_cat, pq_ref[...], preferred_element_type=F32) + oq_ref[...]
    ek = jnp.dot(cum_cat, pk_ref[...], preferred_element_type=F32) + ok_ref[...]

    zq, zk, zv = proj("fq"), proj("fk"), proj("fv")
    fox_norms = [max_sq_norm(zq), max_sq_norm(zk)]
    e_even = (lane == HEAD_DIM).astype(F32)
    for j in range(FOX_HEADS // 2):
        sl = slice(j * LANES, (j + 1) * LANES)
        fqa_ref[0, 2 * j] = jnp.where(lower, zq[:, sl], eq[:, sl]).astype(BF16)
        fqa_ref[0, 2 * j + 1] = jnp.where(lower, eq[:, sl], zq[:, sl]).astype(BF16)
        fka_ref[0, 2 * j] = jnp.where(lower, zk[:, sl], ek[:, sl]).astype(BF16)
        fka_ref[0, 2 * j + 1] = jnp.where(lower, ek[:, sl], zk[:, sl]).astype(BF16)
        fva_ref[0, 2 * j] = jnp.where(lower, zv[:, sl], e_even).astype(BF16)
        fva_ref[0, 2 * j + 1] = jnp.where(lower, pltpu.roll(zv[:, sl], HEAD_DIM, 1), e_even).astype(BF16)
    for hd in range(FOX_HEADS):
        fk_ref[0, hd] = zk[:, hd * HEAD_DIM:(hd + 1) * HEAD_DIM]
        fv_ref[0, hd] = zv[:, hd * HEAD_DIM:(hd + 1) * HEAD_DIM]
    fg_ref[0] = jax.nn.silu(proj("fg")).astype(BF16)

    zq, zk, zv = proj("dq"), proj("dk"), proj("dv")
    pos = (row + (ti * tm + pos_offset)).astype(F32)
    sub = lane % HEAD_DIM
    for hd in range(DIFF_HEADS):
        slope = 2.0 ** (-8.0 * (hd + 1) / DIFF_HEADS)
        sl = slice(hd * LANES, (hd + 1) * LANES)
        p0, p1, p2 = (a.astype(F32) for a in _split3((slope * LOG2E) * pos))
        pieces = jnp.where(sub % 3 == 0, p0, jnp.where(sub % 3 == 1, p1, p2))
        ext_q = jnp.where(sub < 3, -pieces, (sub < 6).astype(F32))
        ext_k = jnp.where(sub < 3, 1.0, jnp.where(sub < 6, pieces, 0.0))
        dqa_ref[0, 2 * hd] = jnp.where(lower, zq[:, sl], ext_q).astype(BF16)
        dqa_ref[0, 2 * hd + 1] = jnp.where(lower, ext_q, zq[:, sl]).astype(BF16)
        dka_ref[0, 2 * hd] = jnp.where(lower, zk[:, sl], ext_k).astype(BF16)
        dka_ref[0, 2 * hd + 1] = jnp.where(lower, ext_k, zk[:, sl]).astype(BF16)
        dk_ref[0, hd] = zk[:, sl]
        dv_ref[0, hd] = zv[:, sl]
        dvb_ref[0, hd] = zv[:, sl].astype(BF16)
    dg_ref[0] = jax.nn.silu(proj("dg")).astype(BF16)
    stat_ref[0, 0] = jnp.concatenate(
        fox_norms + [max_sq_norm(zq), max_sq_norm(zk), cum2[0:1], cum2[tm - 1:tm], jnp.zeros((2, LANES), F32)],
        axis=0)


def _in_proj(x, mod, w, bf16x, consts, pos_offset):
    b, t, d = x.shape
    tm = min(t, 512)
    nt = t // tm
    pq, pk, oq, ok, seg = consts

    def full(a):
        return pl.BlockSpec(a.shape, lambda i, j: (0,) * a.ndim)

    def heads(n, width):
        return pl.BlockSpec((1, n, tm, width), lambda i, j: (i, 0, j, 0))

    rows = pl.BlockSpec((1, tm, FOX_WIDTH), lambda i, j: (i, j, 0))
    out_shape = (
        jax.ShapeDtypeStruct((b, FOX_HEADS, t, HEAD_DIM), F32),
        jax.ShapeDtypeStruct((b, FOX_HEADS, t, HEAD_DIM), F32),
        jax.ShapeDtypeStruct((b, DIFF_HEADS, t, DIFF_VDIM), F32),
        jax.ShapeDtypeStruct((b, DIFF_HEADS, t, DIFF_VDIM), F32),
        jax.ShapeDtypeStruct((b, t, LANES), F32),
        jax.ShapeDtypeStruct((b, FOX_HEADS, t, LANES), BF16),
        jax.ShapeDtypeStruct((b, FOX_HEADS, t, LANES), BF16),
        jax.ShapeDtypeStruct((b, FOX_HEADS, t, LANES), BF16),
        jax.ShapeDtypeStruct((b, t, FOX_WIDTH), BF16),
        jax.ShapeDtypeStruct((b, 2 * DIFF_HEADS, t, LANES), BF16),
        jax.ShapeDtypeStruct((b, 2 * DIFF_HEADS, t, LANES), BF16),
        jax.ShapeDtypeStruct((b, DIFF_HEADS, t, DIFF_VDIM), BF16),
        jax.ShapeDtypeStruct((b, t, DIFF_WIDTH), BF16),
        jax.ShapeDtypeStruct((b, nt, 8, LANES), F32),
    )
    out_specs = (
        heads(FOX_HEADS, HEAD_DIM), heads(FOX_HEADS, HEAD_DIM),
        heads(DIFF_HEADS, DIFF_VDIM), heads(DIFF_HEADS, DIFF_VDIM),
        pl.BlockSpec((1, tm, LANES), lambda i, j: (i, j, 0)),
        heads(FOX_HEADS, LANES), heads(FOX_HEADS, LANES), heads(FOX_HEADS, LANES), rows,
        heads(2 * DIFF_HEADS, LANES), heads(2 * DIFF_HEADS, LANES), heads(DIFF_HEADS, DIFF_VDIM), rows,
        pl.BlockSpec((1, 1, 8, LANES), lambda i, j: (i, j, 0, 0)),
    )
    return pl.pallas_call(
        functools.partial(_in_proj_kernel, tm=tm, pos_offset=pos_offset),
        grid=(b, nt),
        in_specs=[pl.BlockSpec((1, tm, d), lambda i, j: (i, j, 0)),
                  pl.BlockSpec((1, 3, d), lambda i, j: (i, 0, 0)),
                  full(w), full(bf16x), full(pq), full(pk), full(oq), full(ok), full(seg)],
        out_specs=out_specs,
        out_shape=out_shape,
        scratch_shapes=[pltpu.VMEM((1, LANES), F32)],
        compiler_params=_params(("arbitrary", "arbitrary")),
        name="in_proj",
    )(x, mod, w, bf16x, pq, pk, oq, ok, seg)


_NT = (((1,), (1,)), ((), ()))
_TN = (((0,), (0,)), ((), ()))
_STAGES = 4
_FOX_V_ROWS = HEAD_DIM + 16


_SKIP_MARGIN = 152.0
_STAT_FOX_Q, _STAT_FOX_K, _STAT_DIFF_Q, _STAT_DIFF_K, _STAT_CUM_FIRST, _STAT_CUM_LAST = range(6)


def _diagonal_blocks(tq):
    assert tq % (2 * LANES) == 0
    return ((tq // 2, 0), (tq, tq // 2))


def _dot_bound(stat, q_row, k_row, qi):
    tile = lax.broadcasted_iota(jnp.int32, (1, stat.shape[-1]), 1)
    qn2 = jnp.sum(jnp.where(tile == qi, stat[q_row:q_row + 1], 0.0))
    return jnp.sqrt(qn2 * stat[k_row:k_row + 1]) * 1.01 + 1.0


def _first_needed_tile(qi, bounds, m_ref):
    first = qi
    for c, bound in enumerate(bounds):
        tile = lax.broadcasted_iota(jnp.int32, bound.shape, 1)
        needed = (bound >= jnp.min(m_ref[c]) - _SKIP_MARGIN) & (tile < qi)
        first = jnp.minimum(first, jnp.min(jnp.where(needed, tile, qi)))
    return first


def _pipelined_tiles(qi, scores, absorb, first_needed, sa_ref, sb_ref):
    chains = range(2)
    bufs = (sa_ref, sb_ref)
    for c in chains:
        scores(qi, sa_ref, c, True)
        scores(jnp.maximum(qi - 1, 0), sb_ref, c)
        absorb(sa_ref, qi, c, True)
    n = qi - first_needed()

    def stages(top, count, more):
        for s in range(count):
            for c in chains:
                if more or s + 1 < count:
                    scores(jnp.maximum(top - s - 1, 0), bufs[s % 2], c)
                absorb(bufs[(s + 1) % 2], top - s, c, False)

    def body(i, carry):
        stages(qi - 1 - _STAGES * i, _STAGES, True)
        return carry

    trips = jnp.maximum(n - 1, 0) // _STAGES
    lax.fori_loop(0, trips, body, 0)
    rem = n - _STAGES * trips
    for count in range(1, _STAGES + 1):
        @pl.when(rem == count)
        def _(count=count):
            stages(qi - 1 - _STAGES * trips, count, False)


def _fox_kernel(q_ref, k_ref, v_ref, stat_ref, o_ref, sa_ref, sb_ref, m_ref, acc_ref, *, tq):
    qi = pl.program_id(2)
    half = tq // 2

    def scores(ki, dst, hh, diagonal=False):
        start = pl.multiple_of(ki * tq, tq)
        if diagonal:
            for keys, q0 in _diagonal_blocks(tq):
                k = k_ref[0, hh, pl.ds(start, keys), :]
                dst[hh, 0:keys, q0:q0 + half] = lax.dot_general(k, q_ref[0, hh, q0:q0 + half, :], _NT,
                                                                preferred_element_type=F32)
        else:
            k = k_ref[0, hh, pl.ds(start, tq), :]
            dst[hh] = lax.dot_general(k, q_ref[0, hh], _NT, preferred_element_type=F32)

    def absorb(src, ki, hh, diagonal):
        start = pl.multiple_of(ki * tq, tq)
        if diagonal:
            for keys, q0 in _diagonal_blocks(tq):
                s = src[hh, 0:keys, q0:q0 + half]
                key = lax.broadcasted_iota(jnp.int32, s.shape, 0)
                qry = lax.broadcasted_iota(jnp.int32, s.shape, 1) + q0
                s = jnp.where(key <= qry, s, NEG_INF)
                m_new = jnp.max(s, axis=0, keepdims=True)
                p = jnp.exp2(s - m_new)
                v = v_ref[0, hh, pl.ds(start, keys), 0:_FOX_V_ROWS]
                acc_ref[hh, :, q0:q0 + half] = lax.dot_general(v, p.astype(BF16), _TN, preferred_element_type=F32)
                m_ref[hh, :, q0:q0 + half] = m_new
            return
        s = src[hh]
        m_prev = m_ref[hh]
        m_new = jnp.maximum(m_prev, jnp.max(s, axis=0, keepdims=True))
        p = jnp.exp2(s - m_new)
        alpha = jnp.exp2(m_prev - m_new)
        v = v_ref[0, hh, pl.ds(start, tq), 0:_FOX_V_ROWS]
        pv = lax.dot_general(v, p.astype(BF16), _TN, preferred_element_type=F32)
        acc_ref[hh] = alpha * acc_ref[hh] + pv
        m_ref[hh] = m_new

    def first_needed():
        bounds = []
        for hh in range(2):
            stat = stat_ref[0, hh]
            tile = lax.broadcasted_iota(jnp.int32, (1, stat.shape[-1]), 1)
            cum_q = jnp.sum(jnp.where(tile == qi, stat[_STAT_CUM_FIRST:_STAT_CUM_FIRST + 1], 0.0))
            decay = cum_q - stat[_STAT_CUM_LAST:_STAT_CUM_LAST + 1]
            bounds.append(_dot_bound(stat, _STAT_FOX_Q, _STAT_FOX_K, qi) + decay)
        return _first_needed_tile(qi, bounds, m_ref)

    _pipelined_tiles(qi, scores, absorb, first_needed, sa_ref, sb_ref)
    outs = [acc_ref[hh, 0:HEAD_DIM] / acc_ref[hh, HEAD_DIM:HEAD_DIM + 1] for hh in range(2)]
    o_ref[0] = jnp.concatenate(outs, axis=0).T.astype(BF16)


def _fox_attention(fqa, fka, fva, stats):
    b, nh, t, _ = fqa.shape
    tq = min(t, 512)
    return pl.pallas_call(
        functools.partial(_fox_kernel, tq=tq),
        grid=(b, nh // 2, t // tq),
        in_specs=[pl.BlockSpec((1, 2, tq, LANES), lambda i, j, q: (i, j, q, 0)),
                  pl.BlockSpec((1, 2, t, LANES), lambda i, j, q: (i, j, 0, 0)),
                  pl.BlockSpec((1, 2, t, LANES), lambda i, j, q: (i, j, 0, 0)),
                  pl.BlockSpec((1, 2) + stats.shape[2:], lambda i, j, q: (i, j, 0, 0))],
        out_specs=pl.BlockSpec((1, tq, LANES), lambda i, j, q: (i, q, j)),
        out_shape=jax.ShapeDtypeStruct((b, t, FOX_WIDTH), BF16),
        scratch_shapes=[pltpu.VMEM((2, tq, tq), F32), pltpu.VMEM((2, tq, tq), F32),
                        pltpu.VMEM((2, 1, tq), F32), pltpu.VMEM((2, _FOX_V_ROWS, tq), F32)],
        compiler_params=_params(("arbitrary", "arbitrary", "arbitrary")),
        name="fox_attn",
    )(fqa, fka, fva, stats)


def _alibi_slope(head):
    return jnp.exp2(jnp.full((1, 1), -8.0 / DIFF_HEADS, F32) * (head + 1).astype(F32))


def _lambda(lam_ref, lam_init):
    lv = lam_ref[...]
    e1 = jnp.exp(jnp.sum(lv[0:1] * lv[1:2], axis=1, keepdims=True))
    e2 = jnp.exp(jnp.sum(lv[2:3] * lv[3:4], axis=1, keepdims=True))
    return e1 - e2 + lam_init


def _diff_normed(o0, o1, lam, axis):
    df = o0 - lam * o1
    return df * lax.rsqrt(jnp.mean(df * df, axis=axis, keepdims=True) + RMS_EPS)


_ONES_ROWS = 16


def _diff_kernel(q_ref, k_ref, v_ref, stat_ref, lam_ref, g_ref, o_ref, sa_ref, sb_ref, m_ref, acc_ref, *, tq,
                 lam_init):
    hd = pl.program_id(1)
    qi = pl.program_id(2)
    slope = _alibi_slope(hd)
    half = tq // 2

    def scores(ki, dst, c, diagonal=False):
        start = pl.multiple_of(ki * tq, tq)
        if diagonal:
            for keys, q0 in _diagonal_blocks(tq):
                k = k_ref[0, c, pl.ds(start, keys), :]
                dst[c, 0:keys, q0:q0 + half] = lax.dot_general(k, q_ref[0, c, q0:q0 + half, :], _NT,
                                                               preferred_element_type=F32)
        else:
            k = k_ref[0, c, pl.ds(start, tq), :]
            dst[c] = lax.dot_general(k, q_ref[0, c], _NT, preferred_element_type=F32)

    def values_t(start, keys):
        v = v_ref[0, 0, pl.ds(start, keys), :]
        return jnp.concatenate([v.T, jnp.ones((_ONES_ROWS, keys), BF16)], axis=0)

    def absorb(src, ki, c, diagonal):
        start = pl.multiple_of(ki * tq, tq)
        if diagonal:
            for keys, q0 in _diagonal_blocks(tq):
                s = src[c, 0:keys, q0:q0 + half]
                key = lax.broadcasted_iota(jnp.int32, s.shape, 0)
                qry = lax.broadcasted_iota(jnp.int32, s.shape, 1) + q0
                ahead = jnp.maximum(key - qry, 0).astype(F32)
                s = jnp.where((key // CHUNK) <= (qry // CHUNK), s - (2.0 * LOG2E * slope) * ahead, NEG_INF)
                m_new = jnp.max(s, axis=0, keepdims=True)
                p = jnp.exp2(s - m_new)
                acc_ref[c, :, q0:q0 + half] = jnp.dot(values_t(start, keys), p.astype(BF16),
                                                      preferred_element_type=F32)
                m_ref[c, :, q0:q0 + half] = m_new
            return
        s = src[c]
        m_prev = m_ref[c]
        m_new = jnp.maximum(m_prev, jnp.max(s, axis=0, keepdims=True))
        p = jnp.exp2(s - m_new)
        alpha = jnp.exp2(m_prev - m_new)
        acc_ref[c] = alpha * acc_ref[c] + jnp.dot(values_t(start, tq), p.astype(BF16), preferred_element_type=F32)
        m_ref[c] = m_new

    def first_needed():
        bounds = []
        for c in range(2):
            stat = stat_ref[0, c]
            tile = lax.broadcasted_iota(jnp.int32, (1, stat.shape[-1]), 1)
            gap = ((qi - 1 - tile) * tq + 1).astype(F32)
            bounds.append(_dot_bound(stat, _STAT_DIFF_Q, _STAT_DIFF_K, qi) - (LOG2E * slope) * gap)
        return _first_needed_tile(qi, bounds, m_ref)

    _pipelined_tiles(qi, scores, absorb, first_needed, sa_ref, sb_ref)
    lam = _lambda(lam_ref, lam_init)
    acc0, acc1 = acc_ref[0], acc_ref[1]
    o0 = acc0[:DIFF_VDIM] / acc0[DIFF_VDIM:DIFF_VDIM + 1]
    o1 = acc1[:DIFF_VDIM] / acc1[DIFF_VDIM:DIFF_VDIM + 1]
    o_ref[0] = (_diff_normed(o0, o1, lam, 0).T * g_ref[...] * (1.0 - lam_init)).astype(BF16)


def _diff_attention(dqa, dka, dvb, stats, lam_vecs, subln_g, lam_init):
    b, nm, t, _ = dqa.shape
    tq = min(t, 512)
    return pl.pallas_call(
        functools.partial(_diff_kernel, tq=tq, lam_init=lam_init),
        grid=(b, nm // 2, t // tq),
        in_specs=[pl.BlockSpec((1, 2, tq, LANES), lambda i, j, q: (i, j, q, 0)),
                  pl.BlockSpec((1, 2, t, LANES), lambda i, j, q: (i, j, 0, 0)),
                  pl.BlockSpec((1, 1, t, DIFF_VDIM), lambda i, j, q: (i, j, 0, 0)),
                  pl.BlockSpec((1, 2) + stats.shape[2:], lambda i, j, q: (i, j, 0, 0)),
                  pl.BlockSpec(lam_vecs.shape, lambda i, j, q: (0, 0)),
                  pl.BlockSpec(subln_g.shape, lambda i, j, q: (0, 0))],
        out_specs=pl.BlockSpec((1, tq, DIFF_VDIM), lambda i, j, q: (i, q, j)),
        out_shape=jax.ShapeDtypeStruct((b, t, DIFF_WIDTH), BF16),
        scratch_shapes=[pltpu.VMEM((2, tq, tq), F32), pltpu.VMEM((2, tq, tq), F32),
                        pltpu.VMEM((2, 1, tq), F32), pltpu.VMEM((2, DIFF_VDIM + _ONES_ROWS, tq), F32)],
        compiler_params=_params(("arbitrary", "arbitrary", "arbitrary")),
        name="diff_attn",
    )(dqa, dka, dvb, stats, lam_vecs, subln_g)


def _prefix_sum_lanes(x):
    n = x.shape[-1]
    lane = lax.broadcasted_iota(jnp.int32, x.shape, x.ndim - 1)
    sh = 1
    while sh < n:
        x = x + jnp.where(lane >= sh, pltpu.roll(x, sh, x.ndim - 1), 0.0)
        sh *= 2
    return x


def _fox_sample_kernel(q_ref, k_ref, v_ref, ck_ref, cv_ref, clf_ref, o_ref, *, t, past):
    r = lax.broadcasted_iota(jnp.int32, (t, t), 0)
    c = lax.broadcasted_iota(jnp.int32, (t, t), 1)
    for hh in range(2):
        data = slice(0, HEAD_DIM) if hh == 0 else slice(HEAD_DIM, LANES)
        ext = HEAD_DIM if hh == 0 else 0
        qa = q_ref[0, hh]
        qf = qa.astype(F32)
        cum_q = qf[:, ext:ext + 1] + qf[:, ext + 1:ext + 2] + qf[:, ext + 2:ext + 3]
        cum_c = _prefix_sum_lanes(clf_ref[0, hh])
        suffix = cum_c[:, past - 1:past] - cum_c
        s_c = jnp.dot(qa[:, data], ck_ref[0, 0, hh].astype(BF16), preferred_element_type=F32)
        s_c = s_c + cum_q + suffix * LOG2E
        s_n = lax.dot_general(qa, k_ref[0, hh], _NT, preferred_element_type=F32)
        s_n = jnp.where(c <= r, s_n, NEG_INF)
        m = jnp.maximum(jnp.max(s_c, axis=1, keepdims=True), jnp.max(s_n, axis=1, keepdims=True))
        p_c = jnp.exp2(s_c - m)
        p_n = jnp.exp2(s_n - m)
        denom = jnp.sum(p_c, axis=1, keepdims=True) + jnp.sum(p_n, axis=1, keepdims=True)
        o = lax.dot_general(p_c.astype(BF16), cv_ref[0, 0, hh].astype(BF16), _NT, preferred_element_type=F32)
        o = o + jnp.dot(p_n.astype(BF16), v_ref[0, hh][:, 0:HEAD_DIM], preferred_element_type=F32)
        o_ref[0, :, data] = (o / denom).astype(BF16)


def _fox_sample(fqa, fka, fva, cache_k, cache_v, cache_logf):
    b, nh, t, _ = fqa.shape
    past = cache_k.shape[3]
    clf = cache_logf.reshape(b, nh, 1, past)
    cache_k, cache_v = jnp.swapaxes(cache_k, 3, 4), jnp.swapaxes(cache_v, 3, 4)
    new = pl.BlockSpec((1, 2, t, LANES), lambda i, j: (i, j, 0, 0))
    cache = pl.BlockSpec((1, 1, 2, HEAD_DIM, past), lambda i, j: (0, i, j, 0, 0))
    return pl.pallas_call(
        functools.partial(_fox_sample_kernel, t=t, past=past),
        grid=(b, nh // 2),
        in_specs=[new, new, new, cache, cache,
                  pl.BlockSpec((1, 2, 1, past), lambda i, j: (i, j, 0, 0))],
        out_specs=pl.BlockSpec((1, t, LANES), lambda i, j: (i, 0, j)),
        out_shape=jax.ShapeDtypeStruct((b, t, FOX_WIDTH), BF16),
        compiler_params=_params(("arbitrary", "arbitrary")),
        name="fox_sample",
    )(fqa, fka, fva, cache_k, cache_v, clf)


def _diff_sample_kernel(q_ref, k_ref, v_ref, ck_ref, cv_ref, lam_ref, g_ref, o_ref, *, t, past, lam_init):
    hd = pl.program_id(1)
    slope = _alibi_slope(hd)
    lane = lax.broadcasted_iota(jnp.int32, (t, LANES), 1)
    pos_q = past + lax.broadcasted_iota(jnp.int32, (t, past), 0)
    pos_c = lax.broadcasted_iota(jnp.int32, (t, past), 1)
    bias_c = (-LOG2E * slope) * jnp.abs(pos_q - pos_c).astype(F32)
    vis_c = (pos_c // CHUNK) <= (pos_q // CHUNK)
    pq_n = past + lax.broadcasted_iota(jnp.int32, (t, t), 0)
    pk_n = past + lax.broadcasted_iota(jnp.int32, (t, t), 1)
    bias_n = (-LOG2E * slope) * jnp.abs(pq_n - pk_n).astype(F32)
    vis_n = (pk_n // CHUNK) <= (pq_n // CHUNK)
    ck = ck_ref[0, 0, 0].astype(BF16)
    cv = cv_ref[0, 0, 0].astype(BF16)
    v = v_ref[0, 0]
    outs = []
    for c in range(2):
        half = (lane < HEAD_DIM) if c == 0 else (lane >= HEAD_DIM)
        qz = jnp.where(half, q_ref[0, c], jnp.zeros((), BF16))
        kz = jnp.where(half, k_ref[0, c], jnp.zeros((), BF16))
        s_c = jnp.dot(qz, ck, preferred_element_type=F32)
        s_c = jnp.where(vis_c, s_c + bias_c, NEG_INF)
        s_n = lax.dot_general(qz, kz, _NT, preferred_element_type=F32)
        s_n = jnp.where(vis_n, s_n + bias_n, NEG_INF)
        m = jnp.maximum(jnp.max(s_c, axis=1, keepdims=True), jnp.max(s_n, axis=1, keepdims=True))
        p_c = jnp.exp2(s_c - m)
        p_n = jnp.exp2(s_n - m)
        denom = jnp.sum(p_c, axis=1, keepdims=True) + jnp.sum(p_n, axis=1, keepdims=True)
        o = jnp.dot(p_c.astype(BF16), cv, preferred_element_type=F32)
        o = o + jnp.dot(p_n.astype(BF16), v, preferred_element_type=F32)
        outs.append(o / denom)
    lam = _lambda(lam_ref, lam_init)
    o_ref[0] = (_diff_normed(outs[0], outs[1], lam, 1) * g_ref[...] * (1.0 - lam_init)).astype(BF16)


def _diff_sample(dqa, dka, dvb, cache_k, cache_v, lam_vecs, subln_g, lam_init):
    b, nm, t, _ = dqa.shape
    past = cache_k.shape[3]
    ck = jnp.transpose(cache_k, (0, 1, 2, 4, 5, 3)).reshape(cache_k.shape[:3] + (DIFF_VDIM, past))
    new = pl.BlockSpec((1, 2, t, LANES), lambda i, j: (i, j, 0, 0))
    cache = pl.BlockSpec((1, 1, 1, past, DIFF_VDIM), lambda i, j: (0, i, j, 0, 0))
    cache_t = pl.BlockSpec((1, 1, 1, DIFF_VDIM, past), lambda i, j: (0, i, j, 0, 0))
    return pl.pallas_call(
        functools.partial(_diff_sample_kernel, t=t, past=past, lam_init=lam_init),
        grid=(b, nm // 2),
        in_specs=[new, new, pl.BlockSpec((1, 1, t, DIFF_VDIM), lambda i, j: (i, j, 0, 0)),
                  cache_t, cache,
                  pl.BlockSpec(lam_vecs.shape, lambda i, j: (0, 0)),
                  pl.BlockSpec(subln_g.shape, lambda i, j: (0, 0))],
        out_specs=pl.BlockSpec((1, t, DIFF_VDIM), lambda i, j: (i, 0, j)),
        out_shape=jax.ShapeDtypeStruct((b, t, DIFF_WIDTH), BF16),
        compiler_params=_params(("arbitrary", "arbitrary")),
        name="diff_sample",
    )(dqa, dka, dvb, ck, cache_v, lam_vecs, subln_g)


def _out_proj_kernel(fo_ref, do_ref, fg_ref, dg_ref, x_ref, mod_ref, w_ref, g_ref, b_ref, y_ref, *, alpha):
    def gated(o_ref, gate_ref):
        return o_ref[0] * gate_ref[0]

    branch = jnp.dot(gated(fo_ref, fg_ref), w_ref[0:FOX_WIDTH, :], preferred_element_type=F32)
    branch = branch + jnp.dot(gated(do_ref, dg_ref), w_ref[FOX_WIDTH:, :], preferred_element_type=F32)
    r = alpha * x_ref[0] + mod_ref[0, 2:3, :] * branch
    mu = jnp.mean(r, axis=-1, keepdims=True)
    rc = r - mu
    var = jnp.mean(rc * rc, axis=-1, keepdims=True)
    y_ref[0] = rc * lax.rsqrt(var + LN_EPS) * g_ref[...] + b_ref[...]


def _out_proj(fox_o, diff_o, fg, dg, x, mod, w_out, ln_g, ln_b, alpha):
    b, t, d = x.shape
    tm = min(t, 512)
    rows = pl.BlockSpec((1, tm, FOX_WIDTH), lambda i, j: (i, j, 0))
    wide = pl.BlockSpec((1, tm, d), lambda i, j: (i, j, 0))

    def full(a):
        return pl.BlockSpec(a.shape, lambda i, j: (0,) * a.ndim)

    return pl.pallas_call(
        functools.partial(_out_proj_kernel, alpha=alpha),
        grid=(b, t // tm),
        in_specs=[rows, rows, rows, rows, wide,
                  pl.BlockSpec((1, 3, d), lambda i, j: (i, 0, 0)),
                  full(w_out), full(ln_g), full(ln_b)],
        out_specs=wide,
        out_shape=jax.ShapeDtypeStruct((b, t, d), F32),
        compiler_params=_params(("arbitrary", "arbitrary")),
        name="out_proj",
    )(fox_o, diff_o, fg, dg, x, mod, w_out, ln_g, ln_b)


def _pack_w_in(w_in):
    fw = FOX_WIDTH
    o = 0
    fq = w_in[:, o:o + fw]; o += fw
    fk = w_in[:, o:o + fw]; o += fw
    fv = w_in[:, o:o + fw]; o += fw
    ff = w_in[:, o:o + FOX_HEADS]; o += FOX_HEADS
    fg = w_in[:, o:o + fw]; o += fw
    dq = w_in[:, o:o + fw]; o += fw
    dk = w_in[:, o:o + fw]; o += fw
    dv = w_in[:, o:o + fw]; o += fw
    dg = w_in[:, o:o + fw]
    qs = HEAD_DIM ** -0.5 * LOG2E
    return jnp.concatenate([fq * qs, fk, fv, fg, dq * qs, dk, dv, dg,
                            jnp.tile(ff, (1, LANES // FOX_HEADS))], axis=1).astype(BF16)


def kernel(x_prompt, x_sample, cache_fox_k, cache_fox_v, cache_fox_logf, cache_diff_k, cache_diff_v,
           c_prompt, c_sample, w_ada, b_ada, w_in, b_f, lambda_q1, lambda_k1, lambda_q2, lambda_k2,
           subln_g, w_out, ln_g, ln_b):
    depth = w_in.shape[0]
    assert depth == 1, "single-layer step"
    layer = 0
    lam_init = 0.8 - 0.6 * math.exp(-0.3 * layer)
    alpha = (2 * depth) ** 0.25
    bp, bs = c_prompt.shape[0], c_sample.shape[0]
    d = x_prompt.shape[-1]
    past = cache_fox_k.shape[3]

    c_all = jnp.concatenate([c_prompt, c_sample, jnp.zeros((-(bp + bs) % 8, d), F32)], axis=0)
    mod = _ada(c_all, w_ada[layer], b_ada[layer]).reshape(c_all.shape[0], 3, d)
    mod_p, mod_s = mod[:bp], mod[bp:bp + bs]

    w = _pack_w_in(w_in[layer])
    bf16x = jnp.tile(b_f[layer], LANES // FOX_HEADS).reshape(1, LANES)
    seg = np.repeat(np.eye(FOX_WIDTH // HEAD_DIM, LANES, dtype=np.float32), HEAD_DIM, axis=0)
    consts = tuple(jnp.asarray(a, BF16 if a.shape[0] > 1 else F32) for a in _fox_placement() + (seg,))
    lam_vecs = jnp.stack([lambda_q1[layer], lambda_k1[layer], lambda_q2[layer], lambda_k2[layer]])
    g = subln_g[layer].reshape(1, DIFF_VDIM)
    w_o = w_out[layer].astype(BF16)
    lg, lb = ln_g[layer].reshape(1, d), ln_b[layer].reshape(1, d)

    def run(x, mod_x, pos_offset, attend):
        (fk, fv, dk, dv, logf, fqa, fka, fva, fg, dqa, dka, dvb, dg, stats) = _in_proj(
            x, mod_x, w, bf16x, consts, pos_offset)
        stats = jnp.transpose(stats[..., :FOX_WIDTH // HEAD_DIM], (0, 3, 2, 1))
        fox_o, diff_o = attend(fqa, fka, fva, dqa, dka, dvb, stats)
        y = _out_proj(fox_o, diff_o, fg, dg, x, mod_x, w_o, lg, lb, alpha)
        b, t = x.shape[:2]
        states = (fk[None], fv[None], jnp.swapaxes(logf[:, :, :FOX_HEADS], 1, 2)[None],
                  dk.reshape(1, b, DIFF_HEADS, t, 2, HEAD_DIM), dv[None])
        return y, states

    def prompt_attend(fqa, fka, fva, dqa, dka, dvb, stats):
        return (_fox_attention(fqa, fka, fva, stats),
                _diff_attention(dqa, dka, dvb, stats, lam_vecs, g, lam_init))

    def sample_attend(fqa, fka, fva, dqa, dka, dvb, stats):
        del stats
        return (_fox_sample(fqa, fka, fva, cache_fox_k, cache_fox_v, cache_fox_logf[layer]),
                _diff_sample(dqa, dka, dvb, cache_diff_k, cache_diff_v, lam_vecs, g, lam_init))

    yp, sp = run(x_prompt, mod_p, 0, prompt_attend)
    ys, ss = run(x_sample, mod_s, past, sample_attend)
    return (yp, ys) + sp + ss
```

```python
import functools
import math

import jax
import jax.numpy as jnp
import numpy as np
from jax import lax
from jax.experimental import pallas as pl
from jax.experimental.pallas import tpu as pltpu

F32 = jnp.float32
BF16 = jnp.bfloat16

HEAD_DIM = 64
FOX_HEADS = 8
DIFF_HEADS = 4
DIFF_VDIM = 2 * HEAD_DIM
FOX_WIDTH = FOX_HEADS * HEAD_DIM
DIFF_WIDTH = DIFF_HEADS * DIFF_VDIM
CHUNK = 64
LN_EPS = 1e-5
RMS_EPS = 1e-5
LANES = 128
NEG_INF = float("-inf")
LOG2E = math.log2(math.e)

_SEC = {name: i * FOX_WIDTH for i, name in enumerate(("fq", "fk", "fv", "fg", "dq", "dk", "dv", "dg"))}
_FF_OFF = 8 * FOX_WIDTH
_W_COLS = _FF_OFF + LANES

_VMEM_LIMIT = 56 * 1024 * 1024


def _params(sem):
    return pltpu.CompilerParams(dimension_semantics=sem, vmem_limit_bytes=_VMEM_LIMIT)


def _ada_kernel(c_ref, w_ref, b_ref, o_ref):
    c = c_ref[...]
    s = c * jax.nn.sigmoid(c)
    o_ref[...] = jnp.dot(s, w_ref[...], preferred_element_type=F32) + b_ref[...]


def _ada(c_all, w_ada, b_ada):
    rows, d = c_all.shape
    n = w_ada.shape[1]
    tn = 512
    return pl.pallas_call(
        _ada_kernel,
        grid=(n // tn,),
        in_specs=[pl.BlockSpec((rows, d), lambda j: (0, 0)),
                  pl.BlockSpec((d, tn), lambda j: (0, j)),
                  pl.BlockSpec((1, tn), lambda j: (0, j))],
        out_specs=pl.BlockSpec((rows, tn), lambda j: (0, j)),
        out_shape=jax.ShapeDtypeStruct((rows, n), F32),
        compiler_params=_params(("arbitrary",)),
        name="ada",
    )(c_all, w_ada, b_ada.reshape(1, n))


def _split3(x):
    hi = x.astype(BF16)
    r1 = x - hi.astype(F32)
    mid = r1.astype(BF16)
    lo = (r1 - mid.astype(F32)).astype(BF16)
    return hi, mid, lo


def _fox_placement():
    pq = np.zeros((LANES, FOX_WIDTH), np.float32)
    pk = np.zeros((LANES, FOX_WIDTH), np.float32)
    oq = np.zeros((1, FOX_WIDTH), np.float32)
    ok = np.zeros((1, FOX_WIDTH), np.float32)
    for h in range(FOX_HEADS):
        base = (h // 2) * LANES + (HEAD_DIM if h % 2 == 0 else 0)
        for p in range(3):
            pq[8 * p + h, base + p] = 1.0
            ok[0, base + p] = 1.0
            oq[0, base + 3 + p] = 1.0
            pk[8 * p + h, base + 3 + p] = -1.0
    return pq, pk, oq, ok


def _in_proj_kernel(x_ref, mod_ref, w_ref, bf_ref, pq_ref, pk_ref, oq_ref, ok_ref, seg_ref,
                    fk_ref, fv_ref, dk_ref, dv_ref, logf_ref,
                    fqa_ref, fka_ref, fva_ref, fg_ref, dqa_ref, dka_ref, dvb_ref, dg_ref, stat_ref,
                    carry_ref, *, tm, pos_offset):
    ti = pl.program_id(1)
    x = x_ref[0]
    shift = mod_ref[0, 0:1, :]
    scale = mod_ref[0, 1:2, :]
    h = (x * (1.0 + scale) + shift).astype(BF16)

    def proj(name, width=FOX_WIDTH):
        off = _SEC[name] if name in _SEC else _FF_OFF
        return jnp.dot(h, w_ref[:, off:off + width], preferred_element_type=F32)

    lane = lax.broadcasted_iota(jnp.int32, (tm, LANES), 1)
    row = lax.broadcasted_iota(jnp.int32, (tm, LANES), 0)
    lower = lane < HEAD_DIM
    grp = (lane // 8) % 4

    def by_group(a, b, c):
        return jnp.where(grp == 0, a, jnp.where(grp == 1, b, jnp.where(grp == 2, c, jnp.zeros_like(a))))

    def max_sq_norm(z):
        zr = z.astype(BF16).astype(F32)
        sq = jnp.dot((zr * zr).astype(BF16), seg_ref[...], preferred_element_type=F32)
        return jnp.max(sq, axis=0, keepdims=True)

    logf = jax.nn.log_sigmoid(proj("ff", LANES) + bf_ref[...])
    logf_ref[0] = logf
    tri_r = lax.broadcasted_iota(jnp.int32, (tm, tm), 0)
    tri_c = lax.broadcasted_iota(jnp.int32, (tm, tm), 1)
    tri = (tri_c <= tri_r).astype(BF16)
    part = jnp.dot(tri, by_group(*_split3(logf)), preferred_element_type=F32)
    local = part + pltpu.roll(part, 8, 1) + pltpu.roll(part, 16, 1) + pltpu.roll(part, 24, 1)

    @pl.when(ti == 0)
    def _():
        carry_ref[...] = jnp.zeros_like(carry_ref)

    cum = local + carry_ref[...]
    carry_ref[...] = cum[tm - 1:tm, :]
    cum2 = cum * LOG2E
    cum_cat = by_group(*_split3(cum2))
    eq = jnp.dot(cum_cat, pq_ref[...], preferred_element_type=F32) + oq_ref[...]
    ek = jnp.dot(cum_cat, pk_ref[...], preferred_element_type=F32) + ok_ref[...]

    zq, zk, zv = proj("fq"), proj("fk"), proj("fv")
    fox_norms = [max_sq_norm(zq), max_sq_norm(zk)]
    e_even = (lane == HEAD_DIM).astype(F32)
    for j in range(FOX_HEADS // 2):
        sl = slice(j * LANES, (j + 1) * LANES)
        fqa_ref[0, 2 * j] = jnp.where(lower, zq[:, sl], eq[:, sl]).astype(BF16)
        fqa_ref[0, 2 * j + 1] = jnp.where(lower, eq[:, sl], zq[:, sl]).astype(BF16)
        fka_ref[0, 2 * j] = jnp.where(lower, zk[:, sl], ek[:, sl]).astype(BF16)
        fka_ref[0, 2 * j + 1] = jnp.where(lower, ek[:, sl], zk[:, sl]).astype(BF16)
        fva_ref[0, 2 * j] = jnp.where(lower, zv[:, sl], e_even).astype(BF16)
        fva_ref[0, 2 * j + 1] = jnp.where(lower, pltpu.roll(zv[:, sl], HEAD_DIM, 1), e_even).astype(BF16)
    for hd in range(FOX_HEADS):
        fk_ref[0, hd] = zk[:, hd * HEAD_DIM:(hd + 1) * HEAD_DIM]
        fv_ref[0, hd] = zv[:, hd * HEAD_DIM:(hd + 1) * HEAD_DIM]
    fg_ref[0] = jax.nn.silu(proj("fg")).astype(BF16)

    zq, zk, zv = proj("dq"), proj("dk"), proj("dv")
    pos = (row + (ti * tm + pos_offset)).astype(F32)
    sub = lane % HEAD_DIM
    for hd in range(DIFF_HEADS):
        slope = 2.0 ** (-8.0 * (hd + 1) / DIFF_HEADS)
        sl = slice(hd * LANES, (hd + 1) * LANES)
        p0, p1, p2 = (a.astype(F32) for a in _split3((slope * LOG2E) * pos))
        pieces = jnp.where(sub % 3 == 0, p0, jnp.where(sub % 3 == 1, p1, p2))
        ext_q = jnp.where(sub < 3, -pieces, (sub < 6).astype(F32))
        ext_k = jnp.where(sub < 3, 1.0, jnp.where(sub < 6, pieces, 0.0))
        dqa_ref[0, 2 * hd] = jnp.where(lower, zq[:, sl], ext_q).astype(BF16)
        dqa_ref[0, 2 * hd + 1] = jnp.where(lower, ext_q, zq[:, sl]).astype(BF16)
        dka_ref[0, 2 * hd] = jnp.where(lower, zk[:, sl], ext_k).astype(BF16)
        dka_ref[0, 2 * hd + 1] = jnp.where(lower, ext_k, zk[:, sl]).astype(BF16)
        dk_ref[0, hd] = zk[:, sl]
        dv_ref[0, hd] = zv[:, sl]
        dvb_ref[0, hd] = zv[:, sl].astype(BF16)
    dg_ref[0] = jax.nn.silu(proj("dg")).astype(BF16)
    stat_ref[0, 0] = jnp.concatenate(
        fox_norms + [max_sq_norm(zq), max_sq_norm(zk), cum2[0:1], cum2[tm - 1:tm], jnp.zeros((2, LANES), F32)],
        axis=0)


def _in_proj(x, mod, w, bf16x, consts, pos_offset):
    b, t, d = x.shape
    tm = min(t, 512)
    nt = t // tm
    pq, pk, oq, ok, seg = consts

    def full(a):
        return pl.BlockSpec(a.shape, lambda i, j: (0,) * a.ndim)

    def heads(n, width):
        return pl.BlockSpec((1, n, tm, width), lambda i, j: (i, 0, j, 0))

    rows = pl.BlockSpec((1, tm, FOX_WIDTH), lambda i, j: (i, j, 0))
    out_shape = (
        jax.ShapeDtypeStruct((b, FOX_HEADS, t, HEAD_DIM), F32),
        jax.ShapeDtypeStruct((b, FOX_HEADS, t, HEAD_DIM), F32),
        jax.ShapeDtypeStruct((b, DIFF_HEADS, t, DIFF_VDIM), F32),
        jax.ShapeDtypeStruct((b, DIFF_HEADS, t, DIFF_VDIM), F32),
        jax.ShapeDtypeStruct((b, t, LANES), F32),
        jax.ShapeDtypeStruct((b, FOX_HEADS, t, LANES), BF16),
        jax.ShapeDtypeStruct((b, FOX_HEADS, t, LANES), BF16),
        jax.ShapeDtypeStruct((b, FOX_HEADS, t, LANES), BF16),
        jax.ShapeDtypeStruct((b, t, FOX_WIDTH), BF16),
        jax.ShapeDtypeStruct((b, 2 * DIFF_HEADS, t, LANES), BF16),
        jax.ShapeDtypeStruct((b, 2 * DIFF_HEADS, t, LANES), BF16),
        jax.ShapeDtypeStruct((b, DIFF_HEADS, t, DIFF_VDIM), BF16),
        jax.ShapeDtypeStruct((b, t, DIFF_WIDTH), BF16),
        jax.ShapeDtypeStruct((b, nt, 8, LANES), F32),
    )
    out_specs = (
        heads(FOX_HEADS, HEAD_DIM), heads(FOX_HEADS, HEAD_DIM),
        heads(DIFF_HEADS, DIFF_VDIM), heads(DIFF_HEADS, DIFF_VDIM),
        pl.BlockSpec((1, tm, LANES), lambda i, j: (i, j, 0)),
        heads(FOX_HEADS, LANES), heads(FOX_HEADS, LANES), heads(FOX_HEADS, LANES), rows,
        heads(2 * DIFF_HEADS, LANES), heads(2 * DIFF_HEADS, LANES), heads(DIFF_HEADS, DIFF_VDIM), rows,
        pl.BlockSpec((1, 1, 8, LANES), lambda i, j: (i, j, 0, 0)),
    )
    return pl.pallas_call(
        functools.partial(_in_proj_kernel, tm=tm, pos_offset=pos_offset),
        grid=(b, nt),
        in_specs=[pl.BlockSpec((1, tm, d), lambda i, j: (i, j, 0)),
                  pl.BlockSpec((1, 3, d), lambda i, j: (i, 0, 0)),
                  full(w), full(bf16x), full(pq), full(pk), full(oq), full(ok), full(seg)],
        out_specs=out_specs,
        out_shape=out_shape,
        scratch_shapes=[pltpu.VMEM((1, LANES), F32)],
        compiler_params=_params(("arbitrary", "arbitrary")),
        name="in_proj",
    )(x, mod, w, bf16x, pq, pk, oq, ok, seg)


_NT = (((1,), (1,)), ((), ()))
_TN = (((0,), (0,)), ((), ()))
_STAGES = 4
_FOX_V_ROWS = HEAD_DIM + 16
_FOX_GROUP = 4


_SKIP_MARGIN = 152.0
_STAT_FOX_Q, _STAT_FOX_K, _STAT_DIFF_Q, _STAT_DIFF_K, _STAT_CUM_FIRST, _STAT_CUM_LAST = range(6)


def _diagonal_blocks(tq):
    assert tq % (2 * LANES) == 0
    return ((tq // 2, 0), (tq, tq // 2))


def _dot_bound(stat, q_row, k_row, qi):
    tile = lax.broadcasted_iota(jnp.int32, (1, stat.shape[-1]), 1)
    qn2 = jnp.sum(jnp.where(tile == qi, stat[q_row:q_row + 1], 0.0))
    return jnp.sqrt(qn2 * stat[k_row:k_row + 1]) * 1.01 + 1.0


def _first_needed_tile(qi, bounds, m_ref):
    first = qi
    for c, bound in enumerate(bounds):
        tile = lax.broadcasted_iota(jnp.int32, bound.shape, 1)
        needed = (bound >= jnp.min(m_ref[c]) - _SKIP_MARGIN) & (tile < qi)
        first = jnp.minimum(first, jnp.min(jnp.where(needed, tile, qi)))
    return first


def _pipelined_tiles(qi, n_chains, scores, absorb, first_needed, sa_ref, sb_ref):
    chains = range(n_chains)
    bufs = (sa_ref, sb_ref)
    for c in chains:
        scores(qi, sa_ref, c, True)
        scores(jnp.maximum(qi - 1, 0), sb_ref, c)
        absorb(sa_ref, qi, c, True)
    n = qi - first_needed()

    def stages(top, count, more):
        for s in range(count):
            for c in chains:
                if more or s + 1 < count:
                    scores(jnp.maximum(top - s - 1, 0), bufs[s % 2], c)
                absorb(bufs[(s + 1) % 2], top - s, c, False)

    def body(i, carry):
        stages(qi - 1 - _STAGES * i, _STAGES, True)
        return carry

    trips = jnp.maximum(n - 1, 0) // _STAGES
    lax.fori_loop(0, trips, body, 0)
    rem = n - _STAGES * trips
    for count in range(1, _STAGES + 1):
        @pl.when(rem == count)
        def _(count=count):
            stages(qi - 1 - _STAGES * trips, count, False)


def _fox_kernel(q_ref, k_ref, v_ref, stat_ref, o_ref, sa_ref, sb_ref, m_ref, acc_ref, *, tq):
    qi = pl.program_id(2)
    half = tq // 2

    def scores(ki, dst, hh, diagonal=False):
        start = pl.multiple_of(ki * tq, tq)
        if diagonal:
            for keys, q0 in _diagonal_blocks(tq):
                k = k_ref[0, hh, pl.ds(start, keys), :]
                dst[hh, 0:keys, q0:q0 + half] = lax.dot_general(k, q_ref[0, hh, q0:q0 + half, :], _NT,
                                                                preferred_element_type=F32)
        else:
            k = k_ref[0, hh, pl.ds(start, tq), :]
            dst[hh] = lax.dot_general(k, q_ref[0, hh], _NT, preferred_element_type=F32)

    def absorb(src, ki, hh, diagonal):
        start = pl.multiple_of(ki * tq, tq)
        if diagonal:
            for keys, q0 in _diagonal_blocks(tq):
                s = src[hh, 0:keys, q0:q0 + half]
                key = lax.broadcasted_iota(jnp.int32, s.shape, 0)
                qry = lax.broadcasted_iota(jnp.int32, s.shape, 1) + q0
                s = jnp.where(key <= qry, s, NEG_INF)
                m_new = jnp.max(s, axis=0, keepdims=True)
                p = jnp.exp2(s - m_new)
                v = v_ref[0, hh, pl.ds(start, keys), 0:_FOX_V_ROWS]
                acc_ref[hh, :, q0:q0 + half] = lax.dot_general(v, p.astype(BF16), _TN, preferred_element_type=F32)
                m_ref[hh, :, q0:q0 + half] = m_new
            return
        s = src[hh]
        m_prev = m_ref[hh]
        m_new = jnp.maximum(m_prev, jnp.max(s, axis=0, keepdims=True))
        p = jnp.exp2(s - m_new)
        alpha = jnp.exp2(m_prev - m_new)
        v = v_ref[0, hh, pl.ds(start, tq), 0:_FOX_V_ROWS]
        pv = lax.dot_general(v, p.astype(BF16), _TN, preferred_element_type=F32)
        acc_ref[hh] = alpha * acc_ref[hh] + pv
        m_ref[hh] = m_new

    def first_needed():
        bounds = []
        for hh in range(_FOX_GROUP):
            stat = stat_ref[0, hh]
            tile = lax.broadcasted_iota(jnp.int32, (1, stat.shape[-1]), 1)
            cum_q = jnp.sum(jnp.where(tile == qi, stat[_STAT_CUM_FIRST:_STAT_CUM_FIRST + 1], 0.0))
            decay = cum_q - stat[_STAT_CUM_LAST:_STAT_CUM_LAST + 1]
            bounds.append(_dot_bound(stat, _STAT_FOX_Q, _STAT_FOX_K, qi) + decay)
        return _first_needed_tile(qi, bounds, m_ref)

    _pipelined_tiles(qi, _FOX_GROUP, scores, absorb, first_needed, sa_ref, sb_ref)
    outs = [acc_ref[hh, 0:HEAD_DIM] / acc_ref[hh, HEAD_DIM:HEAD_DIM + 1] for hh in range(_FOX_GROUP)]
    o_ref[0] = jnp.concatenate(outs, axis=0).T.astype(BF16)


def _fox_attention(fqa, fka, fva, stats):
    b, nh, t, _ = fqa.shape
    tq = min(t, 512)
    g = _FOX_GROUP
    return pl.pallas_call(
        functools.partial(_fox_kernel, tq=tq),
        grid=(b, nh // g, t // tq),
        in_specs=[pl.BlockSpec((1, g, tq, LANES), lambda i, j, q: (i, j, q, 0)),
                  pl.BlockSpec((1, g, t, LANES), lambda i, j, q: (i, j, 0, 0)),
                  pl.BlockSpec((1, g, t, LANES), lambda i, j, q: (i, j, 0, 0)),
                  pl.BlockSpec((1, g) + stats.shape[2:], lambda i, j, q: (i, j, 0, 0))],
        out_specs=pl.BlockSpec((1, tq, g * HEAD_DIM), lambda i, j, q: (i, q, j)),
        out_shape=jax.ShapeDtypeStruct((b, t, FOX_WIDTH), BF16),
        scratch_shapes=[pltpu.VMEM((g, tq, tq), F32), pltpu.VMEM((g, tq, tq), F32),
                        pltpu.VMEM((g, 1, tq), F32), pltpu.VMEM((g, _FOX_V_ROWS, tq), F32)],
        compiler_params=_params(("arbitrary", "arbitrary", "arbitrary")),
        name="fox_attn",
    )(fqa, fka, fva, stats)


def _alibi_slope(head):
    return jnp.exp2(jnp.full((1, 1), -8.0 / DIFF_HEADS, F32) * (head + 1).astype(F32))


def _lambda(lam_ref, lam_init):
    lv = lam_ref[...]
    e1 = jnp.exp(jnp.sum(lv[0:1] * lv[1:2], axis=1, keepdims=True))
    e2 = jnp.exp(jnp.sum(lv[2:3] * lv[3:4], axis=1, keepdims=True))
    return e1 - e2 + lam_init


def _diff_normed(o0, o1, lam, axis):
    df = o0 - lam * o1
    return df * lax.rsqrt(jnp.mean(df * df, axis=axis, keepdims=True) + RMS_EPS)


_ONES_ROWS = 16


def _diff_kernel(q_ref, k_ref, v_ref, stat_ref, lam_ref, g_ref, o_ref, sa_ref, sb_ref, m_ref, acc_ref, *, tq,
                 lam_init):
    hd = pl.program_id(1)
    qi = pl.program_id(2)
    slope = _alibi_slope(hd)
    half = tq // 2

    def scores(ki, dst, c, diagonal=False):
        start = pl.multiple_of(ki * tq, tq)
        if diagonal:
            for keys, q0 in _diagonal_blocks(tq):
                k = k_ref[0, c, pl.ds(start, keys), :]
                dst[c, 0:keys, q0:q0 + half] = lax.dot_general(k, q_ref[0, c, q0:q0 + half, :], _NT,
                                                               preferred_element_type=F32)
        else:
            k = k_ref[0, c, pl.ds(start, tq), :]
            dst[c] = lax.dot_general(k, q_ref[0, c], _NT, preferred_element_type=F32)

    def values_t(start, keys):
        v = v_ref[0, 0, pl.ds(start, keys), :]
        return jnp.concatenate([v.T, jnp.ones((_ONES_ROWS, keys), BF16)], axis=0)

    def absorb(src, ki, c, diagonal):
        start = pl.multiple_of(ki * tq, tq)
        if diagonal:
            for keys, q0 in _diagonal_blocks(tq):
                s = src[c, 0:keys, q0:q0 + half]
                key = lax.broadcasted_iota(jnp.int32, s.shape, 0)
                qry = lax.broadcasted_iota(jnp.int32, s.shape, 1) + q0
                ahead = jnp.maximum(key - qry, 0).astype(F32)
                s = jnp.where((key // CHUNK) <= (qry // CHUNK), s - (2.0 * LOG2E * slope) * ahead, NEG_INF)
                m_new = jnp.max(s, axis=0, keepdims=True)
                p = jnp.exp2(s - m_new)
                acc_ref[c, :, q0:q0 + half] = jnp.dot(values_t(start, keys), p.astype(BF16),
                                                      preferred_element_type=F32)
                m_ref[c, :, q0:q0 + half] = m_new
            return
        s = src[c]
        m_prev = m_ref[c]
        m_new = jnp.maximum(m_prev, jnp.max(s, axis=0, keepdims=True))
        p = jnp.exp2(s - m_new)
        alpha = jnp.exp2(m_prev - m_new)
        acc_ref[c] = alpha * acc_ref[c] + jnp.dot(values_t(start, tq), p.astype(BF16), preferred_element_type=F32)
        m_ref[c] = m_new

    def first_needed():
        bounds = []
        for c in range(2):
            stat = stat_ref[0, c]
            tile = lax.broadcasted_iota(jnp.int32, (1, stat.shape[-1]), 1)
            gap = ((qi - 1 - tile) * tq + 1).astype(F32)
            bounds.append(_dot_bound(stat, _STAT_DIFF_Q, _STAT_DIFF_K, qi) - (LOG2E * slope) * gap)
        return _first_needed_tile(qi, bounds, m_ref)

    _pipelined_tiles(qi, 2, scores, absorb, first_needed, sa_ref, sb_ref)
    lam = _lambda(lam_ref, lam_init)
    acc0, acc1 = acc_ref[0], acc_ref[1]
    o0 = acc0[:DIFF_VDIM] / acc0[DIFF_VDIM:DIFF_VDIM + 1]
    o1 = acc1[:DIFF_VDIM] / acc1[DIFF_VDIM:DIFF_VDIM + 1]
    o_ref[0] = (_diff_normed(o0, o1, lam, 0).T * g_ref[...] * (1.0 - lam_init)).astype(BF16)


def _diff_attention(dqa, dka, dvb, stats, lam_vecs, subln_g, lam_init):
    b, nm, t, _ = dqa.shape
    tq = min(t, 512)
    return pl.pallas_call(
        functools.partial(_diff_kernel, tq=tq, lam_init=lam_init),
        grid=(b, nm // 2, t // tq),
        in_specs=[pl.BlockSpec((1, 2, tq, LANES), lambda i, j, q: (i, j, q, 0)),
                  pl.BlockSpec((1, 2, t, LANES), lambda i, j, q: (i, j, 0, 0)),
                  pl.BlockSpec((1, 1, t, DIFF_VDIM), lambda i, j, q: (i, j, 0, 0)),
                  pl.BlockSpec((1, 2) + stats.shape[2:], lambda i, j, q: (i, j, 0, 0)),
                  pl.BlockSpec(lam_vecs.shape, lambda i, j, q: (0, 0)),
                  pl.BlockSpec(subln_g.shape, lambda i, j, q: (0, 0))],
        out_specs=pl.BlockSpec((1, tq, DIFF_VDIM), lambda i, j, q: (i, q, j)),
        out_shape=jax.ShapeDtypeStruct((b, t, DIFF_WIDTH), BF16),
        scratch_shapes=[pltpu.VMEM((2, tq, tq), F32), pltpu.VMEM((2, tq, tq), F32),
                        pltpu.VMEM((2, 1, tq), F32), pltpu.VMEM((2, DIFF_VDIM + _ONES_ROWS, tq), F32)],
        compiler_params=_params(("arbitrary", "arbitrary", "arbitrary")),
        name="diff_attn",
    )(dqa, dka, dvb, stats, lam_vecs, subln_g)


def _prefix_sum_lanes(x):
    n = x.shape[-1]
    lane = lax.broadcasted_iota(jnp.int32, x.shape, x.ndim - 1)
    sh = 1
    while sh < n:
        x = x + jnp.where(lane >= sh, pltpu.roll(x, sh, x.ndim - 1), 0.0)
        sh *= 2
    return x


def _fox_sample_kernel(q_ref, k_ref, v_ref, ck_ref, cv_ref, clf_ref, o_ref, *, t, past):
    r = lax.broadcasted_iota(jnp.int32, (t, t), 0)
    c = lax.broadcasted_iota(jnp.int32, (t, t), 1)
    for hh in range(2):
        data = slice(0, HEAD_DIM) if hh == 0 else slice(HEAD_DIM, LANES)
        ext = HEAD_DIM if hh == 0 else 0
        qa = q_ref[0, hh]
        qf = qa.astype(F32)
        cum_q = qf[:, ext:ext + 1] + qf[:, ext + 1:ext + 2] + qf[:, ext + 2:ext + 3]
        cum_c = _prefix_sum_lanes(clf_ref[0, hh])
        suffix = cum_c[:, past - 1:past] - cum_c
        s_c = jnp.dot(qa[:, data], ck_ref[0, 0, hh].astype(BF16), preferred_element_type=F32)
        s_c = s_c + cum_q + suffix * LOG2E
        s_n = lax.dot_general(qa, k_ref[0, hh], _NT, preferred_element_type=F32)
        s_n = jnp.where(c <= r, s_n, NEG_INF)
        m = jnp.maximum(jnp.max(s_c, axis=1, keepdims=True), jnp.max(s_n, axis=1, keepdims=True))
        p_c = jnp.exp2(s_c - m)
        p_n = jnp.exp2(s_n - m)
        denom = jnp.sum(p_c, axis=1, keepdims=True) + jnp.sum(p_n, axis=1, keepdims=True)
        o = lax.dot_general(p_c.astype(BF16), cv_ref[0, 0, hh].astype(BF16), _NT, preferred_element_type=F32)
        o = o + jnp.dot(p_n.astype(BF16), v_ref[0, hh][:, 0:HEAD_DIM], preferred_element_type=F32)
        o_ref[0, :, data] = (o / denom).astype(BF16)


def _fox_sample(fqa, fka, fva, cache_k, cache_v, cache_logf):
    b, nh, t, _ = fqa.shape
    past = cache_k.shape[3]
    clf = cache_logf.reshape(b, nh, 1, past)
    cache_k, cache_v = jnp.swapaxes(cache_k, 3, 4), jnp.swapaxes(cache_v, 3, 4)
    new = pl.BlockSpec((1, 2, t, LANES), lambda i, j: (i, j, 0, 0))
    cache = pl.BlockSpec((1, 1, 2, HEAD_DIM, past), lambda i, j: (0, i, j, 0, 0))
    return pl.pallas_call(
        functools.partial(_fox_sample_kernel, t=t, past=past),
        grid=(b, nh // 2),
        in_specs=[new, new, new, cache, cache,
                  pl.BlockSpec((1, 2, 1, past), lambda i, j: (i, j, 0, 0))],
        out_specs=pl.BlockSpec((1, t, LANES), lambda i, j: (i, 0, j)),
        out_shape=jax.ShapeDtypeStruct((b, t, FOX_WIDTH), BF16),
        compiler_params=_params(("arbitrary", "arbitrary")),
        name="fox_sample",
    )(fqa, fka, fva, cache_k, cache_v, clf)


def _diff_sample_kernel(q_ref, k_ref, v_ref, ck_ref, cv_ref, lam_ref, g_ref, o_ref, *, t, past, lam_init):
    hd = pl.program_id(1)
    slope = _alibi_slope(hd)
    lane = lax.broadcasted_iota(jnp.int32, (t, LANES), 1)
    pos_q = past + lax.broadcasted_iota(jnp.int32, (t, past), 0)
    pos_c = lax.broadcasted_iota(jnp.int32, (t, past), 1)
    bias_c = (-LOG2E * slope) * jnp.abs(pos_q - pos_c).astype(F32)
    vis_c = (pos_c // CHUNK) <= (pos_q // CHUNK)
    pq_n = past + lax.broadcasted_iota(jnp.int32, (t, t), 0)
    pk_n = past + lax.broadcasted_iota(jnp.int32, (t, t), 1)
    bias_n = (-LOG2E * slope) * jnp.abs(pq_n - pk_n).astype(F32)
    vis_n = (pk_n // CHUNK) <= (pq_n // CHUNK)
    ck = ck_ref[0, 0, 0].astype(BF16)
    cv = cv_ref[0, 0, 0].astype(BF16)
    v = v_ref[0, 0]
    outs = []
    for c in range(2):
        half = (lane < HEAD_DIM) if c == 0 else (lane >= HEAD_DIM)
        qz = jnp.where(half, q_ref[0, c], jnp.zeros((), BF16))
        kz = jnp.where(half, k_ref[0, c], jnp.zeros((), BF16))
        s_c = jnp.dot(qz, ck, preferred_element_type=F32)
        s_c = jnp.where(vis_c, s_c + bias_c, NEG_INF)
        s_n = lax.dot_general(qz, kz, _NT, preferred_element_type=F32)
        s_n = jnp.where(vis_n, s_n + bias_n, NEG_INF)
        m = jnp.maximum(jnp.max(s_c, axis=1, keepdims=True), jnp.max(s_n, axis=1, keepdims=True))
        p_c = jnp.exp2(s_c - m)
        p_n = jnp.exp2(s_n - m)
        denom = jnp.sum(p_c, axis=1, keepdims=True) + jnp.sum(p_n, axis=1, keepdims=True)
        o = jnp.dot(p_c.astype(BF16), cv, preferred_element_type=F32)
        o = o + jnp.dot(p_n.astype(BF16), v, preferred_element_type=F32)
        outs.append(o / denom)
    lam = _lambda(lam_ref, lam_init)
    o_ref[0] = (_diff_normed(outs[0], outs[1], lam, 1) * g_ref[...] * (1.0 - lam_init)).astype(BF16)


def _diff_sample(dqa, dka, dvb, cache_k, cache_v, lam_vecs, subln_g, lam_init):
    b, nm, t, _ = dqa.shape
    past = cache_k.shape[3]
    ck = jnp.transpose(cache_k, (0, 1, 2, 4, 5, 3)).reshape(cache_k.shape[:3] + (DIFF_VDIM, past))
    new = pl.BlockSpec((1, 2, t, LANES), lambda i, j: (i, j, 0, 0))
    cache = pl.BlockSpec((1, 1, 1, past, DIFF_VDIM), lambda i, j: (0, i, j, 0, 0))
    cache_t = pl.BlockSpec((1, 1, 1, DIFF_VDIM, past), lambda i, j: (0, i, j, 0, 0))
    return pl.pallas_call(
        functools.partial(_diff_sample_kernel, t=t, past=past, lam_init=lam_init),
        grid=(b, nm // 2),
        in_specs=[new, new, pl.BlockSpec((1, 1, t, DIFF_VDIM), lambda i, j: (i, j, 0, 0)),
                  cache_t, cache,
                  pl.BlockSpec(lam_vecs.shape, lambda i, j: (0, 0)),
                  pl.BlockSpec(subln_g.shape, lambda i, j: (0, 0))],
        out_specs=pl.BlockSpec((1, t, DIFF_VDIM), lambda i, j: (i, 0, j)),
        out_shape=jax.ShapeDtypeStruct((b, t, DIFF_WIDTH), BF16),
        compiler_params=_params(("arbitrary", "arbitrary")),
        name="diff_sample",
    )(dqa, dka, dvb, ck, cache_v, lam_vecs, subln_g)


def _out_proj_kernel(fo_ref, do_ref, fg_ref, dg_ref, x_ref, mod_ref, w_ref, g_ref, b_ref, y_ref, *, alpha):
    def gated(o_ref, gate_ref):
        return o_ref[0] * gate_ref[0]

    branch = jnp.dot(gated(fo_ref, fg_ref), w_ref[0:FOX_WIDTH, :], preferred_element_type=F32)
    branch = branch + jnp.dot(gated(do_ref, dg_ref), w_ref[FOX_WIDTH:, :], preferred_element_type=F32)
    r = alpha * x_ref[0] + mod_ref[0, 2:3, :] * branch
    mu = jnp.mean(r, axis=-1, keepdims=True)
    rc = r - mu
    var = jnp.mean(rc * rc, axis=-1, keepdims=True)
    y_ref[0] = rc * lax.rsqrt(var + LN_EPS) * g_ref[...] + b_ref[...]


def _out_proj(fox_o, diff_o, fg, dg, x, mod, w_out, ln_g, ln_b, alpha):
    b, t, d = x.shape
    tm = min(t, 512)
    rows = pl.BlockSpec((1, tm, FOX_WIDTH), lambda i, j: (i, j, 0))
    wide = pl.BlockSpec((1, tm, d), lambda i, j: (i, j, 0))

    def full(a):
        return pl.BlockSpec(a.shape, lambda i, j: (0,) * a.ndim)

    return pl.pallas_call(
        functools.partial(_out_proj_kernel, alpha=alpha),
        grid=(b, t // tm),
        in_specs=[rows, rows, rows, rows, wide,
                  pl.BlockSpec((1, 3, d), lambda i, j: (i, 0, 0)),
                  full(w_out), full(ln_g), full(ln_b)],
        out_specs=wide,
        out_shape=jax.ShapeDtypeStruct((b, t, d), F32),
        compiler_params=_params(("arbitrary", "arbitrary")),
        name="out_proj",
    )(fox_o, diff_o, fg, dg, x, mod, w_out, ln_g, ln_b)


def _pack_w_in(w_in):
    fw = FOX_WIDTH
    o = 0
    fq = w_in[:, o:o + fw]; o += fw
    fk = w_in[:, o:o + fw]; o += fw
    fv = w_in[:, o:o + fw]; o += fw
    ff = w_in[:, o:o + FOX_HEADS]; o += FOX_HEADS
    fg = w_in[:, o:o + fw]; o += fw
    dq = w_in[:, o:o + fw]; o += fw
    dk = w_in[:, o:o + fw]; o += fw
    dv = w_in[:, o:o + fw]; o += fw
    dg = w_in[:, o:o + fw]
    qs = HEAD_DIM ** -0.5 * LOG2E
    return jnp.concatenate([fq * qs, fk, fv, fg, dq * qs, dk, dv, dg,
                            jnp.tile(ff, (1, LANES // FOX_HEADS))], axis=1).astype(BF16)


def kernel(x_prompt, x_sample, cache_fox_k, cache_fox_v, cache_fox_logf, cache_diff_k, cache_diff_v,
           c_prompt, c_sample, w_ada, b_ada, w_in, b_f, lambda_q1, lambda_k1, lambda_q2, lambda_k2,
           subln_g, w_out, ln_g, ln_b):
    depth = w_in.shape[0]
    assert depth == 1, "single-layer step"
    layer = 0
    lam_init = 0.8 - 0.6 * math.exp(-0.3 * layer)
    alpha = (2 * depth) ** 0.25
    bp, bs = c_prompt.shape[0], c_sample.shape[0]
    d = x_prompt.shape[-1]
    past = cache_fox_k.shape[3]

    c_all = jnp.concatenate([c_prompt, c_sample, jnp.zeros((-(bp + bs) % 8, d), F32)], axis=0)
    mod = _ada(c_all, w_ada[layer], b_ada[layer]).reshape(c_all.shape[0], 3, d)
    mod_p, mod_s = mod[:bp], mod[bp:bp + bs]

    w = _pack_w_in(w_in[layer])
    bf16x = jnp.tile(b_f[layer], LANES // FOX_HEADS).reshape(1, LANES)
    seg = np.repeat(np.eye(FOX_WIDTH // HEAD_DIM, LANES, dtype=np.float32), HEAD_DIM, axis=0)
    consts = tuple(jnp.asarray(a, BF16 if a.shape[0] > 1 else F32) for a in _fox_placement() + (seg,))
    lam_vecs = jnp.stack([lambda_q1[layer], lambda_k1[layer], lambda_q2[layer], lambda_k2[layer]])
    g = subln_g[layer].reshape(1, DIFF_VDIM)
    w_o = w_out[layer].astype(BF16)
    lg, lb = ln_g[layer].reshape(1, d), ln_b[layer].reshape(1, d)

    def run(x, mod_x, pos_offset, attend):
        (fk, fv, dk, dv, logf, fqa, fka, fva, fg, dqa, dka, dvb, dg, stats) = _in_proj(
            x, mod_x, w, bf16x, consts, pos_offset)
        stats = jnp.transpose(stats[..., :FOX_WIDTH // HEAD_DIM], (0, 3, 2, 1))
        fox_o, diff_o = attend(fqa, fka, fva, dqa, dka, dvb, stats)
        y = _out_proj(fox_o, diff_o, fg, dg, x, mod_x, w_o, lg, lb, alpha)
        b, t = x.shape[:2]
        states = (fk[None], fv[None], jnp.swapaxes(logf[:, :, :FOX_HEADS], 1, 2)[None],
                  dk.reshape(1, b, DIFF_HEADS, t, 2, HEAD_DIM), dv[None])
        return y, states

    def prompt_attend(fqa, fka, fva, dqa, dka, dvb, stats):
        return (_fox_attention(fqa, fka, fva, stats),
                _diff_attention(dqa, dka, dvb, stats, lam_vecs, g, lam_init))

    def sample_attend(fqa, fka, fva, dqa, dka, dvb, stats):
        del stats
        return (_fox_sample(fqa, fka, fva, cache_fox_k, cache_fox_v, cache_fox_logf[layer]),
                _diff_sample(dqa, dka, dvb, cache_diff_k, cache_diff_v, lam_vecs, g, lam_init))

    yp, sp = run(x_prompt, mod_p, 0, prompt_attend)
    ys, ss = run(x_sample, mod_s, past, sample_attend)
    return (yp, ys) + sp + ss
```

```python
import functools
import math

import jax
import jax.numpy as jnp
import numpy as np
from jax import lax
from jax.experimental import pallas as pl
from jax.experimental.pallas import tpu as pltpu

F32 = jnp.float32
BF16 = jnp.bfloat16

HEAD_DIM = 64
FOX_HEADS = 8
DIFF_HEADS = 4
DIFF_VDIM = 2 * HEAD_DIM
FOX_WIDTH = FOX_HEADS * HEAD_DIM
DIFF_WIDTH = DIFF_HEADS * DIFF_VDIM
CHUNK = 64
LN_EPS = 1e-5
RMS_EPS = 1e-5
LANES = 128
NEG_INF = float("-inf")
LOG2E = math.log2(math.e)

_SEC = {name: i * FOX_WIDTH for i, name in enumerate(("fq", "fk", "fv", "fg", "dq", "dk", "dv", "dg"))}
_FF_OFF = 8 * FOX_WIDTH
_W_COLS = _FF_OFF + LANES

_VMEM_LIMIT = 56 * 1024 * 1024


def _params(sem):
    return pltpu.CompilerParams(dimension_semantics=sem, vmem_limit_bytes=_VMEM_LIMIT)


def _ada_kernel(c_ref, w_ref, b_ref, o_ref):
    c = c_ref[...]
    s = c * jax.nn.sigmoid(c)
    o_ref[...] = jnp.dot(s, w_ref[...], preferred_element_type=F32) + b_ref[...]


def _ada(c_all, w_ada, b_ada):
    rows, d = c_all.shape
    n = w_ada.shape[1]
    tn = 512
    return pl.pallas_call(
        _ada_kernel,
        grid=(n // tn,),
        in_specs=[pl.BlockSpec((rows, d), lambda j: (0, 0)),
                  pl.BlockSpec((d, tn), lambda j: (0, j)),
                  pl.BlockSpec((1, tn), lambda j: (0, j))],
        out_specs=pl.BlockSpec((rows, tn), lambda j: (0, j)),
        out_shape=jax.ShapeDtypeStruct((rows, n), F32),
        compiler_params=_params(("arbitrary",)),
        name="ada",
    )(c_all, w_ada, b_ada.reshape(1, n))


def _split3(x):
    hi = x.astype(BF16)
    r1 = x - hi.astype(F32)
    mid = r1.astype(BF16)
    lo = (r1 - mid.astype(F32)).astype(BF16)
    return hi, mid, lo


def _fox_placement():
    pq = np.zeros((LANES, FOX_WIDTH), np.float32)
    pk = np.zeros((LANES, FOX_WIDTH), np.float32)
    oq = np.zeros((1, FOX_WIDTH), np.float32)
    ok = np.zeros((1, FOX_WIDTH), np.float32)
    for h in range(FOX_HEADS):
        base = (h // 2) * LANES + (HEAD_DIM if h % 2 == 0 else 0)
        for p in range(3):
            pq[8 * p + h, base + p] = 1.0
            ok[0, base + p] = 1.0
            oq[0, base + 3 + p] = 1.0
            pk[8 * p + h, base + 3 + p] = -1.0
    return pq, pk, oq, ok


def _in_proj_kernel(x_ref, mod_ref, w_ref, bf_ref, pq_ref, pk_ref, oq_ref, ok_ref, seg_ref,
                    fk_ref, fv_ref, dk_ref, dv_ref, logf_ref,
                    fqa_ref, fka_ref, fva_ref, fg_ref, dqa_ref, dka_ref, dvb_ref, dg_ref, stat_ref,
                    carry_ref, *, tm, pos_offset):
    ti = pl.program_id(1)
    x = x_ref[0]
    shift = mod_ref[0, 0:1, :]
    scale = mod_ref[0, 1:2, :]
    h = (x * (1.0 + scale) + shift).astype(BF16)

    def proj(name, width=FOX_WIDTH):
        off = _SEC[name] if name in _SEC else _FF_OFF
        return jnp.dot(h, w_ref[:, off:off + width], preferred_element_type=F32)

    lane = lax.broadcasted_iota(jnp.int32, (tm, LANES), 1)
    row = lax.broadcasted_iota(jnp.int32, (tm, LANES), 0)
    lower = lane < HEAD_DIM
    grp = (lane // 8) % 4

    def by_group(a, b, c):
        return jnp.where(grp == 0, a, jnp.where(grp == 1, b, jnp.where(grp == 2, c, jnp.zeros_like(a))))

    def max_sq_norm(z):
        zr = z.astype(BF16).astype(F32)
        sq = zr * zr
        if tm % 32 == 0:
            sq = jnp.maximum(sq[0:tm // 2], sq[tm // 2:tm])
        sq = jnp.dot(sq.astype(BF16), seg_ref[...], preferred_element_type=F32)
        return jnp.max(sq, axis=0, keepdims=True)

    logf = jax.nn.log_sigmoid(proj("ff", LANES) + bf_ref[...])
    logf_ref[0] = logf.T[0:FOX_HEADS] if tm % LANES == 0 else logf
    tri_r = lax.broadcasted_iota(jnp.int32, (tm, tm), 0)
    tri_c = lax.broadcasted_iota(jnp.int32, (tm, tm), 1)
    tri = (tri_c <= tri_r).astype(BF16)
    part = jnp.dot(tri, by_group(*_split3(logf)), preferred_element_type=F32)
    local = part + pltpu.roll(part, 8, 1) + pltpu.roll(part, 16, 1) + pltpu.roll(part, 24, 1)

    @pl.when(ti == 0)
    def _():
        carry_ref[...] = jnp.zeros_like(carry_ref)

    cum = local + carry_ref[...]
    carry_ref[...] = cum[tm - 1:tm, :]
    cum2 = cum * LOG2E
    cum_cat = by_group(*_split3(cum2))
    eq = jnp.dot(cum_cat, pq_ref[...], preferred_element_type=F32) + oq_ref[...]
    ek = jnp.dot(cum_cat, pk_ref[...], preferred_element_type=F32) + ok_ref[...]

    zq, zk, zv = proj("fq"), proj("fk"), proj("fv")
    fox_norms = [max_sq_norm(zq), max_sq_norm(zk)]
    e_even = (lane == HEAD_DIM).astype(F32)
    for j in range(FOX_HEADS // 2):
        sl = slice(j * LANES, (j + 1) * LANES)
        fqa_ref[0, 2 * j] = jnp.where(lower, zq[:, sl], eq[:, sl]).astype(BF16)
        fqa_ref[0, 2 * j + 1] = jnp.where(lower, eq[:, sl], zq[:, sl]).astype(BF16)
        fka_ref[0, 2 * j] = jnp.where(lower, zk[:, sl], ek[:, sl]).astype(BF16)
        fka_ref[0, 2 * j + 1] = jnp.where(lower, ek[:, sl], zk[:, sl]).astype(BF16)
        fva_ref[0, 2 * j] = jnp.where(lower, zv[:, sl], e_even).astype(BF16)
        fva_ref[0, 2 * j + 1] = jnp.where(lower, pltpu.roll(zv[:, sl], HEAD_DIM, 1), e_even).astype(BF16)
    for hd in range(FOX_HEADS):
        fk_ref[0, hd] = zk[:, hd * HEAD_DIM:(hd + 1) * HEAD_DIM]
        fv_ref[0, hd] = zv[:, hd * HEAD_DIM:(hd + 1) * HEAD_DIM]
    fg_ref[0] = jax.nn.silu(proj("fg")).astype(BF16)

    zq, zk, zv = proj("dq"), proj("dk"), proj("dv")
    pos = (row + (ti * tm + pos_offset)).astype(F32)
    sub = lane % HEAD_DIM
    for hd in range(DIFF_HEADS):
        slope = 2.0 ** (-8.0 * (hd + 1) / DIFF_HEADS)
        sl = slice(hd * LANES, (hd + 1) * LANES)
        p0, p1, p2 = (a.astype(F32) for a in _split3((slope * LOG2E) * pos))
        pieces = jnp.where(sub % 3 == 0, p0, jnp.where(sub % 3 == 1, p1, p2))
        ext_q = jnp.where(sub < 3, -pieces, (sub < 6).astype(F32))
        ext_k = jnp.where(sub < 3, 1.0, jnp.where(sub < 6, pieces, 0.0))
        dqa_ref[0, 2 * hd] = jnp.where(lower, zq[:, sl], ext_q).astype(BF16)
        dqa_ref[0, 2 * hd + 1] = jnp.where(lower, ext_q, zq[:, sl]).astype(BF16)
        dka_ref[0, 2 * hd] = jnp.where(lower, zk[:, sl], ext_k).astype(BF16)
        dka_ref[0, 2 * hd + 1] = jnp.where(lower, ext_k, zk[:, sl]).astype(BF16)
        dk_ref[0, hd] = zk[:, sl]
        dv_ref[0, hd] = zv[:, sl]
        dvb_ref[0, hd] = zv[:, sl].astype(BF16)
    dg_ref[0] = jax.nn.silu(proj("dg")).astype(BF16)
    stat_ref[0, 0] = jnp.concatenate(
        fox_norms + [max_sq_norm(zq), max_sq_norm(zk), cum2[0:1], cum2[tm - 1:tm], jnp.zeros((2, LANES), F32)],
        axis=0)


def _in_proj(x, mod, w, bf16x, consts, pos_offset):
    b, t, d = x.shape
    tm = min(t, 512)
    nt = t // tm
    pq, pk, oq, ok, seg = consts
    head_major = tm % LANES == 0

    def full(a):
        return pl.BlockSpec(a.shape, lambda i, j: (0,) * a.ndim)

    def heads(n, width):
        return pl.BlockSpec((1, n, tm, width), lambda i, j: (i, 0, j, 0))

    rows = pl.BlockSpec((1, tm, FOX_WIDTH), lambda i, j: (i, j, 0))
    out_shape = (
        jax.ShapeDtypeStruct((b, FOX_HEADS, t, HEAD_DIM), F32),
        jax.ShapeDtypeStruct((b, FOX_HEADS, t, HEAD_DIM), F32),
        jax.ShapeDtypeStruct((b, DIFF_HEADS, t, DIFF_VDIM), F32),
        jax.ShapeDtypeStruct((b, DIFF_HEADS, t, DIFF_VDIM), F32),
        jax.ShapeDtypeStruct((b, FOX_HEADS, t) if head_major else (b, t, LANES), F32),
        jax.ShapeDtypeStruct((b, FOX_HEADS, t, LANES), BF16),
        jax.ShapeDtypeStruct((b, FOX_HEADS, t, LANES), BF16),
        jax.ShapeDtypeStruct((b, FOX_HEADS, t, LANES), BF16),
        jax.ShapeDtypeStruct((b, t, FOX_WIDTH), BF16),
        jax.ShapeDtypeStruct((b, 2 * DIFF_HEADS, t, LANES), BF16),
        jax.ShapeDtypeStruct((b, 2 * DIFF_HEADS, t, LANES), BF16),
        jax.ShapeDtypeStruct((b, DIFF_HEADS, t, DIFF_VDIM), BF16),
        jax.ShapeDtypeStruct((b, t, DIFF_WIDTH), BF16),
        jax.ShapeDtypeStruct((b, nt, 8, LANES), F32),
    )
    out_specs = (
        heads(FOX_HEADS, HEAD_DIM), heads(FOX_HEADS, HEAD_DIM),
        heads(DIFF_HEADS, DIFF_VDIM), heads(DIFF_HEADS, DIFF_VDIM),
        (pl.BlockSpec((1, FOX_HEADS, tm), lambda i, j: (i, 0, j)) if head_major
         else pl.BlockSpec((1, tm, LANES), lambda i, j: (i, j, 0))),
        heads(FOX_HEADS, LANES), heads(FOX_HEADS, LANES), heads(FOX_HEADS, LANES), rows,
        heads(2 * DIFF_HEADS, LANES), heads(2 * DIFF_HEADS, LANES), heads(DIFF_HEADS, DIFF_VDIM), rows,
        pl.BlockSpec((1, 1, 8, LANES), lambda i, j: (i, j, 0, 0)),
    )
    outs = pl.pallas_call(
        functools.partial(_in_proj_kernel, tm=tm, pos_offset=pos_offset),
        grid=(b, nt),
        in_specs=[pl.BlockSpec((1, tm, d), lambda i, j: (i, j, 0)),
                  pl.BlockSpec((1, 3, d), lambda i, j: (i, 0, 0)),
                  full(w), full(bf16x), full(pq), full(pk), full(oq), full(ok), full(seg)],
        out_specs=out_specs,
        out_shape=out_shape,
        scratch_shapes=[pltpu.VMEM((1, LANES), F32)],
        compiler_params=_params(("arbitrary", "arbitrary")),
        name="in_proj",
    )(x, mod, w, bf16x, pq, pk, oq, ok, seg)
    if not head_major:
        outs = outs[:4] + (jnp.swapaxes(outs[4][:, :, :FOX_HEADS], 1, 2),) + outs[5:]
    return outs


_NT = (((1,), (1,)), ((), ()))
_TN = (((0,), (0,)), ((), ()))
_STAGES = 4
_FOX_V_ROWS = HEAD_DIM + 16
_FOX_GROUP = 4


_SKIP_MARGIN = 152.0
_STAT_FOX_Q, _STAT_FOX_K, _STAT_DIFF_Q, _STAT_DIFF_K, _STAT_CUM_FIRST, _STAT_CUM_LAST = range(6)


def _diagonal_blocks(tq):
    assert tq % (2 * LANES) == 0
    return ((tq // 2, 0), (tq, tq // 2))


def _dot_bound(stat, q_row, k_row, qi):
    tile = lax.broadcasted_iota(jnp.int32, (1, stat.shape[-1]), 1)
    qn2 = jnp.sum(jnp.where(tile == qi, stat[q_row:q_row + 1], 0.0))
    return jnp.sqrt(qn2 * stat[k_row:k_row + 1]) * 1.01 + 1.0


def _first_needed_tile(qi, bounds, m_ref):
    first = qi
    for c, bound in enumerate(bounds):
        tile = lax.broadcasted_iota(jnp.int32, bound.shape, 1)
        needed = (bound >= jnp.min(m_ref[c]) - _SKIP_MARGIN) & (tile < qi)
        first = jnp.minimum(first, jnp.min(jnp.where(needed, tile, qi)))
    return first


def _pipelined_tiles(qi, n_chains, scores, absorb, first_needed, sa_ref, sb_ref):
    chains = range(n_chains)
    bufs = (sa_ref, sb_ref)
    for c in chains:
        scores(qi, sa_ref, c, True)
        scores(jnp.maximum(qi - 1, 0), sb_ref, c)
        absorb(sa_ref, qi, c, True)
    n = qi - first_needed()

    def stages(top, count, more):
        for s in range(count):
            for c in chains:
                if more or s + 1 < count:
                    scores(jnp.maximum(top - s - 1, 0), bufs[s % 2], c)
                absorb(bufs[(s + 1) % 2], top - s, c, False)

    def body(i, carry):
        stages(qi - 1 - _STAGES * i, _STAGES, True)
        return carry

    trips = jnp.maximum(n - 1, 0) // _STAGES
    lax.fori_loop(0, trips, body, 0)
    rem = n - _STAGES * trips
    for count in range(1, _STAGES + 1):
        @pl.when(rem == count)
        def _(count=count):
            stages(qi - 1 - _STAGES * trips, count, False)


def _fox_kernel(q_ref, k_ref, v_ref, stat_ref, o_ref, sa_ref, sb_ref, m_ref, acc_ref, *, tq):
    qi = pl.program_id(2)
    half = tq // 2

    def scores(ki, dst, hh, diagonal=False):
        start = pl.multiple_of(ki * tq, tq)
        if diagonal:
            for keys, q0 in _diagonal_blocks(tq):
                k = k_ref[0, hh, pl.ds(start, keys), :]
                dst[hh, 0:keys, q0:q0 + half] = lax.dot_general(k, q_ref[0, hh, q0:q0 + half, :], _NT,
                                                                preferred_element_type=F32)
        else:
            k = k_ref[0, hh, pl.ds(start, tq), :]
            dst[hh] = lax.dot_general(k, q_ref[0, hh], _NT, preferred_element_type=F32)

    def absorb(src, ki, hh, diagonal):
        start = pl.multiple_of(ki * tq, tq)
        if diagonal:
            for keys, q0 in _diagonal_blocks(tq):
                s = src[hh, 0:keys, q0:q0 + half]
                key = lax.broadcasted_iota(jnp.int32, s.shape, 0)
                qry = lax.broadcasted_iota(jnp.int32, s.shape, 1) + q0
                s = jnp.where(key <= qry, s, NEG_INF)
                m_new = jnp.max(s, axis=0, keepdims=True)
                p = jnp.exp2(s - m_new)
                v = v_ref[0, hh, pl.ds(start, keys), 0:_FOX_V_ROWS]
                acc_ref[hh, :, q0:q0 + half] = lax.dot_general(v, p.astype(BF16), _TN, preferred_element_type=F32)
                m_ref[hh, :, q0:q0 + half] = m_new
            return
        s = src[hh]
        m_prev = m_ref[hh]
        m_new = jnp.maximum(m_prev, jnp.max(s, axis=0, keepdims=True))
        p = jnp.exp2(s - m_new)
        alpha = jnp.exp2(m_prev - m_new)
        v = v_ref[0, hh, pl.ds(start, tq), 0:_FOX_V_ROWS]
        pv = lax.dot_general(v, p.astype(BF16), _TN, preferred_element_type=F32)
        acc_ref[hh] = alpha * acc_ref[hh] + pv
        m_ref[hh] = m_new

    def first_needed():
        bounds = []
        for hh in range(_FOX_GROUP):
            stat = stat_ref[0, hh]
            tile = lax.broadcasted_iota(jnp.int32, (1, stat.shape[-1]), 1)
            cum_q = jnp.sum(jnp.where(tile == qi, stat[_STAT_CUM_FIRST:_STAT_CUM_FIRST + 1], 0.0))
            decay = cum_q - stat[_STAT_CUM_LAST:_STAT_CUM_LAST + 1]
            bounds.append(_dot_bound(stat, _STAT_FOX_Q, _STAT_FOX_K, qi) + decay)
        return _first_needed_tile(qi, bounds, m_ref)

    _pipelined_tiles(qi, _FOX_GROUP, scores, absorb, first_needed, sa_ref, sb_ref)
    outs = [acc_ref[hh, 0:HEAD_DIM] / acc_ref[hh, HEAD_DIM:HEAD_DIM + 1] for hh in range(_FOX_GROUP)]
    o_ref[0] = jnp.concatenate(outs, axis=0).T.astype(BF16)


def _fox_attention(fqa, fka, fva, stats):
    b, nh, t, _ = fqa.shape
    tq = min(t, 512)
    g = _FOX_GROUP
    return pl.pallas_call(
        functools.partial(_fox_kernel, tq=tq),
        grid=(b, nh // g, t // tq),
        in_specs=[pl.BlockSpec((1, g, tq, LANES), lambda i, j, q: (i, j, q, 0)),
                  pl.BlockSpec((1, g, t, LANES), lambda i, j, q: (i, j, 0, 0)),
                  pl.BlockSpec((1, g, t, LANES), lambda i, j, q: (i, j, 0, 0)),
                  pl.BlockSpec((1, g) + stats.shape[2:], lambda i, j, q: (i, j, 0, 0))],
        out_specs=pl.BlockSpec((1, tq, g * HEAD_DIM), lambda i, j, q: (i, q, j)),
        out_shape=jax.ShapeDtypeStruct((b, t, FOX_WIDTH), BF16),
        scratch_shapes=[pltpu.VMEM((g, tq, tq), F32), pltpu.VMEM((g, tq, tq), F32),
                        pltpu.VMEM((g, 1, tq), F32), pltpu.VMEM((g, _FOX_V_ROWS, tq), F32)],
        compiler_params=_params(("arbitrary", "arbitrary", "arbitrary")),
        name="fox_attn",
    )(fqa, fka, fva, stats)


def _alibi_slope(head):
    return jnp.exp2(jnp.full((1, 1), -8.0 / DIFF_HEADS, F32) * (head + 1).astype(F32))


def _lambda(lam_ref, lam_init):
    lv = lam_ref[...]
    e1 = jnp.exp(jnp.sum(lv[0:1] * lv[1:2], axis=1, keepdims=True))
    e2 = jnp.exp(jnp.sum(lv[2:3] * lv[3:4], axis=1, keepdims=True))
    return e1 - e2 + lam_init


def _diff_normed(o0, o1, lam, axis):
    df = o0 - lam * o1
    return df * lax.rsqrt(jnp.mean(df * df, axis=axis, keepdims=True) + RMS_EPS)


_ONES_ROWS = 16


def _diff_kernel(q_ref, k_ref, v_ref, stat_ref, lam_ref, g_ref, o_ref, sa_ref, sb_ref, m_ref, acc_ref, *, tq,
                 lam_init):
    hd = pl.program_id(1)
    qi = pl.program_id(2)
    slope = _alibi_slope(hd)
    half = tq // 2

    def scores(ki, dst, c, diagonal=False):
        start = pl.multiple_of(ki * tq, tq)
        if diagonal:
            for keys, q0 in _diagonal_blocks(tq):
                k = k_ref[0, c, pl.ds(start, keys), :]
                dst[c, 0:keys, q0:q0 + half] = lax.dot_general(k, q_ref[0, c, q0:q0 + half, :], _NT,
                                                               preferred_element_type=F32)
        else:
            k = k_ref[0, c, pl.ds(start, tq), :]
            dst[c] = lax.dot_general(k, q_ref[0, c], _NT, preferred_element_type=F32)

    def values_t(start, keys):
        v = v_ref[0, 0, pl.ds(start, keys), :]
        return jnp.concatenate([v.T, jnp.ones((_ONES_ROWS, keys), BF16)], axis=0)

    def absorb(src, ki, c, diagonal):
        start = pl.multiple_of(ki * tq, tq)
        if diagonal:
            for keys, q0 in _diagonal_blocks(tq):
                s = src[c, 0:keys, q0:q0 + half]
                key = lax.broadcasted_iota(jnp.int32, s.shape, 0)
                qry = lax.broadcasted_iota(jnp.int32, s.shape, 1) + q0
                ahead = jnp.maximum(key - qry, 0).astype(F32)
                s = jnp.where((key // CHUNK) <= (qry // CHUNK), s - (2.0 * LOG2E * slope) * ahead, NEG_INF)
                m_new = jnp.max(s, axis=0, keepdims=True)
                p = jnp.exp2(s - m_new)
                acc_ref[c, :, q0:q0 + half] = jnp.dot(values_t(start, keys), p.astype(BF16),
                                                      preferred_element_type=F32)
                m_ref[c, :, q0:q0 + half] = m_new
            return
        s = src[c]
        m_prev = m_ref[c]
        m_new = jnp.maximum(m_prev, jnp.max(s, axis=0, keepdims=True))
        p = jnp.exp2(s - m_new)
        alpha = jnp.exp2(m_prev - m_new)
        acc_ref[c] = alpha * acc_ref[c] + jnp.dot(values_t(start, tq), p.astype(BF16), preferred_element_type=F32)
        m_ref[c] = m_new

    def first_needed():
        bounds = []
        for c in range(2):
            stat = stat_ref[0, c]
            tile = lax.broadcasted_iota(jnp.int32, (1, stat.shape[-1]), 1)
            gap = ((qi - 1 - tile) * tq + 1).astype(F32)
            bounds.append(_dot_bound(stat, _STAT_DIFF_Q, _STAT_DIFF_K, qi) - (LOG2E * slope) * gap)
        return _first_needed_tile(qi, bounds, m_ref)

    _pipelined_tiles(qi, 2, scores, absorb, first_needed, sa_ref, sb_ref)
    lam = _lambda(lam_ref, lam_init)
    acc0, acc1 = acc_ref[0], acc_ref[1]
    o0 = acc0[:DIFF_VDIM] / acc0[DIFF_VDIM:DIFF_VDIM + 1]
    o1 = acc1[:DIFF_VDIM] / acc1[DIFF_VDIM:DIFF_VDIM + 1]
    o_ref[0] = (_diff_normed(o0, o1, lam, 0).T * g_ref[...] * (1.0 - lam_init)).astype(BF16)


def _diff_attention(dqa, dka, dvb, stats, lam_vecs, subln_g, lam_init):
    b, nm, t, _ = dqa.shape
    tq = min(t, 512)
    return pl.pallas_call(
        functools.partial(_diff_kernel, tq=tq, lam_init=lam_init),
        grid=(b, nm // 2, t // tq),
        in_specs=[pl.BlockSpec((1, 2, tq, LANES), lambda i, j, q: (i, j, q, 0)),
                  pl.BlockSpec((1, 2, t, LANES), lambda i, j, q: (i, j, 0, 0)),
                  pl.BlockSpec((1, 1, t, DIFF_VDIM), lambda i, j, q: (i, j, 0, 0)),
                  pl.BlockSpec((1, 2) + stats.shape[2:], lambda i, j, q: (i, j, 0, 0)),
                  pl.BlockSpec(lam_vecs.shape, lambda i, j, q: (0, 0)),
                  pl.BlockSpec(subln_g.shape, lambda i, j, q: (0, 0))],
        out_specs=pl.BlockSpec((1, tq, DIFF_VDIM), lambda i, j, q: (i, q, j)),
        out_shape=jax.ShapeDtypeStruct((b, t, DIFF_WIDTH), BF16),
        scratch_shapes=[pltpu.VMEM((2, tq, tq), F32), pltpu.VMEM((2, tq, tq), F32),
                        pltpu.VMEM((2, 1, tq), F32), pltpu.VMEM((2, DIFF_VDIM + _ONES_ROWS, tq), F32)],
        compiler_params=_params(("arbitrary", "arbitrary", "arbitrary")),
        name="diff_attn",
    )(dqa, dka, dvb, stats, lam_vecs, subln_g)


def _prefix_sum_lanes(x):
    n = x.shape[-1]
    lane = lax.broadcasted_iota(jnp.int32, x.shape, x.ndim - 1)
    sh = 1
    while sh < n:
        x = x + jnp.where(lane >= sh, pltpu.roll(x, sh, x.ndim - 1), 0.0)
        sh *= 2
    return x


def _fox_sample_kernel(q_ref, k_ref, v_ref, ck_ref, cv_ref, clf_ref, o_ref, *, t, past):
    r = lax.broadcasted_iota(jnp.int32, (t, t), 0)
    c = lax.broadcasted_iota(jnp.int32, (t, t), 1)
    cum_c = _prefix_sum_lanes(clf_ref[0])
    suffix = (cum_c[:, past - 1:past] - cum_c) * LOG2E
    for hh in range(FOX_HEADS):
        data = slice(0, HEAD_DIM) if hh % 2 == 0 else slice(HEAD_DIM, LANES)
        ext = HEAD_DIM if hh % 2 == 0 else 0
        qa = q_ref[0, hh]
        qf = qa.astype(F32)
        cum_q = qf[:, ext:ext + 1] + qf[:, ext + 1:ext + 2] + qf[:, ext + 2:ext + 3]
        s_c = jnp.dot(qa[:, data], ck_ref[0, 0, hh].astype(BF16), preferred_element_type=F32)
        s_c = s_c + cum_q + suffix[hh:hh + 1]
        s_n = lax.dot_general(qa, k_ref[0, hh], _NT, preferred_element_type=F32)
        s_n = jnp.where(c <= r, s_n, NEG_INF)
        m = jnp.maximum(jnp.max(s_c, axis=1, keepdims=True), jnp.max(s_n, axis=1, keepdims=True))
        p_c = jnp.exp2(s_c - m)
        p_n = jnp.exp2(s_n - m)
        denom = jnp.sum(p_c, axis=1, keepdims=True) + jnp.sum(p_n, axis=1, keepdims=True)
        o = lax.dot_general(p_c.astype(BF16), cv_ref[0, 0, hh].astype(BF16), _NT, preferred_element_type=F32)
        o = o + jnp.dot(p_n.astype(BF16), v_ref[0, hh][:, 0:HEAD_DIM], preferred_element_type=F32)
        o_ref[0, :, hh * HEAD_DIM:(hh + 1) * HEAD_DIM] = (o / denom).astype(BF16)


def _fox_sample(fqa, fka, fva, cache_k, cache_v, cache_logf):
    b, nh, t, _ = fqa.shape
    past = cache_k.shape[3]
    cache_k, cache_v = jnp.swapaxes(cache_k, 3, 4), jnp.swapaxes(cache_v, 3, 4)
    new = pl.BlockSpec((1, nh, t, LANES), lambda i: (i, 0, 0, 0))
    cache = pl.BlockSpec((1, 1, nh, HEAD_DIM, past), lambda i: (0, i, 0, 0, 0))
    return pl.pallas_call(
        functools.partial(_fox_sample_kernel, t=t, past=past),
        grid=(b,),
        in_specs=[new, new, new, cache, cache, pl.BlockSpec((1, nh, past), lambda i: (i, 0, 0))],
        out_specs=pl.BlockSpec((1, t, FOX_WIDTH), lambda i: (i, 0, 0)),
        out_shape=jax.ShapeDtypeStruct((b, t, FOX_WIDTH), BF16),
        compiler_params=_params(("arbitrary",)),
        name="fox_sample",
    )(fqa, fka, fva, cache_k, cache_v, cache_logf)


def _diff_sample_kernel(q_ref, k_ref, v_ref, ck_ref, cv_ref, lam_ref, g_ref, o_ref, *, t, past, lam_init):
    lane = lax.broadcasted_iota(jnp.int32, (t, LANES), 1)
    pos_q = past + lax.broadcasted_iota(jnp.int32, (t, past), 0)
    pos_c = lax.broadcasted_iota(jnp.int32, (t, past), 1)
    dist_c = jnp.abs(pos_q - pos_c).astype(F32)
    vis_c = (pos_c // CHUNK) <= (pos_q // CHUNK)
    pq_n = past + lax.broadcasted_iota(jnp.int32, (t, t), 0)
    pk_n = past + lax.broadcasted_iota(jnp.int32, (t, t), 1)
    dist_n = jnp.abs(pq_n - pk_n).astype(F32)
    vis_n = (pk_n // CHUNK) <= (pq_n // CHUNK)
    lam = _lambda(lam_ref, lam_init)
    for hd in range(DIFF_HEADS):
        slope = LOG2E * 2.0 ** (-8.0 * (hd + 1) / DIFF_HEADS)
        ck = ck_ref[0, 0, hd].astype(BF16)
        cv = cv_ref[0, 0, hd].astype(BF16)
        v = v_ref[0, hd]
        outs = []
        for c in range(2):
            half = (lane < HEAD_DIM) if c == 0 else (lane >= HEAD_DIM)
            qz = jnp.where(half, q_ref[0, 2 * hd + c], jnp.zeros((), BF16))
            kz = jnp.where(half, k_ref[0, 2 * hd + c], jnp.zeros((), BF16))
            s_c = jnp.dot(qz, ck, preferred_element_type=F32)
            s_c = jnp.where(vis_c, s_c - slope * dist_c, NEG_INF)
            s_n = lax.dot_general(qz, kz, _NT, preferred_element_type=F32)
            s_n = jnp.where(vis_n, s_n - slope * dist_n, NEG_INF)
            m = jnp.maximum(jnp.max(s_c, axis=1, keepdims=True), jnp.max(s_n, axis=1, keepdims=True))
            p_c = jnp.exp2(s_c - m)
            p_n = jnp.exp2(s_n - m)
            denom = jnp.sum(p_c, axis=1, keepdims=True) + jnp.sum(p_n, axis=1, keepdims=True)
            o = jnp.dot(p_c.astype(BF16), cv, preferred_element_type=F32)
            o = o + jnp.dot(p_n.astype(BF16), v, preferred_element_type=F32)
            outs.append(o / denom)
        normed = _diff_normed(outs[0], outs[1], lam, 1) * g_ref[...] * (1.0 - lam_init)
        o_ref[0, :, hd * DIFF_VDIM:(hd + 1) * DIFF_VDIM] = normed.astype(BF16)


def _diff_sample(dqa, dka, dvb, cache_k, cache_v, lam_vecs, subln_g, lam_init):
    b, nm, t, _ = dqa.shape
    past = cache_k.shape[3]
    ck = jnp.transpose(cache_k, (0, 1, 2, 4, 5, 3)).reshape(cache_k.shape[:3] + (DIFF_VDIM, past))
    new = pl.BlockSpec((1, nm, t, LANES), lambda i: (i, 0, 0, 0))
    return pl.pallas_call(
        functools.partial(_diff_sample_kernel, t=t, past=past, lam_init=lam_init),
        grid=(b,),
        in_specs=[new, new, pl.BlockSpec((1, DIFF_HEADS, t, DIFF_VDIM), lambda i: (i, 0, 0, 0)),
                  pl.BlockSpec((1, 1, DIFF_HEADS, DIFF_VDIM, past), lambda i: (0, i, 0, 0, 0)),
                  pl.BlockSpec((1, 1, DIFF_HEADS, past, DIFF_VDIM), lambda i: (0, i, 0, 0, 0)),
                  pl.BlockSpec(lam_vecs.shape, lambda i: (0, 0)),
                  pl.BlockSpec(subln_g.shape, lambda i: (0, 0))],
        out_specs=pl.BlockSpec((1, t, DIFF_WIDTH), lambda i: (i, 0, 0)),
        out_shape=jax.ShapeDtypeStruct((b, t, DIFF_WIDTH), BF16),
        compiler_params=_params(("arbitrary",)),
        name="diff_sample",
    )(dqa, dka, dvb, ck, cache_v, lam_vecs, subln_g)


def _out_proj_kernel(fo_ref, do_ref, fg_ref, dg_ref, x_ref, mod_ref, w_ref, g_ref, b_ref, y_ref, *, alpha):
    def gated(o_ref, gate_ref):
        return o_ref[0] * gate_ref[0]

    branch = jnp.dot(gated(fo_ref, fg_ref), w_ref[0:FOX_WIDTH, :], preferred_element_type=F32)
    branch = branch + jnp.dot(gated(do_ref, dg_ref), w_ref[FOX_WIDTH:, :], preferred_element_type=F32)
    r = alpha * x_ref[0] + mod_ref[0, 2:3, :] * branch
    mu = jnp.mean(r, axis=-1, keepdims=True)
    rc = r - mu
    var = jnp.mean(rc * rc, axis=-1, keepdims=True)
    y_ref[0] = rc * lax.rsqrt(var + LN_EPS) * g_ref[...] + b_ref[...]


def _out_proj(fox_o, diff_o, fg, dg, x, mod, w_out, ln_g, ln_b, alpha):
    b, t, d = x.shape
    tm = 1024 if t % 1024 == 0 else min(t, 512)
    rows = pl.BlockSpec((1, tm, FOX_WIDTH), lambda i, j: (i, j, 0))
    wide = pl.BlockSpec((1, tm, d), lambda i, j: (i, j, 0))

    def full(a):
        return pl.BlockSpec(a.shape, lambda i, j: (0,) * a.ndim)

    return pl.pallas_call(
        functools.partial(_out_proj_kernel, alpha=alpha),
        grid=(b, t // tm),
        in_specs=[rows, rows, rows, rows, wide,
                  pl.BlockSpec((1, 3, d), lambda i, j: (i, 0, 0)),
                  full(w_out), full(ln_g), full(ln_b)],
        out_specs=wide,
        out_shape=jax.ShapeDtypeStruct((b, t, d), F32),
        compiler_params=_params(("arbitrary", "arbitrary")),
        name="out_proj",
    )(fox_o, diff_o, fg, dg, x, mod, w_out, ln_g, ln_b)


def _pack_w_in(w_in):
    fw = FOX_WIDTH
    o = 0
    fq = w_in[:, o:o + fw]; o += fw
    fk = w_in[:, o:o + fw]; o += fw
    fv = w_in[:, o:o + fw]; o += fw
    ff = w_in[:, o:o + FOX_HEADS]; o += FOX_HEADS
    fg = w_in[:, o:o + fw]; o += fw
    dq = w_in[:, o:o + fw]; o += fw
    dk = w_in[:, o:o + fw]; o += fw
    dv = w_in[:, o:o + fw]; o += fw
    dg = w_in[:, o:o + fw]
    qs = HEAD_DIM ** -0.5 * LOG2E
    return jnp.concatenate([fq * qs, fk, fv, fg, dq * qs, dk, dv, dg,
                            jnp.tile(ff, (1, LANES // FOX_HEADS))], axis=1).astype(BF16)


def kernel(x_prompt, x_sample, cache_fox_k, cache_fox_v, cache_fox_logf, cache_diff_k, cache_diff_v,
           c_prompt, c_sample, w_ada, b_ada, w_in, b_f, lambda_q1, lambda_k1, lambda_q2, lambda_k2,
           subln_g, w_out, ln_g, ln_b):
    depth = w_in.shape[0]
    assert depth == 1, "single-layer step"
    layer = 0
    lam_init = 0.8 - 0.6 * math.exp(-0.3 * layer)
    alpha = (2 * depth) ** 0.25
    bp, bs = c_prompt.shape[0], c_sample.shape[0]
    d = x_prompt.shape[-1]
    past = cache_fox_k.shape[3]

    c_all = jnp.concatenate([c_prompt, c_sample, jnp.zeros((-(bp + bs) % 8, d), F32)], axis=0)
    mod = _ada(c_all, w_ada[layer], b_ada[layer]).reshape(c_all.shape[0], 3, d)
    mod_p, mod_s = mod[:bp], mod[bp:bp + bs]

    w = _pack_w_in(w_in[layer])
    bf16x = jnp.tile(b_f[layer], LANES // FOX_HEADS).reshape(1, LANES)
    seg = np.repeat(np.eye(FOX_WIDTH // HEAD_DIM, LANES, dtype=np.float32), HEAD_DIM, axis=0)
    consts = tuple(jnp.asarray(a, BF16 if a.shape[0] > 1 else F32) for a in _fox_placement() + (seg,))
    lam_vecs = jnp.stack([lambda_q1[layer], lambda_k1[layer], lambda_q2[layer], lambda_k2[layer]])
    g = subln_g[layer].reshape(1, DIFF_VDIM)
    w_o = w_out[layer].astype(BF16)
    lg, lb = ln_g[layer].reshape(1, d), ln_b[layer].reshape(1, d)

    def run(x, mod_x, pos_offset, attend):
        (fk, fv, dk, dv, logf, fqa, fka, fva, fg, dqa, dka, dvb, dg, stats) = _in_proj(
            x, mod_x, w, bf16x, consts, pos_offset)
        stats = jnp.transpose(stats[..., :FOX_WIDTH // HEAD_DIM], (0, 3, 2, 1))
        fox_o, diff_o = attend(fqa, fka, fva, dqa, dka, dvb, stats)
        y = _out_proj(fox_o, diff_o, fg, dg, x, mod_x, w_o, lg, lb, alpha)
        b, t = x.shape[:2]
        states = (fk[None], fv[None], logf[None],
                  dk.reshape(1, b, DIFF_HEADS, t, 2, HEAD_DIM), dv[None])
        return y, states

    def prompt_attend(fqa, fka, fva, dqa, dka, dvb, stats):
        return (_fox_attention(fqa, fka, fva, stats),
                _diff_attention(dqa, dka, dvb, stats, lam_vecs, g, lam_init))

    def sample_attend(fqa, fka, fva, dqa, dka, dvb, stats):
        del stats
        return (_fox_sample(fqa, fka, fva, cache_fox_k, cache_fox_v, cache_fox_logf[layer]),
                _diff_sample(dqa, dka, dvb, cache_diff_k, cache_diff_v, lam_vecs, g, lam_init))

    yp, sp = run(x_prompt, mod_p, 0, prompt_attend)
    ys, ss = run(x_sample, mod_s, past, sample_attend)
    return (yp, ys) + sp + ss
```

```python
import functools
import math

import jax
import jax.numpy as jnp
import numpy as np
from jax import lax
from jax.experimental import pallas as pl
from jax.experimental.pallas import tpu as pltpu

F32 = jnp.float32
BF16 = jnp.bfloat16

HEAD_DIM = 64
FOX_HEADS = 8
DIFF_HEADS = 4
DIFF_VDIM = 2 * HEAD_DIM
FOX_WIDTH = FOX_HEADS * HEAD_DIM
DIFF_WIDTH = DIFF_HEADS * DIFF_VDIM
CHUNK = 64
LN_EPS = 1e-5
RMS_EPS = 1e-5
LANES = 128
NEG_INF = float("-inf")
LOG2E = math.log2(math.e)

_SEC = {name: i * FOX_WIDTH for i, name in enumerate(("fq", "fk", "fv", "fg", "dq", "dk", "dv", "dg"))}
_FF_OFF = 8 * FOX_WIDTH
_W_COLS = _FF_OFF + LANES

_VMEM_LIMIT = 56 * 1024 * 1024


def _params(sem):
    return pltpu.CompilerParams(dimension_semantics=sem, vmem_limit_bytes=_VMEM_LIMIT)


def _ada_kernel(c_ref, w_ref, b_ref, o_ref):
    c = c_ref[...]
    s = c * jax.nn.sigmoid(c)
    o_ref[...] = jnp.dot(s, w_ref[...], preferred_element_type=F32) + b_ref[...]


def _ada(c_all, w_ada, b_ada):
    rows, d = c_all.shape
    n = w_ada.shape[1]
    tn = 512
    return pl.pallas_call(
        _ada_kernel,
        grid=(n // tn,),
        in_specs=[pl.BlockSpec((rows, d), lambda j: (0, 0)),
                  pl.BlockSpec((d, tn), lambda j: (0, j)),
                  pl.BlockSpec((1, tn), lambda j: (0, j))],
        out_specs=pl.BlockSpec((rows, tn), lambda j: (0, j)),
        out_shape=jax.ShapeDtypeStruct((rows, n), F32),
        compiler_params=_params(("arbitrary",)),
        name="ada",
    )(c_all, w_ada, b_ada.reshape(1, n))


def _split3(x):
    hi = x.astype(BF16)
    r1 = x - hi.astype(F32)
    mid = r1.astype(BF16)
    lo = (r1 - mid.astype(F32)).astype(BF16)
    return hi, mid, lo


def _in_proj_kernel(x_ref, mod_ref, w_ref, bf_ref, seg_ref,
                    fk_ref, fv_ref, dk_ref, dv_ref, logf_ref,
                    fqa_ref, fka_ref, fva_ref, fg_ref, dqa_ref, dka_ref, dvb_ref, dg_ref, stat_ref,
                    carry_ref, *, tm, pos_offset):
    ti = pl.program_id(1)
    x = x_ref[0]
    shift = mod_ref[0, 0:1, :]
    scale = mod_ref[0, 1:2, :]
    h = (x * (1.0 + scale) + shift).astype(BF16)

    def proj(name, width=FOX_WIDTH):
        off = _SEC[name] if name in _SEC else _FF_OFF
        return jnp.dot(h, w_ref[:, off:off + width], preferred_element_type=F32)

    lane = lax.broadcasted_iota(jnp.int32, (tm, LANES), 1)
    row = lax.broadcasted_iota(jnp.int32, (tm, LANES), 0)
    lower = lane < HEAD_DIM
    grp = (lane // 8) % 4

    def by_group(a, b, c):
        return jnp.where(grp == 0, a, jnp.where(grp == 1, b, jnp.where(grp == 2, c, jnp.zeros_like(a))))

    def max_sq_norm(z):
        zr = z.astype(BF16).astype(F32)
        sq = zr * zr
        if tm % 32 == 0:
            sq = jnp.maximum(sq[0:tm // 2], sq[tm // 2:tm])
        sq = jnp.dot(sq.astype(BF16), seg_ref[...], preferred_element_type=F32)
        return jnp.max(sq, axis=0, keepdims=True)

    logf = jax.nn.log_sigmoid(proj("ff", LANES) + bf_ref[...])
    logf_ref[0] = logf.T[0:FOX_HEADS] if tm % LANES == 0 else logf

    dq, dk, dv = proj("dq"), proj("dk"), proj("dv")
    pos = (row + (ti * tm + pos_offset)).astype(F32)
    sub = lane % HEAD_DIM
    for hd in range(DIFF_HEADS):
        slope = 2.0 ** (-8.0 * (hd + 1) / DIFF_HEADS)
        sl = slice(hd * LANES, (hd + 1) * LANES)
        p0, p1, p2 = (a.astype(F32) for a in _split3((slope * LOG2E) * pos))
        alibi = jnp.where(sub % 3 == 0, p0, jnp.where(sub % 3 == 1, p1, p2))
        alibi_q = jnp.where(sub < 3, -alibi, (sub < 6).astype(F32))
        alibi_k = jnp.where(sub < 3, 1.0, jnp.where(sub < 6, alibi, 0.0))
        dqa_ref[0, 2 * hd] = jnp.where(lower, dq[:, sl], alibi_q).astype(BF16)
        dqa_ref[0, 2 * hd + 1] = jnp.where(lower, alibi_q, dq[:, sl]).astype(BF16)
        dka_ref[0, 2 * hd] = jnp.where(lower, dk[:, sl], alibi_k).astype(BF16)
        dka_ref[0, 2 * hd + 1] = jnp.where(lower, alibi_k, dk[:, sl]).astype(BF16)
        dk_ref[0, hd] = dk[:, sl]
        dv_ref[0, hd] = dv[:, sl]
        dvb_ref[0, hd] = dv[:, sl].astype(BF16)
    dg_ref[0] = jax.nn.silu(proj("dg")).astype(BF16)
    diff_norms = [max_sq_norm(dq), max_sq_norm(dk)]

    tri_r = lax.broadcasted_iota(jnp.int32, (tm, tm), 0)
    tri_c = lax.broadcasted_iota(jnp.int32, (tm, tm), 1)
    tri = (tri_c <= tri_r).astype(BF16)
    part = jnp.dot(tri, by_group(*_split3(logf)), preferred_element_type=F32)
    zq, zk, zv = proj("fq"), proj("fk"), proj("fv")
    local = part + pltpu.roll(part, 8, 1) + pltpu.roll(part, 16, 1) + pltpu.roll(part, 24, 1)

    @pl.when(ti == 0)
    def _():
        carry_ref[...] = jnp.zeros_like(carry_ref)

    cum = local + carry_ref[...]
    carry_ref[...] = cum[tm - 1:tm, :]
    cum2 = cum * LOG2E
    pieces = by_group(*(a.astype(F32) for a in _split3(cum2)))
    behind = pltpu.roll(pieces, 24, 1)
    u = lane % HEAD_DIM
    ext_q = jnp.where(u < 24, pieces, (u < 48).astype(F32))

    def ext_k(h):
        mine = u % 8 == h
        return jnp.where(mine & (u < 24), 1.0, jnp.where(mine & (u < 48), -behind, 0.0))

    fox_norms = [max_sq_norm(zq), max_sq_norm(zk)]
    e_even = (lane == HEAD_DIM).astype(F32)
    for j in range(FOX_HEADS // 2):
        sl = slice(j * LANES, (j + 1) * LANES)
        fqa_ref[0, 2 * j] = jnp.where(lower, zq[:, sl], ext_q).astype(BF16)
        fqa_ref[0, 2 * j + 1] = jnp.where(lower, ext_q, zq[:, sl]).astype(BF16)
        fka_ref[0, 2 * j] = jnp.where(lower, zk[:, sl], ext_k(2 * j)).astype(BF16)
        fka_ref[0, 2 * j + 1] = jnp.where(lower, ext_k(2 * j + 1), zk[:, sl]).astype(BF16)
        fva_ref[0, 2 * j] = jnp.where(lower, zv[:, sl], e_even).astype(BF16)
        fva_ref[0, 2 * j + 1] = jnp.where(lower, pltpu.roll(zv[:, sl], HEAD_DIM, 1), e_even).astype(BF16)
    for hd in range(FOX_HEADS):
        fk_ref[0, hd] = zk[:, hd * HEAD_DIM:(hd + 1) * HEAD_DIM]
        fv_ref[0, hd] = zv[:, hd * HEAD_DIM:(hd + 1) * HEAD_DIM]
    fg_ref[0] = jax.nn.silu(proj("fg")).astype(BF16)
    stat_ref[0, 0] = jnp.concatenate(
        fox_norms + diff_norms + [cum2[0:1], cum2[tm - 1:tm], jnp.zeros((2, LANES), F32)],
        axis=0)


def _in_proj(x, mod, w, bf16x, seg, pos_offset):
    b, t, d = x.shape
    tm = min(t, 512)
    nt = t // tm
    head_major = tm % LANES == 0

    def full(a):
        return pl.BlockSpec(a.shape, lambda i, j: (0,) * a.ndim)

    def heads(n, width):
        return pl.BlockSpec((1, n, tm, width), lambda i, j: (i, 0, j, 0))

    rows = pl.BlockSpec((1, tm, FOX_WIDTH), lambda i, j: (i, j, 0))
    out_shape = (
        jax.ShapeDtypeStruct((b, FOX_HEADS, t, HEAD_DIM), F32),
        jax.ShapeDtypeStruct((b, FOX_HEADS, t, HEAD_DIM), F32),
        jax.ShapeDtypeStruct((b, DIFF_HEADS, t, DIFF_VDIM), F32),
        jax.ShapeDtypeStruct((b, DIFF_HEADS, t, DIFF_VDIM), F32),
        jax.ShapeDtypeStruct((b, FOX_HEADS, t) if head_major else (b, t, LANES), F32),
        jax.ShapeDtypeStruct((b, FOX_HEADS, t, LANES), BF16),
        jax.ShapeDtypeStruct((b, FOX_HEADS, t, LANES), BF16),
        jax.ShapeDtypeStruct((b, FOX_HEADS, t, LANES), BF16),
        jax.ShapeDtypeStruct((b, t, FOX_WIDTH), BF16),
        jax.ShapeDtypeStruct((b, 2 * DIFF_HEADS, t, LANES), BF16),
        jax.ShapeDtypeStruct((b, 2 * DIFF_HEADS, t, LANES), BF16),
        jax.ShapeDtypeStruct((b, DIFF_HEADS, t, DIFF_VDIM), BF16),
        jax.ShapeDtypeStruct((b, t, DIFF_WIDTH), BF16),
        jax.ShapeDtypeStruct((b, nt, 8, LANES), F32),
    )
    out_specs = (
        heads(FOX_HEADS, HEAD_DIM), heads(FOX_HEADS, HEAD_DIM),
        heads(DIFF_HEADS, DIFF_VDIM), heads(DIFF_HEADS, DIFF_VDIM),
        (pl.BlockSpec((1, FOX_HEADS, tm), lambda i, j: (i, 0, j)) if head_major
         else pl.BlockSpec((1, tm, LANES), lambda i, j: (i, j, 0))),
        heads(FOX_HEADS, LANES), heads(FOX_HEADS, LANES), heads(FOX_HEADS, LANES), rows,
        heads(2 * DIFF_HEADS, LANES), heads(2 * DIFF_HEADS, LANES), heads(DIFF_HEADS, DIFF_VDIM), rows,
        pl.BlockSpec((1, 1, 8, LANES), lambda i, j: (i, j, 0, 0)),
    )
    outs = pl.pallas_call(
        functools.partial(_in_proj_kernel, tm=tm, pos_offset=pos_offset),
        grid=(b, nt),
        in_specs=[pl.BlockSpec((1, tm, d), lambda i, j: (i, j, 0)),
                  pl.BlockSpec((1, 3, d), lambda i, j: (i, 0, 0)),
                  full(w), full(bf16x), full(seg)],
        out_specs=out_specs,
        out_shape=out_shape,
        scratch_shapes=[pltpu.VMEM((1, LANES), F32)],
        compiler_params=_params(("arbitrary", "arbitrary")),
        name="in_proj",
    )(x, mod, w, bf16x, seg)
    if not head_major:
        outs = outs[:4] + (jnp.swapaxes(outs[4][:, :, :FOX_HEADS], 1, 2),) + outs[5:]
    return outs


_NT = (((1,), (1,)), ((), ()))
_TN = (((0,), (0,)), ((), ()))
_STAGES = 4
_FOX_V_ROWS = HEAD_DIM + 16
_FOX_GROUP = 4


_SKIP_MARGIN = 136.0
_STAT_FOX_Q, _STAT_FOX_K, _STAT_DIFF_Q, _STAT_DIFF_K, _STAT_CUM_FIRST, _STAT_CUM_LAST = range(6)


def _diagonal_blocks(tq):
    assert tq % (2 * LANES) == 0
    return ((tq // 2, 0), (tq, tq // 2))


def _dot_bound(stat, q_row, k_row, qi):
    tile = lax.broadcasted_iota(jnp.int32, (1, stat.shape[-1]), 1)
    qn2 = jnp.sum(jnp.where(tile == qi, stat[q_row:q_row + 1], 0.0))
    return jnp.sqrt(qn2 * stat[k_row:k_row + 1]) * 1.01 + 1.0


def _first_needed_tile(qi, bounds, m_ref):
    first = qi
    for c, bound in enumerate(bounds):
        tile = lax.broadcasted_iota(jnp.int32, bound.shape, 1)
        needed = (bound >= jnp.min(m_ref[c]) - _SKIP_MARGIN) & (tile < qi)
        first = jnp.minimum(first, jnp.min(jnp.where(needed, tile, qi)))
    return first


def _pipelined_tiles(qi, n_chains, scores, absorb, first_needed, sa_ref, sb_ref):
    chains = range(n_chains)
    bufs = (sa_ref, sb_ref)
    for c in chains:
        scores(qi, sa_ref, c, True)
        scores(jnp.maximum(qi - 1, 0), sb_ref, c)
        absorb(sa_ref, qi, c, True)
    n = qi - first_needed()

    def stages(top, count, more):
        for s in range(count):
            for c in chains:
                if more or s + 1 < count:
                    scores(jnp.maximum(top - s - 1, 0), bufs[s % 2], c)
                absorb(bufs[(s + 1) % 2], top - s, c, False)

    def body(i, carry):
        stages(qi - 1 - _STAGES * i, _STAGES, True)
        return carry

    trips = jnp.maximum(n - 1, 0) // _STAGES
    lax.fori_loop(0, trips, body, 0)
    rem = n - _STAGES * trips
    for count in range(1, _STAGES + 1):
        @pl.when(rem == count)
        def _(count=count):
            stages(qi - 1 - _STAGES * trips, count, False)


def _fox_kernel(q_ref, k_ref, v_ref, stat_ref, o_ref, sa_ref, sb_ref, m_ref, acc_ref, *, tq):
    qi = pl.program_id(2)
    half = tq // 2

    def scores(ki, dst, hh, diagonal=False):
        start = pl.multiple_of(ki * tq, tq)
        if diagonal:
            for keys, q0 in _diagonal_blocks(tq):
                k = k_ref[0, hh, pl.ds(start, keys), :]
                dst[hh, 0:keys, q0:q0 + half] = lax.dot_general(k, q_ref[0, hh, q0:q0 + half, :], _NT,
                                                                preferred_element_type=F32)
        else:
            k = k_ref[0, hh, pl.ds(start, tq), :]
            dst[hh] = lax.dot_general(k, q_ref[0, hh], _NT, preferred_element_type=F32)

    def absorb(src, ki, hh, diagonal):
        start = pl.multiple_of(ki * tq, tq)
        if diagonal:
            for keys, q0 in _diagonal_blocks(tq):
                s = src[hh, 0:keys, q0:q0 + half]
                key = lax.broadcasted_iota(jnp.int32, s.shape, 0)
                qry = lax.broadcasted_iota(jnp.int32, s.shape, 1) + q0
                s = jnp.where(key <= qry, s, NEG_INF)
                m_new = jnp.max(s, axis=0, keepdims=True)
                p = jnp.exp2(s - m_new)
                v = v_ref[0, hh, pl.ds(start, keys), 0:_FOX_V_ROWS]
                acc_ref[hh, :, q0:q0 + half] = lax.dot_general(v, p.astype(BF16), _TN, preferred_element_type=F32)
                m_ref[hh, :, q0:q0 + half] = m_new
            return
        s = src[hh]
        m_prev = m_ref[hh]
        m_new = jnp.maximum(m_prev, jnp.max(s, axis=0, keepdims=True))
        p = jnp.exp2(s - m_new)
        alpha = jnp.exp2(m_prev - m_new)
        v = v_ref[0, hh, pl.ds(start, tq), 0:_FOX_V_ROWS]
        pv = lax.dot_general(v, p.astype(BF16), _TN, preferred_element_type=F32)
        acc_ref[hh] = alpha * acc_ref[hh] + pv
        m_ref[hh] = m_new

    def first_needed():
        bounds = []
        for hh in range(_FOX_GROUP):
            stat = stat_ref[0, hh]
            tile = lax.broadcasted_iota(jnp.int32, (1, stat.shape[-1]), 1)
            cum_q = jnp.sum(jnp.where(tile == qi, stat[_STAT_CUM_FIRST:_STAT_CUM_FIRST + 1], 0.0))
            decay = cum_q - stat[_STAT_CUM_LAST:_STAT_CUM_LAST + 1]
            bounds.append(_dot_bound(stat, _STAT_FOX_Q, _STAT_FOX_K, qi) + decay)
        return _first_needed_tile(qi, bounds, m_ref)

    _pipelined_tiles(qi, _FOX_GROUP, scores, absorb, first_needed, sa_ref, sb_ref)
    outs = [acc_ref[hh, 0:HEAD_DIM] / acc_ref[hh, HEAD_DIM:HEAD_DIM + 1] for hh in range(_FOX_GROUP)]
    o_ref[0] = jnp.concatenate(outs, axis=0).T.astype(BF16)


def _fox_attention(fqa, fka, fva, stats):
    b, nh, t, _ = fqa.shape
    tq = min(t, 512)
    g = _FOX_GROUP
    return pl.pallas_call(
        functools.partial(_fox_kernel, tq=tq),
        grid=(b, nh // g, t // tq),
        in_specs=[pl.BlockSpec((1, g, tq, LANES), lambda i, j, q: (i, j, q, 0)),
                  pl.BlockSpec((1, g, t, LANES), lambda i, j, q: (i, j, 0, 0)),
                  pl.BlockSpec((1, g, t, LANES), lambda i, j, q: (i, j, 0, 0)),
                  pl.BlockSpec((1, g) + stats.shape[2:], lambda i, j, q: (i, j, 0, 0))],
        out_specs=pl.BlockSpec((1, tq, g * HEAD_DIM), lambda i, j, q: (i, q, j)),
        out_shape=jax.ShapeDtypeStruct((b, t, FOX_WIDTH), BF16),
        scratch_shapes=[pltpu.VMEM((g, tq, tq), F32), pltpu.VMEM((g, tq, tq), F32),
                        pltpu.VMEM((g, 1, tq), F32), pltpu.VMEM((g, _FOX_V_ROWS, tq), F32)],
        compiler_params=_params(("arbitrary", "arbitrary", "arbitrary")),
        name="fox_attn",
    )(fqa, fka, fva, stats)


def _alibi_slope(head):
    return jnp.exp2(jnp.full((1, 1), -8.0 / DIFF_HEADS, F32) * (head + 1).astype(F32))


def _lambda(lam_ref, lam_init):
    lv = lam_ref[...]
    e1 = jnp.exp(jnp.sum(lv[0:1] * lv[1:2], axis=1, keepdims=True))
    e2 = jnp.exp(jnp.sum(lv[2:3] * lv[3:4], axis=1, keepdims=True))
    return e1 - e2 + lam_init


def _diff_normed(o0, o1, lam, axis):
    df = o0 - lam * o1
    return df * lax.rsqrt(jnp.mean(df * df, axis=axis, keepdims=True) + RMS_EPS)


_ONES_ROWS = 16


def _diff_kernel(q_ref, k_ref, v_ref, stat_ref, lam_ref, g_ref, o_ref, sa_ref, sb_ref, m_ref, acc_ref, *, tq,
                 lam_init):
    hd = pl.program_id(1)
    qi = pl.program_id(2)
    slope = _alibi_slope(hd)
    half = tq // 2

    def scores(ki, dst, c, diagonal=False):
        start = pl.multiple_of(ki * tq, tq)
        if diagonal:
            for keys, q0 in _diagonal_blocks(tq):
                k = k_ref[0, c, pl.ds(start, keys), :]
                dst[c, 0:keys, q0:q0 + half] = lax.dot_general(k, q_ref[0, c, q0:q0 + half, :], _NT,
                                                               preferred_element_type=F32)
        else:
            k = k_ref[0, c, pl.ds(start, tq), :]
            dst[c] = lax.dot_general(k, q_ref[0, c], _NT, preferred_element_type=F32)

    def values_t(start, keys):
        v = v_ref[0, 0, pl.ds(start, keys), :]
        return jnp.concatenate([v.T, jnp.ones((_ONES_ROWS, keys), BF16)], axis=0)

    def absorb(src, ki, c, diagonal):
        start = pl.multiple_of(ki * tq, tq)
        if diagonal:
            for keys, q0 in _diagonal_blocks(tq):
                s = src[c, 0:keys, q0:q0 + half]
                key = lax.broadcasted_iota(jnp.int32, s.shape, 0)
                qry = lax.broadcasted_iota(jnp.int32, s.shape, 1) + q0
                ahead = jnp.maximum(key - qry, 0).astype(F32)
                s = jnp.where((key // CHUNK) <= (qry // CHUNK), s - (2.0 * LOG2E * slope) * ahead, NEG_INF)
                m_new = jnp.max(s, axis=0, keepdims=True)
                p = jnp.exp2(s - m_new)
                acc_ref[c, :, q0:q0 + half] = jnp.dot(values_t(start, keys), p.astype(BF16),
                                                      preferred_element_type=F32)
                m_ref[c, :, q0:q0 + half] = m_new
            return
        s = src[c]
        m_prev = m_ref[c]
        m_new = jnp.maximum(m_prev, jnp.max(s, axis=0, keepdims=True))
        p = jnp.exp2(s - m_new)
        alpha = jnp.exp2(m_prev - m_new)
        acc_ref[c] = alpha * acc_ref[c] + jnp.dot(values_t(start, tq), p.astype(BF16), preferred_element_type=F32)
        m_ref[c] = m_new

    def first_needed():
        bounds = []
        for c in range(2):
            stat = stat_ref[0, c]
            tile = lax.broadcasted_iota(jnp.int32, (1, stat.shape[-1]), 1)
            gap = ((qi - 1 - tile) * tq + 1).astype(F32)
            bounds.append(_dot_bound(stat, _STAT_DIFF_Q, _STAT_DIFF_K, qi) - (LOG2E * slope) * gap)
        return _first_needed_tile(qi, bounds, m_ref)

    _pipelined_tiles(qi, 2, scores, absorb, first_needed, sa_ref, sb_ref)
    lam = _lambda(lam_ref, lam_init)
    acc0, acc1 = acc_ref[0], acc_ref[1]
    o0 = acc0[:DIFF_VDIM] / acc0[DIFF_VDIM:DIFF_VDIM + 1]
    o1 = acc1[:DIFF_VDIM] / acc1[DIFF_VDIM:DIFF_VDIM + 1]
    o_ref[0] = (_diff_normed(o0, o1, lam, 0).T * g_ref[...] * (1.0 - lam_init)).astype(BF16)


def _diff_attention(dqa, dka, dvb, stats, lam_vecs, subln_g, lam_init):
    b, nm, t, _ = dqa.shape
    tq = min(t, 512)
    return pl.pallas_call(
        functools.partial(_diff_kernel, tq=tq, lam_init=lam_init),
        grid=(b, nm // 2, t // tq),
        in_specs=[pl.BlockSpec((1, 2, tq, LANES), lambda i, j, q: (i, j, q, 0)),
                  pl.BlockSpec((1, 2, t, LANES), lambda i, j, q: (i, j, 0, 0)),
                  pl.BlockSpec((1, 1, t, DIFF_VDIM), lambda i, j, q: (i, j, 0, 0)),
                  pl.BlockSpec((1, 2) + stats.shape[2:], lambda i, j, q: (i, j, 0, 0)),
                  pl.BlockSpec(lam_vecs.shape, lambda i, j, q: (0, 0)),
                  pl.BlockSpec(subln_g.shape, lambda i, j, q: (0, 0))],
        out_specs=pl.BlockSpec((1, tq, DIFF_VDIM), lambda i, j, q: (i, q, j)),
        out_shape=jax.ShapeDtypeStruct((b, t, DIFF_WIDTH), BF16),
        scratch_shapes=[pltpu.VMEM((2, tq, tq), F32), pltpu.VMEM((2, tq, tq), F32),
                        pltpu.VMEM((2, 1, tq), F32), pltpu.VMEM((2, DIFF_VDIM + _ONES_ROWS, tq), F32)],
        compiler_params=_params(("arbitrary", "arbitrary", "arbitrary")),
        name="diff_attn",
    )(dqa, dka, dvb, stats, lam_vecs, subln_g)


def _prefix_sum_lanes(x):
    n = x.shape[-1]
    lane = lax.broadcasted_iota(jnp.int32, x.shape, x.ndim - 1)
    sh = 1
    while sh < n:
        x = x + jnp.where(lane >= sh, pltpu.roll(x, sh, x.ndim - 1), 0.0)
        sh *= 2
    return x


def _fox_sample_kernel(q_ref, k_ref, v_ref, ck_ref, cv_ref, clf_ref, o_ref, *, t, past):
    r = lax.broadcasted_iota(jnp.int32, (t, t), 0)
    c = lax.broadcasted_iota(jnp.int32, (t, t), 1)
    cum_c = _prefix_sum_lanes(clf_ref[0])
    suffix = (cum_c[:, past - 1:past] - cum_c) * LOG2E
    for hh in range(FOX_HEADS):
        data = slice(0, HEAD_DIM) if hh % 2 == 0 else slice(HEAD_DIM, LANES)
        ext = HEAD_DIM if hh % 2 == 0 else 0
        qa = q_ref[0, hh]
        qf = qa.astype(F32)
        cum_q = sum(qf[:, ext + 8 * p + hh:ext + 8 * p + hh + 1] for p in range(3))
        s_c = jnp.dot(qa[:, data], ck_ref[0, 0, hh].astype(BF16), preferred_element_type=F32)
        s_c = s_c + cum_q + suffix[hh:hh + 1]
        s_n = lax.dot_general(qa, k_ref[0, hh], _NT, preferred_element_type=F32)
        s_n = jnp.where(c <= r, s_n, NEG_INF)
        m = jnp.maximum(jnp.max(s_c, axis=1, keepdims=True), jnp.max(s_n, axis=1, keepdims=True))
        p_c = jnp.exp2(s_c - m)
        p_n = jnp.exp2(s_n - m)
        denom = jnp.sum(p_c, axis=1, keepdims=True) + jnp.sum(p_n, axis=1, keepdims=True)
        o = lax.dot_general(p_c.astype(BF16), cv_ref[0, 0, hh].astype(BF16), _NT, preferred_element_type=F32)
        o = o + jnp.dot(p_n.astype(BF16), v_ref[0, hh][:, 0:HEAD_DIM], preferred_element_type=F32)
        o_ref[0, :, hh * HEAD_DIM:(hh + 1) * HEAD_DIM] = (o / denom).astype(BF16)


def _fox_sample(fqa, fka, fva, cache_k, cache_v, cache_logf):
    b, nh, t, _ = fqa.shape
    past = cache_k.shape[3]
    cache_k, cache_v = jnp.swapaxes(cache_k, 3, 4), jnp.swapaxes(cache_v, 3, 4)
    new = pl.BlockSpec((1, nh, t, LANES), lambda i: (i, 0, 0, 0))
    cache = pl.BlockSpec((1, 1, nh, HEAD_DIM, past), lambda i: (0, i, 0, 0, 0))
    return pl.pallas_call(
        functools.partial(_fox_sample_kernel, t=t, past=past),
        grid=(b,),
        in_specs=[new, new, new, cache, cache, pl.BlockSpec((1, nh, past), lambda i: (i, 0, 0))],
        out_specs=pl.BlockSpec((1, t, FOX_WIDTH), lambda i: (i, 0, 0)),
        out_shape=jax.ShapeDtypeStruct((b, t, FOX_WIDTH), BF16),
        compiler_params=_params(("arbitrary",)),
        name="fox_sample",
    )(fqa, fka, fva, cache_k, cache_v, cache_logf)


def _diff_sample_kernel(q_ref, k_ref, v_ref, ck_ref, cv_ref, lam_ref, g_ref, o_ref, *, t, past, lam_init):
    lane = lax.broadcasted_iota(jnp.int32, (t, LANES), 1)
    pos_q = past + lax.broadcasted_iota(jnp.int32, (t, past), 0)
    pos_c = lax.broadcasted_iota(jnp.int32, (t, past), 1)
    dist_c = jnp.abs(pos_q - pos_c).astype(F32)
    vis_c = (pos_c // CHUNK) <= (pos_q // CHUNK)
    pq_n = past + lax.broadcasted_iota(jnp.int32, (t, t), 0)
    pk_n = past + lax.broadcasted_iota(jnp.int32, (t, t), 1)
    dist_n = jnp.abs(pq_n - pk_n).astype(F32)
    vis_n = (pk_n // CHUNK) <= (pq_n // CHUNK)
    lam = _lambda(lam_ref, lam_init)
    for hd in range(DIFF_HEADS):
        slope = LOG2E * 2.0 ** (-8.0 * (hd + 1) / DIFF_HEADS)
        ck = ck_ref[0, 0, hd].astype(BF16)
        cv = cv_ref[0, 0, hd].astype(BF16)
        v = v_ref[0, hd]
        outs = []
        for c in range(2):
            half = (lane < HEAD_DIM) if c == 0 else (lane >= HEAD_DIM)
            qz = jnp.where(half, q_ref[0, 2 * hd + c], jnp.zeros((), BF16))
            kz = jnp.where(half, k_ref[0, 2 * hd + c], jnp.zeros((), BF16))
            s_c = jnp.dot(qz, ck, preferred_element_type=F32)
            s_c = jnp.where(vis_c, s_c - slope * dist_c, NEG_INF)
            s_n = lax.dot_general(qz, kz, _NT, preferred_element_type=F32)
            s_n = jnp.where(vis_n, s_n - slope * dist_n, NEG_INF)
            m = jnp.maximum(jnp.max(s_c, axis=1, keepdims=True), jnp.max(s_n, axis=1, keepdims=True))
            p_c = jnp.exp2(s_c - m)
            p_n = jnp.exp2(s_n - m)
            denom = jnp.sum(p_c, axis=1, keepdims=True) + jnp.sum(p_n, axis=1, keepdims=True)
            o = jnp.dot(p_c.astype(BF16), cv, preferred_element_type=F32)
            o = o + jnp.dot(p_n.astype(BF16), v, preferred_element_type=F32)
            outs.append(o / denom)
        normed = _diff_normed(outs[0], outs[1], lam, 1) * g_ref[...] * (1.0 - lam_init)
        o_ref[0, :, hd * DIFF_VDIM:(hd + 1) * DIFF_VDIM] = normed.astype(BF16)


def _diff_sample(dqa, dka, dvb, cache_k, cache_v, lam_vecs, subln_g, lam_init):
    b, nm, t, _ = dqa.shape
    past = cache_k.shape[3]
    ck = jnp.transpose(cache_k, (0, 1, 2, 4, 5, 3)).reshape(cache_k.shape[:3] + (DIFF_VDIM, past))
    new = pl.BlockSpec((1, nm, t, LANES), lambda i: (i, 0, 0, 0))
    return pl.pallas_call(
        functools.partial(_diff_sample_kernel, t=t, past=past, lam_init=lam_init),
        grid=(b,),
        in_specs=[new, new, pl.BlockSpec((1, DIFF_HEADS, t, DIFF_VDIM), lambda i: (i, 0, 0, 0)),
                  pl.BlockSpec((1, 1, DIFF_HEADS, DIFF_VDIM, past), lambda i: (0, i, 0, 0, 0)),
                  pl.BlockSpec((1, 1, DIFF_HEADS, past, DIFF_VDIM), lambda i: (0, i, 0, 0, 0)),
                  pl.BlockSpec(lam_vecs.shape, lambda i: (0, 0)),
                  pl.BlockSpec(subln_g.shape, lambda i: (0, 0))],
        out_specs=pl.BlockSpec((1, t, DIFF_WIDTH), lambda i: (i, 0, 0)),
        out_shape=jax.ShapeDtypeStruct((b, t, DIFF_WIDTH), BF16),
        compiler_params=_params(("arbitrary",)),
        name="diff_sample",
    )(dqa, dka, dvb, ck, cache_v, lam_vecs, subln_g)


def _out_proj_kernel(fo_ref, do_ref, fg_ref, dg_ref, x_ref, mod_ref, w_ref, g_ref, b_ref, y_ref, *, alpha):
    def gated(o_ref, gate_ref):
        return o_ref[0] * gate_ref[0]

    branch = jnp.dot(gated(fo_ref, fg_ref), w_ref[0:FOX_WIDTH, :], preferred_element_type=F32)
    branch = branch + jnp.dot(gated(do_ref, dg_ref), w_ref[FOX_WIDTH:, :], preferred_element_type=F32)
    r = alpha * x_ref[0] + mod_ref[0, 2:3, :] * branch
    mu = jnp.mean(r, axis=-1, keepdims=True)
    rc = r - mu
    var = jnp.mean(rc * rc, axis=-1, keepdims=True)
    y_ref[0] = rc * lax.rsqrt(var + LN_EPS) * g_ref[...] + b_ref[...]


def _out_proj(fox_o, diff_o, fg, dg, x, mod, w_out, ln_g, ln_b, alpha):
    b, t, d = x.shape
    tm = 1024 if t % 1024 == 0 else min(t, 512)
    rows = pl.BlockSpec((1, tm, FOX_WIDTH), lambda i, j: (i, j, 0))
    wide = pl.BlockSpec((1, tm, d), lambda i, j: (i, j, 0))

    def full(a):
        return pl.BlockSpec(a.shape, lambda i, j: (0,) * a.ndim)

    return pl.pallas_call(
        functools.partial(_out_proj_kernel, alpha=alpha),
        grid=(b, t // tm),
        in_specs=[rows, rows, rows, rows, wide,
                  pl.BlockSpec((1, 3, d), lambda i, j: (i, 0, 0)),
                  full(w_out), full(ln_g), full(ln_b)],
        out_specs=wide,
        out_shape=jax.ShapeDtypeStruct((b, t, d), F32),
        compiler_params=_params(("arbitrary", "arbitrary")),
        name="out_proj",
    )(fox_o, diff_o, fg, dg, x, mod, w_out, ln_g, ln_b)


def _pack_w_in(w_in):
    fw = FOX_WIDTH
    o = 0
    fq = w_in[:, o:o + fw]; o += fw
    fk = w_in[:, o:o + fw]; o += fw
    fv = w_in[:, o:o + fw]; o += fw
    ff = w_in[:, o:o + FOX_HEADS]; o += FOX_HEADS
    fg = w_in[:, o:o + fw]; o += fw
    dq = w_in[:, o:o + fw]; o += fw
    dk = w_in[:, o:o + fw]; o += fw
    dv = w_in[:, o:o + fw]; o += fw
    dg = w_in[:, o:o + fw]
    qs = HEAD_DIM ** -0.5 * LOG2E
    return jnp.concatenate([fq * qs, fk, fv, fg, dq * qs, dk, dv, dg,
                            jnp.tile(ff, (1, LANES // FOX_HEADS))], axis=1).astype(BF16)


def kernel(x_prompt, x_sample, cache_fox_k, cache_fox_v, cache_fox_logf, cache_diff_k, cache_diff_v,
           c_prompt, c_sample, w_ada, b_ada, w_in, b_f, lambda_q1, lambda_k1, lambda_q2, lambda_k2,
           subln_g, w_out, ln_g, ln_b):
    depth = w_in.shape[0]
    assert depth == 1, "single-layer step"
    layer = 0
    lam_init = 0.8 - 0.6 * math.exp(-0.3 * layer)
    alpha = (2 * depth) ** 0.25
    bp, bs = c_prompt.shape[0], c_sample.shape[0]
    d = x_prompt.shape[-1]
    past = cache_fox_k.shape[3]

    c_all = jnp.concatenate([c_prompt, c_sample, jnp.zeros((-(bp + bs) % 8, d), F32)], axis=0)
    mod = _ada(c_all, w_ada[layer], b_ada[layer]).reshape(c_all.shape[0], 3, d)
    mod_p, mod_s = mod[:bp], mod[bp:bp + bs]

    w = _pack_w_in(w_in[layer])
    bf16x = jnp.tile(b_f[layer], LANES // FOX_HEADS).reshape(1, LANES)
    seg = np.repeat(np.eye(FOX_WIDTH // HEAD_DIM, LANES, dtype=np.float32), HEAD_DIM, axis=0)
    seg = jnp.asarray(seg, BF16)
    lam_vecs = jnp.stack([lambda_q1[layer], lambda_k1[layer], lambda_q2[layer], lambda_k2[layer]])
    g = subln_g[layer].reshape(1, DIFF_VDIM)
    w_o = w_out[layer].astype(BF16)
    lg, lb = ln_g[layer].reshape(1, d), ln_b[layer].reshape(1, d)

    def run(x, mod_x, pos_offset, attend):
        (fk, fv, dk, dv, logf, fqa, fka, fva, fg, dqa, dka, dvb, dg, stats) = _in_proj(
            x, mod_x, w, bf16x, seg, pos_offset)
        stats = jnp.transpose(stats[..., :FOX_WIDTH // HEAD_DIM], (0, 3, 2, 1))
        fox_o, diff_o = attend(fqa, fka, fva, dqa, dka, dvb, stats)
        y = _out_proj(fox_o, diff_o, fg, dg, x, mod_x, w_o, lg, lb, alpha)
        b, t = x.shape[:2]
        states = (fk[None], fv[None], logf[None],
                  dk.reshape(1, b, DIFF_HEADS, t, 2, HEAD_DIM), dv[None])
        return y, states

    def prompt_attend(fqa, fka, fva, dqa, dka, dvb, stats):
        return (_fox_attention(fqa, fka, fva, stats),
                _diff_attention(dqa, dka, dvb, stats, lam_vecs, g, lam_init))

    def sample_attend(fqa, fka, fva, dqa, dka, dvb, stats):
        del stats
        return (_fox_sample(fqa, fka, fva, cache_fox_k, cache_fox_v, cache_fox_logf[layer]),
                _diff_sample(dqa, dka, dvb, cache_diff_k, cache_diff_v, lam_vecs, g, lam_init))

    yp, sp = run(x_prompt, mod_p, 0, prompt_attend)
    ys, ss = run(x_sample, mod_s, past, sample_attend)
    return (yp, ys) + sp + ss
```

```python
import functools
import math

import jax
import jax.numpy as jnp
import numpy as np
from jax import lax
from jax.experimental import pallas as pl
from jax.experimental.pallas import tpu as pltpu

F32 = jnp.float32
BF16 = jnp.bfloat16

HEAD_DIM = 64
FOX_HEADS = 8
DIFF_HEADS = 4
DIFF_VDIM = 2 * HEAD_DIM
FOX_WIDTH = FOX_HEADS * HEAD_DIM
DIFF_WIDTH = DIFF_HEADS * DIFF_VDIM
CHUNK = 64
LN_EPS = 1e-5
RMS_EPS = 1e-5
LANES = 128
NEG_INF = float("-inf")
LOG2E = math.log2(math.e)

_SEC = {name: i * FOX_WIDTH for i, name in enumerate(("fq", "fk", "fv", "fg", "dq", "dk", "dv", "dg"))}
_FF_OFF = 8 * FOX_WIDTH
_W_COLS = _FF_OFF + LANES

_VMEM_LIMIT = 56 * 1024 * 1024


def _params(sem):
    return pltpu.CompilerParams(dimension_semantics=sem, vmem_limit_bytes=_VMEM_LIMIT)


def _ada_kernel(c_ref, w_ref, b_ref, o_ref):
    c = c_ref[...]
    s = c * jax.nn.sigmoid(c)
    o_ref[...] = jnp.dot(s, w_ref[...], preferred_element_type=F32) + b_ref[...]


def _ada(c_all, w_ada, b_ada):
    rows, d = c_all.shape
    n = w_ada.shape[1]
    tn = 512
    return pl.pallas_call(
        _ada_kernel,
        grid=(n // tn,),
        in_specs=[pl.BlockSpec((rows, d), lambda j: (0, 0)),
                  pl.BlockSpec((d, tn), lambda j: (0, j)),
                  pl.BlockSpec((1, tn), lambda j: (0, j))],
        out_specs=pl.BlockSpec((rows, tn), lambda j: (0, j)),
        out_shape=jax.ShapeDtypeStruct((rows, n), F32),
        compiler_params=_params(("arbitrary",)),
        name="ada",
    )(c_all, w_ada, b_ada.reshape(1, n))


def _split3(x):
    hi = x.astype(BF16)
    r1 = x - hi.astype(F32)
    mid = r1.astype(BF16)
    lo = (r1 - mid.astype(F32)).astype(BF16)
    return hi, mid, lo


def _in_proj_kernel(x_ref, mod_ref, w_ref, bf_ref, seg_ref,
                    fk_ref, fv_ref, dk_ref, dv_ref, logf_ref,
                    fqa_ref, fka_ref, fva_ref, fg_ref, dqa_ref, dka_ref, dvb_ref, dg_ref, stat_ref,
                    carry_ref, *, tm, pos_offset):
    ti = pl.program_id(1)
    x = x_ref[0]
    shift = mod_ref[0, 0:1, :]
    scale = mod_ref[0, 1:2, :]
    h = (x * (1.0 + scale) + shift).astype(BF16)

    def proj(name, width=FOX_WIDTH):
        off = _SEC[name] if name in _SEC else _FF_OFF
        return jnp.dot(h, w_ref[:, off:off + width], preferred_element_type=F32)

    lane = lax.broadcasted_iota(jnp.int32, (tm, LANES), 1)
    row = lax.broadcasted_iota(jnp.int32, (tm, LANES), 0)
    lower = lane < HEAD_DIM
    grp = (lane // 8) % 4

    def by_group(a, b, c):
        return jnp.where(grp == 0, a, jnp.where(grp == 1, b, jnp.where(grp == 2, c, jnp.zeros_like(a))))

    def max_sq_norm(z):
        zr = z.astype(BF16).astype(F32)
        sq = zr * zr
        if tm % 32 == 0:
            sq = jnp.maximum(sq[0:tm // 2], sq[tm // 2:tm])
        sq = jnp.dot(sq.astype(BF16), seg_ref[...], preferred_element_type=F32)
        return jnp.max(sq, axis=0, keepdims=True)

    logf = jax.nn.log_sigmoid(proj("ff", LANES) + bf_ref[...])
    logf_ref[0] = logf.T[0:FOX_HEADS] if tm % LANES == 0 else logf

    dq, dk, dv = proj("dq"), proj("dk"), proj("dv")
    pos = (row + (ti * tm + pos_offset)).astype(F32)
    sub = lane % HEAD_DIM
    for hd in range(DIFF_HEADS):
        slope = 2.0 ** (-8.0 * (hd + 1) / DIFF_HEADS)
        sl = slice(hd * LANES, (hd + 1) * LANES)
        p0, p1, p2 = (a.astype(F32) for a in _split3((slope * LOG2E) * pos))
        alibi = jnp.where(sub % 3 == 0, p0, jnp.where(sub % 3 == 1, p1, p2))
        alibi_q = jnp.where(sub < 3, -alibi, (sub < 6).astype(F32))
        alibi_k = jnp.where(sub < 3, 1.0, jnp.where(sub < 6, alibi, 0.0))
        dqa_ref[0, 2 * hd] = jnp.where(lower, dq[:, sl], alibi_q).astype(BF16)
        dqa_ref[0, 2 * hd + 1] = jnp.where(lower, alibi_q, dq[:, sl]).astype(BF16)
        dka_ref[0, 2 * hd] = jnp.where(lower, dk[:, sl], alibi_k).astype(BF16)
        dka_ref[0, 2 * hd + 1] = jnp.where(lower, alibi_k, dk[:, sl]).astype(BF16)
        if tm % LANES:
            dk_ref[0, hd] = dk[:, sl]
        dv_ref[0, hd] = dv[:, sl]
        dvb_ref[0, hd] = dv[:, sl].astype(BF16)
    if tm % LANES == 0:
        dk_ref[0] = dk.T.reshape(dk_ref.shape[1:])
    dg_ref[0] = jax.nn.silu(proj("dg")).astype(BF16)
    diff_norms = [max_sq_norm(dq), max_sq_norm(dk)]

    tri_r = lax.broadcasted_iota(jnp.int32, (tm, tm), 0)
    tri_c = lax.broadcasted_iota(jnp.int32, (tm, tm), 1)
    tri = (tri_c <= tri_r).astype(BF16)
    part = jnp.dot(tri, by_group(*_split3(logf)), preferred_element_type=F32)
    zq, zk, zv = proj("fq"), proj("fk"), proj("fv")
    local = part + pltpu.roll(part, 8, 1) + pltpu.roll(part, 16, 1) + pltpu.roll(part, 24, 1)

    @pl.when(ti == 0)
    def _():
        carry_ref[...] = jnp.zeros_like(carry_ref)

    cum = local + carry_ref[...]
    carry_ref[...] = cum[tm - 1:tm, :]
    cum2 = cum * LOG2E
    pieces = by_group(*(a.astype(F32) for a in _split3(cum2)))
    behind = pltpu.roll(pieces, 24, 1)
    u = lane % HEAD_DIM
    ext_q = jnp.where(u < 24, pieces, (u < 48).astype(F32))

    def ext_k(h):
        mine = u % 8 == h
        return jnp.where(mine & (u < 24), 1.0, jnp.where(mine & (u < 48), -behind, 0.0))

    fox_norms = [max_sq_norm(zq), max_sq_norm(zk)]
    e_even = (lane == HEAD_DIM).astype(F32)
    for j in range(FOX_HEADS // 2):
        sl = slice(j * LANES, (j + 1) * LANES)
        fqa_ref[0, 2 * j] = jnp.where(lower, zq[:, sl], ext_q).astype(BF16)
        fqa_ref[0, 2 * j + 1] = jnp.where(lower, ext_q, zq[:, sl]).astype(BF16)
        fka_ref[0, 2 * j] = jnp.where(lower, zk[:, sl], ext_k(2 * j)).astype(BF16)
        fka_ref[0, 2 * j + 1] = jnp.where(lower, ext_k(2 * j + 1), zk[:, sl]).astype(BF16)
        fva_ref[0, 2 * j] = jnp.where(lower, zv[:, sl], e_even).astype(BF16)
        fva_ref[0, 2 * j + 1] = jnp.where(lower, pltpu.roll(zv[:, sl], HEAD_DIM, 1), e_even).astype(BF16)
    if tm % LANES == 0:
        fk_ref[0] = zk.T.reshape(fk_ref.shape[1:])
        fv_ref[0] = zv.T.reshape(fv_ref.shape[1:])
    else:
        for hd in range(FOX_HEADS):
            fk_ref[0, hd] = zk[:, hd * HEAD_DIM:(hd + 1) * HEAD_DIM]
            fv_ref[0, hd] = zv[:, hd * HEAD_DIM:(hd + 1) * HEAD_DIM]
    fg_ref[0] = jax.nn.silu(proj("fg")).astype(BF16)
    stat_ref[0, 0] = jnp.concatenate(
        fox_norms + diff_norms + [cum2[0:1], cum2[tm - 1:tm], jnp.zeros((2, LANES), F32)],
        axis=0)


def _in_proj(x, mod, w, bf16x, seg, pos_offset):
    b, t, d = x.shape
    tm = min(t, 512)
    nt = t // tm
    head_major = tm % LANES == 0

    def full(a):
        return pl.BlockSpec(a.shape, lambda i, j: (0,) * a.ndim)

    def heads(n, width):
        return pl.BlockSpec((1, n, tm, width), lambda i, j: (i, 0, j, 0))

    def cache_shape(n, width):
        return jax.ShapeDtypeStruct((b, n, width, t) if head_major else (b, n, t, width), F32)

    def cache_spec(n, width):
        return pl.BlockSpec((1, n, width, tm), lambda i, j: (i, 0, 0, j)) if head_major else heads(n, width)

    rows = pl.BlockSpec((1, tm, FOX_WIDTH), lambda i, j: (i, j, 0))
    out_shape = (
        cache_shape(FOX_HEADS, HEAD_DIM),
        cache_shape(FOX_HEADS, HEAD_DIM),
        cache_shape(DIFF_HEADS, DIFF_VDIM),
        jax.ShapeDtypeStruct((b, DIFF_HEADS, t, DIFF_VDIM), F32),
        jax.ShapeDtypeStruct((b, FOX_HEADS, t) if head_major else (b, t, LANES), F32),
        jax.ShapeDtypeStruct((b, FOX_HEADS, t, LANES), BF16),
        jax.ShapeDtypeStruct((b, FOX_HEADS, t, LANES), BF16),
        jax.ShapeDtypeStruct((b, FOX_HEADS, t, LANES), BF16),
        jax.ShapeDtypeStruct((b, t, FOX_WIDTH), BF16),
        jax.ShapeDtypeStruct((b, 2 * DIFF_HEADS, t, LANES), BF16),
        jax.ShapeDtypeStruct((b, 2 * DIFF_HEADS, t, LANES), BF16),
        jax.ShapeDtypeStruct((b, DIFF_HEADS, t, DIFF_VDIM), BF16),
        jax.ShapeDtypeStruct((b, t, DIFF_WIDTH), BF16),
        jax.ShapeDtypeStruct((b, nt, 8, LANES), F32),
    )
    out_specs = (
        cache_spec(FOX_HEADS, HEAD_DIM), cache_spec(FOX_HEADS, HEAD_DIM),
        cache_spec(DIFF_HEADS, DIFF_VDIM), heads(DIFF_HEADS, DIFF_VDIM),
        (pl.BlockSpec((1, FOX_HEADS, tm), lambda i, j: (i, 0, j)) if head_major
         else pl.BlockSpec((1, tm, LANES), lambda i, j: (i, j, 0))),
        heads(FOX_HEADS, LANES), heads(FOX_HEADS, LANES), heads(FOX_HEADS, LANES), rows,
        heads(2 * DIFF_HEADS, LANES), heads(2 * DIFF_HEADS, LANES), heads(DIFF_HEADS, DIFF_VDIM), rows,
        pl.BlockSpec((1, 1, 8, LANES), lambda i, j: (i, j, 0, 0)),
    )
    outs = pl.pallas_call(
        functools.partial(_in_proj_kernel, tm=tm, pos_offset=pos_offset),
        grid=(b, nt),
        in_specs=[pl.BlockSpec((1, tm, d), lambda i, j: (i, j, 0)),
                  pl.BlockSpec((1, 3, d), lambda i, j: (i, 0, 0)),
                  full(w), full(bf16x), full(seg)],
        out_specs=out_specs,
        out_shape=out_shape,
        scratch_shapes=[pltpu.VMEM((1, LANES), F32)],
        compiler_params=_params(("arbitrary", "arbitrary")),
        name="in_proj",
    )(x, mod, w, bf16x, seg)
    if head_major:
        outs = tuple(jnp.swapaxes(o, 2, 3) for o in outs[:3]) + outs[3:]
    else:
        outs = outs[:4] + (jnp.swapaxes(outs[4][:, :, :FOX_HEADS], 1, 2),) + outs[5:]
    return outs


_NT = (((1,), (1,)), ((), ()))
_TN = (((0,), (0,)), ((), ()))
_STAGES = 4
_FOX_V_ROWS = HEAD_DIM + 16
_FOX_GROUP = 4


_SKIP_MARGIN = 136.0
_STAT_FOX_Q, _STAT_FOX_K, _STAT_DIFF_Q, _STAT_DIFF_K, _STAT_CUM_FIRST, _STAT_CUM_LAST = range(6)


def _diagonal_blocks(tq):
    assert tq % (2 * LANES) == 0
    return ((tq // 2, 0), (tq, tq // 2))


def _dot_bound(stat, q_row, k_row, qi):
    tile = lax.broadcasted_iota(jnp.int32, (1, stat.shape[-1]), 1)
    qn2 = jnp.sum(jnp.where(tile == qi, stat[q_row:q_row + 1], 0.0))
    return jnp.sqrt(qn2 * stat[k_row:k_row + 1]) * 1.01 + 1.0


def _first_needed_tile(qi, bounds, m_ref):
    first = qi
    for c, bound in enumerate(bounds):
        tile = lax.broadcasted_iota(jnp.int32, bound.shape, 1)
        needed = (bound >= jnp.min(m_ref[c]) - _SKIP_MARGIN) & (tile < qi)
        first = jnp.minimum(first, jnp.min(jnp.where(needed, tile, qi)))
    return first


def _pipelined_tiles(qi, n_chains, scores, absorb, first_needed, sa_ref, sb_ref):
    chains = range(n_chains)
    bufs = (sa_ref, sb_ref)
    for c in chains:
        scores(qi, sa_ref, c, True)
        scores(jnp.maximum(qi - 1, 0), sb_ref, c)
        absorb(sa_ref, qi, c, True)
    n = qi - first_needed()

    def stages(top, count, more):
        for s in range(count):
            for c in chains:
                if more or s + 1 < count:
                    scores(jnp.maximum(top - s - 1, 0), bufs[s % 2], c)
                absorb(bufs[(s + 1) % 2], top - s, c, False)

    def body(i, carry):
        stages(qi - 1 - _STAGES * i, _STAGES, True)
        return carry

    trips = jnp.maximum(n - 1, 0) // _STAGES
    lax.fori_loop(0, trips, body, 0)
    rem = n - _STAGES * trips
    for count in range(1, _STAGES + 1):
        @pl.when(rem == count)
        def _(count=count):
            stages(qi - 1 - _STAGES * trips, count, False)


def _fox_kernel(q_ref, k_ref, v_ref, stat_ref, o_ref, sa_ref, sb_ref, m_ref, acc_ref, *, tq):
    qi = pl.program_id(2)
    half = tq // 2

    def scores(ki, dst, hh, diagonal=False):
        start = pl.multiple_of(ki * tq, tq)
        if diagonal:
            for keys, q0 in _diagonal_blocks(tq):
                k = k_ref[0, hh, pl.ds(start, keys), :]
                dst[hh, 0:keys, q0:q0 + half] = lax.dot_general(k, q_ref[0, hh, q0:q0 + half, :], _NT,
                                                                preferred_element_type=F32)
        else:
            k = k_ref[0, hh, pl.ds(start, tq), :]
            dst[hh] = lax.dot_general(k, q_ref[0, hh], _NT, preferred_element_type=F32)

    def absorb(src, ki, hh, diagonal):
        start = pl.multiple_of(ki * tq, tq)
        if diagonal:
            for keys, q0 in _diagonal_blocks(tq):
                s = src[hh, 0:keys, q0:q0 + half]
                key = lax.broadcasted_iota(jnp.int32, s.shape, 0)
                qry = lax.broadcasted_iota(jnp.int32, s.shape, 1) + q0
                s = jnp.where(key <= qry, s, NEG_INF)
                m_new = jnp.max(s, axis=0, keepdims=True)
                p = jnp.exp2(s - m_new)
                v = v_ref[0, hh, pl.ds(start, keys), 0:_FOX_V_ROWS]
                acc_ref[hh, :, q0:q0 + half] = lax.dot_general(v, p.astype(BF16), _TN, preferred_element_type=F32)
                m_ref[hh, :, q0:q0 + half] = m_new
            return
        s = src[hh]
        m_prev = m_ref[hh]
        m_new = jnp.maximum(m_prev, jnp.max(s, axis=0, keepdims=True))
        p = jnp.exp2(s - m_new)
        alpha = jnp.exp2(m_prev - m_new)
        v = v_ref[0, hh, pl.ds(start, tq), 0:_FOX_V_ROWS]
        pv = lax.dot_general(v, p.astype(BF16), _TN, preferred_element_type=F32)
        acc_ref[hh] = alpha * acc_ref[hh] + pv
        m_ref[hh] = m_new

    def first_needed():
        bounds = []
        for hh in range(_FOX_GROUP):
            stat = stat_ref[0, hh]
            tile = lax.broadcasted_iota(jnp.int32, (1, stat.shape[-1]), 1)
            cum_q = jnp.sum(jnp.where(tile == qi, stat[_STAT_CUM_FIRST:_STAT_CUM_FIRST + 1], 0.0))
            decay = cum_q - stat[_STAT_CUM_LAST:_STAT_CUM_LAST + 1]
            bounds.append(_dot_bound(stat, _STAT_FOX_Q, _STAT_FOX_K, qi) + decay)
        return _first_needed_tile(qi, bounds, m_ref)

    _pipelined_tiles(qi, _FOX_GROUP, scores, absorb, first_needed, sa_ref, sb_ref)
    outs = [acc_ref[hh, 0:HEAD_DIM] / acc_ref[hh, HEAD_DIM:HEAD_DIM + 1] for hh in range(_FOX_GROUP)]
    o_ref[0] = jnp.concatenate(outs, axis=0).T.astype(BF16)


def _fox_attention(fqa, fka, fva, stats):
    b, nh, t, _ = fqa.shape
    tq = min(t, 512)
    g = _FOX_GROUP
    return pl.pallas_call(
        functools.partial(_fox_kernel, tq=tq),
        grid=(b, nh // g, t // tq),
        in_specs=[pl.BlockSpec((1, g, tq, LANES), lambda i, j, q: (i, j, q, 0)),
                  pl.BlockSpec((1, g, t, LANES), lambda i, j, q: (i, j, 0, 0)),
                  pl.BlockSpec((1, g, t, LANES), lambda i, j, q: (i, j, 0, 0)),
                  pl.BlockSpec((1, g) + stats.shape[2:], lambda i, j, q: (i, j, 0, 0))],
        out_specs=pl.BlockSpec((1, tq, g * HEAD_DIM), lambda i, j, q: (i, q, j)),
        out_shape=jax.ShapeDtypeStruct((b, t, FOX_WIDTH), BF16),
        scratch_shapes=[pltpu.VMEM((g, tq, tq), F32), pltpu.VMEM((g, tq, tq), F32),
                        pltpu.VMEM((g, 1, tq), F32), pltpu.VMEM((g, _FOX_V_ROWS, tq), F32)],
        compiler_params=_params(("arbitrary", "arbitrary", "arbitrary")),
        name="fox_attn",
    )(fqa, fka, fva, stats)


def _alibi_slope(head):
    return jnp.exp2(jnp.full((1, 1), -8.0 / DIFF_HEADS, F32) * (head + 1).astype(F32))


def _lambda(lam_ref, lam_init):
    lv = lam_ref[...]
    e1 = jnp.exp(jnp.sum(lv[0:1] * lv[1:2], axis=1, keepdims=True))
    e2 = jnp.exp(jnp.sum(lv[2:3] * lv[3:4], axis=1, keepdims=True))
    return e1 - e2 + lam_init


def _diff_normed(o0, o1, lam, axis):
    df = o0 - lam * o1
    return df * lax.rsqrt(jnp.mean(df * df, axis=axis, keepdims=True) + RMS_EPS)


_ONES_ROWS = 16


def _diff_kernel(q_ref, k_ref, v_ref, stat_ref, lam_ref, g_ref, o_ref, sa_ref, sb_ref, m_ref, acc_ref, *, tq,
                 lam_init):
    hd = pl.program_id(1)
    qi = pl.program_id(2)
    slope = _alibi_slope(hd)
    half = tq // 2

    def scores(ki, dst, c, diagonal=False):
        start = pl.multiple_of(ki * tq, tq)
        if diagonal:
            for keys, q0 in _diagonal_blocks(tq):
                k = k_ref[0, c, pl.ds(start, keys), :]
                dst[c, 0:keys, q0:q0 + half] = lax.dot_general(k, q_ref[0, c, q0:q0 + half, :], _NT,
                                                               preferred_element_type=F32)
        else:
            k = k_ref[0, c, pl.ds(start, tq), :]
            dst[c] = lax.dot_general(k, q_ref[0, c], _NT, preferred_element_type=F32)

    def values_t(start, keys):
        v = v_ref[0, 0, pl.ds(start, keys), :]
        return jnp.concatenate([v.T, jnp.ones((_ONES_ROWS, keys), BF16)], axis=0)

    def absorb(src, ki, c, diagonal):
        start = pl.multiple_of(ki * tq, tq)
        if diagonal:
            for keys, q0 in _diagonal_blocks(tq):
                s = src[c, 0:keys, q0:q0 + half]
                key = lax.broadcasted_iota(jnp.int32, s.shape, 0)
                qry = lax.broadcasted_iota(jnp.int32, s.shape, 1) + q0
                ahead = jnp.maximum(key - qry, 0).astype(F32)
                s = jnp.where((key // CHUNK) <= (qry // CHUNK), s - (2.0 * LOG2E * slope) * ahead, NEG_INF)
                m_new = jnp.max(s, axis=0, keepdims=True)
                p = jnp.exp2(s - m_new)
                acc_ref[c, :, q0:q0 + half] = jnp.dot(values_t(start, keys), p.astype(BF16),
                                                      preferred_element_type=F32)
                m_ref[c, :, q0:q0 + half] = m_new
            return
        s = src[c]
        m_prev = m_ref[c]
        m_new = jnp.maximum(m_prev, jnp.max(s, axis=0, keepdims=True))
        p = jnp.exp2(s - m_new)
        alpha = jnp.exp2(m_prev - m_new)
        acc_ref[c] = alpha * acc_ref[c] + jnp.dot(values_t(start, tq), p.astype(BF16), preferred_element_type=F32)
        m_ref[c] = m_new

    def first_needed():
        bounds = []
        for c in range(2):
            stat = stat_ref[0, c]
            tile = lax.broadcasted_iota(jnp.int32, (1, stat.shape[-1]), 1)
            gap = ((qi - 1 - tile) * tq + 1).astype(F32)
            bounds.append(_dot_bound(stat, _STAT_DIFF_Q, _STAT_DIFF_K, qi) - (LOG2E * slope) * gap)
        return _first_needed_tile(qi, bounds, m_ref)

    _pipelined_tiles(qi, 2, scores, absorb, first_needed, sa_ref, sb_ref)
    lam = _lambda(lam_ref, lam_init)
    acc0, acc1 = acc_ref[0], acc_ref[1]
    o0 = acc0[:DIFF_VDIM] / acc0[DIFF_VDIM:DIFF_VDIM + 1]
    o1 = acc1[:DIFF_VDIM] / acc1[DIFF_VDIM:DIFF_VDIM + 1]
    o_ref[0] = (_diff_normed(o0, o1, lam, 0).T * g_ref[...] * (1.0 - lam_init)).astype(BF16)


def _diff_attention(dqa, dka, dvb, stats, lam_vecs, subln_g, lam_init):
    b, nm, t, _ = dqa.shape
    tq = min(t, 512)
    return pl.pallas_call(
        functools.partial(_diff_kernel, tq=tq, lam_init=lam_init),
        grid=(b, nm // 2, t // tq),
        in_specs=[pl.BlockSpec((1, 2, tq, LANES), lambda i, j, q: (i, j, q, 0)),
                  pl.BlockSpec((1, 2, t, LANES), lambda i, j, q: (i, j, 0, 0)),
                  pl.BlockSpec((1, 1, t, DIFF_VDIM), lambda i, j, q: (i, j, 0, 0)),
                  pl.BlockSpec((1, 2) + stats.shape[2:], lambda i, j, q: (i, j, 0, 0)),
                  pl.BlockSpec(lam_vecs.shape, lambda i, j, q: (0, 0)),
                  pl.BlockSpec(subln_g.shape, lambda i, j, q: (0, 0))],
        out_specs=pl.BlockSpec((1, tq, DIFF_VDIM), lambda i, j, q: (i, q, j)),
        out_shape=jax.ShapeDtypeStruct((b, t, DIFF_WIDTH), BF16),
        scratch_shapes=[pltpu.VMEM((2, tq, tq), F32), pltpu.VMEM((2, tq, tq), F32),
                        pltpu.VMEM((2, 1, tq), F32), pltpu.VMEM((2, DIFF_VDIM + _ONES_ROWS, tq), F32)],
        compiler_params=_params(("arbitrary", "arbitrary", "arbitrary")),
        name="diff_attn",
    )(dqa, dka, dvb, stats, lam_vecs, subln_g)


def _prefix_sum_lanes(x):
    n = x.shape[-1]
    lane = lax.broadcasted_iota(jnp.int32, x.shape, x.ndim - 1)
    sh = 1
    while sh < n:
        x = x + jnp.where(lane >= sh, pltpu.roll(x, sh, x.ndim - 1), 0.0)
        sh *= 2
    return x


def _fox_sample_kernel(q_ref, k_ref, v_ref, ck_ref, cv_ref, clf_ref, o_ref, *, t, past):
    r = lax.broadcasted_iota(jnp.int32, (t, t), 0)
    c = lax.broadcasted_iota(jnp.int32, (t, t), 1)
    cum_c = _prefix_sum_lanes(clf_ref[0])
    suffix = (cum_c[:, past - 1:past] - cum_c) * LOG2E
    for hh in range(FOX_HEADS):
        data = slice(0, HEAD_DIM) if hh % 2 == 0 else slice(HEAD_DIM, LANES)
        ext = HEAD_DIM if hh % 2 == 0 else 0
        qa = q_ref[0, hh]
        qf = qa.astype(F32)
        cum_q = sum(qf[:, ext + 8 * p + hh:ext + 8 * p + hh + 1] for p in range(3))
        s_c = jnp.dot(qa[:, data], ck_ref[0, 0, hh].astype(BF16), preferred_element_type=F32)
        s_c = s_c + cum_q + suffix[hh:hh + 1]
        s_n = lax.dot_general(qa, k_ref[0, hh], _NT, preferred_element_type=F32)
        s_n = jnp.where(c <= r, s_n, NEG_INF)
        m = jnp.maximum(jnp.max(s_c, axis=1, keepdims=True), jnp.max(s_n, axis=1, keepdims=True))
        p_c = jnp.exp2(s_c - m)
        p_n = jnp.exp2(s_n - m)
        denom = jnp.sum(p_c, axis=1, keepdims=True) + jnp.sum(p_n, axis=1, keepdims=True)
        o = lax.dot_general(p_c.astype(BF16), cv_ref[0, 0, hh].astype(BF16), _NT, preferred_element_type=F32)
        o = o + jnp.dot(p_n.astype(BF16), v_ref[0, hh][:, 0:HEAD_DIM], preferred_element_type=F32)
        o_ref[0, :, hh * HEAD_DIM:(hh + 1) * HEAD_DIM] = (o / denom).astype(BF16)


def _fox_sample(fqa, fka, fva, cache_k, cache_v, cache_logf):
    b, nh, t, _ = fqa.shape
    past = cache_k.shape[3]
    cache_k, cache_v = jnp.swapaxes(cache_k, 3, 4), jnp.swapaxes(cache_v, 3, 4)
    new = pl.BlockSpec((1, nh, t, LANES), lambda i: (i, 0, 0, 0))
    cache = pl.BlockSpec((1, 1, nh, HEAD_DIM, past), lambda i: (0, i, 0, 0, 0))
    return pl.pallas_call(
        functools.partial(_fox_sample_kernel, t=t, past=past),
        grid=(b,),
        in_specs=[new, new, new, cache, cache, pl.BlockSpec((1, nh, past), lambda i: (i, 0, 0))],
        out_specs=pl.BlockSpec((1, t, FOX_WIDTH), lambda i: (i, 0, 0)),
        out_shape=jax.ShapeDtypeStruct((b, t, FOX_WIDTH), BF16),
        compiler_params=_params(("arbitrary",)),
        name="fox_sample",
    )(fqa, fka, fva, cache_k, cache_v, cache_logf)


def _diff_sample_kernel(q_ref, k_ref, v_ref, ck_ref, cv_ref, lam_ref, g_ref, o_ref, *, t, past, lam_init):
    lane = lax.broadcasted_iota(jnp.int32, (t, LANES), 1)
    pos_q = past + lax.broadcasted_iota(jnp.int32, (t, past), 0)
    pos_c = lax.broadcasted_iota(jnp.int32, (t, past), 1)
    dist_c = jnp.abs(pos_q - pos_c).astype(F32)
    vis_c = (pos_c // CHUNK) <= (pos_q // CHUNK)
    pq_n = past + lax.broadcasted_iota(jnp.int32, (t, t), 0)
    pk_n = past + lax.broadcasted_iota(jnp.int32, (t, t), 1)
    dist_n = jnp.abs(pq_n - pk_n).astype(F32)
    vis_n = (pk_n // CHUNK) <= (pq_n // CHUNK)
    lam = _lambda(lam_ref, lam_init)
    for hd in range(DIFF_HEADS):
        slope = LOG2E * 2.0 ** (-8.0 * (hd + 1) / DIFF_HEADS)
        ck = ck_ref[0, 0, hd].astype(BF16)
        cv = cv_ref[0, 0, hd].astype(BF16)
        v = v_ref[0, hd]
        outs = []
        for c in range(2):
            half = (lane < HEAD_DIM) if c == 0 else (lane >= HEAD_DIM)
            qz = jnp.where(half, q_ref[0, 2 * hd + c], jnp.zeros((), BF16))
            kz = jnp.where(half, k_ref[0, 2 * hd + c], jnp.zeros((), BF16))
            s_c = jnp.dot(qz, ck, preferred_element_type=F32)
            s_c = jnp.where(vis_c, s_c - slope * dist_c, NEG_INF)
            s_n = lax.dot_general(qz, kz, _NT, preferred_element_type=F32)
            s_n = jnp.where(vis_n, s_n - slope * dist_n, NEG_INF)
            m = jnp.maximum(jnp.max(s_c, axis=1, keepdims=True), jnp.max(s_n, axis=1, keepdims=True))
            p_c = jnp.exp2(s_c - m)
            p_n = jnp.exp2(s_n - m)
            denom = jnp.sum(p_c, axis=1, keepdims=True) + jnp.sum(p_n, axis=1, keepdims=True)
            o = jnp.dot(p_c.astype(BF16), cv, preferred_element_type=F32)
            o = o + jnp.dot(p_n.astype(BF16), v, preferred_element_type=F32)
            outs.append(o / denom)
        normed = _diff_normed(outs[0], outs[1], lam, 1) * g_ref[...] * (1.0 - lam_init)
        o_ref[0, :, hd * DIFF_VDIM:(hd + 1) * DIFF_VDIM] = normed.astype(BF16)


def _diff_sample(dqa, dka, dvb, cache_k, cache_v, lam_vecs, subln_g, lam_init):
    b, nm, t, _ = dqa.shape
    past = cache_k.shape[3]
    ck = jnp.transpose(cache_k, (0, 1, 2, 4, 5, 3)).reshape(cache_k.shape[:3] + (DIFF_VDIM, past))
    new = pl.BlockSpec((1, nm, t, LANES), lambda i: (i, 0, 0, 0))
    return pl.pallas_call(
        functools.partial(_diff_sample_kernel, t=t, past=past, lam_init=lam_init),
        grid=(b,),
        in_specs=[new, new, pl.BlockSpec((1, DIFF_HEADS, t, DIFF_VDIM), lambda i: (i, 0, 0, 0)),
                  pl.BlockSpec((1, 1, DIFF_HEADS, DIFF_VDIM, past), lambda i: (0, i, 0, 0, 0)),
                  pl.BlockSpec((1, 1, DIFF_HEADS, past, DIFF_VDIM), lambda i: (0, i, 0, 0, 0)),
                  pl.BlockSpec(lam_vecs.shape, lambda i: (0, 0)),
                  pl.BlockSpec(subln_g.shape, lambda i: (0, 0))],
        out_specs=pl.BlockSpec((1, t, DIFF_WIDTH), lambda i: (i, 0, 0)),
        out_shape=jax.ShapeDtypeStruct((b, t, DIFF_WIDTH), BF16),
        compiler_params=_params(("arbitrary",)),
        name="diff_sample",
    )(dqa, dka, dvb, ck, cache_v, lam_vecs, subln_g)


def _out_proj_kernel(fo_ref, do_ref, fg_ref, dg_ref, x_ref, mod_ref, w_ref, g_ref, b_ref, y_ref, *, alpha):
    def gated(o_ref, gate_ref):
        return o_ref[0] * gate_ref[0]

    branch = jnp.dot(gated(fo_ref, fg_ref), w_ref[0:FOX_WIDTH, :], preferred_element_type=F32)
    branch = branch + jnp.dot(gated(do_ref, dg_ref), w_ref[FOX_WIDTH:, :], preferred_element_type=F32)
    r = alpha * x_ref[0] + mod_ref[0, 2:3, :] * branch
    mu = jnp.mean(r, axis=-1, keepdims=True)
    rc = r - mu
    var = jnp.mean(rc * rc, axis=-1, keepdims=True)
    y_ref[0] = rc * lax.rsqrt(var + LN_EPS) * g_ref[...] + b_ref[...]


def _out_proj(fox_o, diff_o, fg, dg, x, mod, w_out, ln_g, ln_b, alpha):
    b, t, d = x.shape
    tm = 1024 if t % 1024 == 0 else min(t, 512)
    rows = pl.BlockSpec((1, tm, FOX_WIDTH), lambda i, j: (i, j, 0))
    wide = pl.BlockSpec((1, tm, d), lambda i, j: (i, j, 0))

    def full(a):
        return pl.BlockSpec(a.shape, lambda i, j: (0,) * a.ndim)

    return pl.pallas_call(
        functools.partial(_out_proj_kernel, alpha=alpha),
        grid=(b, t // tm),
        in_specs=[rows, rows, rows, rows, wide,
                  pl.BlockSpec((1, 3, d), lambda i, j: (i, 0, 0)),
                  full(w_out), full(ln_g), full(ln_b)],
        out_specs=wide,
        out_shape=jax.ShapeDtypeStruct((b, t, d), F32),
        compiler_params=_params(("arbitrary", "arbitrary")),
        name="out_proj",
    )(fox_o, diff_o, fg, dg, x, mod, w_out, ln_g, ln_b)


def _pack_w_in(w_in):
    fw = FOX_WIDTH
    o = 0
    fq = w_in[:, o:o + fw]; o += fw
    fk = w_in[:, o:o + fw]; o += fw
    fv = w_in[:, o:o + fw]; o += fw
    ff = w_in[:, o:o + FOX_HEADS]; o += FOX_HEADS
    fg = w_in[:, o:o + fw]; o += fw
    dq = w_in[:, o:o + fw]; o += fw
    dk = w_in[:, o:o + fw]; o += fw
    dv = w_in[:, o:o + fw]; o += fw
    dg = w_in[:, o:o + fw]
    qs = HEAD_DIM ** -0.5 * LOG2E
    return jnp.concatenate([fq * qs, fk, fv, fg, dq * qs, dk, dv, dg,
                            jnp.tile(ff, (1, LANES // FOX_HEADS))], axis=1).astype(BF16)


def kernel(x_prompt, x_sample, cache_fox_k, cache_fox_v, cache_fox_logf, cache_diff_k, cache_diff_v,
           c_prompt, c_sample, w_ada, b_ada, w_in, b_f, lambda_q1, lambda_k1, lambda_q2, lambda_k2,
           subln_g, w_out, ln_g, ln_b):
    depth = w_in.shape[0]
    assert depth == 1, "single-layer step"
    layer = 0
    lam_init = 0.8 - 0.6 * math.exp(-0.3 * layer)
    alpha = (2 * depth) ** 0.25
    bp, bs = c_prompt.shape[0], c_sample.shape[0]
    d = x_prompt.shape[-1]
    past = cache_fox_k.shape[3]

    c_all = jnp.concatenate([c_prompt, c_sample, jnp.zeros((-(bp + bs) % 8, d), F32)], axis=0)
    mod = _ada(c_all, w_ada[layer], b_ada[layer]).reshape(c_all.shape[0], 3, d)
    mod_p, mod_s = mod[:bp], mod[bp:bp + bs]

    w = _pack_w_in(w_in[layer])
    bf16x = jnp.tile(b_f[layer], LANES // FOX_HEADS).reshape(1, LANES)
    seg = np.repeat(np.eye(FOX_WIDTH // HEAD_DIM, LANES, dtype=np.float32), HEAD_DIM, axis=0)
    seg = jnp.asarray(seg, BF16)
    lam_vecs = jnp.stack([lambda_q1[layer], lambda_k1[layer], lambda_q2[layer], lambda_k2[layer]])
    g = subln_g[layer].reshape(1, DIFF_VDIM)
    w_o = w_out[layer].astype(BF16)
    lg, lb = ln_g[layer].reshape(1, d), ln_b[layer].reshape(1, d)

    def run(x, mod_x, pos_offset, attend):
        (fk, fv, dk, dv, logf, fqa, fka, fva, fg, dqa, dka, dvb, dg, stats) = _in_proj(
            x, mod_x, w, bf16x, seg, pos_offset)
        stats = jnp.transpose(stats[..., :FOX_WIDTH // HEAD_DIM], (0, 3, 2, 1))
        fox_o, diff_o = attend(fqa, fka, fva, dqa, dka, dvb, stats)
        y = _out_proj(fox_o, diff_o, fg, dg, x, mod_x, w_o, lg, lb, alpha)
        b, t = x.shape[:2]
        states = (fk[None], fv[None], logf[None],
                  dk.reshape(1, b, DIFF_HEADS, t, 2, HEAD_DIM), dv[None])
        return y, states

    def prompt_attend(fqa, fka, fva, dqa, dka, dvb, stats):
        return (_fox_attention(fqa, fka, fva, stats),
                _diff_attention(dqa, dka, dvb, stats, lam_vecs, g, lam_init))

    def sample_attend(fqa, fka, fva, dqa, dka, dvb, stats):
        del stats
        return (_fox_sample(fqa, fka, fva, cache_fox_k, cache_fox_v, cache_fox_logf[layer]),
                _diff_sample(dqa, dka, dvb, cache_diff_k, cache_diff_v, lam_vecs, g, lam_init))

    yp, sp = run(x_prompt, mod_p, 0, prompt_attend)
    ys, ss = run(x_sample, mod_s, past, sample_attend)
    return (yp, ys) + sp + ss
```

```python
import functools
import math

import jax
import jax.numpy as jnp
import numpy as np
from jax import lax
from jax.experimental import pallas as pl
from jax.experimental.pallas import tpu as pltpu

F32 = jnp.float32
BF16 = jnp.bfloat16

HEAD_DIM = 64
FOX_HEADS = 8
DIFF_HEADS = 4
DIFF_VDIM = 2 * HEAD_DIM
FOX_WIDTH = FOX_HEADS * HEAD_DIM
DIFF_WIDTH = DIFF_HEADS * DIFF_VDIM
CHUNK = 64
LN_EPS = 1e-5
RMS_EPS = 1e-5
LANES = 128
NEG_INF = float("-inf")
LOG2E = math.log2(math.e)

_SEC = {name: i * FOX_WIDTH for i, name in enumerate(("fq", "fk", "fv", "fg", "dq", "dk", "dv", "dg"))}
_FF_OFF = 8 * FOX_WIDTH
_W_COLS = _FF_OFF + LANES

_VMEM_LIMIT = 56 * 1024 * 1024


def _params(sem):
    return pltpu.CompilerParams(dimension_semantics=sem, vmem_limit_bytes=_VMEM_LIMIT)


def _ada_kernel(c_ref, w_ref, b_ref, o_ref):
    c = c_ref[...]
    s = c * jax.nn.sigmoid(c)
    o_ref[...] = jnp.dot(s, w_ref[...], preferred_element_type=F32) + b_ref[...]


def _ada(c_all, w_ada, b_ada):
    rows, d = c_all.shape
    n = w_ada.shape[1]
    tn = 512
    return pl.pallas_call(
        _ada_kernel,
        grid=(n // tn,),
        in_specs=[pl.BlockSpec((rows, d), lambda j: (0, 0)),
                  pl.BlockSpec((d, tn), lambda j: (0, j)),
                  pl.BlockSpec((1, tn), lambda j: (0, j))],
        out_specs=pl.BlockSpec((rows, tn), lambda j: (0, j)),
        out_shape=jax.ShapeDtypeStruct((rows, n), F32),
        compiler_params=_params(("arbitrary",)),
        name="ada",
    )(c_all, w_ada, b_ada.reshape(1, n))


def _split3(x):
    hi = x.astype(BF16)
    r1 = x - hi.astype(F32)
    mid = r1.astype(BF16)
    lo = (r1 - mid.astype(F32)).astype(BF16)
    return hi, mid, lo


def _in_proj_kernel(x_ref, mod_ref, w_ref, bf_ref, seg_ref,
                    fk_ref, fv_ref, dk_ref, dv_ref, logf_ref,
                    fqa_ref, fka_ref, fva_ref, fg_ref, dqa_ref, dka_ref, dvb_ref, dg_ref, stat_ref,
                    carry_ref, *, tm, pos_offset):
    ti = pl.program_id(1)
    x = x_ref[0]
    shift = mod_ref[0, 0:1, :]
    scale = mod_ref[0, 1:2, :]
    h = (x * (1.0 + scale) + shift).astype(BF16)

    def proj(name, width=FOX_WIDTH):
        off = _SEC[name] if name in _SEC else _FF_OFF
        return jnp.dot(h, w_ref[:, off:off + width], preferred_element_type=F32)

    lane = lax.broadcasted_iota(jnp.int32, (tm, LANES), 1)
    row = lax.broadcasted_iota(jnp.int32, (tm, LANES), 0)
    lower = lane < HEAD_DIM
    grp = (lane // 8) % 4

    def by_group(a, b, c):
        return jnp.where(grp == 0, a, jnp.where(grp == 1, b, jnp.where(grp == 2, c, jnp.zeros_like(a))))

    def max_sq_norm(z):
        zr = z.astype(BF16).astype(F32)
        sq = zr * zr
        if tm % 32 == 0:
            sq = jnp.maximum(sq[0:tm // 2], sq[tm // 2:tm])
        sq = jnp.dot(sq.astype(BF16), seg_ref[...], preferred_element_type=F32)
        return jnp.max(sq, axis=0, keepdims=True)

    logf = jax.nn.log_sigmoid(proj("ff", LANES) + bf_ref[...])
    logf_ref[0] = logf.T[0:FOX_HEADS] if tm % LANES == 0 else logf

    dq, dk, dv = proj("dq"), proj("dk"), proj("dv")
    pos = (row + (ti * tm + pos_offset)).astype(F32)
    sub = lane % HEAD_DIM
    for hd in range(DIFF_HEADS):
        slope = 2.0 ** (-8.0 * (hd + 1) / DIFF_HEADS)
        sl = slice(hd * LANES, (hd + 1) * LANES)
        p0, p1, p2 = (a.astype(F32) for a in _split3((slope * LOG2E) * pos))
        alibi = jnp.where(sub % 3 == 0, p0, jnp.where(sub % 3 == 1, p1, p2))
        alibi_q = jnp.where(sub < 3, -alibi, (sub < 6).astype(F32))
        alibi_k = jnp.where(sub < 3, 1.0, jnp.where(sub < 6, alibi, 0.0))
        dqa_ref[0, 2 * hd] = jnp.where(lower, dq[:, sl], alibi_q).astype(BF16)
        dqa_ref[0, 2 * hd + 1] = jnp.where(lower, alibi_q, dq[:, sl]).astype(BF16)
        dka_ref[0, 2 * hd] = jnp.where(lower, dk[:, sl], alibi_k).astype(BF16)
        dka_ref[0, 2 * hd + 1] = jnp.where(lower, alibi_k, dk[:, sl]).astype(BF16)
        if tm % LANES:
            dk_ref[0, hd] = dk[:, sl]
        dv_ref[0, hd] = dv[:, sl]
        dvb_ref[0, hd] = dv[:, sl].astype(BF16)
    if tm % LANES == 0:
        dk_ref[0] = dk.T.reshape(dk_ref.shape[1:])
    dg_ref[0] = jax.nn.silu(proj("dg")).astype(BF16)
    diff_norms = [max_sq_norm(dq), max_sq_norm(dk)]

    tri_r = lax.broadcasted_iota(jnp.int32, (tm, tm), 0)
    tri_c = lax.broadcasted_iota(jnp.int32, (tm, tm), 1)
    tri = (tri_c <= tri_r).astype(BF16)
    part = jnp.dot(tri, by_group(*_split3(logf)), preferred_element_type=F32)
    zq, zk, zv = proj("fq"), proj("fk"), proj("fv")
    local = part + pltpu.roll(part, 8, 1) + pltpu.roll(part, 16, 1) + pltpu.roll(part, 24, 1)

    @pl.when(ti == 0)
    def _():
        carry_ref[...] = jnp.zeros_like(carry_ref)

    cum = local + carry_ref[...]
    carry_ref[...] = cum[tm - 1:tm, :]
    cum2 = cum * LOG2E
    pieces = by_group(*(a.astype(F32) for a in _split3(cum2)))
    behind = pltpu.roll(pieces, 24, 1)
    u = lane % HEAD_DIM
    ext_q = jnp.where(u < 24, pieces, (u < 48).astype(F32))

    def ext_k(h):
        mine = u % 8 == h
        return jnp.where(mine & (u < 24), 1.0, jnp.where(mine & (u < 48), -behind, 0.0))

    fox_norms = [max_sq_norm(zq), max_sq_norm(zk)]
    e_even = (lane == HEAD_DIM).astype(F32)
    for j in range(FOX_HEADS // 2):
        sl = slice(j * LANES, (j + 1) * LANES)
        fqa_ref[0, 2 * j] = jnp.where(lower, zq[:, sl], ext_q).astype(BF16)
        fqa_ref[0, 2 * j + 1] = jnp.where(lower, ext_q, zq[:, sl]).astype(BF16)
        fka_ref[0, 2 * j] = jnp.where(lower, zk[:, sl], ext_k(2 * j)).astype(BF16)
        fka_ref[0, 2 * j + 1] = jnp.where(lower, ext_k(2 * j + 1), zk[:, sl]).astype(BF16)
        fva_ref[0, 2 * j] = jnp.where(lower, zv[:, sl], e_even).astype(BF16)
        fva_ref[0, 2 * j + 1] = jnp.where(lower, pltpu.roll(zv[:, sl], HEAD_DIM, 1), e_even).astype(BF16)
    if tm % LANES == 0:
        fk_ref[0] = zk.T.reshape(fk_ref.shape[1:])
        fv_ref[0] = zv.T.reshape(fv_ref.shape[1:])
    else:
        for hd in range(FOX_HEADS):
            fk_ref[0, hd] = zk[:, hd * HEAD_DIM:(hd + 1) * HEAD_DIM]
            fv_ref[0, hd] = zv[:, hd * HEAD_DIM:(hd + 1) * HEAD_DIM]
    fg_ref[0] = jax.nn.silu(proj("fg")).astype(BF16)
    stat_ref[0, 0] = jnp.concatenate(
        fox_norms + diff_norms + [cum2[0:1], cum2[tm - 1:tm], jnp.zeros((2, LANES), F32)],
        axis=0)


def _in_proj(x, mod, w, bf16x, seg, pos_offset):
    b, t, d = x.shape
    tm = min(t, 512)
    nt = t // tm
    head_major = tm % LANES == 0

    def full(a):
        return pl.BlockSpec(a.shape, lambda i, j: (0,) * a.ndim)

    def heads(n, width):
        return pl.BlockSpec((1, n, tm, width), lambda i, j: (i, 0, j, 0))

    def cache_shape(n, width):
        return jax.ShapeDtypeStruct((b, n, width, t) if head_major else (b, n, t, width), F32)

    def cache_spec(n, width):
        return pl.BlockSpec((1, n, width, tm), lambda i, j: (i, 0, 0, j)) if head_major else heads(n, width)

    rows = pl.BlockSpec((1, tm, FOX_WIDTH), lambda i, j: (i, j, 0))
    out_shape = (
        cache_shape(FOX_HEADS, HEAD_DIM),
        cache_shape(FOX_HEADS, HEAD_DIM),
        cache_shape(DIFF_HEADS, DIFF_VDIM),
        jax.ShapeDtypeStruct((b, DIFF_HEADS, t, DIFF_VDIM), F32),
        jax.ShapeDtypeStruct((b, FOX_HEADS, t) if head_major else (b, t, LANES), F32),
        jax.ShapeDtypeStruct((b, FOX_HEADS, t, LANES), BF16),
        jax.ShapeDtypeStruct((b, FOX_HEADS, t, LANES), BF16),
        jax.ShapeDtypeStruct((b, FOX_HEADS, t, LANES), BF16),
        jax.ShapeDtypeStruct((b, t, FOX_WIDTH), BF16),
        jax.ShapeDtypeStruct((b, 2 * DIFF_HEADS, t, LANES), BF16),
        jax.ShapeDtypeStruct((b, 2 * DIFF_HEADS, t, LANES), BF16),
        jax.ShapeDtypeStruct((b, DIFF_HEADS, t, DIFF_VDIM), BF16),
        jax.ShapeDtypeStruct((b, t, DIFF_WIDTH), BF16),
        jax.ShapeDtypeStruct((b, nt, 8, LANES), F32),
    )
    out_specs = (
        cache_spec(FOX_HEADS, HEAD_DIM), cache_spec(FOX_HEADS, HEAD_DIM),
        cache_spec(DIFF_HEADS, DIFF_VDIM), heads(DIFF_HEADS, DIFF_VDIM),
        (pl.BlockSpec((1, FOX_HEADS, tm), lambda i, j: (i, 0, j)) if head_major
         else pl.BlockSpec((1, tm, LANES), lambda i, j: (i, j, 0))),
        heads(FOX_HEADS, LANES), heads(FOX_HEADS, LANES), heads(FOX_HEADS, LANES), rows,
        heads(2 * DIFF_HEADS, LANES), heads(2 * DIFF_HEADS, LANES), heads(DIFF_HEADS, DIFF_VDIM), rows,
        pl.BlockSpec((1, 1, 8, LANES), lambda i, j: (i, j, 0, 0)),
    )
    outs = pl.pallas_call(
        functools.partial(_in_proj_kernel, tm=tm, pos_offset=pos_offset),
        grid=(b, nt),
        in_specs=[pl.BlockSpec((1, tm, d), lambda i, j: (i, j, 0)),
                  pl.BlockSpec((1, 3, d), lambda i, j: (i, 0, 0)),
                  full(w), full(bf16x), full(seg)],
        out_specs=out_specs,
        out_shape=out_shape,
        scratch_shapes=[pltpu.VMEM((1, LANES), F32)],
        compiler_params=_params(("arbitrary", "arbitrary")),
        name="in_proj",
    )(x, mod, w, bf16x, seg)
    if head_major:
        outs = tuple(jnp.swapaxes(o, 2, 3) for o in outs[:3]) + outs[3:]
    else:
        outs = outs[:4] + (jnp.swapaxes(outs[4][:, :, :FOX_HEADS], 1, 2),) + outs[5:]
    return outs


_NT = (((1,), (1,)), ((), ()))
_TN = (((0,), (0,)), ((), ()))
_STAGES = 4
_FOX_V_ROWS = HEAD_DIM + 16
_FOX_GROUP = 4


_SKIP_MARGIN = 136.0
_STAT_FOX_Q, _STAT_FOX_K, _STAT_DIFF_Q, _STAT_DIFF_K, _STAT_CUM_FIRST, _STAT_CUM_LAST = range(6)


def _diagonal_blocks(tq):
    assert tq % (2 * LANES) == 0
    return ((tq // 2, 0), (tq, tq // 2))


def _dot_bound(stat, q_row, k_row, qi):
    tile = lax.broadcasted_iota(jnp.int32, (1, stat.shape[-1]), 1)
    qn2 = jnp.sum(jnp.where(tile == qi, stat[q_row:q_row + 1], 0.0))
    return jnp.sqrt(qn2 * stat[k_row:k_row + 1]) * 1.01 + 1.0


def _first_needed_tile(qi, bounds, m_ref):
    first = qi
    for c, bound in enumerate(bounds):
        tile = lax.broadcasted_iota(jnp.int32, bound.shape, 1)
        needed = (bound >= jnp.min(m_ref[c]) - _SKIP_MARGIN) & (tile < qi)
        first = jnp.minimum(first, jnp.min(jnp.where(needed, tile, qi)))
    return first


def _pipelined_tiles(qi, n_chains, scores, absorb, first_needed, sa_ref, sb_ref):
    chains = range(n_chains)
    bufs = (sa_ref, sb_ref)
    for c in chains:
        scores(qi, sa_ref, c, True)
        scores(jnp.maximum(qi - 1, 0), sb_ref, c)
        absorb(sa_ref, qi, c, True)
    n = qi - first_needed()

    def stages(top, count, more):
        for s in range(count):
            for c in chains:
                if more or s + 1 < count:
                    scores(jnp.maximum(top - s - 1, 0), bufs[s % 2], c)
                absorb(bufs[(s + 1) % 2], top - s, c, False)

    def body(i, carry):
        stages(qi - 1 - _STAGES * i, _STAGES, True)
        return carry

    trips = jnp.maximum(n - 1, 0) // _STAGES
    lax.fori_loop(0, trips, body, 0)
    rem = n - _STAGES * trips
    for count in range(1, _STAGES + 1):
        @pl.when(rem == count)
        def _(count=count):
            stages(qi - 1 - _STAGES * trips, count, False)


def _fox_kernel(q_ref, k_ref, v_ref, stat_ref, o_ref, sa_ref, sb_ref, m_ref, acc_ref, *, tq):
    qi = pl.program_id(2)
    half = tq // 2

    def scores(ki, dst, hh, diagonal=False):
        start = pl.multiple_of(ki * tq, tq)
        if diagonal:
            for keys, q0 in _diagonal_blocks(tq):
                k = k_ref[0, hh, pl.ds(start, keys), :]
                dst[hh, 0:keys, q0:q0 + half] = lax.dot_general(k, q_ref[0, hh, q0:q0 + half, :], _NT,
                                                                preferred_element_type=F32)
        else:
            k = k_ref[0, hh, pl.ds(start, tq), :]
            dst[hh] = lax.dot_general(k, q_ref[0, hh], _NT, preferred_element_type=F32)

    def absorb(src, ki, hh, diagonal):
        start = pl.multiple_of(ki * tq, tq)
        if diagonal:
            for keys, q0 in _diagonal_blocks(tq):
                s = src[hh, 0:keys, q0:q0 + half]
                key = lax.broadcasted_iota(jnp.int32, s.shape, 0)
                qry = lax.broadcasted_iota(jnp.int32, s.shape, 1) + q0
                s = jnp.where(key <= qry, s, NEG_INF)
                m_new = jnp.max(s, axis=0, keepdims=True)
                p = jnp.exp2(s - m_new)
                v = v_ref[0, hh, pl.ds(start, keys), 0:_FOX_V_ROWS]
                acc_ref[hh, :, q0:q0 + half] = lax.dot_general(v, p.astype(BF16), _TN, preferred_element_type=F32)
                m_ref[hh, :, q0:q0 + half] = m_new
            return
        s = src[hh]
        m_prev = m_ref[hh]
        m_new = jnp.maximum(m_prev, jnp.max(s, axis=0, keepdims=True))
        p = jnp.exp2(s - m_new)
        alpha = jnp.exp2(m_prev - m_new)
        v = v_ref[0, hh, pl.ds(start, tq), 0:_FOX_V_ROWS]
        pv = lax.dot_general(v, p.astype(BF16), _TN, preferred_element_type=F32)
        acc_ref[hh] = alpha * acc_ref[hh] + pv
        m_ref[hh] = m_new

    def first_needed():
        bounds = []
        for hh in range(_FOX_GROUP):
            stat = stat_ref[0, hh]
            tile = lax.broadcasted_iota(jnp.int32, (1, stat.shape[-1]), 1)
            cum_q = jnp.sum(jnp.where(tile == qi, stat[_STAT_CUM_FIRST:_STAT_CUM_FIRST + 1], 0.0))
            decay = cum_q - stat[_STAT_CUM_LAST:_STAT_CUM_LAST + 1]
            bounds.append(_dot_bound(stat, _STAT_FOX_Q, _STAT_FOX_K, qi) + decay)
        return _first_needed_tile(qi, bounds, m_ref)

    _pipelined_tiles(qi, _FOX_GROUP, scores, absorb, first_needed, sa_ref, sb_ref)
    outs = [acc_ref[hh, 0:HEAD_DIM] / acc_ref[hh, HEAD_DIM:HEAD_DIM + 1] for hh in range(_FOX_GROUP)]
    o_ref[0] = jnp.concatenate(outs, axis=0).T.astype(BF16)


def _fox_attention(fqa, fka, fva, stats):
    b, nh, t, _ = fqa.shape
    tq = min(t, 512)
    g = _FOX_GROUP
    return pl.pallas_call(
        functools.partial(_fox_kernel, tq=tq),
        grid=(b, nh // g, t // tq),
        in_specs=[pl.BlockSpec((1, g, tq, LANES), lambda i, j, q: (i, j, q, 0)),
                  pl.BlockSpec((1, g, t, LANES), lambda i, j, q: (i, j, 0, 0)),
                  pl.BlockSpec((1, g, t, LANES), lambda i, j, q: (i, j, 0, 0)),
                  pl.BlockSpec((1, g) + stats.shape[2:], lambda i, j, q: (i, j, 0, 0))],
        out_specs=pl.BlockSpec((1, tq, g * HEAD_DIM), lambda i, j, q: (i, q, j)),
        out_shape=jax.ShapeDtypeStruct((b, t, FOX_WIDTH), BF16),
        scratch_shapes=[pltpu.VMEM((g, tq, tq), F32), pltpu.VMEM((g, tq, tq), F32),
                        pltpu.VMEM((g, 1, tq), F32), pltpu.VMEM((g, _FOX_V_ROWS, tq), F32)],
        compiler_params=_params(("arbitrary", "arbitrary", "arbitrary")),
        name="fox_attn",
    )(fqa, fka, fva, stats)


def _alibi_slope(head):
    return jnp.exp2(jnp.full((1, 1), -8.0 / DIFF_HEADS, F32) * (head + 1).astype(F32))


def _lambda(lam_ref, lam_init):
    lv = lam_ref[...]
    e1 = jnp.exp(jnp.sum(lv[0:1] * lv[1:2], axis=1, keepdims=True))
    e2 = jnp.exp(jnp.sum(lv[2:3] * lv[3:4], axis=1, keepdims=True))
    return e1 - e2 + lam_init


def _diff_normed(o0, o1, lam, axis):
    df = o0 - lam * o1
    return df * lax.rsqrt(jnp.mean(df * df, axis=axis, keepdims=True) + RMS_EPS)


_ONES_ROWS = 16


def _diff_kernel(q_ref, k_ref, v_ref, stat_ref, lam_ref, g_ref, o_ref, sa_ref, sb_ref, m_ref, acc_ref, *, tq,
                 lam_init, first_head, group):
    qi = pl.program_id(2)
    half = tq // 2
    chains = 2 * group
    slopes = [_alibi_slope(first_head + pl.program_id(1) * group + h) for h in range(group)]

    def scores(ki, dst, c, diagonal=False):
        start = pl.multiple_of(ki * tq, tq)
        if diagonal:
            for keys, q0 in _diagonal_blocks(tq):
                k = k_ref[0, c, pl.ds(start, keys), :]
                dst[c, 0:keys, q0:q0 + half] = lax.dot_general(k, q_ref[0, c, q0:q0 + half, :], _NT,
                                                               preferred_element_type=F32)
        else:
            k = k_ref[0, c, pl.ds(start, tq), :]
            dst[c] = lax.dot_general(k, q_ref[0, c], _NT, preferred_element_type=F32)

    def values_t(c, start, keys):
        v = v_ref[0, c // 2, pl.ds(start, keys), :]
        return jnp.concatenate([v.T, jnp.ones((_ONES_ROWS, keys), BF16)], axis=0)

    def absorb(src, ki, c, diagonal):
        start = pl.multiple_of(ki * tq, tq)
        if diagonal:
            for keys, q0 in _diagonal_blocks(tq):
                s = src[c, 0:keys, q0:q0 + half]
                key = lax.broadcasted_iota(jnp.int32, s.shape, 0)
                qry = lax.broadcasted_iota(jnp.int32, s.shape, 1) + q0
                ahead = jnp.maximum(key - qry, 0).astype(F32)
                s = jnp.where((key // CHUNK) <= (qry // CHUNK), s - (2.0 * LOG2E * slopes[c // 2]) * ahead, NEG_INF)
                m_new = jnp.max(s, axis=0, keepdims=True)
                p = jnp.exp2(s - m_new)
                acc_ref[c, :, q0:q0 + half] = jnp.dot(values_t(c, start, keys), p.astype(BF16),
                                                      preferred_element_type=F32)
                m_ref[c, :, q0:q0 + half] = m_new
            return
        s = src[c]
        m_prev = m_ref[c]
        m_new = jnp.maximum(m_prev, jnp.max(s, axis=0, keepdims=True))
        p = jnp.exp2(s - m_new)
        alpha = jnp.exp2(m_prev - m_new)
        acc_ref[c] = alpha * acc_ref[c] + jnp.dot(values_t(c, start, tq), p.astype(BF16),
                                                  preferred_element_type=F32)
        m_ref[c] = m_new

    def first_needed():
        bounds = []
        for c in range(chains):
            stat = stat_ref[0, c]
            tile = lax.broadcasted_iota(jnp.int32, (1, stat.shape[-1]), 1)
            gap = ((qi - 1 - tile) * tq + 1).astype(F32)
            bounds.append(_dot_bound(stat, _STAT_DIFF_Q, _STAT_DIFF_K, qi) - (LOG2E * slopes[c // 2]) * gap)
        return _first_needed_tile(qi, bounds, m_ref)

    _pipelined_tiles(qi, chains, scores, absorb, first_needed, sa_ref, sb_ref)
    lam = _lambda(lam_ref, lam_init)
    for h in range(group):
        o0, o1 = (acc_ref[2 * h + c, :DIFF_VDIM] / acc_ref[2 * h + c, DIFF_VDIM:DIFF_VDIM + 1] for c in range(2))
        normed = _diff_normed(o0, o1, lam, 0).T * g_ref[...] * (1.0 - lam_init)
        o_ref[0, :, h * DIFF_VDIM:(h + 1) * DIFF_VDIM] = normed.astype(BF16)


_DIFF_GROUPS = ((0, 1, 2), (2, 2, 1))


def _diff_attention(dqa, dka, dvb, stats, lam_vecs, subln_g, lam_init):
    b, _, t, _ = dqa.shape
    tq = min(t, 512)
    outs = []
    for first_head, g, n_groups in _DIFF_GROUPS:
        off = first_head // g

        def heads(block, off=off):
            return pl.BlockSpec(block, lambda i, j, q: (i, j + off, 0, 0))

        outs.append(pl.pallas_call(
            functools.partial(_diff_kernel, tq=tq, lam_init=lam_init, first_head=first_head, group=g),
            grid=(b, n_groups, t // tq),
            in_specs=[pl.BlockSpec((1, 2 * g, tq, LANES), lambda i, j, q, off=off: (i, j + off, q, 0)),
                      heads((1, 2 * g, t, LANES)), heads((1, g, t, DIFF_VDIM)),
                      heads((1, 2 * g) + stats.shape[2:]),
                      pl.BlockSpec(lam_vecs.shape, lambda i, j, q: (0, 0)),
                      pl.BlockSpec(subln_g.shape, lambda i, j, q: (0, 0))],
            out_specs=pl.BlockSpec((1, tq, g * DIFF_VDIM), lambda i, j, q: (i, q, j)),
            out_shape=jax.ShapeDtypeStruct((b, t, n_groups * g * DIFF_VDIM), BF16),
            scratch_shapes=[pltpu.VMEM((2 * g, tq, tq), F32), pltpu.VMEM((2 * g, tq, tq), F32),
                            pltpu.VMEM((2 * g, 1, tq), F32),
                            pltpu.VMEM((2 * g, DIFF_VDIM + _ONES_ROWS, tq), F32)],
            compiler_params=_params(("arbitrary", "arbitrary", "arbitrary")),
            name="diff_attn",
        )(dqa, dka, dvb, stats, lam_vecs, subln_g))
    return tuple(outs)


def _prefix_sum_lanes(x):
    n = x.shape[-1]
    lane = lax.broadcasted_iota(jnp.int32, x.shape, x.ndim - 1)
    sh = 1
    while sh < n:
        x = x + jnp.where(lane >= sh, pltpu.roll(x, sh, x.ndim - 1), 0.0)
        sh *= 2
    return x


def _fox_sample_kernel(q_ref, k_ref, v_ref, ck_ref, cv_ref, clf_ref, o_ref, *, t, past):
    r = lax.broadcasted_iota(jnp.int32, (t, t), 0)
    c = lax.broadcasted_iota(jnp.int32, (t, t), 1)
    cum_c = _prefix_sum_lanes(clf_ref[0])
    suffix = (cum_c[:, past - 1:past] - cum_c) * LOG2E
    for hh in range(FOX_HEADS):
        data = slice(0, HEAD_DIM) if hh % 2 == 0 else slice(HEAD_DIM, LANES)
        ext = HEAD_DIM if hh % 2 == 0 else 0
        qa = q_ref[0, hh]
        qf = qa.astype(F32)
        cum_q = sum(qf[:, ext + 8 * p + hh:ext + 8 * p + hh + 1] for p in range(3))
        s_c = jnp.dot(qa[:, data], ck_ref[0, 0, hh].astype(BF16), preferred_element_type=F32)
        s_c = s_c + cum_q + suffix[hh:hh + 1]
        s_n = lax.dot_general(qa, k_ref[0, hh], _NT, preferred_element_type=F32)
        s_n = jnp.where(c <= r, s_n, NEG_INF)
        m = jnp.maximum(jnp.max(s_c, axis=1, keepdims=True), jnp.max(s_n, axis=1, keepdims=True))
        p_c = jnp.exp2(s_c - m)
        p_n = jnp.exp2(s_n - m)
        denom = jnp.sum(p_c, axis=1, keepdims=True) + jnp.sum(p_n, axis=1, keepdims=True)
        o = lax.dot_general(p_c.astype(BF16), cv_ref[0, 0, hh].astype(BF16), _NT, preferred_element_type=F32)
        o = o + jnp.dot(p_n.astype(BF16), v_ref[0, hh][:, 0:HEAD_DIM], preferred_element_type=F32)
        o_ref[0, :, hh * HEAD_DIM:(hh + 1) * HEAD_DIM] = (o / denom).astype(BF16)


def _fox_sample(fqa, fka, fva, cache_k, cache_v, cache_logf):
    b, nh, t, _ = fqa.shape
    past = cache_k.shape[3]
    cache_k, cache_v = jnp.swapaxes(cache_k, 3, 4), jnp.swapaxes(cache_v, 3, 4)
    new = pl.BlockSpec((1, nh, t, LANES), lambda i: (i, 0, 0, 0))
    cache = pl.BlockSpec((1, 1, nh, HEAD_DIM, past), lambda i: (0, i, 0, 0, 0))
    return pl.pallas_call(
        functools.partial(_fox_sample_kernel, t=t, past=past),
        grid=(b,),
        in_specs=[new, new, new, cache, cache, pl.BlockSpec((1, nh, past), lambda i: (i, 0, 0))],
        out_specs=pl.BlockSpec((1, t, FOX_WIDTH), lambda i: (i, 0, 0)),
        out_shape=jax.ShapeDtypeStruct((b, t, FOX_WIDTH), BF16),
        compiler_params=_params(("arbitrary",)),
        name="fox_sample",
    )(fqa, fka, fva, cache_k, cache_v, cache_logf)


def _diff_sample_kernel(q_ref, k_ref, v_ref, ck_ref, cv_ref, lam_ref, g_ref, o_ref, *, t, past, lam_init):
    lane = lax.broadcasted_iota(jnp.int32, (t, LANES), 1)
    pos_q = past + lax.broadcasted_iota(jnp.int32, (t, past), 0)
    pos_c = lax.broadcasted_iota(jnp.int32, (t, past), 1)
    dist_c = jnp.abs(pos_q - pos_c).astype(F32)
    vis_c = (pos_c // CHUNK) <= (pos_q // CHUNK)
    pq_n = past + lax.broadcasted_iota(jnp.int32, (t, t), 0)
    pk_n = past + lax.broadcasted_iota(jnp.int32, (t, t), 1)
    dist_n = jnp.abs(pq_n - pk_n).astype(F32)
    vis_n = (pk_n // CHUNK) <= (pq_n // CHUNK)
    lam = _lambda(lam_ref, lam_init)
    for hd in range(DIFF_HEADS):
        slope = LOG2E * 2.0 ** (-8.0 * (hd + 1) / DIFF_HEADS)
        ck = ck_ref[0, 0, hd].astype(BF16)
        cv = cv_ref[0, 0, hd].astype(BF16)
        v = v_ref[0, hd]
        outs = []
        for c in range(2):
            half = (lane < HEAD_DIM) if c == 0 else (lane >= HEAD_DIM)
            qz = jnp.where(half, q_ref[0, 2 * hd + c], jnp.zeros((), BF16))
            kz = jnp.where(half, k_ref[0, 2 * hd + c], jnp.zeros((), BF16))
            s_c = jnp.dot(qz, ck, preferred_element_type=F32)
            s_c = jnp.where(vis_c, s_c - slope * dist_c, NEG_INF)
            s_n = lax.dot_general(qz, kz, _NT, preferred_element_type=F32)
            s_n = jnp.where(vis_n, s_n - slope * dist_n, NEG_INF)
            m = jnp.maximum(jnp.max(s_c, axis=1, keepdims=True), jnp.max(s_n, axis=1, keepdims=True))
            p_c = jnp.exp2(s_c - m)
            p_n = jnp.exp2(s_n - m)
            denom = jnp.sum(p_c, axis=1, keepdims=True) + jnp.sum(p_n, axis=1, keepdims=True)
            o = jnp.dot(p_c.astype(BF16), cv, preferred_element_type=F32)
            o = o + jnp.dot(p_n.astype(BF16), v, preferred_element_type=F32)
            outs.append(o / denom)
        normed = _diff_normed(outs[0], outs[1], lam, 1) * g_ref[...] * (1.0 - lam_init)
        o_ref[0, :, hd * DIFF_VDIM:(hd + 1) * DIFF_VDIM] = normed.astype(BF16)


def _diff_sample(dqa, dka, dvb, cache_k, cache_v, lam_vecs, subln_g, lam_init):
    b, nm, t, _ = dqa.shape
    past = cache_k.shape[3]
    ck = jnp.transpose(cache_k, (0, 1, 2, 4, 5, 3)).reshape(cache_k.shape[:3] + (DIFF_VDIM, past))
    new = pl.BlockSpec((1, nm, t, LANES), lambda i: (i, 0, 0, 0))
    return pl.pallas_call(
        functools.partial(_diff_sample_kernel, t=t, past=past, lam_init=lam_init),
        grid=(b,),
        in_specs=[new, new, pl.BlockSpec((1, DIFF_HEADS, t, DIFF_VDIM), lambda i: (i, 0, 0, 0)),
                  pl.BlockSpec((1, 1, DIFF_HEADS, DIFF_VDIM, past), lambda i: (0, i, 0, 0, 0)),
                  pl.BlockSpec((1, 1, DIFF_HEADS, past, DIFF_VDIM), lambda i: (0, i, 0, 0, 0)),
                  pl.BlockSpec(lam_vecs.shape, lambda i: (0, 0)),
                  pl.BlockSpec(subln_g.shape, lambda i: (0, 0))],
        out_specs=pl.BlockSpec((1, t, DIFF_WIDTH), lambda i: (i, 0, 0)),
        out_shape=jax.ShapeDtypeStruct((b, t, DIFF_WIDTH), BF16),
        compiler_params=_params(("arbitrary",)),
        name="diff_sample",
    )(dqa, dka, dvb, ck, cache_v, lam_vecs, subln_g)


def _out_proj_kernel(fo_ref, *refs, alpha, n_diff):
    do_refs, (fg_ref, dg_ref, x_ref, mod_ref, w_ref, g_ref, b_ref, y_ref) = refs[:n_diff], refs[n_diff:]
    fox = fo_ref[0] * fg_ref[0]
    diff = jnp.concatenate([r[0] for r in do_refs], axis=1) * dg_ref[0]
    branch = jnp.dot(fox, w_ref[0:FOX_WIDTH, :], preferred_element_type=F32)
    branch = branch + jnp.dot(diff, w_ref[FOX_WIDTH:, :], preferred_element_type=F32)
    r = alpha * x_ref[0] + mod_ref[0, 2:3, :] * branch
    mu = jnp.mean(r, axis=-1, keepdims=True)
    rc = r - mu
    var = jnp.mean(rc * rc, axis=-1, keepdims=True)
    y_ref[0] = rc * lax.rsqrt(var + LN_EPS) * g_ref[...] + b_ref[...]


def _out_proj(fox_o, diff_parts, fg, dg, x, mod, w_out, ln_g, ln_b, alpha):
    b, t, d = x.shape
    tm = 1024 if t % 1024 == 0 else min(t, 512)
    rows = pl.BlockSpec((1, tm, FOX_WIDTH), lambda i, j: (i, j, 0))
    wide = pl.BlockSpec((1, tm, d), lambda i, j: (i, j, 0))

    def full(a):
        return pl.BlockSpec(a.shape, lambda i, j: (0,) * a.ndim)

    return pl.pallas_call(
        functools.partial(_out_proj_kernel, alpha=alpha, n_diff=len(diff_parts)),
        grid=(b, t // tm),
        in_specs=([rows] + [pl.BlockSpec((1, tm, p.shape[-1]), lambda i, j: (i, j, 0)) for p in diff_parts]
                  + [rows, rows, wide, pl.BlockSpec((1, 3, d), lambda i, j: (i, 0, 0)),
                     full(w_out), full(ln_g), full(ln_b)]),
        out_specs=wide,
        out_shape=jax.ShapeDtypeStruct((b, t, d), F32),
        compiler_params=_params(("arbitrary", "arbitrary")),
        name="out_proj",
    )(fox_o, *diff_parts, fg, dg, x, mod, w_out, ln_g, ln_b)


def _pack_w_in(w_in):
    fw = FOX_WIDTH
    o = 0
    fq = w_in[:, o:o + fw]; o += fw
    fk = w_in[:, o:o + fw]; o += fw
    fv = w_in[:, o:o + fw]; o += fw
    ff = w_in[:, o:o + FOX_HEADS]; o += FOX_HEADS
    fg = w_in[:, o:o + fw]; o += fw
    dq = w_in[:, o:o + fw]; o += fw
    dk = w_in[:, o:o + fw]; o += fw
    dv = w_in[:, o:o + fw]; o += fw
    dg = w_in[:, o:o + fw]
    qs = HEAD_DIM ** -0.5 * LOG2E
    return jnp.concatenate([fq * qs, fk, fv, fg, dq * qs, dk, dv, dg,
                            jnp.tile(ff, (1, LANES // FOX_HEADS))], axis=1).astype(BF16)


def kernel(x_prompt, x_sample, cache_fox_k, cache_fox_v, cache_fox_logf, cache_diff_k, cache_diff_v,
           c_prompt, c_sample, w_ada, b_ada, w_in, b_f, lambda_q1, lambda_k1, lambda_q2, lambda_k2,
           subln_g, w_out, ln_g, ln_b):
    depth = w_in.shape[0]
    assert depth == 1, "single-layer step"
    layer = 0
    lam_init = 0.8 - 0.6 * math.exp(-0.3 * layer)
    alpha = (2 * depth) ** 0.25
    bp, bs = c_prompt.shape[0], c_sample.shape[0]
    d = x_prompt.shape[-1]
    past = cache_fox_k.shape[3]

    c_all = jnp.concatenate([c_prompt, c_sample, jnp.zeros((-(bp + bs) % 8, d), F32)], axis=0)
    mod = _ada(c_all, w_ada[layer], b_ada[layer]).reshape(c_all.shape[0], 3, d)
    mod_p, mod_s = mod[:bp], mod[bp:bp + bs]

    w = _pack_w_in(w_in[layer])
    bf16x = jnp.tile(b_f[layer], LANES // FOX_HEADS).reshape(1, LANES)
    seg = np.repeat(np.eye(FOX_WIDTH // HEAD_DIM, LANES, dtype=np.float32), HEAD_DIM, axis=0)
    seg = jnp.asarray(seg, BF16)
    lam_vecs = jnp.stack([lambda_q1[layer], lambda_k1[layer], lambda_q2[layer], lambda_k2[layer]])
    g = subln_g[layer].reshape(1, DIFF_VDIM)
    w_o = w_out[layer].astype(BF16)
    lg, lb = ln_g[layer].reshape(1, d), ln_b[layer].reshape(1, d)

    def run(x, mod_x, pos_offset, attend):
        (fk, fv, dk, dv, logf, fqa, fka, fva, fg, dqa, dka, dvb, dg, stats) = _in_proj(
            x, mod_x, w, bf16x, seg, pos_offset)
        stats = jnp.transpose(stats[..., :FOX_WIDTH // HEAD_DIM], (0, 3, 2, 1))
        fox_o, diff_o = attend(fqa, fka, fva, dqa, dka, dvb, stats)
        y = _out_proj(fox_o, diff_o, fg, dg, x, mod_x, w_o, lg, lb, alpha)
        b, t = x.shape[:2]
        states = (fk[None], fv[None], logf[None],
                  dk.reshape(1, b, DIFF_HEADS, t, 2, HEAD_DIM), dv[None])
        return y, states

    def prompt_attend(fqa, fka, fva, dqa, dka, dvb, stats):
        return (_fox_attention(fqa, fka, fva, stats),
                _diff_attention(dqa, dka, dvb, stats, lam_vecs, g, lam_init))

    def sample_attend(fqa, fka, fva, dqa, dka, dvb, stats):
        del stats
        return (_fox_sample(fqa, fka, fva, cache_fox_k, cache_fox_v, cache_fox_logf[layer]),
                (_diff_sample(dqa, dka, dvb, cache_diff_k, cache_diff_v, lam_vecs, g, lam_init),))

    yp, sp = run(x_prompt, mod_p, 0, prompt_attend)
    ys, ss = run(x_sample, mod_s, past, sample_attend)
    return (yp, ys) + sp + ss
```

```python
import functools
import math

import jax
import jax.numpy as jnp
import numpy as np
from jax import lax
from jax.experimental import pallas as pl
from jax.experimental.pallas import tpu as pltpu

F32 = jnp.float32
BF16 = jnp.bfloat16

HEAD_DIM = 64
FOX_HEADS = 8
DIFF_HEADS = 4
DIFF_VDIM = 2 * HEAD_DIM
FOX_WIDTH = FOX_HEADS * HEAD_DIM
DIFF_WIDTH = DIFF_HEADS * DIFF_VDIM
CHUNK = 64
LN_EPS = 1e-5
RMS_EPS = 1e-5
LANES = 128
NEG_INF = float("-inf")
LOG2E = math.log2(math.e)

_SEC = {name: i * FOX_WIDTH for i, name in enumerate(("fq", "fk", "fv", "fg", "dq", "dk", "dv", "dg"))}
_FF_OFF = 8 * FOX_WIDTH
_W_COLS = _FF_OFF + LANES

_VMEM_LIMIT = 56 * 1024 * 1024


def _params(sem):
    return pltpu.CompilerParams(dimension_semantics=sem, vmem_limit_bytes=_VMEM_LIMIT)


def _ada_kernel(c_ref, w_ref, b_ref, o_ref):
    c = c_ref[...]
    s = c * jax.nn.sigmoid(c)
    o_ref[...] = jnp.dot(s, w_ref[...], preferred_element_type=F32) + b_ref[...]


def _ada(c_all, w_ada, b_ada):
    rows, d = c_all.shape
    n = w_ada.shape[1]
    tn = 512
    return pl.pallas_call(
        _ada_kernel,
        grid=(n // tn,),
        in_specs=[pl.BlockSpec((rows, d), lambda j: (0, 0)),
                  pl.BlockSpec((d, tn), lambda j: (0, j)),
                  pl.BlockSpec((1, tn), lambda j: (0, j))],
        out_specs=pl.BlockSpec((rows, tn), lambda j: (0, j)),
        out_shape=jax.ShapeDtypeStruct((rows, n), F32),
        compiler_params=_params(("arbitrary",)),
        name="ada",
    )(c_all, w_ada, b_ada.reshape(1, n))


def _split3(x):
    hi = x.astype(BF16)
    r1 = x - hi.astype(F32)
    mid = r1.astype(BF16)
    lo = (r1 - mid.astype(F32)).astype(BF16)
    return hi, mid, lo


def _in_proj_kernel(x_ref, mod_ref, w_ref, bf_ref, seg_ref,
                    fk_ref, fv_ref, dk_ref, dv_ref, logf_ref,
                    fqa_ref, fka_ref, fva_ref, fg_ref, dqa_ref, dka_ref, dvb_ref, dg_ref, stat_ref,
                    carry_ref, *, tm, pos_offset):
    ti = pl.program_id(1)
    x = x_ref[0]
    shift = mod_ref[0, 0:1, :]
    scale = mod_ref[0, 1:2, :]
    h = (x * (1.0 + scale) + shift).astype(BF16)

    def proj(name, width=FOX_WIDTH):
        off = _SEC[name] if name in _SEC else _FF_OFF
        return lax.dot_general(h, w_ref[off:off + width, :], _NT, preferred_element_type=F32)

    lane = lax.broadcasted_iota(jnp.int32, (tm, LANES), 1)
    row = lax.broadcasted_iota(jnp.int32, (tm, LANES), 0)
    lower = lane < HEAD_DIM
    grp = (lane // 8) % 4

    def by_group(a, b, c):
        return jnp.where(grp == 0, a, jnp.where(grp == 1, b, jnp.where(grp == 2, c, jnp.zeros_like(a))))

    def max_sq_norm(z):
        zr = z.astype(BF16).astype(F32)
        sq = zr * zr
        if tm % 32 == 0:
            sq = jnp.maximum(sq[0:tm // 2], sq[tm // 2:tm])
        sq = jnp.dot(sq.astype(BF16), seg_ref[...], preferred_element_type=F32)
        return jnp.max(sq, axis=0, keepdims=True)

    logf = jax.nn.log_sigmoid(proj("ff", LANES) + bf_ref[...])
    logf_ref[0] = logf.T[0:FOX_HEADS] if tm % LANES == 0 else logf

    dq, dk, dv = proj("dq"), proj("dk"), proj("dv")
    pos = (row + (ti * tm + pos_offset)).astype(F32)
    sub = lane % HEAD_DIM
    for hd in range(DIFF_HEADS):
        slope = 2.0 ** (-8.0 * (hd + 1) / DIFF_HEADS)
        sl = slice(hd * LANES, (hd + 1) * LANES)
        p0, p1, p2 = (a.astype(F32) for a in _split3((slope * LOG2E) * pos))
        alibi = jnp.where(sub % 3 == 0, p0, jnp.where(sub % 3 == 1, p1, p2))
        alibi_q = jnp.where(sub < 3, -alibi, (sub < 6).astype(F32))
        alibi_k = jnp.where(sub < 3, 1.0, jnp.where(sub < 6, alibi, 0.0))
        dqa_ref[0, 2 * hd] = jnp.where(lower, dq[:, sl], alibi_q).astype(BF16)
        dqa_ref[0, 2 * hd + 1] = jnp.where(lower, alibi_q, dq[:, sl]).astype(BF16)
        dka_ref[0, 2 * hd] = jnp.where(lower, dk[:, sl], alibi_k).astype(BF16)
        dka_ref[0, 2 * hd + 1] = jnp.where(lower, alibi_k, dk[:, sl]).astype(BF16)
        if tm % LANES:
            dk_ref[0, hd] = dk[:, sl]
        dv_ref[0, hd] = dv[:, sl]
        dvb_ref[0, hd] = dv[:, sl].astype(BF16)
    if tm % LANES == 0:
        dk_ref[0] = dk.T.reshape(dk_ref.shape[1:])
    diff_norms = [max_sq_norm(dq), max_sq_norm(dk)]

    tri_r = lax.broadcasted_iota(jnp.int32, (tm, tm), 0)
    tri_c = lax.broadcasted_iota(jnp.int32, (tm, tm), 1)
    tri = (tri_c <= tri_r).astype(BF16)
    part = jnp.dot(tri, by_group(*_split3(logf)), preferred_element_type=F32)
    zq, zk, zv = proj("fq"), proj("fk"), proj("fv")
    local = part + pltpu.roll(part, 8, 1) + pltpu.roll(part, 16, 1) + pltpu.roll(part, 24, 1)

    @pl.when(ti == 0)
    def _():
        carry_ref[...] = jnp.zeros_like(carry_ref)

    cum = local + carry_ref[...]
    carry_ref[...] = cum[tm - 1:tm, :]
    cum2 = cum * LOG2E
    pieces = by_group(*(a.astype(F32) for a in _split3(cum2)))
    behind = pltpu.roll(pieces, 24, 1)
    u = lane % HEAD_DIM
    ext_q = jnp.where(u < 24, pieces, (u < 48).astype(F32))

    def ext_k(h):
        mine = u % 8 == h
        return jnp.where(mine & (u < 24), 1.0, jnp.where(mine & (u < 48), -behind, 0.0))

    fox_norms = [max_sq_norm(zq), max_sq_norm(zk)]
    e_even = (lane == HEAD_DIM).astype(F32)
    for j in range(FOX_HEADS // 2):
        sl = slice(j * LANES, (j + 1) * LANES)
        fqa_ref[0, 2 * j] = jnp.where(lower, zq[:, sl], ext_q).astype(BF16)
        fqa_ref[0, 2 * j + 1] = jnp.where(lower, ext_q, zq[:, sl]).astype(BF16)
        fka_ref[0, 2 * j] = jnp.where(lower, zk[:, sl], ext_k(2 * j)).astype(BF16)
        fka_ref[0, 2 * j + 1] = jnp.where(lower, ext_k(2 * j + 1), zk[:, sl]).astype(BF16)
        fva_ref[0, 2 * j] = jnp.where(lower, zv[:, sl], e_even).astype(BF16)
        fva_ref[0, 2 * j + 1] = jnp.where(lower, pltpu.roll(zv[:, sl], HEAD_DIM, 1), e_even).astype(BF16)
    if tm % LANES == 0:
        fk_ref[0] = zk.T.reshape(fk_ref.shape[1:])
        fv_ref[0] = zv.T.reshape(fv_ref.shape[1:])
    else:
        for hd in range(FOX_HEADS):
            fk_ref[0, hd] = zk[:, hd * HEAD_DIM:(hd + 1) * HEAD_DIM]
            fv_ref[0, hd] = zv[:, hd * HEAD_DIM:(hd + 1) * HEAD_DIM]
    dg_ref[0] = jax.nn.silu(proj("dg")).astype(BF16)
    fg_ref[0] = jax.nn.silu(proj("fg")).astype(BF16)
    stat_ref[0, 0] = jnp.concatenate(
        fox_norms + diff_norms + [cum2[0:1], cum2[tm - 1:tm], jnp.zeros((2, LANES), F32)],
        axis=0)


def _in_proj(x, mod, w, bf16x, seg, pos_offset):
    b, t, d = x.shape
    tm = min(t, 512)
    nt = t // tm
    head_major = tm % LANES == 0

    def full(a):
        return pl.BlockSpec(a.shape, lambda i, j: (0,) * a.ndim)

    def heads(n, width):
        return pl.BlockSpec((1, n, tm, width), lambda i, j: (i, 0, j, 0))

    def cache_shape(n, width):
        return jax.ShapeDtypeStruct((b, n, width, t) if head_major else (b, n, t, width), F32)

    def cache_spec(n, width):
        return pl.BlockSpec((1, n, width, tm), lambda i, j: (i, 0, 0, j)) if head_major else heads(n, width)

    rows = pl.BlockSpec((1, tm, FOX_WIDTH), lambda i, j: (i, j, 0))
    out_shape = (
        cache_shape(FOX_HEADS, HEAD_DIM),
        cache_shape(FOX_HEADS, HEAD_DIM),
        cache_shape(DIFF_HEADS, DIFF_VDIM),
        jax.ShapeDtypeStruct((b, DIFF_HEADS, t, DIFF_VDIM), F32),
        jax.ShapeDtypeStruct((b, FOX_HEADS, t) if head_major else (b, t, LANES), F32),
        jax.ShapeDtypeStruct((b, FOX_HEADS, t, LANES), BF16),
        jax.ShapeDtypeStruct((b, FOX_HEADS, t, LANES), BF16),
        jax.ShapeDtypeStruct((b, FOX_HEADS, t, LANES), BF16),
        jax.ShapeDtypeStruct((b, t, FOX_WIDTH), BF16),
        jax.ShapeDtypeStruct((b, 2 * DIFF_HEADS, t, LANES), BF16),
        jax.ShapeDtypeStruct((b, 2 * DIFF_HEADS, t, LANES), BF16),
        jax.ShapeDtypeStruct((b, DIFF_HEADS, t, DIFF_VDIM), BF16),
        jax.ShapeDtypeStruct((b, t, DIFF_WIDTH), BF16),
        jax.ShapeDtypeStruct((b, nt, 8, LANES), F32),
    )
    out_specs = (
        cache_spec(FOX_HEADS, HEAD_DIM), cache_spec(FOX_HEADS, HEAD_DIM),
        cache_spec(DIFF_HEADS, DIFF_VDIM), heads(DIFF_HEADS, DIFF_VDIM),
        (pl.BlockSpec((1, FOX_HEADS, tm), lambda i, j: (i, 0, j)) if head_major
         else pl.BlockSpec((1, tm, LANES), lambda i, j: (i, j, 0))),
        heads(FOX_HEADS, LANES), heads(FOX_HEADS, LANES), heads(FOX_HEADS, LANES), rows,
        heads(2 * DIFF_HEADS, LANES), heads(2 * DIFF_HEADS, LANES), heads(DIFF_HEADS, DIFF_VDIM), rows,
        pl.BlockSpec((1, 1, 8, LANES), lambda i, j: (i, j, 0, 0)),
    )
    outs = pl.pallas_call(
        functools.partial(_in_proj_kernel, tm=tm, pos_offset=pos_offset),
        grid=(b, nt),
        in_specs=[pl.BlockSpec((1, tm, d), lambda i, j: (i, j, 0)),
                  pl.BlockSpec((1, 3, d), lambda i, j: (i, 0, 0)),
                  full(w), full(bf16x), full(seg)],
        out_specs=out_specs,
        out_shape=out_shape,
        scratch_shapes=[pltpu.VMEM((1, LANES), F32)],
        compiler_params=_params(("arbitrary", "arbitrary")),
        name="in_proj",
    )(x, mod, w, bf16x, seg)
    if head_major:
        outs = tuple(jnp.swapaxes(o, 2, 3) for o in outs[:3]) + outs[3:]
    else:
        outs = outs[:4] + (jnp.swapaxes(outs[4][:, :, :FOX_HEADS], 1, 2),) + outs[5:]
    return outs


_NT = (((1,), (1,)), ((), ()))
_TN = (((0,), (0,)), ((), ()))
_STAGES = 4
_FOX_V_ROWS = HEAD_DIM + 16
_FOX_GROUP = 4


_SKIP_MARGIN = 136.0
_STAT_FOX_Q, _STAT_FOX_K, _STAT_DIFF_Q, _STAT_DIFF_K, _STAT_CUM_FIRST, _STAT_CUM_LAST = range(6)


def _diagonal_blocks(tq):
    assert tq % (2 * LANES) == 0
    return ((tq // 2, 0), (tq, tq // 2))


def _dot_bound(stat, q_row, k_row, qi):
    tile = lax.broadcasted_iota(jnp.int32, (1, stat.shape[-1]), 1)
    qn2 = jnp.sum(jnp.where(tile == qi, stat[q_row:q_row + 1], 0.0))
    return jnp.sqrt(qn2 * stat[k_row:k_row + 1]) * 1.01 + 1.0


def _first_needed_tile(qi, bounds, m_ref):
    first = qi
    for c, bound in enumerate(bounds):
        tile = lax.broadcasted_iota(jnp.int32, bound.shape, 1)
        needed = (bound >= jnp.min(m_ref[c]) - _SKIP_MARGIN) & (tile < qi)
        first = jnp.minimum(first, jnp.min(jnp.where(needed, tile, qi)))
    return first


def _pipelined_tiles(qi, n_chains, scores, absorb, first_needed, sa_ref, sb_ref):
    chains = range(n_chains)
    bufs = (sa_ref, sb_ref)
    for c in chains:
        scores(qi, sa_ref, c, True)
        scores(jnp.maximum(qi - 1, 0), sb_ref, c)
        absorb(sa_ref, qi, c, True)
    n = qi - first_needed()

    def stages(top, count, more):
        for s in range(count):
            for c in chains:
                if more or s + 1 < count:
                    scores(jnp.maximum(top - s - 1, 0), bufs[s % 2], c)
                absorb(bufs[(s + 1) % 2], top - s, c, False)

    def body(i, carry):
        stages(qi - 1 - _STAGES * i, _STAGES, True)
        return carry

    trips = jnp.maximum(n - 1, 0) // _STAGES
    lax.fori_loop(0, trips, body, 0)
    rem = n - _STAGES * trips
    for count in range(1, _STAGES + 1):
        @pl.when(rem == count)
        def _(count=count):
            stages(qi - 1 - _STAGES * trips, count, False)


def _fox_kernel(q_ref, k_ref, v_ref, stat_ref, o_ref, sa_ref, sb_ref, m_ref, acc_ref, *, tq):
    qi = pl.program_id(2)
    half = tq // 2

    def scores(ki, dst, hh, diagonal=False):
        start = pl.multiple_of(ki * tq, tq)
        if diagonal:
            for keys, q0 in _diagonal_blocks(tq):
                k = k_ref[0, hh, pl.ds(start, keys), :]
                dst[hh, 0:keys, q0:q0 + half] = lax.dot_general(k, q_ref[0, hh, q0:q0 + half, :], _NT,
                                                                preferred_element_type=F32)
        else:
            k = k_ref[0, hh, pl.ds(start, tq), :]
            dst[hh] = lax.dot_general(k, q_ref[0, hh], _NT, preferred_element_type=F32)

    def absorb(src, ki, hh, diagonal):
        start = pl.multiple_of(ki * tq, tq)
        if diagonal:
            for keys, q0 in _diagonal_blocks(tq):
                s = src[hh, 0:keys, q0:q0 + half]
                key = lax.broadcasted_iota(jnp.int32, s.shape, 0)
                qry = lax.broadcasted_iota(jnp.int32, s.shape, 1) + q0
                s = jnp.where(key <= qry, s, NEG_INF)
                m_new = jnp.max(s, axis=0, keepdims=True)
                p = jnp.exp2(s - m_new)
                v = v_ref[0, hh, pl.ds(start, keys), 0:_FOX_V_ROWS]
                acc_ref[hh, :, q0:q0 + half] = lax.dot_general(v, p.astype(BF16), _TN, preferred_element_type=F32)
                m_ref[hh, :, q0:q0 + half] = m_new
            return
        s = src[hh]
        m_prev = m_ref[hh]
        m_new = jnp.maximum(m_prev, jnp.max(s, axis=0, keepdims=True))
        p = jnp.exp2(s - m_new)
        alpha = jnp.exp2(m_prev - m_new)
        v = v_ref[0, hh, pl.ds(start, tq), 0:_FOX_V_ROWS]
        pv = lax.dot_general(v, p.astype(BF16), _TN, preferred_element_type=F32)
        acc_ref[hh] = alpha * acc_ref[hh] + pv
        m_ref[hh] = m_new

    def first_needed():
        bounds = []
        for hh in range(_FOX_GROUP):
            stat = stat_ref[0, hh]
            tile = lax.broadcasted_iota(jnp.int32, (1, stat.shape[-1]), 1)
            cum_q = jnp.sum(jnp.where(tile == qi, stat[_STAT_CUM_FIRST:_STAT_CUM_FIRST + 1], 0.0))
            decay = cum_q - stat[_STAT_CUM_LAST:_STAT_CUM_LAST + 1]
            bounds.append(_dot_bound(stat, _STAT_FOX_Q, _STAT_FOX_K, qi) + decay)
        return _first_needed_tile(qi, bounds, m_ref)

    _pipelined_tiles(qi, _FOX_GROUP, scores, absorb, first_needed, sa_ref, sb_ref)
    outs = [acc_ref[hh, 0:HEAD_DIM] / acc_ref[hh, HEAD_DIM:HEAD_DIM + 1] for hh in range(_FOX_GROUP)]
    o_ref[0] = jnp.concatenate(outs, axis=0).T.astype(BF16)


def _fox_attention(fqa, fka, fva, stats):
    b, nh, t, _ = fqa.shape
    tq = min(t, 512)
    g = _FOX_GROUP
    return pl.pallas_call(
        functools.partial(_fox_kernel, tq=tq),
        grid=(b, nh // g, t // tq),
        in_specs=[pl.BlockSpec((1, g, tq, LANES), lambda i, j, q: (i, j, q, 0)),
                  pl.BlockSpec((1, g, t, LANES), lambda i, j, q: (i, j, 0, 0)),
                  pl.BlockSpec((1, g, t, LANES), lambda i, j, q: (i, j, 0, 0)),
                  pl.BlockSpec((1, g) + stats.shape[2:], lambda i, j, q: (i, j, 0, 0))],
        out_specs=pl.BlockSpec((1, tq, g * HEAD_DIM), lambda i, j, q: (i, q, j)),
        out_shape=jax.ShapeDtypeStruct((b, t, FOX_WIDTH), BF16),
        scratch_shapes=[pltpu.VMEM((g, tq, tq), F32), pltpu.VMEM((g, tq, tq), F32),
                        pltpu.VMEM((g, 1, tq), F32), pltpu.VMEM((g, _FOX_V_ROWS, tq), F32)],
        compiler_params=_params(("arbitrary", "arbitrary", "arbitrary")),
        name="fox_attn",
    )(fqa, fka, fva, stats)


def _alibi_slope(head):
    return jnp.exp2(jnp.full((1, 1), -8.0 / DIFF_HEADS, F32) * (head + 1).astype(F32))


def _lambda(lam_ref, lam_init):
    lv = lam_ref[...]
    e1 = jnp.exp(jnp.sum(lv[0:1] * lv[1:2], axis=1, keepdims=True))
    e2 = jnp.exp(jnp.sum(lv[2:3] * lv[3:4], axis=1, keepdims=True))
    return e1 - e2 + lam_init


def _diff_normed(o0, o1, lam, axis):
    df = o0 - lam * o1
    return df * lax.rsqrt(jnp.mean(df * df, axis=axis, keepdims=True) + RMS_EPS)


_ONES_ROWS = 16


def _diff_kernel(q_ref, k_ref, v_ref, stat_ref, lam_ref, g_ref, o_ref, sa_ref, sb_ref, m_ref, acc_ref, *, tq,
                 lam_init, first_head, group):
    qi = pl.program_id(2)
    half = tq // 2
    chains = 2 * group
    slopes = [_alibi_slope(first_head + pl.program_id(1) * group + h) for h in range(group)]

    def scores(ki, dst, c, diagonal=False):
        start = pl.multiple_of(ki * tq, tq)
        if diagonal:
            for keys, q0 in _diagonal_blocks(tq):
                k = k_ref[0, c, pl.ds(start, keys), :]
                dst[c, 0:keys, q0:q0 + half] = lax.dot_general(k, q_ref[0, c, q0:q0 + half, :], _NT,
                                                               preferred_element_type=F32)
        else:
            k = k_ref[0, c, pl.ds(start, tq), :]
            dst[c] = lax.dot_general(k, q_ref[0, c], _NT, preferred_element_type=F32)

    def values_t(c, start, keys):
        v = v_ref[0, c // 2, pl.ds(start, keys), :]
        return jnp.concatenate([v.T, jnp.ones((_ONES_ROWS, keys), BF16)], axis=0)

    def absorb(src, ki, c, diagonal):
        start = pl.multiple_of(ki * tq, tq)
        if diagonal:
            for keys, q0 in _diagonal_blocks(tq):
                s = src[c, 0:keys, q0:q0 + half]
                key = lax.broadcasted_iota(jnp.int32, s.shape, 0)
                qry = lax.broadcasted_iota(jnp.int32, s.shape, 1) + q0
                ahead = jnp.maximum(key - qry, 0).astype(F32)
                s = jnp.where((key // CHUNK) <= (qry // CHUNK), s - (2.0 * LOG2E * slopes[c // 2]) * ahead, NEG_INF)
                m_new = jnp.max(s, axis=0, keepdims=True)
                p = jnp.exp2(s - m_new)
                acc_ref[c, :, q0:q0 + half] = jnp.dot(values_t(c, start, keys), p.astype(BF16),
                                                      preferred_element_type=F32)
                m_ref[c, :, q0:q0 + half] = m_new
            return
        s = src[c]
        m_prev = m_ref[c]
        m_new = jnp.maximum(m_prev, jnp.max(s, axis=0, keepdims=True))
        p = jnp.exp2(s - m_new)
        alpha = jnp.exp2(m_prev - m_new)
        acc_ref[c] = alpha * acc_ref[c] + jnp.dot(values_t(c, start, tq), p.astype(BF16),
                                                  preferred_element_type=F32)
        m_ref[c] = m_new

    def first_needed():
        bounds = []
        for c in range(chains):
            stat = stat_ref[0, c]
            tile = lax.broadcasted_iota(jnp.int32, (1, stat.shape[-1]), 1)
            gap = ((qi - 1 - tile) * tq + 1).astype(F32)
            bounds.append(_dot_bound(stat, _STAT_DIFF_Q, _STAT_DIFF_K, qi) - (LOG2E * slopes[c // 2]) * gap)
        return _first_needed_tile(qi, bounds, m_ref)

    _pipelined_tiles(qi, chains, scores, absorb, first_needed, sa_ref, sb_ref)
    lam = _lambda(lam_ref, lam_init)
    for h in range(group):
        o0, o1 = (acc_ref[2 * h + c, :DIFF_VDIM] / acc_ref[2 * h + c, DIFF_VDIM:DIFF_VDIM + 1] for c in range(2))
        normed = _diff_normed(o0, o1, lam, 0).T * g_ref[...] * (1.0 - lam_init)
        o_ref[0, :, h * DIFF_VDIM:(h + 1) * DIFF_VDIM] = normed.astype(BF16)


_DIFF_GROUPS = ((0, 1, 2), (2, 2, 1))


def _diff_attention(dqa, dka, dvb, stats, lam_vecs, subln_g, lam_init):
    b, _, t, _ = dqa.shape
    tq = min(t, 512)
    outs = []
    for first_head, g, n_groups in _DIFF_GROUPS:
        off = first_head // g

        def heads(block, off=off):
            return pl.BlockSpec(block, lambda i, j, q: (i, j + off, 0, 0))

        outs.append(pl.pallas_call(
            functools.partial(_diff_kernel, tq=tq, lam_init=lam_init, first_head=first_head, group=g),
            grid=(b, n_groups, t // tq),
            in_specs=[pl.BlockSpec((1, 2 * g, tq, LANES), lambda i, j, q, off=off: (i, j + off, q, 0)),
                      heads((1, 2 * g, t, LANES)), heads((1, g, t, DIFF_VDIM)),
                      heads((1, 2 * g) + stats.shape[2:]),
                      pl.BlockSpec(lam_vecs.shape, lambda i, j, q: (0, 0)),
                      pl.BlockSpec(subln_g.shape, lambda i, j, q: (0, 0))],
            out_specs=pl.BlockSpec((1, tq, g * DIFF_VDIM), lambda i, j, q: (i, q, j)),
            out_shape=jax.ShapeDtypeStruct((b, t, n_groups * g * DIFF_VDIM), BF16),
            scratch_shapes=[pltpu.VMEM((2 * g, tq, tq), F32), pltpu.VMEM((2 * g, tq, tq), F32),
                            pltpu.VMEM((2 * g, 1, tq), F32),
                            pltpu.VMEM((2 * g, DIFF_VDIM + _ONES_ROWS, tq), F32)],
            compiler_params=_params(("arbitrary", "arbitrary", "arbitrary")),
            name="diff_attn",
        )(dqa, dka, dvb, stats, lam_vecs, subln_g))
    return tuple(outs)


def _prefix_sum_lanes(x):
    n = x.shape[-1]
    lane = lax.broadcasted_iota(jnp.int32, x.shape, x.ndim - 1)
    sh = 1
    while sh < n:
        x = x + jnp.where(lane >= sh, pltpu.roll(x, sh, x.ndim - 1), 0.0)
        sh *= 2
    return x


def _fox_sample_kernel(q_ref, k_ref, v_ref, ck_ref, cv_ref, clf_ref, o_ref, *, t, past):
    r = lax.broadcasted_iota(jnp.int32, (t, t), 0)
    c = lax.broadcasted_iota(jnp.int32, (t, t), 1)
    cum_c = _prefix_sum_lanes(clf_ref[0])
    suffix = (cum_c[:, past - 1:past] - cum_c) * LOG2E
    for hh in range(FOX_HEADS):
        data = slice(0, HEAD_DIM) if hh % 2 == 0 else slice(HEAD_DIM, LANES)
        ext = HEAD_DIM if hh % 2 == 0 else 0
        qa = q_ref[0, hh]
        qf = qa.astype(F32)
        cum_q = sum(qf[:, ext + 8 * p + hh:ext + 8 * p + hh + 1] for p in range(3))
        s_c = jnp.dot(qa[:, data], ck_ref[0, 0, hh].astype(BF16), preferred_element_type=F32)
        s_c = s_c + cum_q + suffix[hh:hh + 1]
        s_n = lax.dot_general(qa, k_ref[0, hh], _NT, preferred_element_type=F32)
        s_n = jnp.where(c <= r, s_n, NEG_INF)
        m = jnp.maximum(jnp.max(s_c, axis=1, keepdims=True), jnp.max(s_n, axis=1, keepdims=True))
        p_c = jnp.exp2(s_c - m)
        p_n = jnp.exp2(s_n - m)
        denom = jnp.sum(p_c, axis=1, keepdims=True) + jnp.sum(p_n, axis=1, keepdims=True)
        o = lax.dot_general(p_c.astype(BF16), cv_ref[0, 0, hh].astype(BF16), _NT, preferred_element_type=F32)
        o = o + jnp.dot(p_n.astype(BF16), v_ref[0, hh][:, 0:HEAD_DIM], preferred_element_type=F32)
        o_ref[0, :, hh * HEAD_DIM:(hh + 1) * HEAD_DIM] = (o / denom).astype(BF16)


def _fox_sample(fqa, fka, fva, cache_k, cache_v, cache_logf):
    b, nh, t, _ = fqa.shape
    past = cache_k.shape[3]
    cache_k, cache_v = jnp.swapaxes(cache_k, 3, 4), jnp.swapaxes(cache_v, 3, 4)
    new = pl.BlockSpec((1, nh, t, LANES), lambda i: (i, 0, 0, 0))
    cache = pl.BlockSpec((1, 1, nh, HEAD_DIM, past), lambda i: (0, i, 0, 0, 0))
    return pl.pallas_call(
        functools.partial(_fox_sample_kernel, t=t, past=past),
        grid=(b,),
        in_specs=[new, new, new, cache, cache, pl.BlockSpec((1, nh, past), lambda i: (i, 0, 0))],
        out_specs=pl.BlockSpec((1, t, FOX_WIDTH), lambda i: (i, 0, 0)),
        out_shape=jax.ShapeDtypeStruct((b, t, FOX_WIDTH), BF16),
        compiler_params=_params(("arbitrary",)),
        name="fox_sample",
    )(fqa, fka, fva, cache_k, cache_v, cache_logf)


def _diff_sample_kernel(q_ref, k_ref, v_ref, ck_ref, cv_ref, lam_ref, g_ref, o_ref, *, t, past, lam_init):
    lane = lax.broadcasted_iota(jnp.int32, (t, LANES), 1)
    pos_q = past + lax.broadcasted_iota(jnp.int32, (t, past), 0)
    pos_c = lax.broadcasted_iota(jnp.int32, (t, past), 1)
    dist_c = jnp.abs(pos_q - pos_c).astype(F32)
    vis_c = (pos_c // CHUNK) <= (pos_q // CHUNK)
    pq_n = past + lax.broadcasted_iota(jnp.int32, (t, t), 0)
    pk_n = past + lax.broadcasted_iota(jnp.int32, (t, t), 1)
    dist_n = jnp.abs(pq_n - pk_n).astype(F32)
    vis_n = (pk_n // CHUNK) <= (pq_n // CHUNK)
    lam = _lambda(lam_ref, lam_init)
    for hd in range(DIFF_HEADS):
        slope = LOG2E * 2.0 ** (-8.0 * (hd + 1) / DIFF_HEADS)
        ck = ck_ref[0, 0, hd].astype(BF16)
        cv = cv_ref[0, 0, hd].astype(BF16)
        v = v_ref[0, hd]
        outs = []
        for c in range(2):
            half = (lane < HEAD_DIM) if c == 0 else (lane >= HEAD_DIM)
            qz = jnp.where(half, q_ref[0, 2 * hd + c], jnp.zeros((), BF16))
            kz = jnp.where(half, k_ref[0, 2 * hd + c], jnp.zeros((), BF16))
            s_c = jnp.dot(qz, ck, preferred_element_type=F32)
            s_c = jnp.where(vis_c, s_c - slope * dist_c, NEG_INF)
            s_n = lax.dot_general(qz, kz, _NT, preferred_element_type=F32)
            s_n = jnp.where(vis_n, s_n - slope * dist_n, NEG_INF)
            m = jnp.maximum(jnp.max(s_c, axis=1, keepdims=True), jnp.max(s_n, axis=1, keepdims=True))
            p_c = jnp.exp2(s_c - m)
            p_n = jnp.exp2(s_n - m)
            denom = jnp.sum(p_c, axis=1, keepdims=True) + jnp.sum(p_n, axis=1, keepdims=True)
            o = jnp.dot(p_c.astype(BF16), cv, preferred_element_type=F32)
            o = o + jnp.dot(p_n.astype(BF16), v, preferred_element_type=F32)
            outs.append(o / denom)
        normed = _diff_normed(outs[0], outs[1], lam, 1) * g_ref[...] * (1.0 - lam_init)
        o_ref[0, :, hd * DIFF_VDIM:(hd + 1) * DIFF_VDIM] = normed.astype(BF16)


def _diff_sample(dqa, dka, dvb, cache_k, cache_v, lam_vecs, subln_g, lam_init):
    b, nm, t, _ = dqa.shape
    past = cache_k.shape[3]
    ck = jnp.transpose(cache_k, (0, 1, 2, 4, 5, 3)).reshape(cache_k.shape[:3] + (DIFF_VDIM, past))
    new = pl.BlockSpec((1, nm, t, LANES), lambda i: (i, 0, 0, 0))
    return pl.pallas_call(
        functools.partial(_diff_sample_kernel, t=t, past=past, lam_init=lam_init),
        grid=(b,),
        in_specs=[new, new, pl.BlockSpec((1, DIFF_HEADS, t, DIFF_VDIM), lambda i: (i, 0, 0, 0)),
                  pl.BlockSpec((1, 1, DIFF_HEADS, DIFF_VDIM, past), lambda i: (0, i, 0, 0, 0)),
                  pl.BlockSpec((1, 1, DIFF_HEADS, past, DIFF_VDIM), lambda i: (0, i, 0, 0, 0)),
                  pl.BlockSpec(lam_vecs.shape, lambda i: (0, 0)),
                  pl.BlockSpec(subln_g.shape, lambda i: (0, 0))],
        out_specs=pl.BlockSpec((1, t, DIFF_WIDTH), lambda i: (i, 0, 0)),
        out_shape=jax.ShapeDtypeStruct((b, t, DIFF_WIDTH), BF16),
        compiler_params=_params(("arbitrary",)),
        name="diff_sample",
    )(dqa, dka, dvb, ck, cache_v, lam_vecs, subln_g)


def _out_proj_kernel(fo_ref, *refs, alpha, n_diff):
    do_refs, (fg_ref, dg_ref, x_ref, mod_ref, w_ref, g_ref, b_ref, y_ref) = refs[:n_diff], refs[n_diff:]
    fox = fo_ref[0] * fg_ref[0]
    diff = jnp.concatenate([r[0] for r in do_refs], axis=1) * dg_ref[0]
    branch = jnp.dot(fox, w_ref[0:FOX_WIDTH, :], preferred_element_type=F32)
    branch = branch + jnp.dot(diff, w_ref[FOX_WIDTH:, :], preferred_element_type=F32)
    r = alpha * x_ref[0] + mod_ref[0, 2:3, :] * branch
    mu = jnp.mean(r, axis=-1, keepdims=True)
    rc = r - mu
    var = jnp.mean(rc * rc, axis=-1, keepdims=True)
    y_ref[0] = rc * lax.rsqrt(var + LN_EPS) * g_ref[...] + b_ref[...]


def _out_proj(fox_o, diff_parts, fg, dg, x, mod, w_out, ln_g, ln_b, alpha):
    b, t, d = x.shape
    tm = 1024 if t % 1024 == 0 else min(t, 512)
    rows = pl.BlockSpec((1, tm, FOX_WIDTH), lambda i, j: (i, j, 0))
    wide = pl.BlockSpec((1, tm, d), lambda i, j: (i, j, 0))

    def full(a):
        return pl.BlockSpec(a.shape, lambda i, j: (0,) * a.ndim)

    return pl.pallas_call(
        functools.partial(_out_proj_kernel, alpha=alpha, n_diff=len(diff_parts)),
        grid=(b, t // tm),
        in_specs=([rows] + [pl.BlockSpec((1, tm, p.shape[-1]), lambda i, j: (i, j, 0)) for p in diff_parts]
                  + [rows, rows, wide, pl.BlockSpec((1, 3, d), lambda i, j: (i, 0, 0)),
                     full(w_out), full(ln_g), full(ln_b)]),
        out_specs=wide,
        out_shape=jax.ShapeDtypeStruct((b, t, d), F32),
        compiler_params=_params(("arbitrary", "arbitrary")),
        name="out_proj",
    )(fox_o, *diff_parts, fg, dg, x, mod, w_out, ln_g, ln_b)


def _pack_w_in(w_in):
    wt = w_in.T
    fw = FOX_WIDTH
    o = 0
    fq = wt[o:o + fw]; o += fw
    fk = wt[o:o + fw]; o += fw
    fv = wt[o:o + fw]; o += fw
    ff = wt[o:o + FOX_HEADS]; o += FOX_HEADS
    fg = wt[o:o + fw]; o += fw
    dq = wt[o:o + fw]; o += fw
    dk = wt[o:o + fw]; o += fw
    dv = wt[o:o + fw]; o += fw
    dg = wt[o:o + fw]
    qs = HEAD_DIM ** -0.5 * LOG2E
    return jnp.concatenate([fq * qs, fk, fv, fg, dq * qs, dk, dv, dg,
                            jnp.tile(ff, (LANES // FOX_HEADS, 1))], axis=0).astype(BF16)


def kernel(x_prompt, x_sample, cache_fox_k, cache_fox_v, cache_fox_logf, cache_diff_k, cache_diff_v,
           c_prompt, c_sample, w_ada, b_ada, w_in, b_f, lambda_q1, lambda_k1, lambda_q2, lambda_k2,
           subln_g, w_out, ln_g, ln_b):
    depth = w_in.shape[0]
    assert depth == 1, "single-layer step"
    layer = 0
    lam_init = 0.8 - 0.6 * math.exp(-0.3 * layer)
    alpha = (2 * depth) ** 0.25
    bp, bs = c_prompt.shape[0], c_sample.shape[0]
    d = x_prompt.shape[-1]
    past = cache_fox_k.shape[3]

    c_all = jnp.concatenate([c_prompt, c_sample, jnp.zeros((-(bp + bs) % 8, d), F32)], axis=0)
    mod = _ada(c_all, w_ada[layer], b_ada[layer]).reshape(c_all.shape[0], 3, d)
    mod_p, mod_s = mod[:bp], mod[bp:bp + bs]

    w = _pack_w_in(w_in[layer])
    bf16x = jnp.tile(b_f[layer], LANES // FOX_HEADS).reshape(1, LANES)
    seg = np.repeat(np.eye(FOX_WIDTH // HEAD_DIM, LANES, dtype=np.float32), HEAD_DIM, axis=0)
    seg = jnp.asarray(seg, BF16)
    lam_vecs = jnp.stack([lambda_q1[layer], lambda_k1[layer], lambda_q2[layer], lambda_k2[layer]])
    g = subln_g[layer].reshape(1, DIFF_VDIM)
    w_o = w_out[layer].astype(BF16)
    lg, lb = ln_g[layer].reshape(1, d), ln_b[layer].reshape(1, d)

    def run(x, mod_x, pos_offset, attend):
        (fk, fv, dk, dv, logf, fqa, fka, fva, fg, dqa, dka, dvb, dg, stats) = _in_proj(
            x, mod_x, w, bf16x, seg, pos_offset)
        stats = jnp.transpose(stats[..., :FOX_WIDTH // HEAD_DIM], (0, 3, 2, 1))
        fox_o, diff_o = attend(fqa, fka, fva, dqa, dka, dvb, stats)
        y = _out_proj(fox_o, diff_o, fg, dg, x, mod_x, w_o, lg, lb, alpha)
        b, t = x.shape[:2]
        states = (fk[None], fv[None], logf[None],
                  dk.reshape(1, b, DIFF_HEADS, t, 2, HEAD_DIM), dv[None])
        return y, states

    def prompt_attend(fqa, fka, fva, dqa, dka, dvb, stats):
        return (_fox_attention(fqa, fka, fva, stats),
                _diff_attention(dqa, dka, dvb, stats, lam_vecs, g, lam_init))

    def sample_attend(fqa, fka, fva, dqa, dka, dvb, stats):
        del stats
        return (_fox_sample(fqa, fka, fva, cache_fox_k, cache_fox_v, cache_fox_logf[layer]),
                (_diff_sample(dqa, dka, dvb, cache_diff_k, cache_diff_v, lam_vecs, g, lam_init),))

    yp, sp = run(x_prompt, mod_p, 0, prompt_attend)
    ys, ss = run(x_sample, mod_s, past, sample_attend)
    return (yp, ys) + sp + ss
```

```python
import functools
import math

import jax
import jax.numpy as jnp
import numpy as np
from jax import lax
from jax.experimental import pallas as pl
from jax.experimental.pallas import tpu as pltpu

F32 = jnp.float32
BF16 = jnp.bfloat16

HEAD_DIM = 64
FOX_HEADS = 8
DIFF_HEADS = 4
DIFF_VDIM = 2 * HEAD_DIM
FOX_WIDTH = FOX_HEADS * HEAD_DIM
DIFF_WIDTH = DIFF_HEADS * DIFF_VDIM
CHUNK = 64
LN_EPS = 1e-5
RMS_EPS = 1e-5
LANES = 128
NEG_INF = float("-inf")
LOG2E = math.log2(math.e)

_SEC = {name: i * FOX_WIDTH for i, name in enumerate(("fq", "fk", "fv", "fg", "dq", "dk", "dv", "dg"))}
_FF_OFF = 8 * FOX_WIDTH

_VMEM_LIMIT = 56 * 1024 * 1024
_TILE = 512


def _params(sem):
    return pltpu.CompilerParams(dimension_semantics=sem, vmem_limit_bytes=_VMEM_LIMIT)


def _ada_kernel(c_ref, w_ref, b_ref, o_ref):
    c = c_ref[...]
    s = c * jax.nn.sigmoid(c)
    o_ref[...] = jnp.dot(s, w_ref[...], preferred_element_type=F32) + b_ref[...]


def _ada(c_all, w_ada, b_ada):
    rows, d = c_all.shape
    n = w_ada.shape[1]
    tn = 512
    return pl.pallas_call(
        _ada_kernel,
        grid=(n // tn,),
        in_specs=[pl.BlockSpec((rows, d), lambda j: (0, 0)),
                  pl.BlockSpec((d, tn), lambda j: (0, j)),
                  pl.BlockSpec((1, tn), lambda j: (0, j))],
        out_specs=pl.BlockSpec((rows, tn), lambda j: (0, j)),
        out_shape=jax.ShapeDtypeStruct((rows, n), F32),
        compiler_params=_params(("arbitrary",)),
        name="ada",
    )(c_all, w_ada, b_ada.reshape(1, n))


def _split3(x):
    hi = x.astype(BF16)
    r1 = x - hi.astype(F32)
    mid = r1.astype(BF16)
    lo = (r1 - mid.astype(F32)).astype(BF16)
    return hi, mid, lo


def _in_proj_kernel(x_ref, mod_ref, w_ref, bf_ref, seg_ref,
                    fk_ref, fv_ref, dk_ref, dv_ref, logf_ref,
                    fqa_ref, fka_ref, fva_ref, fg_ref, dqa_ref, dka_ref, dvb_ref, dg_ref, stat_ref,
                    carry_ref, *, tm, pos_offset):
    ti = pl.program_id(1)
    x = x_ref[0]
    shift = mod_ref[0, 0:1, :]
    scale = mod_ref[0, 1:2, :]
    h = (x * (1.0 + scale) + shift).astype(BF16)

    def proj(name, width=FOX_WIDTH):
        off = _SEC[name] if name in _SEC else _FF_OFF
        return jnp.dot(h, w_ref[:, off:off + width], preferred_element_type=F32)

    lane = lax.broadcasted_iota(jnp.int32, (tm, LANES), 1)
    row = lax.broadcasted_iota(jnp.int32, (tm, LANES), 0)
    lower = lane < HEAD_DIM
    grp = (lane // 8) % 4

    def by_group(a, b, c):
        return jnp.where(grp == 0, a, jnp.where(grp == 1, b, jnp.where(grp == 2, c, jnp.zeros_like(a))))

    def max_sq_norm(z):
        zr = z.astype(BF16).astype(F32)
        sq = zr * zr
        if tm % 32 == 0:
            sq = jnp.maximum(sq[0:tm // 2], sq[tm // 2:tm])
        sq = jnp.dot(sq.astype(BF16), seg_ref[...], preferred_element_type=F32)
        return jnp.max(sq, axis=0, keepdims=True)

    logf = jax.nn.log_sigmoid(proj("ff", LANES) + bf_ref[...])
    logf_ref[0] = logf.T[0:FOX_HEADS] if tm % LANES == 0 else logf

    dq, dk, dv = proj("dq"), proj("dk"), proj("dv")
    pos = (row + (ti * tm + pos_offset)).astype(F32)
    sub = lane % HEAD_DIM
    for hd in range(DIFF_HEADS):
        slope = 2.0 ** (-8.0 * (hd + 1) / DIFF_HEADS)
        sl = slice(hd * LANES, (hd + 1) * LANES)
        p0, p1, p2 = (a.astype(F32) for a in _split3((slope * LOG2E) * pos))
        alibi = jnp.where(sub % 3 == 0, p0, jnp.where(sub % 3 == 1, p1, p2))
        alibi_q = jnp.where(sub < 3, -alibi, (sub < 6).astype(F32))
        alibi_k = jnp.where(sub < 3, 1.0, jnp.where(sub < 6, alibi, 0.0))
        dqa_ref[0, 2 * hd] = jnp.where(lower, dq[:, sl], alibi_q).astype(BF16)
        dqa_ref[0, 2 * hd + 1] = jnp.where(lower, alibi_q, dq[:, sl]).astype(BF16)
        dka_ref[0, 2 * hd] = jnp.where(lower, dk[:, sl], alibi_k).astype(BF16)
        dka_ref[0, 2 * hd + 1] = jnp.where(lower, alibi_k, dk[:, sl]).astype(BF16)
        if tm % LANES:
            dk_ref[0, hd] = dk[:, sl]
        dv_ref[0, hd] = dv[:, sl]
        dvb_ref[0, hd] = dv[:, sl].astype(BF16)
    if tm % LANES == 0:
        dk_ref[0] = dk.T.reshape(dk_ref.shape[1:])
    dg_ref[0] = jax.nn.silu(proj("dg")).astype(BF16)
    diff_norms = [max_sq_norm(dq), max_sq_norm(dk)]

    tri_r = lax.broadcasted_iota(jnp.int32, (tm, tm), 0)
    tri_c = lax.broadcasted_iota(jnp.int32, (tm, tm), 1)
    tri = (tri_c <= tri_r).astype(BF16)
    part = jnp.dot(tri, by_group(*_split3(logf)), preferred_element_type=F32)
    zq, zk, zv = proj("fq"), proj("fk"), proj("fv")
    local = part + pltpu.roll(part, 8, 1) + pltpu.roll(part, 16, 1) + pltpu.roll(part, 24, 1)

    @pl.when(ti == 0)
    def _():
        carry_ref[...] = jnp.zeros_like(carry_ref)

    cum = local + carry_ref[...]
    carry_ref[...] = cum[tm - 1:tm, :]
    cum2 = cum * LOG2E
    pieces = by_group(*(a.astype(F32) for a in _split3(cum2)))
    behind = pltpu.roll(pieces, 24, 1)
    u = lane % HEAD_DIM
    ext_q = jnp.where(u < 24, pieces, (u < 48).astype(F32))

    def ext_k(h):
        mine = u % 8 == h
        return jnp.where(mine & (u < 24), 1.0, jnp.where(mine & (u < 48), -behind, 0.0))

    fox_norms = [max_sq_norm(zq), max_sq_norm(zk)]
    e_even = (lane == HEAD_DIM).astype(F32)
    for j in range(FOX_HEADS // 2):
        sl = slice(j * LANES, (j + 1) * LANES)
        fqa_ref[0, 2 * j] = jnp.where(lower, zq[:, sl], ext_q).astype(BF16)
        fqa_ref[0, 2 * j + 1] = jnp.where(lower, ext_q, zq[:, sl]).astype(BF16)
        fka_ref[0, 2 * j] = jnp.where(lower, zk[:, sl], ext_k(2 * j)).astype(BF16)
        fka_ref[0, 2 * j + 1] = jnp.where(lower, ext_k(2 * j + 1), zk[:, sl]).astype(BF16)
        fva_ref[0, 2 * j] = jnp.where(lower, zv[:, sl], e_even).astype(BF16)
        fva_ref[0, 2 * j + 1] = jnp.where(lower, pltpu.roll(zv[:, sl], HEAD_DIM, 1), e_even).astype(BF16)
    if tm % LANES == 0:
        fk_ref[0] = zk.T.reshape(fk_ref.shape[1:])
        fv_ref[0] = zv.T.reshape(fv_ref.shape[1:])
    else:
        for hd in range(FOX_HEADS):
            fk_ref[0, hd] = zk[:, hd * HEAD_DIM:(hd + 1) * HEAD_DIM]
            fv_ref[0, hd] = zv[:, hd * HEAD_DIM:(hd + 1) * HEAD_DIM]
    fg_ref[0] = jax.nn.silu(proj("fg")).astype(BF16)
    stat_ref[0, 0] = jnp.concatenate(
        fox_norms + diff_norms + [cum2[0:1], cum2[tm - 1:tm], jnp.zeros((2, LANES), F32)],
        axis=0)


def _in_proj(x, mod, w, bf16x, seg, pos_offset):
    b, t, d = x.shape
    tm = min(t, _TILE)
    nt = t // tm
    head_major = tm % LANES == 0

    def full(a):
        return pl.BlockSpec(a.shape, lambda i, j: (0,) * a.ndim)

    def heads(n, width):
        return pl.BlockSpec((1, n, tm, width), lambda i, j: (i, 0, j, 0))

    def cache_shape(n, width):
        return jax.ShapeDtypeStruct((b, n, width, t) if head_major else (b, n, t, width), F32)

    def cache_spec(n, width):
        return pl.BlockSpec((1, n, width, tm), lambda i, j: (i, 0, 0, j)) if head_major else heads(n, width)

    rows = pl.BlockSpec((1, tm, FOX_WIDTH), lambda i, j: (i, j, 0))
    out_shape = (
        cache_shape(FOX_HEADS, HEAD_DIM),
        cache_shape(FOX_HEADS, HEAD_DIM),
        cache_shape(DIFF_HEADS, DIFF_VDIM),
        jax.ShapeDtypeStruct((b, DIFF_HEADS, t, DIFF_VDIM), F32),
        jax.ShapeDtypeStruct((b, FOX_HEADS, t) if head_major else (b, t, LANES), F32),
        jax.ShapeDtypeStruct((b, FOX_HEADS, t, LANES), BF16),
        jax.ShapeDtypeStruct((b, FOX_HEADS, t, LANES), BF16),
        jax.ShapeDtypeStruct((b, FOX_HEADS, t, LANES), BF16),
        jax.ShapeDtypeStruct((b, t, FOX_WIDTH), BF16),
        jax.ShapeDtypeStruct((b, 2 * DIFF_HEADS, t, LANES), BF16),
        jax.ShapeDtypeStruct((b, 2 * DIFF_HEADS, t, LANES), BF16),
        jax.ShapeDtypeStruct((b, DIFF_HEADS, t, DIFF_VDIM), BF16),
        jax.ShapeDtypeStruct((b, t, DIFF_WIDTH), BF16),
        jax.ShapeDtypeStruct((b, nt, 8, LANES), F32),
    )
    out_specs = (
        cache_spec(FOX_HEADS, HEAD_DIM), cache_spec(FOX_HEADS, HEAD_DIM),
        cache_spec(DIFF_HEADS, DIFF_VDIM), heads(DIFF_HEADS, DIFF_VDIM),
        (pl.BlockSpec((1, FOX_HEADS, tm), lambda i, j: (i, 0, j)) if head_major
         else pl.BlockSpec((1, tm, LANES), lambda i, j: (i, j, 0))),
        heads(FOX_HEADS, LANES), heads(FOX_HEADS, LANES), heads(FOX_HEADS, LANES), rows,
        heads(2 * DIFF_HEADS, LANES), heads(2 * DIFF_HEADS, LANES), heads(DIFF_HEADS, DIFF_VDIM), rows,
        pl.BlockSpec((1, 1, 8, LANES), lambda i, j: (i, j, 0, 0)),
    )
    outs = pl.pallas_call(
        functools.partial(_in_proj_kernel, tm=tm, pos_offset=pos_offset),
        grid=(b, nt),
        in_specs=[pl.BlockSpec((1, tm, d), lambda i, j: (i, j, 0)),
                  pl.BlockSpec((1, 3, d), lambda i, j: (i, 0, 0)),
                  full(w), full(bf16x), full(seg)],
        out_specs=out_specs,
        out_shape=out_shape,
        scratch_shapes=[pltpu.VMEM((1, LANES), F32)],
        compiler_params=_params(("arbitrary", "arbitrary")),
        name="in_proj",
    )(x, mod, w, bf16x, seg)
    if head_major:
        outs = tuple(jnp.swapaxes(o, 2, 3) for o in outs[:3]) + outs[3:]
    else:
        outs = outs[:4] + (jnp.swapaxes(outs[4][:, :, :FOX_HEADS], 1, 2),) + outs[5:]
    return outs


_NT = (((1,), (1,)), ((), ()))
_TN = (((0,), (0,)), ((), ()))
_STAGES = 4
_FOX_V_ROWS = HEAD_DIM + 16
_FOX_GROUP = 4


_SKIP_MARGIN = 136.0
_STAT_FOX_Q, _STAT_FOX_K, _STAT_DIFF_Q, _STAT_DIFF_K, _STAT_CUM_FIRST, _STAT_CUM_LAST = range(6)


def _diagonal_blocks(tq):
    assert tq % (2 * LANES) == 0
    return ((tq // 2, 0), (tq, tq // 2))


def _dot_bound(stat, q_row, k_row, qi):
    tile = lax.broadcasted_iota(jnp.int32, (1, stat.shape[-1]), 1)
    qn2 = jnp.sum(jnp.where(tile == qi, stat[q_row:q_row + 1], 0.0))
    return jnp.sqrt(qn2 * stat[k_row:k_row + 1]) * 1.01 + 1.0


def _first_needed_tile(qi, bounds, m_ref):
    first = qi
    for c, bound in enumerate(bounds):
        tile = lax.broadcasted_iota(jnp.int32, bound.shape, 1)
        needed = (bound >= jnp.min(m_ref[c]) - _SKIP_MARGIN) & (tile < qi)
        first = jnp.minimum(first, jnp.min(jnp.where(needed, tile, qi)))
    return first


def _pipelined_tiles(qi, n_chains, scores, absorb, first_needed, sa_ref, sb_ref):
    chains = range(n_chains)
    bufs = (sa_ref, sb_ref)
    for c in chains:
        scores(qi, sa_ref, c, True)
        scores(jnp.maximum(qi - 1, 0), sb_ref, c)
        absorb(sa_ref, qi, c, True)
    n = qi - first_needed()

    def stages(top, count, more):
        for s in range(count):
            for c in chains:
                if more or s + 1 < count:
                    scores(jnp.maximum(top - s - 1, 0), bufs[s % 2], c)
                absorb(bufs[(s + 1) % 2], top - s, c, False)

    def body(i, carry):
        stages(qi - 1 - _STAGES * i, _STAGES, True)
        return carry

    trips = jnp.maximum(n - 1, 0) // _STAGES
    lax.fori_loop(0, trips, body, 0)
    rem = n - _STAGES * trips
    for count in range(1, _STAGES + 1):
        @pl.when(rem == count)
        def _(count=count):
            stages(qi - 1 - _STAGES * trips, count, False)


def _fox_kernel(q_ref, k_ref, v_ref, stat_ref, o_ref, sa_ref, sb_ref, m_ref, acc_ref, *, tq):
    qi = pl.program_id(2)
    half = tq // 2

    def scores(ki, dst, hh, diagonal=False):
        start = pl.multiple_of(ki * tq, tq)
        if diagonal:
            for keys, q0 in _diagonal_blocks(tq):
                k = k_ref[0, hh, pl.ds(start, keys), :]
                dst[hh, 0:keys, q0:q0 + half] = lax.dot_general(k, q_ref[0, hh, q0:q0 + half, :], _NT,
                                                                preferred_element_type=F32)
        else:
            k = k_ref[0, hh, pl.ds(start, tq), :]
            dst[hh] = lax.dot_general(k, q_ref[0, hh], _NT, preferred_element_type=F32)

    def absorb(src, ki, hh, diagonal):
        start = pl.multiple_of(ki * tq, tq)
        if diagonal:
            for keys, q0 in _diagonal_blocks(tq):
                s = src[hh, 0:keys, q0:q0 + half]
                key = lax.broadcasted_iota(jnp.int32, s.shape, 0)
                qry = lax.broadcasted_iota(jnp.int32, s.shape, 1) + q0
                s = jnp.where(key <= qry, s, NEG_INF)
                m_new = jnp.max(s, axis=0, keepdims=True)
                p = jnp.exp2(s - m_new)
                v = v_ref[0, hh, pl.ds(start, keys), 0:_FOX_V_ROWS]
                acc_ref[hh, :, q0:q0 + half] = lax.dot_general(v, p.astype(BF16), _TN, preferred_element_type=F32)
                m_ref[hh, :, q0:q0 + half] = m_new
            return
        s = src[hh]
        m_prev = m_ref[hh]
        m_new = jnp.maximum(m_prev, jnp.max(s, axis=0, keepdims=True))
        p = jnp.exp2(s - m_new)
        alpha = jnp.exp2(m_prev - m_new)
        v = v_ref[0, hh, pl.ds(start, tq), 0:_FOX_V_ROWS]
        pv = lax.dot_general(v, p.astype(BF16), _TN, preferred_element_type=F32)
        acc_ref[hh] = alpha * acc_ref[hh] + pv
        m_ref[hh] = m_new

    def first_needed():
        bounds = []
        for hh in range(_FOX_GROUP):
            stat = stat_ref[0, hh]
            tile = lax.broadcasted_iota(jnp.int32, (1, stat.shape[-1]), 1)
            cum_q = jnp.sum(jnp.where(tile == qi, stat[_STAT_CUM_FIRST:_STAT_CUM_FIRST + 1], 0.0))
            decay = cum_q - stat[_STAT_CUM_LAST:_STAT_CUM_LAST + 1]
            bounds.append(_dot_bound(stat, _STAT_FOX_Q, _STAT_FOX_K, qi) + decay)
        return _first_needed_tile(qi, bounds, m_ref)

    _pipelined_tiles(qi, _FOX_GROUP, scores, absorb, first_needed, sa_ref, sb_ref)
    outs = [acc_ref[hh, 0:HEAD_DIM] / acc_ref[hh, HEAD_DIM:HEAD_DIM + 1] for hh in range(_FOX_GROUP)]
    o_ref[0] = jnp.concatenate(outs, axis=0).T.astype(BF16)


def _fox_attention(fqa, fka, fva, stats):
    b, nh, t, _ = fqa.shape
    tq = min(t, _TILE)
    g = _FOX_GROUP
    return pl.pallas_call(
        functools.partial(_fox_kernel, tq=tq),
        grid=(b, nh // g, t // tq),
        in_specs=[pl.BlockSpec((1, g, tq, LANES), lambda i, j, q: (i, j, q, 0)),
                  pl.BlockSpec((1, g, t, LANES), lambda i, j, q: (i, j, 0, 0)),
                  pl.BlockSpec((1, g, t, LANES), lambda i, j, q: (i, j, 0, 0)),
                  pl.BlockSpec((1, g) + stats.shape[2:], lambda i, j, q: (i, j, 0, 0))],
        out_specs=pl.BlockSpec((1, tq, g * HEAD_DIM), lambda i, j, q: (i, q, j)),
        out_shape=jax.ShapeDtypeStruct((b, t, FOX_WIDTH), BF16),
        scratch_shapes=[pltpu.VMEM((g, tq, tq), F32), pltpu.VMEM((g, tq, tq), F32),
                        pltpu.VMEM((g, 1, tq), F32), pltpu.VMEM((g, _FOX_V_ROWS, tq), F32)],
        compiler_params=_params(("arbitrary", "arbitrary", "arbitrary")),
        name="fox_attn",
    )(fqa, fka, fva, stats)


def _alibi_slope(head):
    return jnp.exp2(jnp.full((1, 1), -8.0 / DIFF_HEADS, F32) * (head + 1).astype(F32))


def _lambda(lam_ref, lam_init):
    lv = lam_ref[...]
    e1 = jnp.exp(jnp.sum(lv[0:1] * lv[1:2], axis=1, keepdims=True))
    e2 = jnp.exp(jnp.sum(lv[2:3] * lv[3:4], axis=1, keepdims=True))
    return e1 - e2 + lam_init


def _diff_normed(o0, o1, lam, axis):
    df = o0 - lam * o1
    return df * lax.rsqrt(jnp.mean(df * df, axis=axis, keepdims=True) + RMS_EPS)


_ONES_ROWS = 16


def _diff_kernel(q_ref, k_ref, v_ref, stat_ref, lam_ref, g_ref, o_ref, sa_ref, sb_ref, m_ref, acc_ref, *, tq,
                 lam_init, first_head, group):
    qi = pl.program_id(2)
    half = tq // 2
    chains = 2 * group
    slopes = [_alibi_slope(first_head + pl.program_id(1) * group + h) for h in range(group)]

    def scores(ki, dst, c, diagonal=False):
        start = pl.multiple_of(ki * tq, tq)
        if diagonal:
            for keys, q0 in _diagonal_blocks(tq):
                k = k_ref[0, c, pl.ds(start, keys), :]
                dst[c, 0:keys, q0:q0 + half] = lax.dot_general(k, q_ref[0, c, q0:q0 + half, :], _NT,
                                                               preferred_element_type=F32)
        else:
            k = k_ref[0, c, pl.ds(start, tq), :]
            dst[c] = lax.dot_general(k, q_ref[0, c], _NT, preferred_element_type=F32)

    def values_t(c, start, keys):
        v = v_ref[0, c // 2, pl.ds(start, keys), :]
        return jnp.concatenate([v.T, jnp.ones((_ONES_ROWS, keys), BF16)], axis=0)

    def absorb(src, ki, c, diagonal):
        start = pl.multiple_of(ki * tq, tq)
        if diagonal:
            for keys, q0 in _diagonal_blocks(tq):
                s = src[c, 0:keys, q0:q0 + half]
                key = lax.broadcasted_iota(jnp.int32, s.shape, 0)
                qry = lax.broadcasted_iota(jnp.int32, s.shape, 1) + q0
                ahead = jnp.maximum(key - qry, 0).astype(F32)
                s = jnp.where((key // CHUNK) <= (qry // CHUNK), s - (2.0 * LOG2E * slopes[c // 2]) * ahead, NEG_INF)
                m_new = jnp.max(s, axis=0, keepdims=True)
                p = jnp.exp2(s - m_new)
                acc_ref[c, :, q0:q0 + half] = jnp.dot(values_t(c, start, keys), p.astype(BF16),
                                                      preferred_element_type=F32)
                m_ref[c, :, q0:q0 + half] = m_new
            return
        s = src[c]
        m_prev = m_ref[c]
        m_new = jnp.maximum(m_prev, jnp.max(s, axis=0, keepdims=True))
        p = jnp.exp2(s - m_new)
        alpha = jnp.exp2(m_prev - m_new)
        acc_ref[c] = alpha * acc_ref[c] + jnp.dot(values_t(c, start, tq), p.astype(BF16),
                                                  preferred_element_type=F32)
        m_ref[c] = m_new

    def first_needed():
        bounds = []
        for c in range(chains):
            stat = stat_ref[0, c]
            tile = lax.broadcasted_iota(jnp.int32, (1, stat.shape[-1]), 1)
            gap = ((qi - 1 - tile) * tq + 1).astype(F32)
            bounds.append(_dot_bound(stat, _STAT_DIFF_Q, _STAT_DIFF_K, qi) - (LOG2E * slopes[c // 2]) * gap)
        return _first_needed_tile(qi, bounds, m_ref)

    _pipelined_tiles(qi, chains, scores, absorb, first_needed, sa_ref, sb_ref)
    lam = _lambda(lam_ref, lam_init)
    for h in range(group):
        o0, o1 = (acc_ref[2 * h + c, :DIFF_VDIM] / acc_ref[2 * h + c, DIFF_VDIM:DIFF_VDIM + 1] for c in range(2))
        normed = _diff_normed(o0, o1, lam, 0).T * g_ref[...] * (1.0 - lam_init)
        o_ref[0, :, h * DIFF_VDIM:(h + 1) * DIFF_VDIM] = normed.astype(BF16)


_DIFF_GROUPS = ((0, 1, 2), (2, 2, 1))


def _diff_attention(dqa, dka, dvb, stats, lam_vecs, subln_g, lam_init):
    b, _, t, _ = dqa.shape
    tq = min(t, _TILE)
    outs = []
    for first_head, g, n_groups in _DIFF_GROUPS:
        off = first_head // g

        def heads(block, off=off):
            return pl.BlockSpec(block, lambda i, j, q: (i, j + off, 0, 0))

        outs.append(pl.pallas_call(
            functools.partial(_diff_kernel, tq=tq, lam_init=lam_init, first_head=first_head, group=g),
            grid=(b, n_groups, t // tq),
            in_specs=[pl.BlockSpec((1, 2 * g, tq, LANES), lambda i, j, q, off=off: (i, j + off, q, 0)),
                      heads((1, 2 * g, t, LANES)), heads((1, g, t, DIFF_VDIM)),
                      heads((1, 2 * g) + stats.shape[2:]),
                      pl.BlockSpec(lam_vecs.shape, lambda i, j, q: (0, 0)),
                      pl.BlockSpec(subln_g.shape, lambda i, j, q: (0, 0))],
            out_specs=pl.BlockSpec((1, tq, g * DIFF_VDIM), lambda i, j, q: (i, q, j)),
            out_shape=jax.ShapeDtypeStruct((b, t, n_groups * g * DIFF_VDIM), BF16),
            scratch_shapes=[pltpu.VMEM((2 * g, tq, tq), F32), pltpu.VMEM((2 * g, tq, tq), F32),
                            pltpu.VMEM((2 * g, 1, tq), F32),
                            pltpu.VMEM((2 * g, DIFF_VDIM + _ONES_ROWS, tq), F32)],
            compiler_params=_params(("arbitrary", "arbitrary", "arbitrary")),
            name="diff_attn",
        )(dqa, dka, dvb, stats, lam_vecs, subln_g))
    return tuple(outs)


def _prefix_sum_lanes(x):
    n = x.shape[-1]
    lane = lax.broadcasted_iota(jnp.int32, x.shape, x.ndim - 1)
    sh = 1
    while sh < n:
        x = x + jnp.where(lane >= sh, pltpu.roll(x, sh, x.ndim - 1), 0.0)
        sh *= 2
    return x


def _fox_sample_kernel(q_ref, k_ref, v_ref, ck_ref, cv_ref, clf_ref, o_ref, *, t, past):
    r = lax.broadcasted_iota(jnp.int32, (t, t), 0)
    c = lax.broadcasted_iota(jnp.int32, (t, t), 1)
    cum_c = _prefix_sum_lanes(clf_ref[0])
    suffix = (cum_c[:, past - 1:past] - cum_c) * LOG2E
    for hh in range(FOX_HEADS):
        data = slice(0, HEAD_DIM) if hh % 2 == 0 else slice(HEAD_DIM, LANES)
        ext = HEAD_DIM if hh % 2 == 0 else 0
        qa = q_ref[0, hh]
        qf = qa.astype(F32)
        cum_q = sum(qf[:, ext + 8 * p + hh:ext + 8 * p + hh + 1] for p in range(3))
        s_c = jnp.dot(qa[:, data], ck_ref[0, 0, hh].astype(BF16), preferred_element_type=F32)
        s_c = s_c + cum_q + suffix[hh:hh + 1]
        s_n = lax.dot_general(qa, k_ref[0, hh], _NT, preferred_element_type=F32)
        s_n = jnp.where(c <= r, s_n, NEG_INF)
        m = jnp.maximum(jnp.max(s_c, axis=1, keepdims=True), jnp.max(s_n, axis=1, keepdims=True))
        p_c = jnp.exp2(s_c - m)
        p_n = jnp.exp2(s_n - m)
        denom = jnp.sum(p_c, axis=1, keepdims=True) + jnp.sum(p_n, axis=1, keepdims=True)
        o = lax.dot_general(p_c.astype(BF16), cv_ref[0, 0, hh].astype(BF16), _NT, preferred_element_type=F32)
        o = o + jnp.dot(p_n.astype(BF16), v_ref[0, hh][:, 0:HEAD_DIM], preferred_element_type=F32)
        o_ref[0, :, hh * HEAD_DIM:(hh + 1) * HEAD_DIM] = (o / denom).astype(BF16)


def _fox_sample(fqa, fka, fva, cache_k, cache_v, cache_logf):
    b, nh, t, _ = fqa.shape
    past = cache_k.shape[3]
    cache_k, cache_v = jnp.swapaxes(cache_k, 3, 4), jnp.swapaxes(cache_v, 3, 4)
    new = pl.BlockSpec((1, nh, t, LANES), lambda i: (i, 0, 0, 0))
    cache = pl.BlockSpec((1, 1, nh, HEAD_DIM, past), lambda i: (0, i, 0, 0, 0))
    return pl.pallas_call(
        functools.partial(_fox_sample_kernel, t=t, past=past),
        grid=(b,),
        in_specs=[new, new, new, cache, cache, pl.BlockSpec((1, nh, past), lambda i: (i, 0, 0))],
        out_specs=pl.BlockSpec((1, t, FOX_WIDTH), lambda i: (i, 0, 0)),
        out_shape=jax.ShapeDtypeStruct((b, t, FOX_WIDTH), BF16),
        compiler_params=_params(("arbitrary",)),
        name="fox_sample",
    )(fqa, fka, fva, cache_k, cache_v, cache_logf)


def _diff_sample_kernel(q_ref, k_ref, v_ref, ck_ref, cv_ref, lam_ref, g_ref, o_ref, *, t, past, lam_init):
    lane = lax.broadcasted_iota(jnp.int32, (t, LANES), 1)
    pos_q = past + lax.broadcasted_iota(jnp.int32, (t, past), 0)
    pos_c = lax.broadcasted_iota(jnp.int32, (t, past), 1)
    dist_c = jnp.abs(pos_q - pos_c).astype(F32)
    vis_c = (pos_c // CHUNK) <= (pos_q // CHUNK)
    pq_n = past + lax.broadcasted_iota(jnp.int32, (t, t), 0)
    pk_n = past + lax.broadcasted_iota(jnp.int32, (t, t), 1)
    dist_n = jnp.abs(pq_n - pk_n).astype(F32)
    vis_n = (pk_n // CHUNK) <= (pq_n // CHUNK)
    lam = _lambda(lam_ref, lam_init)
    for hd in range(DIFF_HEADS):
        slope = LOG2E * 2.0 ** (-8.0 * (hd + 1) / DIFF_HEADS)
        ck = ck_ref[0, 0, hd].astype(BF16)
        cv = cv_ref[0, 0, hd].astype(BF16)
        v = v_ref[0, hd]
        outs = []
        for c in range(2):
            half = (lane < HEAD_DIM) if c == 0 else (lane >= HEAD_DIM)
            qz = jnp.where(half, q_ref[0, 2 * hd + c], jnp.zeros((), BF16))
            kz = jnp.where(half, k_ref[0, 2 * hd + c], jnp.zeros((), BF16))
            s_c = jnp.dot(qz, ck, preferred_element_type=F32)
            s_c = jnp.where(vis_c, s_c - slope * dist_c, NEG_INF)
            s_n = lax.dot_general(qz, kz, _NT, preferred_element_type=F32)
            s_n = jnp.where(vis_n, s_n - slope * dist_n, NEG_INF)
            m = jnp.maximum(jnp.max(s_c, axis=1, keepdims=True), jnp.max(s_n, axis=1, keepdims=True))
            p_c = jnp.exp2(s_c - m)
            p_n = jnp.exp2(s_n - m)
            denom = jnp.sum(p_c, axis=1, keepdims=True) + jnp.sum(p_n, axis=1, keepdims=True)
            o = jnp.dot(p_c.astype(BF16), cv, preferred_element_type=F32)
            o = o + jnp.dot(p_n.astype(BF16), v, preferred_element_type=F32)
            outs.append(o / denom)
        normed = _diff_normed(outs[0], outs[1], lam, 1) * g_ref[...] * (1.0 - lam_init)
        o_ref[0, :, hd * DIFF_VDIM:(hd + 1) * DIFF_VDIM] = normed.astype(BF16)


def _diff_sample(dqa, dka, dvb, cache_k, cache_v, lam_vecs, subln_g, lam_init):
    b, nm, t, _ = dqa.shape
    past = cache_k.shape[3]
    ck = jnp.transpose(cache_k, (0, 1, 2, 4, 5, 3)).reshape(cache_k.shape[:3] + (DIFF_VDIM, past))
    new = pl.BlockSpec((1, nm, t, LANES), lambda i: (i, 0, 0, 0))
    return pl.pallas_call(
        functools.partial(_diff_sample_kernel, t=t, past=past, lam_init=lam_init),
        grid=(b,),
        in_specs=[new, new, pl.BlockSpec((1, DIFF_HEADS, t, DIFF_VDIM), lambda i: (i, 0, 0, 0)),
                  pl.BlockSpec((1, 1, DIFF_HEADS, DIFF_VDIM, past), lambda i: (0, i, 0, 0, 0)),
                  pl.BlockSpec((1, 1, DIFF_HEADS, past, DIFF_VDIM), lambda i: (0, i, 0, 0, 0)),
                  pl.BlockSpec(lam_vecs.shape, lambda i: (0, 0)),
                  pl.BlockSpec(subln_g.shape, lambda i: (0, 0))],
        out_specs=pl.BlockSpec((1, t, DIFF_WIDTH), lambda i: (i, 0, 0)),
        out_shape=jax.ShapeDtypeStruct((b, t, DIFF_WIDTH), BF16),
        compiler_params=_params(("arbitrary",)),
        name="diff_sample",
    )(dqa, dka, dvb, ck, cache_v, lam_vecs, subln_g)


def _out_proj_kernel(fo_ref, *refs, alpha, n_diff):
    do_refs, (fg_ref, dg_ref, x_ref, mod_ref, w_ref, g_ref, b_ref, y_ref) = refs[:n_diff], refs[n_diff:]
    fox = fo_ref[0] * fg_ref[0]
    diff = jnp.concatenate([r[0] for r in do_refs], axis=1) * dg_ref[0]
    branch = jnp.dot(fox, w_ref[0:FOX_WIDTH, :], preferred_element_type=F32)
    branch = branch + jnp.dot(diff, w_ref[FOX_WIDTH:, :], preferred_element_type=F32)
    r = alpha * x_ref[0] + mod_ref[0, 2:3, :] * branch
    mu = jnp.mean(r, axis=-1, keepdims=True)
    rc = r - mu
    var = jnp.mean(rc * rc, axis=-1, keepdims=True)
    y_ref[0] = rc * lax.rsqrt(var + LN_EPS) * g_ref[...] + b_ref[...]


def _out_proj(fox_o, diff_parts, fg, dg, x, mod, w_out, ln_g, ln_b, alpha):
    b, t, d = x.shape
    tm = 1024 if t % 1024 == 0 else min(t, _TILE)
    rows = pl.BlockSpec((1, tm, FOX_WIDTH), lambda i, j: (i, j, 0))
    wide = pl.BlockSpec((1, tm, d), lambda i, j: (i, j, 0))

    def full(a):
        return pl.BlockSpec(a.shape, lambda i, j: (0,) * a.ndim)

    return pl.pallas_call(
        functools.partial(_out_proj_kernel, alpha=alpha, n_diff=len(diff_parts)),
        grid=(b, t // tm),
        in_specs=([rows] + [pl.BlockSpec((1, tm, p.shape[-1]), lambda i, j: (i, j, 0)) for p in diff_parts]
                  + [rows, rows, wide, pl.BlockSpec((1, 3, d), lambda i, j: (i, 0, 0)),
                     full(w_out), full(ln_g), full(ln_b)]),
        out_specs=wide,
        out_shape=jax.ShapeDtypeStruct((b, t, d), F32),
        compiler_params=_params(("arbitrary", "arbitrary")),
        name="out_proj",
    )(fox_o, *diff_parts, fg, dg, x, mod, w_out, ln_g, ln_b)


def _pack_w_in(w_in):
    fw = FOX_WIDTH
    o = 0
    fq = w_in[:, o:o + fw]; o += fw
    fk = w_in[:, o:o + fw]; o += fw
    fv = w_in[:, o:o + fw]; o += fw
    ff = w_in[:, o:o + FOX_HEADS]; o += FOX_HEADS
    fg = w_in[:, o:o + fw]; o += fw
    dq = w_in[:, o:o + fw]; o += fw
    dk = w_in[:, o:o + fw]; o += fw
    dv = w_in[:, o:o + fw]; o += fw
    dg = w_in[:, o:o + fw]
    qs = HEAD_DIM ** -0.5 * LOG2E
    return jnp.concatenate([fq * qs, fk, fv, fg, dq * qs, dk, dv, dg,
                            jnp.tile(ff, (1, LANES // FOX_HEADS))], axis=1).astype(BF16)


def kernel(x_prompt, x_sample, cache_fox_k, cache_fox_v, cache_fox_logf, cache_diff_k, cache_diff_v,
           c_prompt, c_sample, w_ada, b_ada, w_in, b_f, lambda_q1, lambda_k1, lambda_q2, lambda_k2,
           subln_g, w_out, ln_g, ln_b):
    depth = w_in.shape[0]
    assert depth == 1, "single-layer step"
    layer = 0
    lam_init = 0.8 - 0.6 * math.exp(-0.3 * layer)
    alpha = (2 * depth) ** 0.25
    bp, bs = c_prompt.shape[0], c_sample.shape[0]
    d = x_prompt.shape[-1]
    past = cache_fox_k.shape[3]

    c_all = jnp.concatenate([c_prompt, c_sample, jnp.zeros((-(bp + bs) % 8, d), F32)], axis=0)
    mod = _ada(c_all, w_ada[layer], b_ada[layer]).reshape(c_all.shape[0], 3, d)
    mod_p, mod_s = mod[:bp], mod[bp:bp + bs]

    w = _pack_w_in(w_in[layer])
    bf16x = jnp.tile(b_f[layer], LANES // FOX_HEADS).reshape(1, LANES)
    seg = np.repeat(np.eye(FOX_WIDTH // HEAD_DIM, LANES, dtype=np.float32), HEAD_DIM, axis=0)
    seg = jnp.asarray(seg, BF16)
    lam_vecs = jnp.stack([lambda_q1[layer], lambda_k1[layer], lambda_q2[layer], lambda_k2[layer]])
    g = subln_g[layer].reshape(1, DIFF_VDIM)
    w_o = w_out[layer].astype(BF16)
    lg, lb = ln_g[layer].reshape(1, d), ln_b[layer].reshape(1, d)

    def run(x, mod_x, pos_offset, attend):
        (fk, fv, dk, dv, logf, fqa, fka, fva, fg, dqa, dka, dvb, dg, stats) = _in_proj(
            x, mod_x, w, bf16x, seg, pos_offset)
        stats = jnp.transpose(stats[..., :FOX_WIDTH // HEAD_DIM], (0, 3, 2, 1))
        fox_o, diff_o = attend(fqa, fka, fva, dqa, dka, dvb, stats)
        y = _out_proj(fox_o, diff_o, fg, dg, x, mod_x, w_o, lg, lb, alpha)
        b, t = x.shape[:2]
        states = (fk[None], fv[None], logf[None],
                  dk.reshape(1, b, DIFF_HEADS, t, 2, HEAD_DIM), dv[None])
        return y, states

    def prompt_attend(fqa, fka, fva, dqa, dka, dvb, stats):
        return (_fox_attention(fqa, fka, fva, stats),
                _diff_attention(dqa, dka, dvb, stats, lam_vecs, g, lam_init))

    def sample_attend(fqa, fka, fva, dqa, dka, dvb, stats):
        del stats
        return (_fox_sample(fqa, fka, fva, cache_fox_k, cache_fox_v, cache_fox_logf[layer]),
                (_diff_sample(dqa, dka, dvb, cache_diff_k, cache_diff_v, lam_vecs, g, lam_init),))

    yp, sp = run(x_prompt, mod_p, 0, prompt_attend)
    ys, ss = run(x_sample, mod_s, past, sample_attend)
    return (yp, ys) + sp + ss
```

```python
import functools
import math

import jax
import jax.numpy as jnp
import numpy as np
from jax import lax
from jax.experimental import pallas as pl
from jax.experimental.pallas import tpu as pltpu

F32 = jnp.float32
BF16 = jnp.bfloat16

HEAD_DIM = 64
FOX_HEADS = 8
DIFF_HEADS = 4
DIFF_VDIM = 2 * HEAD_DIM
FOX_WIDTH = FOX_HEADS * HEAD_DIM
DIFF_WIDTH = DIFF_HEADS * DIFF_VDIM
CHUNK = 64
LN_EPS = 1e-5
RMS_EPS = 1e-5
LANES = 128
NEG_INF = float("-inf")
LOG2E = math.log2(math.e)

_SEC = {name: i * FOX_WIDTH for i, name in enumerate(("fq", "fk", "fv", "fg", "dq", "dk", "dv", "dg"))}
_FF_OFF = 8 * FOX_WIDTH

_VMEM_LIMIT = 56 * 1024 * 1024
_TILE = 512


def _params(sem):
    return pltpu.CompilerParams(dimension_semantics=sem, vmem_limit_bytes=_VMEM_LIMIT)


def _ada_kernel(c_ref, w_ref, b_ref, o_ref):
    c = c_ref[...]
    s = c * jax.nn.sigmoid(c)
    o_ref[...] = jnp.dot(s, w_ref[...], preferred_element_type=F32) + b_ref[...]


def _ada(c_all, w_ada, b_ada):
    rows, d = c_all.shape
    n = w_ada.shape[1]
    tn = 512
    return pl.pallas_call(
        _ada_kernel,
        grid=(n // tn,),
        in_specs=[pl.BlockSpec((rows, d), lambda j: (0, 0)),
                  pl.BlockSpec((d, tn), lambda j: (0, j)),
                  pl.BlockSpec((1, tn), lambda j: (0, j))],
        out_specs=pl.BlockSpec((rows, tn), lambda j: (0, j)),
        out_shape=jax.ShapeDtypeStruct((rows, n), F32),
        compiler_params=_params(("arbitrary",)),
        name="ada",
    )(c_all, w_ada, b_ada.reshape(1, n))


def _split3(x):
    hi = x.astype(BF16)
    r1 = x - hi.astype(F32)
    mid = r1.astype(BF16)
    lo = (r1 - mid.astype(F32)).astype(BF16)
    return hi, mid, lo


def _in_proj_kernel(x_ref, mod_ref, w_ref, bf_ref, seg_ref,
                    fk_ref, fv_ref, dk_ref, dv_ref, logf_ref,
                    fqa_ref, fka_ref, fva_ref, fg_ref, dqa_ref, dka_ref, dvb_ref, dg_ref, stat_ref,
                    carry_ref, *, tm, pos_offset):
    ti = pl.program_id(1)
    x = x_ref[0]
    shift = mod_ref[0, 0:1, :]
    scale = mod_ref[0, 1:2, :]
    h = (x * (1.0 + scale) + shift).astype(BF16)

    def proj(name, width=FOX_WIDTH):
        off = _SEC[name] if name in _SEC else _FF_OFF
        return jnp.dot(h, w_ref[:, off:off + width], preferred_element_type=F32)

    lane = lax.broadcasted_iota(jnp.int32, (tm, LANES), 1)
    row = lax.broadcasted_iota(jnp.int32, (tm, LANES), 0)
    lower = lane < HEAD_DIM
    grp = (lane // 8) % 4

    def by_group(a, b, c):
        return jnp.where(grp == 0, a, jnp.where(grp == 1, b, jnp.where(grp == 2, c, jnp.zeros_like(a))))

    def max_sq_norm(z):
        zr = z.astype(BF16).astype(F32)
        sq = zr * zr
        if tm % 32 == 0:
            sq = jnp.maximum(sq[0:tm // 2], sq[tm // 2:tm])
        sq = jnp.dot(sq.astype(BF16), seg_ref[...], preferred_element_type=F32)
        return jnp.max(sq, axis=0, keepdims=True)

    logf = jax.nn.log_sigmoid(proj("ff", LANES) + bf_ref[...])
    logf_ref[0] = logf.T[0:FOX_HEADS] if tm % LANES == 0 else logf

    dq, dk, dv = proj("dq"), proj("dk"), proj("dv")
    pos = (row + (ti * tm + pos_offset)).astype(F32)
    sub = lane % HEAD_DIM
    for hd in range(DIFF_HEADS):
        slope = 2.0 ** (-8.0 * (hd + 1) / DIFF_HEADS)
        sl = slice(hd * LANES, (hd + 1) * LANES)
        p0, p1, p2 = (a.astype(F32) for a in _split3((slope * LOG2E) * pos))
        alibi = jnp.where(sub % 3 == 0, p0, jnp.where(sub % 3 == 1, p1, p2))
        alibi_q = jnp.where(sub < 3, -alibi, (sub < 6).astype(F32))
        alibi_k = jnp.where(sub < 3, 1.0, jnp.where(sub < 6, alibi, 0.0))
        dqa_ref[0, 2 * hd] = jnp.where(lower, dq[:, sl], alibi_q).astype(BF16)
        dqa_ref[0, 2 * hd + 1] = jnp.where(lower, alibi_q, dq[:, sl]).astype(BF16)
        dka_ref[0, 2 * hd] = jnp.where(lower, dk[:, sl], alibi_k).astype(BF16)
        dka_ref[0, 2 * hd + 1] = jnp.where(lower, alibi_k, dk[:, sl]).astype(BF16)
        if tm % LANES:
            dk_ref[0, hd] = dk[:, sl]
        dv_ref[0, hd] = dv[:, sl]
        dvb_ref[0, hd] = dv[:, sl].astype(BF16)
    if tm % LANES == 0:
        dk_ref[0] = dk.T.reshape(dk_ref.shape[1:])
    dg_ref[0] = jax.nn.silu(proj("dg")).astype(BF16)
    diff_norms = [max_sq_norm(dq), max_sq_norm(dk)]

    tri_r = lax.broadcasted_iota(jnp.int32, (tm, tm), 0)
    tri_c = lax.broadcasted_iota(jnp.int32, (tm, tm), 1)
    tri = (tri_c <= tri_r).astype(BF16)
    part = jnp.dot(tri, by_group(*_split3(logf)), preferred_element_type=F32)
    zq, zk, zv = proj("fq"), proj("fk"), proj("fv")
    local = part + pltpu.roll(part, 8, 1) + pltpu.roll(part, 16, 1) + pltpu.roll(part, 24, 1)

    @pl.when(ti == 0)
    def _():
        carry_ref[...] = jnp.zeros_like(carry_ref)

    cum = local + carry_ref[...]
    carry_ref[...] = cum[tm - 1:tm, :]
    cum2 = cum * LOG2E
    pieces = by_group(*(a.astype(F32) for a in _split3(cum2)))
    behind = pltpu.roll(pieces, 24, 1)
    u = lane % HEAD_DIM
    ext_q = jnp.where(u < 24, pieces, (u < 48).astype(F32))

    def ext_k(h):
        mine = u % 8 == h
        return jnp.where(mine & (u < 24), 1.0, jnp.where(mine & (u < 48), -behind, 0.0))

    fox_norms = [max_sq_norm(zq), max_sq_norm(zk)]
    e_even = (lane == HEAD_DIM).astype(F32)
    for j in range(FOX_HEADS // 2):
        sl = slice(j * LANES, (j + 1) * LANES)
        fqa_ref[0, 2 * j] = jnp.where(lower, zq[:, sl], ext_q).astype(BF16)
        fqa_ref[0, 2 * j + 1] = jnp.where(lower, ext_q, zq[:, sl]).astype(BF16)
        fka_ref[0, 2 * j] = jnp.where(lower, zk[:, sl], ext_k(2 * j)).astype(BF16)
        fka_ref[0, 2 * j + 1] = jnp.where(lower, ext_k(2 * j + 1), zk[:, sl]).astype(BF16)
        fva_ref[0, 2 * j] = jnp.where(lower, zv[:, sl], e_even).astype(BF16)
        fva_ref[0, 2 * j + 1] = jnp.where(lower, pltpu.roll(zv[:, sl], HEAD_DIM, 1), e_even).astype(BF16)
    if tm % LANES == 0:
        fk_ref[0] = zk.T.reshape(fk_ref.shape[1:])
        fv_ref[0] = zv.T.reshape(fv_ref.shape[1:])
    else:
        for hd in range(FOX_HEADS):
            fk_ref[0, hd] = zk[:, hd * HEAD_DIM:(hd + 1) * HEAD_DIM]
            fv_ref[0, hd] = zv[:, hd * HEAD_DIM:(hd + 1) * HEAD_DIM]
    fg_ref[0] = jax.nn.silu(proj("fg")).astype(BF16)
    stat_ref[0, 0] = jnp.concatenate(
        fox_norms + diff_norms + [cum2[0:1], cum2[tm - 1:tm], jnp.zeros((2, LANES), F32)],
        axis=0)


def _in_proj(x, mod, w, bf16x, seg, pos_offset):
    b, t, d = x.shape
    tm = min(t, _TILE)
    nt = t // tm
    head_major = tm % LANES == 0

    def full(a):
        return pl.BlockSpec(a.shape, lambda i, j: (0,) * a.ndim)

    def heads(n, width):
        return pl.BlockSpec((1, n, tm, width), lambda i, j: (i, 0, j, 0))

    def cache_shape(n, width):
        return jax.ShapeDtypeStruct((b, n, width, t) if head_major else (b, n, t, width), F32)

    def cache_spec(n, width):
        return pl.BlockSpec((1, n, width, tm), lambda i, j: (i, 0, 0, j)) if head_major else heads(n, width)

    rows = pl.BlockSpec((1, tm, FOX_WIDTH), lambda i, j: (i, j, 0))
    out_shape = (
        cache_shape(FOX_HEADS, HEAD_DIM),
        cache_shape(FOX_HEADS, HEAD_DIM),
        cache_shape(DIFF_HEADS, DIFF_VDIM),
        jax.ShapeDtypeStruct((b, DIFF_HEADS, t, DIFF_VDIM), F32),
        jax.ShapeDtypeStruct((b, FOX_HEADS, t) if head_major else (b, t, LANES), F32),
        jax.ShapeDtypeStruct((b, FOX_HEADS, t, LANES), BF16),
        jax.ShapeDtypeStruct((b, FOX_HEADS, t, LANES), BF16),
        jax.ShapeDtypeStruct((b, FOX_HEADS, t, LANES), BF16),
        jax.ShapeDtypeStruct((b, t, FOX_WIDTH), BF16),
        jax.ShapeDtypeStruct((b, 2 * DIFF_HEADS, t, LANES), BF16),
        jax.ShapeDtypeStruct((b, 2 * DIFF_HEADS, t, LANES), BF16),
        jax.ShapeDtypeStruct((b, DIFF_HEADS, t, DIFF_VDIM), BF16),
        jax.ShapeDtypeStruct((b, t, DIFF_WIDTH), BF16),
        jax.ShapeDtypeStruct((b, nt, 8, LANES), F32),
    )
    out_specs = (
        cache_spec(FOX_HEADS, HEAD_DIM), cache_spec(FOX_HEADS, HEAD_DIM),
        cache_spec(DIFF_HEADS, DIFF_VDIM), heads(DIFF_HEADS, DIFF_VDIM),
        (pl.BlockSpec((1, FOX_HEADS, tm), lambda i, j: (i, 0, j)) if head_major
         else pl.BlockSpec((1, tm, LANES), lambda i, j: (i, j, 0))),
        heads(FOX_HEADS, LANES), heads(FOX_HEADS, LANES), heads(FOX_HEADS, LANES), rows,
        heads(2 * DIFF_HEADS, LANES), heads(2 * DIFF_HEADS, LANES), heads(DIFF_HEADS, DIFF_VDIM), rows,
        pl.BlockSpec((1, 1, 8, LANES), lambda i, j: (i, j, 0, 0)),
    )
    outs = pl.pallas_call(
        functools.partial(_in_proj_kernel, tm=tm, pos_offset=pos_offset),
        grid=(b, nt),
        in_specs=[pl.BlockSpec((1, tm, d), lambda i, j: (i, j, 0)),
                  pl.BlockSpec((1, 3, d), lambda i, j: (i, 0, 0)),
                  full(w), full(bf16x), full(seg)],
        out_specs=out_specs,
        out_shape=out_shape,
        scratch_shapes=[pltpu.VMEM((1, LANES), F32)],
        compiler_params=_params(("arbitrary", "arbitrary")),
        name="in_proj",
    )(x, mod, w, bf16x, seg)
    if head_major:
        outs = tuple(jnp.swapaxes(o, 2, 3) for o in outs[:3]) + outs[3:]
    else:
        outs = outs[:4] + (jnp.swapaxes(outs[4][:, :, :FOX_HEADS], 1, 2),) + outs[5:]
    return outs


_NT = (((1,), (1,)), ((), ()))
_TN = (((0,), (0,)), ((), ()))
_STAGES = 2
_FOX_V_ROWS = HEAD_DIM + 16
_FOX_GROUP = 4


_SKIP_MARGIN = 136.0
_STAT_FOX_Q, _STAT_FOX_K, _STAT_DIFF_Q, _STAT_DIFF_K, _STAT_CUM_FIRST, _STAT_CUM_LAST = range(6)


def _diagonal_blocks(tq):
    assert tq % (2 * LANES) == 0
    return ((tq // 2, 0), (tq, tq // 2))


def _dot_bound(stat, q_row, k_row, qi):
    tile = lax.broadcasted_iota(jnp.int32, (1, stat.shape[-1]), 1)
    qn2 = jnp.sum(jnp.where(tile == qi, stat[q_row:q_row + 1], 0.0))
    return jnp.sqrt(qn2 * stat[k_row:k_row + 1]) * 1.01 + 1.0


def _first_needed_tile(qi, bounds, m_ref):
    first = qi
    for c, bound in enumerate(bounds):
        tile = lax.broadcasted_iota(jnp.int32, bound.shape, 1)
        needed = (bound >= jnp.min(m_ref[c]) - _SKIP_MARGIN) & (tile < qi)
        first = jnp.minimum(first, jnp.min(jnp.where(needed, tile, qi)))
    return first


def _pipelined_tiles(qi, n_chains, scores, absorb, first_needed, sa_ref, sb_ref):
    chains = range(n_chains)
    bufs = (sa_ref, sb_ref)
    for c in chains:
        scores(qi, sa_ref, c, True)
        scores(jnp.maximum(qi - 1, 0), sb_ref, c)
        absorb(sa_ref, qi, c, True)
    n = qi - first_needed()

    def stages(top, count, more):
        for s in range(count):
            for c in chains:
                if more or s + 1 < count:
                    scores(jnp.maximum(top - s - 1, 0), bufs[s % 2], c)
                absorb(bufs[(s + 1) % 2], top - s, c, False)

    def body(i, carry):
        stages(qi - 1 - _STAGES * i, _STAGES, True)
        return carry

    trips = jnp.maximum(n - 1, 0) // _STAGES
    lax.fori_loop(0, trips, body, 0)
    rem = n - _STAGES * trips
    for count in range(1, _STAGES + 1):
        @pl.when(rem == count)
        def _(count=count):
            stages(qi - 1 - _STAGES * trips, count, False)


def _fox_kernel(q_ref, k_ref, v_ref, stat_ref, o_ref, sa_ref, sb_ref, m_ref, acc_ref, *, tq):
    qi = pl.program_id(2)
    half = tq // 2

    def scores(ki, dst, hh, diagonal=False):
        start = pl.multiple_of(ki * tq, tq)
        if diagonal:
            for keys, q0 in _diagonal_blocks(tq):
                k = k_ref[0, hh, pl.ds(start, keys), :]
                dst[hh, 0:keys, q0:q0 + half] = lax.dot_general(k, q_ref[0, hh, q0:q0 + half, :], _NT,
                                                                preferred_element_type=F32)
        else:
            k = k_ref[0, hh, pl.ds(start, tq), :]
            dst[hh] = lax.dot_general(k, q_ref[0, hh], _NT, preferred_element_type=F32)

    def absorb(src, ki, hh, diagonal):
        start = pl.multiple_of(ki * tq, tq)
        if diagonal:
            for keys, q0 in _diagonal_blocks(tq):
                s = src[hh, 0:keys, q0:q0 + half]
                key = lax.broadcasted_iota(jnp.int32, s.shape, 0)
                qry = lax.broadcasted_iota(jnp.int32, s.shape, 1) + q0
                s = jnp.where(key <= qry, s, NEG_INF)
                m_new = jnp.max(s, axis=0, keepdims=True)
                p = jnp.exp2(s - m_new)
                v = v_ref[0, hh, pl.ds(start, keys), 0:_FOX_V_ROWS]
                acc_ref[hh, :, q0:q0 + half] = lax.dot_general(v, p.astype(BF16), _TN, preferred_element_type=F32)
                m_ref[hh, :, q0:q0 + half] = m_new
            return
        s = src[hh]
        m_prev = m_ref[hh]
        m_new = jnp.maximum(m_prev, jnp.max(s, axis=0, keepdims=True))
        p = jnp.exp2(s - m_new)
        alpha = jnp.exp2(m_prev - m_new)
        v = v_ref[0, hh, pl.ds(start, tq), 0:_FOX_V_ROWS]
        pv = lax.dot_general(v, p.astype(BF16), _TN, preferred_element_type=F32)
        acc_ref[hh] = alpha * acc_ref[hh] + pv
        m_ref[hh] = m_new

    def first_needed():
        bounds = []
        for hh in range(_FOX_GROUP):
            stat = stat_ref[0, hh]
            tile = lax.broadcasted_iota(jnp.int32, (1, stat.shape[-1]), 1)
            cum_q = jnp.sum(jnp.where(tile == qi, stat[_STAT_CUM_FIRST:_STAT_CUM_FIRST + 1], 0.0))
            decay = cum_q - stat[_STAT_CUM_LAST:_STAT_CUM_LAST + 1]
            bounds.append(_dot_bound(stat, _STAT_FOX_Q, _STAT_FOX_K, qi) + decay)
        return _first_needed_tile(qi, bounds, m_ref)

    _pipelined_tiles(qi, _FOX_GROUP, scores, absorb, first_needed, sa_ref, sb_ref)
    outs = [acc_ref[hh, 0:HEAD_DIM] / acc_ref[hh, HEAD_DIM:HEAD_DIM + 1] for hh in range(_FOX_GROUP)]
    o_ref[0] = jnp.concatenate(outs, axis=0).T.astype(BF16)


def _fox_attention(fqa, fka, fva, stats):
    b, nh, t, _ = fqa.shape
    tq = min(t, _TILE)
    g = _FOX_GROUP
    return pl.pallas_call(
        functools.partial(_fox_kernel, tq=tq),
        grid=(b, nh // g, t // tq),
        in_specs=[pl.BlockSpec((1, g, tq, LANES), lambda i, j, q: (i, j, q, 0)),
                  pl.BlockSpec((1, g, t, LANES), lambda i, j, q: (i, j, 0, 0)),
                  pl.BlockSpec((1, g, t, LANES), lambda i, j, q: (i, j, 0, 0)),
                  pl.BlockSpec((1, g) + stats.shape[2:], lambda i, j, q: (i, j, 0, 0))],
        out_specs=pl.BlockSpec((1, tq, g * HEAD_DIM), lambda i, j, q: (i, q, j)),
        out_shape=jax.ShapeDtypeStruct((b, t, FOX_WIDTH), BF16),
        scratch_shapes=[pltpu.VMEM((g, tq, tq), F32), pltpu.VMEM((g, tq, tq), F32),
                        pltpu.VMEM((g, 1, tq), F32), pltpu.VMEM((g, _FOX_V_ROWS, tq), F32)],
        compiler_params=_params(("arbitrary", "arbitrary", "arbitrary")),
        name="fox_attn",
    )(fqa, fka, fva, stats)


def _alibi_slope(head):
    return jnp.exp2(jnp.full((1, 1), -8.0 / DIFF_HEADS, F32) * (head + 1).astype(F32))


def _lambda(lam_ref, lam_init):
    lv = lam_ref[...]
    e1 = jnp.exp(jnp.sum(lv[0:1] * lv[1:2], axis=1, keepdims=True))
    e2 = jnp.exp(jnp.sum(lv[2:3] * lv[3:4], axis=1, keepdims=True))
    return e1 - e2 + lam_init


def _diff_normed(o0, o1, lam, axis):
    df = o0 - lam * o1
    return df * lax.rsqrt(jnp.mean(df * df, axis=axis, keepdims=True) + RMS_EPS)


_ONES_ROWS = 16


def _diff_kernel(q_ref, k_ref, v_ref, stat_ref, lam_ref, g_ref, o_ref, sa_ref, sb_ref, m_ref, acc_ref, *, tq,
                 lam_init, first_head, group):
    qi = pl.program_id(2)
    half = tq // 2
    chains = 2 * group
    slopes = [_alibi_slope(first_head + pl.program_id(1) * group + h) for h in range(group)]

    def scores(ki, dst, c, diagonal=False):
        start = pl.multiple_of(ki * tq, tq)
        if diagonal:
            for keys, q0 in _diagonal_blocks(tq):
                k = k_ref[0, c, pl.ds(start, keys), :]
                dst[c, 0:keys, q0:q0 + half] = lax.dot_general(k, q_ref[0, c, q0:q0 + half, :], _NT,
                                                               preferred_element_type=F32)
        else:
            k = k_ref[0, c, pl.ds(start, tq), :]
            dst[c] = lax.dot_general(k, q_ref[0, c], _NT, preferred_element_type=F32)

    def values_t(c, start, keys):
        v = v_ref[0, c // 2, pl.ds(start, keys), :]
        return jnp.concatenate([v.T, jnp.ones((_ONES_ROWS, keys), BF16)], axis=0)

    def absorb(src, ki, c, diagonal):
        start = pl.multiple_of(ki * tq, tq)
        if diagonal:
            for keys, q0 in _diagonal_blocks(tq):
                s = src[c, 0:keys, q0:q0 + half]
                key = lax.broadcasted_iota(jnp.int32, s.shape, 0)
                qry = lax.broadcasted_iota(jnp.int32, s.shape, 1) + q0
                ahead = jnp.maximum(key - qry, 0).astype(F32)
                s = jnp.where((key // CHUNK) <= (qry // CHUNK), s - (2.0 * LOG2E * slopes[c // 2]) * ahead, NEG_INF)
                m_new = jnp.max(s, axis=0, keepdims=True)
                p = jnp.exp2(s - m_new)
                acc_ref[c, :, q0:q0 + half] = jnp.dot(values_t(c, start, keys), p.astype(BF16),
                                                      preferred_element_type=F32)
                m_ref[c, :, q0:q0 + half] = m_new
            return
        s = src[c]
        m_prev = m_ref[c]
        m_new = jnp.maximum(m_prev, jnp.max(s, axis=0, keepdims=True))
        p = jnp.exp2(s - m_new)
        alpha = jnp.exp2(m_prev - m_new)
        acc_ref[c] = alpha * acc_ref[c] + jnp.dot(values_t(c, start, tq), p.astype(BF16),
                                                  preferred_element_type=F32)
        m_ref[c] = m_new

    def first_needed():
        bounds = []
        for c in range(chains):
            stat = stat_ref[0, c]
            tile = lax.broadcasted_iota(jnp.int32, (1, stat.shape[-1]), 1)
            gap = ((qi - 1 - tile) * tq + 1).astype(F32)
            bounds.append(_dot_bound(stat, _STAT_DIFF_Q, _STAT_DIFF_K, qi) - (LOG2E * slopes[c // 2]) * gap)
        return _first_needed_tile(qi, bounds, m_ref)

    _pipelined_tiles(qi, chains, scores, absorb, first_needed, sa_ref, sb_ref)
    lam = _lambda(lam_ref, lam_init)
    for h in range(group):
        o0, o1 = (acc_ref[2 * h + c, :DIFF_VDIM] / acc_ref[2 * h + c, DIFF_VDIM:DIFF_VDIM + 1] for c in range(2))
        normed = _diff_normed(o0, o1, lam, 0).T * g_ref[...] * (1.0 - lam_init)
        o_ref[0, :, h * DIFF_VDIM:(h + 1) * DIFF_VDIM] = normed.astype(BF16)


_DIFF_GROUPS = ((0, 1, 2), (2, 2, 1))


def _diff_attention(dqa, dka, dvb, stats, lam_vecs, subln_g, lam_init):
    b, _, t, _ = dqa.shape
    tq = min(t, _TILE)
    outs = []
    for first_head, g, n_groups in _DIFF_GROUPS:
        off = first_head // g

        def heads(block, off=off):
            return pl.BlockSpec(block, lambda i, j, q: (i, j + off, 0, 0))

        outs.append(pl.pallas_call(
            functools.partial(_diff_kernel, tq=tq, lam_init=lam_init, first_head=first_head, group=g),
            grid=(b, n_groups, t // tq),
            in_specs=[pl.BlockSpec((1, 2 * g, tq, LANES), lambda i, j, q, off=off: (i, j + off, q, 0)),
                      heads((1, 2 * g, t, LANES)), heads((1, g, t, DIFF_VDIM)),
                      heads((1, 2 * g) + stats.shape[2:]),
                      pl.BlockSpec(lam_vecs.shape, lambda i, j, q: (0, 0)),
                      pl.BlockSpec(subln_g.shape, lambda i, j, q: (0, 0))],
            out_specs=pl.BlockSpec((1, tq, g * DIFF_VDIM), lambda i, j, q: (i, q, j)),
            out_shape=jax.ShapeDtypeStruct((b, t, n_groups * g * DIFF_VDIM), BF16),
            scratch_shapes=[pltpu.VMEM((2 * g, tq, tq), F32), pltpu.VMEM((2 * g, tq, tq), F32),
                            pltpu.VMEM((2 * g, 1, tq), F32),
                            pltpu.VMEM((2 * g, DIFF_VDIM + _ONES_ROWS, tq), F32)],
            compiler_params=_params(("arbitrary", "arbitrary", "arbitrary")),
            name="diff_attn",
        )(dqa, dka, dvb, stats, lam_vecs, subln_g))
    return tuple(outs)


def _prefix_sum_lanes(x):
    n = x.shape[-1]
    lane = lax.broadcasted_iota(jnp.int32, x.shape, x.ndim - 1)
    sh = 1
    while sh < n:
        x = x + jnp.where(lane >= sh, pltpu.roll(x, sh, x.ndim - 1), 0.0)
        sh *= 2
    return x


def _fox_sample_kernel(q_ref, k_ref, v_ref, ck_ref, cv_ref, clf_ref, o_ref, *, t, past):
    r = lax.broadcasted_iota(jnp.int32, (t, t), 0)
    c = lax.broadcasted_iota(jnp.int32, (t, t), 1)
    cum_c = _prefix_sum_lanes(clf_ref[0])
    suffix = (cum_c[:, past - 1:past] - cum_c) * LOG2E
    for hh in range(FOX_HEADS):
        data = slice(0, HEAD_DIM) if hh % 2 == 0 else slice(HEAD_DIM, LANES)
        ext = HEAD_DIM if hh % 2 == 0 else 0
        qa = q_ref[0, hh]
        qf = qa.astype(F32)
        cum_q = sum(qf[:, ext + 8 * p + hh:ext + 8 * p + hh + 1] for p in range(3))
        s_c = jnp.dot(qa[:, data], ck_ref[0, 0, hh].astype(BF16), preferred_element_type=F32)
        s_c = s_c + cum_q + suffix[hh:hh + 1]
        s_n = lax.dot_general(qa, k_ref[0, hh], _NT, preferred_element_type=F32)
        s_n = jnp.where(c <= r, s_n, NEG_INF)
        m = jnp.maximum(jnp.max(s_c, axis=1, keepdims=True), jnp.max(s_n, axis=1, keepdims=True))
        p_c = jnp.exp2(s_c - m)
        p_n = jnp.exp2(s_n - m)
        denom = jnp.sum(p_c, axis=1, keepdims=True) + jnp.sum(p_n, axis=1, keepdims=True)
        o = lax.dot_general(p_c.astype(BF16), cv_ref[0, 0, hh].astype(BF16), _NT, preferred_element_type=F32)
        o = o + jnp.dot(p_n.astype(BF16), v_ref[0, hh][:, 0:HEAD_DIM], preferred_element_type=F32)
        o_ref[0, :, hh * HEAD_DIM:(hh + 1) * HEAD_DIM] = (o / denom).astype(BF16)


def _fox_sample(fqa, fka, fva, cache_k, cache_v, cache_logf):
    b, nh, t, _ = fqa.shape
    past = cache_k.shape[3]
    cache_k, cache_v = jnp.swapaxes(cache_k, 3, 4), jnp.swapaxes(cache_v, 3, 4)
    new = pl.BlockSpec((1, nh, t, LANES), lambda i: (i, 0, 0, 0))
    cache = pl.BlockSpec((1, 1, nh, HEAD_DIM, past), lambda i: (0, i, 0, 0, 0))
    return pl.pallas_call(
        functools.partial(_fox_sample_kernel, t=t, past=past),
        grid=(b,),
        in_specs=[new, new, new, cache, cache, pl.BlockSpec((1, nh, past), lambda i: (i, 0, 0))],
        out_specs=pl.BlockSpec((1, t, FOX_WIDTH), lambda i: (i, 0, 0)),
        out_shape=jax.ShapeDtypeStruct((b, t, FOX_WIDTH), BF16),
        compiler_params=_params(("arbitrary",)),
        name="fox_sample",
    )(fqa, fka, fva, cache_k, cache_v, cache_logf)


def _diff_sample_kernel(q_ref, k_ref, v_ref, ck_ref, cv_ref, lam_ref, g_ref, o_ref, *, t, past, lam_init):
    lane = lax.broadcasted_iota(jnp.int32, (t, LANES), 1)
    pos_q = past + lax.broadcasted_iota(jnp.int32, (t, past), 0)
    pos_c = lax.broadcasted_iota(jnp.int32, (t, past), 1)
    dist_c = jnp.abs(pos_q - pos_c).astype(F32)
    vis_c = (pos_c // CHUNK) <= (pos_q // CHUNK)
    pq_n = past + lax.broadcasted_iota(jnp.int32, (t, t), 0)
    pk_n = past + lax.broadcasted_iota(jnp.int32, (t, t), 1)
    dist_n = jnp.abs(pq_n - pk_n).astype(F32)
    vis_n = (pk_n // CHUNK) <= (pq_n // CHUNK)
    lam = _lambda(lam_ref, lam_init)
    for hd in range(DIFF_HEADS):
        slope = LOG2E * 2.0 ** (-8.0 * (hd + 1) / DIFF_HEADS)
        ck = ck_ref[0, 0, hd].astype(BF16)
        cv = cv_ref[0, 0, hd].astype(BF16)
        v = v_ref[0, hd]
        outs = []
        for c in range(2):
            half = (lane < HEAD_DIM) if c == 0 else (lane >= HEAD_DIM)
            qz = jnp.where(half, q_ref[0, 2 * hd + c], jnp.zeros((), BF16))
            kz = jnp.where(half, k_ref[0, 2 * hd + c], jnp.zeros((), BF16))
            s_c = jnp.dot(qz, ck, preferred_element_type=F32)
            s_c = jnp.where(vis_c, s_c - slope * dist_c, NEG_INF)
            s_n = lax.dot_general(qz, kz, _NT, preferred_element_type=F32)
            s_n = jnp.where(vis_n, s_n - slope * dist_n, NEG_INF)
            m = jnp.maximum(jnp.max(s_c, axis=1, keepdims=True), jnp.max(s_n, axis=1, keepdims=True))
            p_c = jnp.exp2(s_c - m)
            p_n = jnp.exp2(s_n - m)
            denom = jnp.sum(p_c, axis=1, keepdims=True) + jnp.sum(p_n, axis=1, keepdims=True)
            o = jnp.dot(p_c.astype(BF16), cv, preferred_element_type=F32)
            o = o + jnp.dot(p_n.astype(BF16), v, preferred_element_type=F32)
            outs.append(o / denom)
        normed = _diff_normed(outs[0], outs[1], lam, 1) * g_ref[...] * (1.0 - lam_init)
        o_ref[0, :, hd * DIFF_VDIM:(hd + 1) * DIFF_VDIM] = normed.astype(BF16)


def _diff_sample(dqa, dka, dvb, cache_k, cache_v, lam_vecs, subln_g, lam_init):
    b, nm, t, _ = dqa.shape
    past = cache_k.shape[3]
    ck = jnp.transpose(cache_k, (0, 1, 2, 4, 5, 3)).reshape(cache_k.shape[:3] + (DIFF_VDIM, past))
    new = pl.BlockSpec((1, nm, t, LANES), lambda i: (i, 0, 0, 0))
    return pl.pallas_call(
        functools.partial(_diff_sample_kernel, t=t, past=past, lam_init=lam_init),
        grid=(b,),
        in_specs=[new, new, pl.BlockSpec((1, DIFF_HEADS, t, DIFF_VDIM), lambda i: (i, 0, 0, 0)),
                  pl.BlockSpec((1, 1, DIFF_HEADS, DIFF_VDIM, past), lambda i: (0, i, 0, 0, 0)),
                  pl.BlockSpec((1, 1, DIFF_HEADS, past, DIFF_VDIM), lambda i: (0, i, 0, 0, 0)),
                  pl.BlockSpec(lam_vecs.shape, lambda i: (0, 0)),
                  pl.BlockSpec(subln_g.shape, lambda i: (0, 0))],
        out_specs=pl.BlockSpec((1, t, DIFF_WIDTH), lambda i: (i, 0, 0)),
        out_shape=jax.ShapeDtypeStruct((b, t, DIFF_WIDTH), BF16),
        compiler_params=_params(("arbitrary",)),
        name="diff_sample",
    )(dqa, dka, dvb, ck, cache_v, lam_vecs, subln_g)


def _out_proj_kernel(fo_ref, *refs, alpha, n_diff):
    do_refs, (fg_ref, dg_ref, x_ref, mod_ref, w_ref, g_ref, b_ref, y_ref) = refs[:n_diff], refs[n_diff:]
    fox = fo_ref[0] * fg_ref[0]
    diff = jnp.concatenate([r[0] for r in do_refs], axis=1) * dg_ref[0]
    branch = jnp.dot(fox, w_ref[0:FOX_WIDTH, :], preferred_element_type=F32)
    branch = branch + jnp.dot(diff, w_ref[FOX_WIDTH:, :], preferred_element_type=F32)
    r = alpha * x_ref[0] + mod_ref[0, 2:3, :] * branch
    mu = jnp.mean(r, axis=-1, keepdims=True)
    rc = r - mu
    var = jnp.mean(rc * rc, axis=-1, keepdims=True)
    y_ref[0] = rc * lax.rsqrt(var + LN_EPS) * g_ref[...] + b_ref[...]


def _out_proj(fox_o, diff_parts, fg, dg, x, mod, w_out, ln_g, ln_b, alpha):
    b, t, d = x.shape
    tm = 1024 if t % 1024 == 0 else min(t, _TILE)
    rows = pl.BlockSpec((1, tm, FOX_WIDTH), lambda i, j: (i, j, 0))
    wide = pl.BlockSpec((1, tm, d), lambda i, j: (i, j, 0))

    def full(a):
        return pl.BlockSpec(a.shape, lambda i, j: (0,) * a.ndim)

    return pl.pallas_call(
        functools.partial(_out_proj_kernel, alpha=alpha, n_diff=len(diff_parts)),
        grid=(b, t // tm),
        in_specs=([rows] + [pl.BlockSpec((1, tm, p.shape[-1]), lambda i, j: (i, j, 0)) for p in diff_parts]
                  + [rows, rows, wide, pl.BlockSpec((1, 3, d), lambda i, j: (i, 0, 0)),
                     full(w_out), full(ln_g), full(ln_b)]),
        out_specs=wide,
        out_shape=jax.ShapeDtypeStruct((b, t, d), F32),
        compiler_params=_params(("arbitrary", "arbitrary")),
        name="out_proj",
    )(fox_o, *diff_parts, fg, dg, x, mod, w_out, ln_g, ln_b)


def _pack_w_in(w_in):
    fw = FOX_WIDTH
    o = 0
    fq = w_in[:, o:o + fw]; o += fw
    fk = w_in[:, o:o + fw]; o += fw
    fv = w_in[:, o:o + fw]; o += fw
    ff = w_in[:, o:o + FOX_HEADS]; o += FOX_HEADS
    fg = w_in[:, o:o + fw]; o += fw
    dq = w_in[:, o:o + fw]; o += fw
    dk = w_in[:, o:o + fw]; o += fw
    dv = w_in[:, o:o + fw]; o += fw
    dg = w_in[:, o:o + fw]
    qs = HEAD_DIM ** -0.5 * LOG2E
    return jnp.concatenate([fq * qs, fk, fv, fg, dq * qs, dk, dv, dg,
                            jnp.tile(ff, (1, LANES // FOX_HEADS))], axis=1).astype(BF16)


def kernel(x_prompt, x_sample, cache_fox_k, cache_fox_v, cache_fox_logf, cache_diff_k, cache_diff_v,
           c_prompt, c_sample, w_ada, b_ada, w_in, b_f, lambda_q1, lambda_k1, lambda_q2, lambda_k2,
           subln_g, w_out, ln_g, ln_b):
    depth = w_in.shape[0]
    assert depth == 1, "single-layer step"
    layer = 0
    lam_init = 0.8 - 0.6 * math.exp(-0.3 * layer)
    alpha = (2 * depth) ** 0.25
    bp, bs = c_prompt.shape[0], c_sample.shape[0]
    d = x_prompt.shape[-1]
    past = cache_fox_k.shape[3]

    c_all = jnp.concatenate([c_prompt, c_sample, jnp.zeros((-(bp + bs) % 8, d), F32)], axis=0)
    mod = _ada(c_all, w_ada[layer], b_ada[layer]).reshape(c_all.shape[0], 3, d)
    mod_p, mod_s = mod[:bp], mod[bp:bp + bs]

    w = _pack_w_in(w_in[layer])
    bf16x = jnp.tile(b_f[layer], LANES // FOX_HEADS).reshape(1, LANES)
    seg = np.repeat(np.eye(FOX_WIDTH // HEAD_DIM, LANES, dtype=np.float32), HEAD_DIM, axis=0)
    seg = jnp.asarray(seg, BF16)
    lam_vecs = jnp.stack([lambda_q1[layer], lambda_k1[layer], lambda_q2[layer], lambda_k2[layer]])
    g = subln_g[layer].reshape(1, DIFF_VDIM)
    w_o = w_out[layer].astype(BF16)
    lg, lb = ln_g[layer].reshape(1, d), ln_b[layer].reshape(1, d)

    def run(x, mod_x, pos_offset, attend):
        (fk, fv, dk, dv, logf, fqa, fka, fva, fg, dqa, dka, dvb, dg, stats) = _in_proj(
            x, mod_x, w, bf16x, seg, pos_offset)
        stats = jnp.transpose(stats[..., :FOX_WIDTH // HEAD_DIM], (0, 3, 2, 1))
        fox_o, diff_o = attend(fqa, fka, fva, dqa, dka, dvb, stats)
        y = _out_proj(fox_o, diff_o, fg, dg, x, mod_x, w_o, lg, lb, alpha)
        b, t = x.shape[:2]
        states = (fk[None], fv[None], logf[None],
                  dk.reshape(1, b, DIFF_HEADS, t, 2, HEAD_DIM), dv[None])
        return y, states

    def prompt_attend(fqa, fka, fva, dqa, dka, dvb, stats):
        return (_fox_attention(fqa, fka, fva, stats),
                _diff_attention(dqa, dka, dvb, stats, lam_vecs, g, lam_init))

    def sample_attend(fqa, fka, fva, dqa, dka, dvb, stats):
        del stats
        return (_fox_sample(fqa, fka, fva, cache_fox_k, cache_fox_v, cache_fox_logf[layer]),
                (_diff_sample(dqa, dka, dvb, cache_diff_k, cache_diff_v, lam_vecs, g, lam_init),))

    yp, sp = run(x_prompt, mod_p, 0, prompt_attend)
    ys, ss = run(x_sample, mod_s, past, sample_attend)
    return (yp, ys) + sp + ss
```

```python
import functools
import math

import jax
import jax.numpy as jnp
import numpy as np
from jax import lax
from jax.experimental import pallas as pl
from jax.experimental.pallas import tpu as pltpu

F32 = jnp.float32
BF16 = jnp.bfloat16

HEAD_DIM = 64
FOX_HEADS = 8
DIFF_HEADS = 4
DIFF_VDIM = 2 * HEAD_DIM
FOX_WIDTH = FOX_HEADS * HEAD_DIM
DIFF_WIDTH = DIFF_HEADS * DIFF_VDIM
CHUNK = 64
LN_EPS = 1e-5
RMS_EPS = 1e-5
LANES = 128
NEG_INF = float("-inf")
LOG2E = math.log2(math.e)

_SEC = {name: i * FOX_WIDTH for i, name in enumerate(("fq", "fk", "fv", "fg", "dq", "dk", "dv", "dg"))}
_FF_OFF = 8 * FOX_WIDTH

_VMEM_LIMIT = 56 * 1024 * 1024
_TILE = 512


def _params(sem):
    return pltpu.CompilerParams(dimension_semantics=sem, vmem_limit_bytes=_VMEM_LIMIT)


def _ada_kernel(c_ref, w_ref, b_ref, o_ref):
    c = c_ref[...]
    s = c * jax.nn.sigmoid(c)
    o_ref[...] = jnp.dot(s, w_ref[...], preferred_element_type=F32) + b_ref[...]


def _ada(c_all, w_ada, b_ada):
    rows, d = c_all.shape
    n = w_ada.shape[1]
    tn = 512
    return pl.pallas_call(
        _ada_kernel,
        grid=(n // tn,),
        in_specs=[pl.BlockSpec((rows, d), lambda j: (0, 0)),
                  pl.BlockSpec((d, tn), lambda j: (0, j)),
                  pl.BlockSpec((1, tn), lambda j: (0, j))],
        out_specs=pl.BlockSpec((rows, tn), lambda j: (0, j)),
        out_shape=jax.ShapeDtypeStruct((rows, n), F32),
        compiler_params=_params(("arbitrary",)),
        name="ada",
    )(c_all, w_ada, b_ada.reshape(1, n))


def _split3(x):
    hi = x.astype(BF16)
    r1 = x - hi.astype(F32)
    mid = r1.astype(BF16)
    lo = (r1 - mid.astype(F32)).astype(BF16)
    return hi, mid, lo


def _in_proj_kernel(x_ref, mod_ref, w_ref, bf_ref, seg_ref,
                    fk_ref, fv_ref, dk_ref, dv_ref, logf_ref,
                    fqa_ref, fka_ref, fva_ref, fg_ref, dqa_ref, dka_ref, dvb_ref, dg_ref, stat_ref,
                    carry_ref, *, tm, pos_offset):
    ti = pl.program_id(1)
    x = x_ref[0]
    shift = mod_ref[0, 0:1, :]
    scale = mod_ref[0, 1:2, :]
    h = (x * (1.0 + scale) + shift).astype(BF16)

    def proj(name, width=FOX_WIDTH):
        off = _SEC[name] if name in _SEC else _FF_OFF
        return jnp.dot(h, w_ref[:, off:off + width], preferred_element_type=F32)

    lane = lax.broadcasted_iota(jnp.int32, (tm, LANES), 1)
    row = lax.broadcasted_iota(jnp.int32, (tm, LANES), 0)
    lower = lane < HEAD_DIM
    grp = (lane // 8) % 4

    def by_group(a, b, c):
        return jnp.where(grp == 0, a, jnp.where(grp == 1, b, jnp.where(grp == 2, c, jnp.zeros_like(a))))

    def max_sq_norm(z):
        zr = z.astype(BF16).astype(F32)
        sq = zr * zr
        if tm % 32 == 0:
            sq = jnp.maximum(sq[0:tm // 2], sq[tm // 2:tm])
        sq = jnp.dot(sq.astype(BF16), seg_ref[...], preferred_element_type=F32)
        return jnp.max(sq, axis=0, keepdims=True)

    logf = jax.nn.log_sigmoid(proj("ff", LANES) + bf_ref[...])
    logf_ref[0] = logf.T[0:FOX_HEADS] if tm % LANES == 0 else logf

    dq, dk, dv = proj("dq"), proj("dk"), proj("dv")
    pos = (row + (ti * tm + pos_offset)).astype(F32)
    sub = lane % HEAD_DIM
    for hd in range(DIFF_HEADS):
        slope = 2.0 ** (-8.0 * (hd + 1) / DIFF_HEADS)
        sl = slice(hd * LANES, (hd + 1) * LANES)
        p0, p1, p2 = (a.astype(F32) for a in _split3((slope * LOG2E) * pos))
        alibi = jnp.where(sub % 3 == 0, p0, jnp.where(sub % 3 == 1, p1, p2))
        alibi_q = jnp.where(sub < 3, -alibi, (sub < 6).astype(F32))
        alibi_k = jnp.where(sub < 3, 1.0, jnp.where(sub < 6, alibi, 0.0))
        dqa_ref[0, 2 * hd] = jnp.where(lower, dq[:, sl], alibi_q).astype(BF16)
        dqa_ref[0, 2 * hd + 1] = jnp.where(lower, alibi_q, dq[:, sl]).astype(BF16)
        dka_ref[0, 2 * hd] = jnp.where(lower, dk[:, sl], alibi_k).astype(BF16)
        dka_ref[0, 2 * hd + 1] = jnp.where(lower, alibi_k, dk[:, sl]).astype(BF16)
        if tm % LANES:
            dk_ref[0, hd] = dk[:, sl]
        dv_ref[0, hd] = dv[:, sl]
        dvb_ref[0, hd] = dv[:, sl].astype(BF16)
    if tm % LANES == 0:
        dk_ref[0] = dk.T.reshape(dk_ref.shape[1:])
    dg_ref[0] = jax.nn.silu(proj("dg")).astype(BF16)
    diff_norms = [max_sq_norm(dq), max_sq_norm(dk)]

    tri_r = lax.broadcasted_iota(jnp.int32, (tm, tm), 0)
    tri_c = lax.broadcasted_iota(jnp.int32, (tm, tm), 1)
    tri = (tri_c <= tri_r).astype(BF16)
    part = jnp.dot(tri, by_group(*_split3(logf)), preferred_element_type=F32)
    zq, zk, zv = proj("fq"), proj("fk"), proj("fv")
    local = part + pltpu.roll(part, 8, 1) + pltpu.roll(part, 16, 1) + pltpu.roll(part, 24, 1)

    @pl.when(ti == 0)
    def _():
        carry_ref[...] = jnp.zeros_like(carry_ref)

    cum = local + carry_ref[...]
    carry_ref[...] = cum[tm - 1:tm, :]
    cum2 = cum * LOG2E
    pieces = by_group(*(a.astype(F32) for a in _split3(cum2)))
    behind = pltpu.roll(pieces, 24, 1)
    u = lane % HEAD_DIM
    ext_q = jnp.where(u < 24, pieces, (u < 48).astype(F32))

    def ext_k(h):
        mine = u % 8 == h
        return jnp.where(mine & (u < 24), 1.0, jnp.where(mine & (u < 48), -behind, 0.0))

    fox_norms = [max_sq_norm(zq), max_sq_norm(zk)]
    e_even = (lane == HEAD_DIM).astype(F32)
    for j in range(FOX_HEADS // 2):
        sl = slice(j * LANES, (j + 1) * LANES)
        fqa_ref[0, 2 * j] = jnp.where(lower, zq[:, sl], ext_q).astype(BF16)
        fqa_ref[0, 2 * j + 1] = jnp.where(lower, ext_q, zq[:, sl]).astype(BF16)
        fka_ref[0, 2 * j] = jnp.where(lower, zk[:, sl], ext_k(2 * j)).astype(BF16)
        fka_ref[0, 2 * j + 1] = jnp.where(lower, ext_k(2 * j + 1), zk[:, sl]).astype(BF16)
        fva_ref[0, 2 * j] = jnp.where(lower, zv[:, sl], e_even).astype(BF16)
        fva_ref[0, 2 * j + 1] = jnp.where(lower, pltpu.roll(zv[:, sl], HEAD_DIM, 1), e_even).astype(BF16)
    if tm % LANES == 0:
        fk_ref[0] = zk.T.reshape(fk_ref.shape[1:])
        fv_ref[0] = zv.T.reshape(fv_ref.shape[1:])
    else:
        for hd in range(FOX_HEADS):
            fk_ref[0, hd] = zk[:, hd * HEAD_DIM:(hd + 1) * HEAD_DIM]
            fv_ref[0, hd] = zv[:, hd * HEAD_DIM:(hd + 1) * HEAD_DIM]
    fg_ref[0] = jax.nn.silu(proj("fg")).astype(BF16)
    stat_ref[0, 0] = jnp.concatenate(
        fox_norms + diff_norms + [cum2[0:1], cum2[tm - 1:tm], jnp.zeros((2, LANES), F32)],
        axis=0)


def _in_proj(x, mod, w, bf16x, seg, pos_offset):
    b, t, d = x.shape
    tm = min(t, _TILE)
    nt = t // tm
    head_major = tm % LANES == 0

    def full(a):
        return pl.BlockSpec(a.shape, lambda i, j: (0,) * a.ndim)

    def heads(n, width):
        return pl.BlockSpec((1, n, tm, width), lambda i, j: (i, 0, j, 0))

    def cache_shape(n, width):
        return jax.ShapeDtypeStruct((b, n, width, t) if head_major else (b, n, t, width), F32)

    def cache_spec(n, width):
        return pl.BlockSpec((1, n, width, tm), lambda i, j: (i, 0, 0, j)) if head_major else heads(n, width)

    rows = pl.BlockSpec((1, tm, FOX_WIDTH), lambda i, j: (i, j, 0))
    out_shape = (
        cache_shape(FOX_HEADS, HEAD_DIM),
        cache_shape(FOX_HEADS, HEAD_DIM),
        cache_shape(DIFF_HEADS, DIFF_VDIM),
        jax.ShapeDtypeStruct((b, DIFF_HEADS, t, DIFF_VDIM), F32),
        jax.ShapeDtypeStruct((b, FOX_HEADS, t) if head_major else (b, t, LANES), F32),
        jax.ShapeDtypeStruct((b, FOX_HEADS, t, LANES), BF16),
        jax.ShapeDtypeStruct((b, FOX_HEADS, t, LANES), BF16),
        jax.ShapeDtypeStruct((b, FOX_HEADS, t, LANES), BF16),
        jax.ShapeDtypeStruct((b, t, FOX_WIDTH), BF16),
        jax.ShapeDtypeStruct((b, 2 * DIFF_HEADS, t, LANES), BF16),
        jax.ShapeDtypeStruct((b, 2 * DIFF_HEADS, t, LANES), BF16),
        jax.ShapeDtypeStruct((b, DIFF_HEADS, t, DIFF_VDIM), BF16),
        jax.ShapeDtypeStruct((b, t, DIFF_WIDTH), BF16),
        jax.ShapeDtypeStruct((b, nt, 8, LANES), F32),
    )
    out_specs = (
        cache_spec(FOX_HEADS, HEAD_DIM), cache_spec(FOX_HEADS, HEAD_DIM),
        cache_spec(DIFF_HEADS, DIFF_VDIM), heads(DIFF_HEADS, DIFF_VDIM),
        (pl.BlockSpec((1, FOX_HEADS, tm), lambda i, j: (i, 0, j)) if head_major
         else pl.BlockSpec((1, tm, LANES), lambda i, j: (i, j, 0))),
        heads(FOX_HEADS, LANES), heads(FOX_HEADS, LANES), heads(FOX_HEADS, LANES), rows,
        heads(2 * DIFF_HEADS, LANES), heads(2 * DIFF_HEADS, LANES), heads(DIFF_HEADS, DIFF_VDIM), rows,
        pl.BlockSpec((1, 1, 8, LANES), lambda i, j: (i, j, 0, 0)),
    )
    outs = pl.pallas_call(
        functools.partial(_in_proj_kernel, tm=tm, pos_offset=pos_offset),
        grid=(b, nt),
        in_specs=[pl.BlockSpec((1, tm, d), lambda i, j: (i, j, 0)),
                  pl.BlockSpec((1, 3, d), lambda i, j: (i, 0, 0)),
                  full(w), full(bf16x), full(seg)],
        out_specs=out_specs,
        out_shape=out_shape,
        scratch_shapes=[pltpu.VMEM((1, LANES), F32)],
        compiler_params=_params(("arbitrary", "arbitrary")),
        name="in_proj",
    )(x, mod, w, bf16x, seg)
    if head_major:
        outs = tuple(jnp.swapaxes(o, 2, 3) for o in outs[:3]) + outs[3:]
    else:
        outs = outs[:4] + (jnp.swapaxes(outs[4][:, :, :FOX_HEADS], 1, 2),) + outs[5:]
    return outs


_NT = (((1,), (1,)), ((), ()))
_TN = (((0,), (0,)), ((), ()))
_STAGES = 6
_FOX_V_ROWS = HEAD_DIM + 16
_FOX_GROUP = 4


_SKIP_MARGIN = 136.0
_STAT_FOX_Q, _STAT_FOX_K, _STAT_DIFF_Q, _STAT_DIFF_K, _STAT_CUM_FIRST, _STAT_CUM_LAST = range(6)


def _diagonal_blocks(tq):
    assert tq % (2 * LANES) == 0
    return ((tq // 2, 0), (tq, tq // 2))


def _dot_bound(stat, q_row, k_row, qi):
    tile = lax.broadcasted_iota(jnp.int32, (1, stat.shape[-1]), 1)
    qn2 = jnp.sum(jnp.where(tile == qi, stat[q_row:q_row + 1], 0.0))
    return jnp.sqrt(qn2 * stat[k_row:k_row + 1]) * 1.01 + 1.0


def _first_needed_tile(qi, bounds, m_ref):
    first = qi
    for c, bound in enumerate(bounds):
        tile = lax.broadcasted_iota(jnp.int32, bound.shape, 1)
        needed = (bound >= jnp.min(m_ref[c]) - _SKIP_MARGIN) & (tile < qi)
        first = jnp.minimum(first, jnp.min(jnp.where(needed, tile, qi)))
    return first


def _pipelined_tiles(qi, n_chains, scores, absorb, first_needed, sa_ref, sb_ref):
    chains = range(n_chains)
    bufs = (sa_ref, sb_ref)
    for c in chains:
        scores(qi, sa_ref, c, True)
        scores(jnp.maximum(qi - 1, 0), sb_ref, c)
        absorb(sa_ref, qi, c, True)
    n = qi - first_needed()

    def stages(top, count, more):
        for s in range(count):
            for c in chains:
                if more or s + 1 < count:
                    scores(jnp.maximum(top - s - 1, 0), bufs[s % 2], c)
                absorb(bufs[(s + 1) % 2], top - s, c, False)

    def body(i, carry):
        stages(qi - 1 - _STAGES * i, _STAGES, True)
        return carry

    trips = jnp.maximum(n - 1, 0) // _STAGES
    lax.fori_loop(0, trips, body, 0)
    rem = n - _STAGES * trips
    for count in range(1, _STAGES + 1):
        @pl.when(rem == count)
        def _(count=count):
            stages(qi - 1 - _STAGES * trips, count, False)


def _fox_kernel(q_ref, k_ref, v_ref, stat_ref, o_ref, sa_ref, sb_ref, m_ref, acc_ref, *, tq):
    qi = pl.program_id(2)
    half = tq // 2

    def scores(ki, dst, hh, diagonal=False):
        start = pl.multiple_of(ki * tq, tq)
        if diagonal:
            for keys, q0 in _diagonal_blocks(tq):
                k = k_ref[0, hh, pl.ds(start, keys), :]
                dst[hh, 0:keys, q0:q0 + half] = lax.dot_general(k, q_ref[0, hh, q0:q0 + half, :], _NT,
                                                                preferred_element_type=F32)
        else:
            k = k_ref[0, hh, pl.ds(start, tq), :]
            dst[hh] = lax.dot_general(k, q_ref[0, hh], _NT, preferred_element_type=F32)

    def absorb(src, ki, hh, diagonal):
        start = pl.multiple_of(ki * tq, tq)
        if diagonal:
            for keys, q0 in _diagonal_blocks(tq):
                s = src[hh, 0:keys, q0:q0 + half]
                key = lax.broadcasted_iota(jnp.int32, s.shape, 0)
                qry = lax.broadcasted_iota(jnp.int32, s.shape, 1) + q0
                s = jnp.where(key <= qry, s, NEG_INF)
                m_new = jnp.max(s, axis=0, keepdims=True)
                p = jnp.exp2(s - m_new)
                v = v_ref[0, hh, pl.ds(start, keys), 0:_FOX_V_ROWS]
                acc_ref[hh, :, q0:q0 + half] = lax.dot_general(v, p.astype(BF16), _TN, preferred_element_type=F32)
                m_ref[hh, :, q0:q0 + half] = m_new
            return
        s = src[hh]
        m_prev = m_ref[hh]
        m_new = jnp.maximum(m_prev, jnp.max(s, axis=0, keepdims=True))
        p = jnp.exp2(s - m_new)
        alpha = jnp.exp2(m_prev - m_new)
        v = v_ref[0, hh, pl.ds(start, tq), 0:_FOX_V_ROWS]
        pv = lax.dot_general(v, p.astype(BF16), _TN, preferred_element_type=F32)
        acc_ref[hh] = alpha * acc_ref[hh] + pv
        m_ref[hh] = m_new

    def first_needed():
        bounds = []
        for hh in range(_FOX_GROUP):
            stat = stat_ref[0, hh]
            tile = lax.broadcasted_iota(jnp.int32, (1, stat.shape[-1]), 1)
            cum_q = jnp.sum(jnp.where(tile == qi, stat[_STAT_CUM_FIRST:_STAT_CUM_FIRST + 1], 0.0))
            decay = cum_q - stat[_STAT_CUM_LAST:_STAT_CUM_LAST + 1]
            bounds.append(_dot_bound(stat, _STAT_FOX_Q, _STAT_FOX_K, qi) + decay)
        return _first_needed_tile(qi, bounds, m_ref)

    _pipelined_tiles(qi, _FOX_GROUP, scores, absorb, first_needed, sa_ref, sb_ref)
    outs = [acc_ref[hh, 0:HEAD_DIM] / acc_ref[hh, HEAD_DIM:HEAD_DIM + 1] for hh in range(_FOX_GROUP)]
    o_ref[0] = jnp.concatenate(outs, axis=0).T.astype(BF16)


def _fox_attention(fqa, fka, fva, stats):
    b, nh, t, _ = fqa.shape
    tq = min(t, _TILE)
    g = _FOX_GROUP
    return pl.pallas_call(
        functools.partial(_fox_kernel, tq=tq),
        grid=(b, nh // g, t // tq),
        in_specs=[pl.BlockSpec((1, g, tq, LANES), lambda i, j, q: (i, j, q, 0)),
                  pl.BlockSpec((1, g, t, LANES), lambda i, j, q: (i, j, 0, 0)),
                  pl.BlockSpec((1, g, t, LANES), lambda i, j, q: (i, j, 0, 0)),
                  pl.BlockSpec((1, g) + stats.shape[2:], lambda i, j, q: (i, j, 0, 0))],
        out_specs=pl.BlockSpec((1, tq, g * HEAD_DIM), lambda i, j, q: (i, q, j)),
        out_shape=jax.ShapeDtypeStruct((b, t, FOX_WIDTH), BF16),
        scratch_shapes=[pltpu.VMEM((g, tq, tq), F32), pltpu.VMEM((g, tq, tq), F32),
                        pltpu.VMEM((g, 1, tq), F32), pltpu.VMEM((g, _FOX_V_ROWS, tq), F32)],
        compiler_params=_params(("arbitrary", "arbitrary", "arbitrary")),
        name="fox_attn",
    )(fqa, fka, fva, stats)


def _alibi_slope(head):
    return jnp.exp2(jnp.full((1, 1), -8.0 / DIFF_HEADS, F32) * (head + 1).astype(F32))


def _lambda(lam_ref, lam_init):
    lv = lam_ref[...]
    e1 = jnp.exp(jnp.sum(lv[0:1] * lv[1:2], axis=1, keepdims=True))
    e2 = jnp.exp(jnp.sum(lv[2:3] * lv[3:4], axis=1, keepdims=True))
    return e1 - e2 + lam_init


def _diff_normed(o0, o1, lam, axis):
    df = o0 - lam * o1
    return df * lax.rsqrt(jnp.mean(df * df, axis=axis, keepdims=True) + RMS_EPS)


_ONES_ROWS = 16


def _diff_kernel(q_ref, k_ref, v_ref, stat_ref, lam_ref, g_ref, o_ref, sa_ref, sb_ref, m_ref, acc_ref, *, tq,
                 lam_init, first_head, group):
    qi = pl.program_id(2)
    half = tq // 2
    chains = 2 * group
    slopes = [_alibi_slope(first_head + pl.program_id(1) * group + h) for h in range(group)]

    def scores(ki, dst, c, diagonal=False):
        start = pl.multiple_of(ki * tq, tq)
        if diagonal:
            for keys, q0 in _diagonal_blocks(tq):
                k = k_ref[0, c, pl.ds(start, keys), :]
                dst[c, 0:keys, q0:q0 + half] = lax.dot_general(k, q_ref[0, c, q0:q0 + half, :], _NT,
                                                               preferred_element_type=F32)
        else:
            k = k_ref[0, c, pl.ds(start, tq), :]
            dst[c] = lax.dot_general(k, q_ref[0, c], _NT, preferred_element_type=F32)

    def values_t(c, start, keys):
        v = v_ref[0, c // 2, pl.ds(start, keys), :]
        return jnp.concatenate([v.T, jnp.ones((_ONES_ROWS, keys), BF16)], axis=0)

    def absorb(src, ki, c, diagonal):
        start = pl.multiple_of(ki * tq, tq)
        if diagonal:
            for keys, q0 in _diagonal_blocks(tq):
                s = src[c, 0:keys, q0:q0 + half]
                key = lax.broadcasted_iota(jnp.int32, s.shape, 0)
                qry = lax.broadcasted_iota(jnp.int32, s.shape, 1) + q0
                ahead = jnp.maximum(key - qry, 0).astype(F32)
                s = jnp.where((key // CHUNK) <= (qry // CHUNK), s - (2.0 * LOG2E * slopes[c // 2]) * ahead, NEG_INF)
                m_new = jnp.max(s, axis=0, keepdims=True)
                p = jnp.exp2(s - m_new)
                acc_ref[c, :, q0:q0 + half] = jnp.dot(values_t(c, start, keys), p.astype(BF16),
                                                      preferred_element_type=F32)
                m_ref[c, :, q0:q0 + half] = m_new
            return
        s = src[c]
        m_prev = m_ref[c]
        m_new = jnp.maximum(m_prev, jnp.max(s, axis=0, keepdims=True))
        p = jnp.exp2(s - m_new)
        alpha = jnp.exp2(m_prev - m_new)
        acc_ref[c] = alpha * acc_ref[c] + jnp.dot(values_t(c, start, tq), p.astype(BF16),
                                                  preferred_element_type=F32)
        m_ref[c] = m_new

    def first_needed():
        bounds = []
        for c in range(chains):
            stat = stat_ref[0, c]
            tile = lax.broadcasted_iota(jnp.int32, (1, stat.shape[-1]), 1)
            gap = ((qi - 1 - tile) * tq + 1).astype(F32)
            bounds.append(_dot_bound(stat, _STAT_DIFF_Q, _STAT_DIFF_K, qi) - (LOG2E * slopes[c // 2]) * gap)
        return _first_needed_tile(qi, bounds, m_ref)

    _pipelined_tiles(qi, chains, scores, absorb, first_needed, sa_ref, sb_ref)
    lam = _lambda(lam_ref, lam_init)
    for h in range(group):
        o0, o1 = (acc_ref[2 * h + c, :DIFF_VDIM] / acc_ref[2 * h + c, DIFF_VDIM:DIFF_VDIM + 1] for c in range(2))
        normed = _diff_normed(o0, o1, lam, 0).T * g_ref[...] * (1.0 - lam_init)
        o_ref[0, :, h * DIFF_VDIM:(h + 1) * DIFF_VDIM] = normed.astype(BF16)


_DIFF_GROUPS = ((0, 1, 2), (2, 2, 1))


def _diff_attention(dqa, dka, dvb, stats, lam_vecs, subln_g, lam_init):
    b, _, t, _ = dqa.shape
    tq = min(t, _TILE)
    outs = []
    for first_head, g, n_groups in _DIFF_GROUPS:
        off = first_head // g

        def heads(block, off=off):
            return pl.BlockSpec(block, lambda i, j, q: (i, j + off, 0, 0))

        outs.append(pl.pallas_call(
            functools.partial(_diff_kernel, tq=tq, lam_init=lam_init, first_head=first_head, group=g),
            grid=(b, n_groups, t // tq),
            in_specs=[pl.BlockSpec((1, 2 * g, tq, LANES), lambda i, j, q, off=off: (i, j + off, q, 0)),
                      heads((1, 2 * g, t, LANES)), heads((1, g, t, DIFF_VDIM)),
                      heads((1, 2 * g) + stats.shape[2:]),
                      pl.BlockSpec(lam_vecs.shape, lambda i, j, q: (0, 0)),
                      pl.BlockSpec(subln_g.shape, lambda i, j, q: (0, 0))],
            out_specs=pl.BlockSpec((1, tq, g * DIFF_VDIM), lambda i, j, q: (i, q, j)),
            out_shape=jax.ShapeDtypeStruct((b, t, n_groups * g * DIFF_VDIM), BF16),
            scratch_shapes=[pltpu.VMEM((2 * g, tq, tq), F32), pltpu.VMEM((2 * g, tq, tq), F32),
                            pltpu.VMEM((2 * g, 1, tq), F32),
                            pltpu.VMEM((2 * g, DIFF_VDIM + _ONES_ROWS, tq), F32)],
            compiler_params=_params(("arbitrary", "arbitrary", "arbitrary")),
            name="diff_attn",
        )(dqa, dka, dvb, stats, lam_vecs, subln_g))
    return tuple(outs)


def _prefix_sum_lanes(x):
    n = x.shape[-1]
    lane = lax.broadcasted_iota(jnp.int32, x.shape, x.ndim - 1)
    sh = 1
    while sh < n:
        x = x + jnp.where(lane >= sh, pltpu.roll(x, sh, x.ndim - 1), 0.0)
        sh *= 2
    return x


def _fox_sample_kernel(q_ref, k_ref, v_ref, ck_ref, cv_ref, clf_ref, o_ref, *, t, past):
    r = lax.broadcasted_iota(jnp.int32, (t, t), 0)
    c = lax.broadcasted_iota(jnp.int32, (t, t), 1)
    cum_c = _prefix_sum_lanes(clf_ref[0])
    suffix = (cum_c[:, past - 1:past] - cum_c) * LOG2E
    for hh in range(FOX_HEADS):
        data = slice(0, HEAD_DIM) if hh % 2 == 0 else slice(HEAD_DIM, LANES)
        ext = HEAD_DIM if hh % 2 == 0 else 0
        qa = q_ref[0, hh]
        qf = qa.astype(F32)
        cum_q = sum(qf[:, ext + 8 * p + hh:ext + 8 * p + hh + 1] for p in range(3))
        s_c = jnp.dot(qa[:, data], ck_ref[0, 0, hh].astype(BF16), preferred_element_type=F32)
        s_c = s_c + cum_q + suffix[hh:hh + 1]
        s_n = lax.dot_general(qa, k_ref[0, hh], _NT, preferred_element_type=F32)
        s_n = jnp.where(c <= r, s_n, NEG_INF)
        m = jnp.maximum(jnp.max(s_c, axis=1, keepdims=True), jnp.max(s_n, axis=1, keepdims=True))
        p_c = jnp.exp2(s_c - m)
        p_n = jnp.exp2(s_n - m)
        denom = jnp.sum(p_c, axis=1, keepdims=True) + jnp.sum(p_n, axis=1, keepdims=True)
        o = lax.dot_general(p_c.astype(BF16), cv_ref[0, 0, hh].astype(BF16), _NT, preferred_element_type=F32)
        o = o + jnp.dot(p_n.astype(BF16), v_ref[0, hh][:, 0:HEAD_DIM], preferred_element_type=F32)
        o_ref[0, :, hh * HEAD_DIM:(hh + 1) * HEAD_DIM] = (o / denom).astype(BF16)


def _fox_sample(fqa, fka, fva, cache_k, cache_v, cache_logf):
    b, nh, t, _ = fqa.shape
    past = cache_k.shape[3]
    cache_k, cache_v = jnp.swapaxes(cache_k, 3, 4), jnp.swapaxes(cache_v, 3, 4)
    new = pl.BlockSpec((1, nh, t, LANES), lambda i: (i, 0, 0, 0))
    cache = pl.BlockSpec((1, 1, nh, HEAD_DIM, past), lambda i: (0, i, 0, 0, 0))
    return pl.pallas_call(
        functools.partial(_fox_sample_kernel, t=t, past=past),
        grid=(b,),
        in_specs=[new, new, new, cache, cache, pl.BlockSpec((1, nh, past), lambda i: (i, 0, 0))],
        out_specs=pl.BlockSpec((1, t, FOX_WIDTH), lambda i: (i, 0, 0)),
        out_shape=jax.ShapeDtypeStruct((b, t, FOX_WIDTH), BF16),
        compiler_params=_params(("arbitrary",)),
        name="fox_sample",
    )(fqa, fka, fva, cache_k, cache_v, cache_logf)


def _diff_sample_kernel(q_ref, k_ref, v_ref, ck_ref, cv_ref, lam_ref, g_ref, o_ref, *, t, past, lam_init):
    lane = lax.broadcasted_iota(jnp.int32, (t, LANES), 1)
    pos_q = past + lax.broadcasted_iota(jnp.int32, (t, past), 0)
    pos_c = lax.broadcasted_iota(jnp.int32, (t, past), 1)
    dist_c = jnp.abs(pos_q - pos_c).astype(F32)
    vis_c = (pos_c // CHUNK) <= (pos_q // CHUNK)
    pq_n = past + lax.broadcasted_iota(jnp.int32, (t, t), 0)
    pk_n = past + lax.broadcasted_iota(jnp.int32, (t, t), 1)
    dist_n = jnp.abs(pq_n - pk_n).astype(F32)
    vis_n = (pk_n // CHUNK) <= (pq_n // CHUNK)
    lam = _lambda(lam_ref, lam_init)
    for hd in range(DIFF_HEADS):
        slope = LOG2E * 2.0 ** (-8.0 * (hd + 1) / DIFF_HEADS)
        ck = ck_ref[0, 0, hd].astype(BF16)
        cv = cv_ref[0, 0, hd].astype(BF16)
        v = v_ref[0, hd]
        outs = []
        for c in range(2):
            half = (lane < HEAD_DIM) if c == 0 else (lane >= HEAD_DIM)
            qz = jnp.where(half, q_ref[0, 2 * hd + c], jnp.zeros((), BF16))
            kz = jnp.where(half, k_ref[0, 2 * hd + c], jnp.zeros((), BF16))
            s_c = jnp.dot(qz, ck, preferred_element_type=F32)
            s_c = jnp.where(vis_c, s_c - slope * dist_c, NEG_INF)
            s_n = lax.dot_general(qz, kz, _NT, preferred_element_type=F32)
            s_n = jnp.where(vis_n, s_n - slope * dist_n, NEG_INF)
            m = jnp.maximum(jnp.max(s_c, axis=1, keepdims=True), jnp.max(s_n, axis=1, keepdims=True))
            p_c = jnp.exp2(s_c - m)
            p_n = jnp.exp2(s_n - m)
            denom = jnp.sum(p_c, axis=1, keepdims=True) + jnp.sum(p_n, axis=1, keepdims=True)
            o = jnp.dot(p_c.astype(BF16), cv, preferred_element_type=F32)
            o = o + jnp.dot(p_n.astype(BF16), v, preferred_element_type=F32)
            outs.append(o / denom)
        normed = _diff_normed(outs[0], outs[1], lam, 1) * g_ref[...] * (1.0 - lam_init)
        o_ref[0, :, hd * DIFF_VDIM:(hd + 1) * DIFF_VDIM] = normed.astype(BF16)


def _diff_sample(dqa, dka, dvb, cache_k, cache_v, lam_vecs, subln_g, lam_init):
    b, nm, t, _ = dqa.shape
    past = cache_k.shape[3]
    ck = jnp.transpose(cache_k, (0, 1, 2, 4, 5, 3)).reshape(cache_k.shape[:3] + (DIFF_VDIM, past))
    new = pl.BlockSpec((1, nm, t, LANES), lambda i: (i, 0, 0, 0))
    return pl.pallas_call(
        functools.partial(_diff_sample_kernel, t=t, past=past, lam_init=lam_init),
        grid=(b,),
        in_specs=[new, new, pl.BlockSpec((1, DIFF_HEADS, t, DIFF_VDIM), lambda i: (i, 0, 0, 0)),
                  pl.BlockSpec((1, 1, DIFF_HEADS, DIFF_VDIM, past), lambda i: (0, i, 0, 0, 0)),
                  pl.BlockSpec((1, 1, DIFF_HEADS, past, DIFF_VDIM), lambda i: (0, i, 0, 0, 0)),
                  pl.BlockSpec(lam_vecs.shape, lambda i: (0, 0)),
                  pl.BlockSpec(subln_g.shape, lambda i: (0, 0))],
        out_specs=pl.BlockSpec((1, t, DIFF_WIDTH), lambda i: (i, 0, 0)),
        out_shape=jax.ShapeDtypeStruct((b, t, DIFF_WIDTH), BF16),
        compiler_params=_params(("arbitrary",)),
        name="diff_sample",
    )(dqa, dka, dvb, ck, cache_v, lam_vecs, subln_g)


def _out_proj_kernel(fo_ref, *refs, alpha, n_diff):
    do_refs, (fg_ref, dg_ref, x_ref, mod_ref, w_ref, g_ref, b_ref, y_ref) = refs[:n_diff], refs[n_diff:]
    fox = fo_ref[0] * fg_ref[0]
    diff = jnp.concatenate([r[0] for r in do_refs], axis=1) * dg_ref[0]
    branch = jnp.dot(fox, w_ref[0:FOX_WIDTH, :], preferred_element_type=F32)
    branch = branch + jnp.dot(diff, w_ref[FOX_WIDTH:, :], preferred_element_type=F32)
    r = alpha * x_ref[0] + mod_ref[0, 2:3, :] * branch
    mu = jnp.mean(r, axis=-1, keepdims=True)
    rc = r - mu
    var = jnp.mean(rc * rc, axis=-1, keepdims=True)
    y_ref[0] = rc * lax.rsqrt(var + LN_EPS) * g_ref[...] + b_ref[...]


def _out_proj(fox_o, diff_parts, fg, dg, x, mod, w_out, ln_g, ln_b, alpha):
    b, t, d = x.shape
    tm = 1024 if t % 1024 == 0 else min(t, _TILE)
    rows = pl.BlockSpec((1, tm, FOX_WIDTH), lambda i, j: (i, j, 0))
    wide = pl.BlockSpec((1, tm, d), lambda i, j: (i, j, 0))

    def full(a):
        return pl.BlockSpec(a.shape, lambda i, j: (0,) * a.ndim)

    return pl.pallas_call(
        functools.partial(_out_proj_kernel, alpha=alpha, n_diff=len(diff_parts)),
        grid=(b, t // tm),
        in_specs=([rows] + [pl.BlockSpec((1, tm, p.shape[-1]), lambda i, j: (i, j, 0)) for p in diff_parts]
                  + [rows, rows, wide, pl.BlockSpec((1, 3, d), lambda i, j: (i, 0, 0)),
                     full(w_out), full(ln_g), full(ln_b)]),
        out_specs=wide,
        out_shape=jax.ShapeDtypeStruct((b, t, d), F32),
        compiler_params=_params(("arbitrary", "arbitrary")),
        name="out_proj",
    )(fox_o, *diff_parts, fg, dg, x, mod, w_out, ln_g, ln_b)


def _pack_w_in(w_in):
    fw = FOX_WIDTH
    o = 0
    fq = w_in[:, o:o + fw]; o += fw
    fk = w_in[:, o:o + fw]; o += fw
    fv = w_in[:, o:o + fw]; o += fw
    ff = w_in[:, o:o + FOX_HEADS]; o += FOX_HEADS
    fg = w_in[:, o:o + fw]; o += fw
    dq = w_in[:, o:o + fw]; o += fw
    dk = w_in[:, o:o + fw]; o += fw
    dv = w_in[:, o:o + fw]; o += fw
    dg = w_in[:, o:o + fw]
    qs = HEAD_DIM ** -0.5 * LOG2E
    return jnp.concatenate([fq * qs, fk, fv, fg, dq * qs, dk, dv, dg,
                            jnp.tile(ff, (1, LANES // FOX_HEADS))], axis=1).astype(BF16)


def kernel(x_prompt, x_sample, cache_fox_k, cache_fox_v, cache_fox_logf, cache_diff_k, cache_diff_v,
           c_prompt, c_sample, w_ada, b_ada, w_in, b_f, lambda_q1, lambda_k1, lambda_q2, lambda_k2,
           subln_g, w_out, ln_g, ln_b):
    depth = w_in.shape[0]
    assert depth == 1, "single-layer step"
    layer = 0
    lam_init = 0.8 - 0.6 * math.exp(-0.3 * layer)
    alpha = (2 * depth) ** 0.25
    bp, bs = c_prompt.shape[0], c_sample.shape[0]
    d = x_prompt.shape[-1]
    past = cache_fox_k.shape[3]

    c_all = jnp.concatenate([c_prompt, c_sample, jnp.zeros((-(bp + bs) % 8, d), F32)], axis=0)
    mod = _ada(c_all, w_ada[layer], b_ada[layer]).reshape(c_all.shape[0], 3, d)
    mod_p, mod_s = mod[:bp], mod[bp:bp + bs]

    w = _pack_w_in(w_in[layer])
    bf16x = jnp.tile(b_f[layer], LANES // FOX_HEADS).reshape(1, LANES)
    seg = np.repeat(np.eye(FOX_WIDTH // HEAD_DIM, LANES, dtype=np.float32), HEAD_DIM, axis=0)
    seg = jnp.asarray(seg, BF16)
    lam_vecs = jnp.stack([lambda_q1[layer], lambda_k1[layer], lambda_q2[layer], lambda_k2[layer]])
    g = subln_g[layer].reshape(1, DIFF_VDIM)
    w_o = w_out[layer].astype(BF16)
    lg, lb = ln_g[layer].reshape(1, d), ln_b[layer].reshape(1, d)

    def run(x, mod_x, pos_offset, attend):
        (fk, fv, dk, dv, logf, fqa, fka, fva, fg, dqa, dka, dvb, dg, stats) = _in_proj(
            x, mod_x, w, bf16x, seg, pos_offset)
        stats = jnp.transpose(stats[..., :FOX_WIDTH // HEAD_DIM], (0, 3, 2, 1))
        fox_o, diff_o = attend(fqa, fka, fva, dqa, dka, dvb, stats)
        y = _out_proj(fox_o, diff_o, fg, dg, x, mod_x, w_o, lg, lb, alpha)
        b, t = x.shape[:2]
        states = (fk[None], fv[None], logf[None],
                  dk.reshape(1, b, DIFF_HEADS, t, 2, HEAD_DIM), dv[None])
        return y, states

    def prompt_attend(fqa, fka, fva, dqa, dka, dvb, stats):
        return (_fox_attention(fqa, fka, fva, stats),
                _diff_attention(dqa, dka, dvb, stats, lam_vecs, g, lam_init))

    def sample_attend(fqa, fka, fva, dqa, dka, dvb, stats):
        del stats
        return (_fox_sample(fqa, fka, fva, cache_fox_k, cache_fox_v, cache_fox_logf[layer]),
                (_diff_sample(dqa, dka, dvb, cache_diff_k, cache_diff_v, lam_vecs, g, lam_init),))

    yp, sp = run(x_prompt, mod_p, 0, prompt_attend)
    ys, ss = run(x_sample, mod_s, past, sample_attend)
    return (yp, ys) + sp + ss
```

```python
import functools
import math

import jax
import jax.numpy as jnp
import numpy as np
from jax import lax
from jax.experimental import pallas as pl
from jax.experimental.pallas import tpu as pltpu

F32 = jnp.float32
BF16 = jnp.bfloat16

HEAD_DIM = 64
FOX_HEADS = 8
DIFF_HEADS = 4
DIFF_VDIM = 2 * HEAD_DIM
FOX_WIDTH = FOX_HEADS * HEAD_DIM
DIFF_WIDTH = DIFF_HEADS * DIFF_VDIM
CHUNK = 64
LN_EPS = 1e-5
RMS_EPS = 1e-5
LANES = 128
NEG_INF = float("-inf")
LOG2E = math.log2(math.e)

_SEC = {name: i * FOX_WIDTH for i, name in enumerate(("fq", "fk", "fv", "fg", "dq", "dk", "dv", "dg"))}
_FF_OFF = 8 * FOX_WIDTH

_VMEM_LIMIT = 56 * 1024 * 1024
_TILE = 512


def _params(sem):
    return pltpu.CompilerParams(dimension_semantics=sem, vmem_limit_bytes=_VMEM_LIMIT)


def _ada_kernel(c_ref, w_ref, b_ref, o_ref):
    c = c_ref[...]
    s = c * jax.nn.sigmoid(c)
    o_ref[...] = jnp.dot(s, w_ref[...], preferred_element_type=F32) + b_ref[...]


def _ada(c_all, w_ada, b_ada):
    rows, d = c_all.shape
    n = w_ada.shape[1]
    tn = 512
    return pl.pallas_call(
        _ada_kernel,
        grid=(n // tn,),
        in_specs=[pl.BlockSpec((rows, d), lambda j: (0, 0)),
                  pl.BlockSpec((d, tn), lambda j: (0, j)),
                  pl.BlockSpec((1, tn), lambda j: (0, j))],
        out_specs=pl.BlockSpec((rows, tn), lambda j: (0, j)),
        out_shape=jax.ShapeDtypeStruct((rows, n), F32),
        compiler_params=_params(("arbitrary",)),
        name="ada",
    )(c_all, w_ada, b_ada.reshape(1, n))


def _split3(x):
    hi = x.astype(BF16)
    r1 = x - hi.astype(F32)
    mid = r1.astype(BF16)
    lo = (r1 - mid.astype(F32)).astype(BF16)
    return hi, mid, lo


def _in_proj_kernel(x_ref, mod_ref, w_ref, bf_ref, seg_ref,
                    fk_ref, fv_ref, dk_ref, dv_ref, logf_ref,
                    fqa_ref, fka_ref, fva_ref, fg_ref, dqa_ref, dka_ref, dvb_ref, dg_ref, stat_ref,
                    carry_ref, *, tm, pos_offset):
    ti = pl.program_id(1)
    x = x_ref[0]
    shift = mod_ref[0, 0:1, :]
    scale = mod_ref[0, 1:2, :]
    h = (x * (1.0 + scale) + shift).astype(BF16)

    def proj(name, width=FOX_WIDTH):
        off = _SEC[name] if name in _SEC else _FF_OFF
        return jnp.dot(h, w_ref[:, off:off + width], preferred_element_type=F32)

    lane = lax.broadcasted_iota(jnp.int32, (tm, LANES), 1)
    row = lax.broadcasted_iota(jnp.int32, (tm, LANES), 0)
    lower = lane < HEAD_DIM
    grp = (lane // 8) % 4

    def by_group(a, b, c):
        return jnp.where(grp == 0, a, jnp.where(grp == 1, b, jnp.where(grp == 2, c, jnp.zeros_like(a))))

    def max_sq_norm(z):
        zr = z.astype(BF16).astype(F32)
        sq = zr * zr
        if tm % 32 == 0:
            sq = jnp.maximum(sq[0:tm // 2], sq[tm // 2:tm])
        sq = jnp.dot(sq.astype(BF16), seg_ref[...], preferred_element_type=F32)
        return jnp.max(sq, axis=0, keepdims=True)

    logf = jax.nn.log_sigmoid(proj("ff", LANES) + bf_ref[...])
    logf_ref[0] = logf.T[0:FOX_HEADS] if tm % LANES == 0 else logf

    dq, dk, dv = proj("dq"), proj("dk"), proj("dv")
    pos = (row + (ti * tm + pos_offset)).astype(F32)
    sub = lane % HEAD_DIM
    for hd in range(DIFF_HEADS):
        slope = 2.0 ** (-8.0 * (hd + 1) / DIFF_HEADS)
        sl = slice(hd * LANES, (hd + 1) * LANES)
        p0, p1, p2 = (a.astype(F32) for a in _split3((slope * LOG2E) * pos))
        alibi = jnp.where(sub % 3 == 0, p0, jnp.where(sub % 3 == 1, p1, p2))
        alibi_q = jnp.where(sub < 3, -alibi, (sub < 6).astype(F32))
        alibi_k = jnp.where(sub < 3, 1.0, jnp.where(sub < 6, alibi, 0.0))
        dqa_ref[0, 2 * hd] = jnp.where(lower, dq[:, sl], alibi_q).astype(BF16)
        dqa_ref[0, 2 * hd + 1] = jnp.where(lower, alibi_q, dq[:, sl]).astype(BF16)
        dka_ref[0, 2 * hd] = jnp.where(lower, dk[:, sl], alibi_k).astype(BF16)
        dka_ref[0, 2 * hd + 1] = jnp.where(lower, alibi_k, dk[:, sl]).astype(BF16)
        if tm % LANES:
            dk_ref[0, hd] = dk[:, sl]
        dv_ref[0, hd] = dv[:, sl]
        dvb_ref[0, hd] = dv[:, sl].astype(BF16)
    if tm % LANES == 0:
        dk_ref[0] = dk.T.reshape(dk_ref.shape[1:])
    dg_ref[0] = jax.nn.silu(proj("dg")).astype(BF16)
    diff_norms = [max_sq_norm(dq), max_sq_norm(dk)]

    tri_r = lax.broadcasted_iota(jnp.int32, (tm, tm), 0)
    tri_c = lax.broadcasted_iota(jnp.int32, (tm, tm), 1)
    tri = (tri_c <= tri_r).astype(BF16)
    part = jnp.dot(tri, by_group(*_split3(logf)), preferred_element_type=F32)
    zq, zk, zv = proj("fq"), proj("fk"), proj("fv")
    local = part + pltpu.roll(part, 8, 1) + pltpu.roll(part, 16, 1) + pltpu.roll(part, 24, 1)

    @pl.when(ti == 0)
    def _():
        carry_ref[...] = jnp.zeros_like(carry_ref)

    cum = local + carry_ref[...]
    carry_ref[...] = cum[tm - 1:tm, :]
    cum2 = cum * LOG2E
    pieces = by_group(*(a.astype(F32) for a in _split3(cum2)))
    behind = pltpu.roll(pieces, 24, 1)
    u = lane % HEAD_DIM
    ext_q = jnp.where(u < 24, pieces, (u < 48).astype(F32))

    def ext_k(h):
        mine = u % 8 == h
        return jnp.where(mine & (u < 24), 1.0, jnp.where(mine & (u < 48), -behind, 0.0))

    fox_norms = [max_sq_norm(zq), max_sq_norm(zk)]
    e_even = (lane == HEAD_DIM).astype(F32)
    for j in range(FOX_HEADS // 2):
        sl = slice(j * LANES, (j + 1) * LANES)
        fqa_ref[0, 2 * j] = jnp.where(lower, zq[:, sl], ext_q).astype(BF16)
        fqa_ref[0, 2 * j + 1] = jnp.where(lower, ext_q, zq[:, sl]).astype(BF16)
        fka_ref[0, 2 * j] = jnp.where(lower, zk[:, sl], ext_k(2 * j)).astype(BF16)
        fka_ref[0, 2 * j + 1] = jnp.where(lower, ext_k(2 * j + 1), zk[:, sl]).astype(BF16)
        fva_ref[0, 2 * j] = jnp.where(lower, zv[:, sl], e_even).astype(BF16)
        fva_ref[0, 2 * j + 1] = jnp.where(lower, pltpu.roll(zv[:, sl], HEAD_DIM, 1), e_even).astype(BF16)
    if tm % LANES == 0:
        fk_ref[0] = zk.T.reshape(fk_ref.shape[1:])
        fv_ref[0] = zv.T.reshape(fv_ref.shape[1:])
    else:
        for hd in range(FOX_HEADS):
            fk_ref[0, hd] = zk[:, hd * HEAD_DIM:(hd + 1) * HEAD_DIM]
            fv_ref[0, hd] = zv[:, hd * HEAD_DIM:(hd + 1) * HEAD_DIM]
    fg_ref[0] = jax.nn.silu(proj("fg")).astype(BF16)
    stat_ref[0, 0] = jnp.concatenate(
        fox_norms + diff_norms + [cum2[0:1], cum2[tm - 1:tm], jnp.zeros((2, LANES), F32)],
        axis=0)


def _in_proj(x, mod, w, bf16x, seg, pos_offset):
    b, t, d = x.shape
    tm = min(t, _TILE)
    nt = t // tm
    head_major = tm % LANES == 0

    def full(a):
        return pl.BlockSpec(a.shape, lambda i, j: (0,) * a.ndim)

    def heads(n, width):
        return pl.BlockSpec((1, n, tm, width), lambda i, j: (i, 0, j, 0))

    def cache_shape(n, width):
        return jax.ShapeDtypeStruct((b, n, width, t) if head_major else (b, n, t, width), F32)

    def cache_spec(n, width):
        return pl.BlockSpec((1, n, width, tm), lambda i, j: (i, 0, 0, j)) if head_major else heads(n, width)

    rows = pl.BlockSpec((1, tm, FOX_WIDTH), lambda i, j: (i, j, 0))
    out_shape = (
        cache_shape(FOX_HEADS, HEAD_DIM),
        cache_shape(FOX_HEADS, HEAD_DIM),
        cache_shape(DIFF_HEADS, DIFF_VDIM),
        jax.ShapeDtypeStruct((b, DIFF_HEADS, t, DIFF_VDIM), F32),
        jax.ShapeDtypeStruct((b, FOX_HEADS, t) if head_major else (b, t, LANES), F32),
        jax.ShapeDtypeStruct((b, FOX_HEADS, t, LANES), BF16),
        jax.ShapeDtypeStruct((b, FOX_HEADS, t, LANES), BF16),
        jax.ShapeDtypeStruct((b, FOX_HEADS, t, LANES), BF16),
        jax.ShapeDtypeStruct((b, t, FOX_WIDTH), BF16),
        jax.ShapeDtypeStruct((b, 2 * DIFF_HEADS, t, LANES), BF16),
        jax.ShapeDtypeStruct((b, 2 * DIFF_HEADS, t, LANES), BF16),
        jax.ShapeDtypeStruct((b, DIFF_HEADS, t, DIFF_VDIM), BF16),
        jax.ShapeDtypeStruct((b, t, DIFF_WIDTH), BF16),
        jax.ShapeDtypeStruct((b, nt, 8, LANES), F32),
    )
    out_specs = (
        cache_spec(FOX_HEADS, HEAD_DIM), cache_spec(FOX_HEADS, HEAD_DIM),
        cache_spec(DIFF_HEADS, DIFF_VDIM), heads(DIFF_HEADS, DIFF_VDIM),
        (pl.BlockSpec((1, FOX_HEADS, tm), lambda i, j: (i, 0, j)) if head_major
         else pl.BlockSpec((1, tm, LANES), lambda i, j: (i, j, 0))),
        heads(FOX_HEADS, LANES), heads(FOX_HEADS, LANES), heads(FOX_HEADS, LANES), rows,
        heads(2 * DIFF_HEADS, LANES), heads(2 * DIFF_HEADS, LANES), heads(DIFF_HEADS, DIFF_VDIM), rows,
        pl.BlockSpec((1, 1, 8, LANES), lambda i, j: (i, j, 0, 0)),
    )
    outs = pl.pallas_call(
        functools.partial(_in_proj_kernel, tm=tm, pos_offset=pos_offset),
        grid=(b, nt),
        in_specs=[pl.BlockSpec((1, tm, d), lambda i, j: (i, j, 0)),
                  pl.BlockSpec((1, 3, d), lambda i, j: (i, 0, 0)),
                  full(w), full(bf16x), full(seg)],
        out_specs=out_specs,
        out_shape=out_shape,
        scratch_shapes=[pltpu.VMEM((1, LANES), F32)],
        compiler_params=_params(("arbitrary", "arbitrary")),
        name="in_proj",
    )(x, mod, w, bf16x, seg)
    if head_major:
        outs = tuple(jnp.swapaxes(o, 2, 3) for o in outs[:3]) + outs[3:]
    else:
        outs = outs[:4] + (jnp.swapaxes(outs[4][:, :, :FOX_HEADS], 1, 2),) + outs[5:]
    return outs


_NT = (((1,), (1,)), ((), ()))
_TN = (((0,), (0,)), ((), ()))
_STAGES = 4
_FOX_V_ROWS = HEAD_DIM + 16
_FOX_GROUP = 4


_SKIP_MARGIN = 136.0
_STAT_FOX_Q, _STAT_FOX_K, _STAT_DIFF_Q, _STAT_DIFF_K, _STAT_CUM_FIRST, _STAT_CUM_LAST = range(6)


def _diagonal_blocks(tq):
    assert tq % (2 * LANES) == 0
    return ((tq // 2, 0), (tq, tq // 2))


def _dot_bound(stat, q_row, k_row, qi):
    tile = lax.broadcasted_iota(jnp.int32, (1, stat.shape[-1]), 1)
    qn2 = jnp.sum(jnp.where(tile == qi, stat[q_row:q_row + 1], 0.0))
    return jnp.sqrt(qn2 * stat[k_row:k_row + 1]) * 1.01 + 1.0


def _first_needed_tile(qi, bounds, m_ref):
    first = qi
    for c, bound in enumerate(bounds):
        tile = lax.broadcasted_iota(jnp.int32, bound.shape, 1)
        needed = (bound >= jnp.min(m_ref[c]) - _SKIP_MARGIN) & (tile < qi)
        first = jnp.minimum(first, jnp.min(jnp.where(needed, tile, qi)))
    return first


def _pipelined_tiles(qi, n_chains, scores, absorb, first_needed, sa_ref, sb_ref):
    chains = range(n_chains)
    bufs = (sa_ref, sb_ref)
    for c in chains:
        scores(qi, sa_ref, c, True)
        scores(jnp.maximum(qi - 1, 0), sb_ref, c)
        absorb(sa_ref, qi, c, True)
    n = qi - first_needed()

    def stages(top, count, more):
        for s in range(count):
            for c in chains:
                if more or s + 1 < count:
                    scores(jnp.maximum(top - s - 1, 0), bufs[s % 2], c)
                absorb(bufs[(s + 1) % 2], top - s, c, False)

    def body(i, carry):
        stages(qi - 1 - _STAGES * i, _STAGES, True)
        return carry

    trips = jnp.maximum(n - 1, 0) // _STAGES
    lax.fori_loop(0, trips, body, 0)
    rem = n - _STAGES * trips
    for count in range(1, _STAGES + 1):
        @pl.when(rem == count)
        def _(count=count):
            stages(qi - 1 - _STAGES * trips, count, False)


def _fox_kernel(q_ref, k_ref, v_ref, stat_ref, o_ref, sa_ref, sb_ref, m_ref, acc_ref, *, tq):
    qi = pl.program_id(2)
    half = tq // 2

    def scores(ki, dst, hh, diagonal=False):
        start = pl.multiple_of(ki * tq, tq)
        if diagonal:
            for keys, q0 in _diagonal_blocks(tq):
                k = k_ref[0, hh, pl.ds(start, keys), :]
                dst[hh, 0:keys, q0:q0 + half] = lax.dot_general(k, q_ref[0, hh, q0:q0 + half, :], _NT,
                                                                preferred_element_type=F32)
        else:
            k = k_ref[0, hh, pl.ds(start, tq), :]
            dst[hh] = lax.dot_general(k, q_ref[0, hh], _NT, preferred_element_type=F32)

    def absorb(src, ki, hh, diagonal):
        start = pl.multiple_of(ki * tq, tq)
        if diagonal:
            for keys, q0 in _diagonal_blocks(tq):
                s = src[hh, 0:keys, q0:q0 + half]
                key = lax.broadcasted_iota(jnp.int32, s.shape, 0)
                qry = lax.broadcasted_iota(jnp.int32, s.shape, 1) + q0
                s = jnp.where(key <= qry, s, NEG_INF)
                m_new = jnp.max(s, axis=0, keepdims=True)
                p = jnp.exp2(s - m_new)
                v = v_ref[0, hh, pl.ds(start, keys), 0:_FOX_V_ROWS]
                acc_ref[hh, :, q0:q0 + half] = lax.dot_general(v, p.astype(BF16), _TN, preferred_element_type=F32)
                m_ref[hh, :, q0:q0 + half] = m_new
            return
        s = src[hh]
        m_prev = m_ref[hh]
        m_new = jnp.maximum(m_prev, jnp.max(s, axis=0, keepdims=True))
        p = jnp.exp2(s - m_new)
        alpha = jnp.exp2(m_prev - m_new)
        v = v_ref[0, hh, pl.ds(start, tq), 0:_FOX_V_ROWS]
        pv = lax.dot_general(v, p.astype(BF16), _TN, preferred_element_type=F32)
        acc_ref[hh] = alpha * acc_ref[hh] + pv
        m_ref[hh] = m_new

    def first_needed():
        bounds = []
        for hh in range(_FOX_GROUP):
            stat = stat_ref[0, hh]
            tile = lax.broadcasted_iota(jnp.int32, (1, stat.shape[-1]), 1)
            cum_q = jnp.sum(jnp.where(tile == qi, stat[_STAT_CUM_FIRST:_STAT_CUM_FIRST + 1], 0.0))
            decay = cum_q - stat[_STAT_CUM_LAST:_STAT_CUM_LAST + 1]
            bounds.append(_dot_bound(stat, _STAT_FOX_Q, _STAT_FOX_K, qi) + decay)
        return _first_needed_tile(qi, bounds, m_ref)

    _pipelined_tiles(qi, _FOX_GROUP, scores, absorb, first_needed, sa_ref, sb_ref)
    outs = [acc_ref[hh, 0:HEAD_DIM] / acc_ref[hh, HEAD_DIM:HEAD_DIM + 1] for hh in range(_FOX_GROUP)]
    o_ref[0] = jnp.concatenate(outs, axis=0).T.astype(BF16)


def _fox_attention(fqa, fka, fva, stats):
    b, nh, t, _ = fqa.shape
    tq = min(t, _TILE)
    g = _FOX_GROUP
    return pl.pallas_call(
        functools.partial(_fox_kernel, tq=tq),
        grid=(b, nh // g, t // tq),
        in_specs=[pl.BlockSpec((1, g, tq, LANES), lambda i, j, q: (i, j, q, 0)),
                  pl.BlockSpec((1, g, t, LANES), lambda i, j, q: (i, j, 0, 0)),
                  pl.BlockSpec((1, g, t, LANES), lambda i, j, q: (i, j, 0, 0)),
                  pl.BlockSpec((1, g) + stats.shape[2:], lambda i, j, q: (i, j, 0, 0))],
        out_specs=pl.BlockSpec((1, tq, g * HEAD_DIM), lambda i, j, q: (i, q, j)),
        out_shape=jax.ShapeDtypeStruct((b, t, FOX_WIDTH), BF16),
        scratch_shapes=[pltpu.VMEM((g, tq, tq), F32), pltpu.VMEM((g, tq, tq), F32),
                        pltpu.VMEM((g, 1, tq), F32), pltpu.VMEM((g, _FOX_V_ROWS, tq), F32)],
        compiler_params=_params(("arbitrary", "arbitrary", "arbitrary")),
        name="fox_attn",
    )(fqa, fka, fva, stats)


def _alibi_slope(head):
    return jnp.exp2(jnp.full((1, 1), -8.0 / DIFF_HEADS, F32) * (head + 1).astype(F32))


def _lambda(lam_ref, lam_init):
    lv = lam_ref[...]
    e1 = jnp.exp(jnp.sum(lv[0:1] * lv[1:2], axis=1, keepdims=True))
    e2 = jnp.exp(jnp.sum(lv[2:3] * lv[3:4], axis=1, keepdims=True))
    return e1 - e2 + lam_init


def _diff_normed(o0, o1, lam, axis):
    df = o0 - lam * o1
    return df * lax.rsqrt(jnp.mean(df * df, axis=axis, keepdims=True) + RMS_EPS)


_ONES_ROWS = 16


def _diff_kernel(q_ref, k_ref, v_ref, stat_ref, lam_ref, g_ref, o_ref, sa_ref, sb_ref, m_ref, acc_ref, *, tq,
                 lam_init, first_head, group):
    qi = pl.program_id(2)
    half = tq // 2
    chains = 2 * group
    slopes = [_alibi_slope(first_head + pl.program_id(1) * group + h) for h in range(group)]

    def scores(ki, dst, c, diagonal=False):
        start = pl.multiple_of(ki * tq, tq)
        if diagonal:
            for keys, q0 in _diagonal_blocks(tq):
                k = k_ref[0, c, pl.ds(start, keys), :]
                dst[c, 0:keys, q0:q0 + half] = lax.dot_general(k, q_ref[0, c, q0:q0 + half, :], _NT,
                                                               preferred_element_type=F32)
        else:
            k = k_ref[0, c, pl.ds(start, tq), :]
            dst[c] = lax.dot_general(k, q_ref[0, c], _NT, preferred_element_type=F32)

    def values_t(c, start, keys):
        v = v_ref[0, c // 2, pl.ds(start, keys), :]
        return jnp.concatenate([v.T, jnp.ones((_ONES_ROWS, keys), BF16)], axis=0)

    def absorb(src, ki, c, diagonal):
        start = pl.multiple_of(ki * tq, tq)
        if diagonal:
            for keys, q0 in _diagonal_blocks(tq):
                s = src[c, 0:keys, q0:q0 + half]
                key = lax.broadcasted_iota(jnp.int32, s.shape, 0)
                qry = lax.broadcasted_iota(jnp.int32, s.shape, 1) + q0
                ahead = jnp.maximum(key - qry, 0).astype(F32)
                s = jnp.where((key // CHUNK) <= (qry // CHUNK), s - (2.0 * LOG2E * slopes[c // 2]) * ahead, NEG_INF)
                m_new = jnp.max(s, axis=0, keepdims=True)
                p = jnp.exp2(s - m_new)
                acc_ref[c, :, q0:q0 + half] = jnp.dot(values_t(c, start, keys), p.astype(BF16),
                                                      preferred_element_type=F32)
                m_ref[c, :, q0:q0 + half] = m_new
            return
        s = src[c]
        m_prev = m_ref[c]
        m_new = jnp.maximum(m_prev, jnp.max(s, axis=0, keepdims=True))
        p = jnp.exp2(s - m_new)
        alpha = jnp.exp2(m_prev - m_new)
        acc_ref[c] = alpha * acc_ref[c] + jnp.dot(values_t(c, start, tq), p.astype(BF16),
                                                  preferred_element_type=F32)
        m_ref[c] = m_new

    def first_needed():
        bounds = []
        for c in range(chains):
            stat = stat_ref[0, c]
            tile = lax.broadcasted_iota(jnp.int32, (1, stat.shape[-1]), 1)
            gap = ((qi - 1 - tile) * tq + 1).astype(F32)
            bounds.append(_dot_bound(stat, _STAT_DIFF_Q, _STAT_DIFF_K, qi) - (LOG2E * slopes[c // 2]) * gap)
        return _first_needed_tile(qi, bounds, m_ref)

    _pipelined_tiles(qi, chains, scores, absorb, first_needed, sa_ref, sb_ref)
    lam = _lambda(lam_ref, lam_init)
    for h in range(group):
        o0, o1 = (acc_ref[2 * h + c, :DIFF_VDIM] / acc_ref[2 * h + c, DIFF_VDIM:DIFF_VDIM + 1] for c in range(2))
        normed = _diff_normed(o0, o1, lam, 0).T * g_ref[...] * (1.0 - lam_init)
        o_ref[0, :, h * DIFF_VDIM:(h + 1) * DIFF_VDIM] = normed.astype(BF16)


_DIFF_GROUPS = ((0, 1, 2), (2, 2, 1))


def _diff_attention(dqa, dka, dvb, stats, lam_vecs, subln_g, lam_init):
    b, _, t, _ = dqa.shape
    tq = min(t, _TILE)
    outs = []
    for first_head, g, n_groups in _DIFF_GROUPS:
        off = first_head // g

        def heads(block, off=off):
            return pl.BlockSpec(block, lambda i, j, q: (i, j + off, 0, 0))

        outs.append(pl.pallas_call(
            functools.partial(_diff_kernel, tq=tq, lam_init=lam_init, first_head=first_head, group=g),
            grid=(b, n_groups, t // tq),
            in_specs=[pl.BlockSpec((1, 2 * g, tq, LANES), lambda i, j, q, off=off: (i, j + off, q, 0)),
                      heads((1, 2 * g, t, LANES)), heads((1, g, t, DIFF_VDIM)),
                      heads((1, 2 * g) + stats.shape[2:]),
                      pl.BlockSpec(lam_vecs.shape, lambda i, j, q: (0, 0)),
                      pl.BlockSpec(subln_g.shape, lambda i, j, q: (0, 0))],
            out_specs=pl.BlockSpec((1, tq, g * DIFF_VDIM), lambda i, j, q: (i, q, j)),
            out_shape=jax.ShapeDtypeStruct((b, t, n_groups * g * DIFF_VDIM), BF16),
            scratch_shapes=[pltpu.VMEM((2 * g, tq, tq), F32), pltpu.VMEM((2 * g, tq, tq), F32),
                            pltpu.VMEM((2 * g, 1, tq), F32),
                            pltpu.VMEM((2 * g, DIFF_VDIM + _ONES_ROWS, tq), F32)],
            compiler_params=_params(("arbitrary", "arbitrary", "arbitrary")),
            name="diff_attn",
        )(dqa, dka, dvb, stats, lam_vecs, subln_g))
    return tuple(outs)


def _prefix_sum_lanes(x):
    n = x.shape[-1]
    lane = lax.broadcasted_iota(jnp.int32, x.shape, x.ndim - 1)
    sh = 1
    while sh < n:
        x = x + jnp.where(lane >= sh, pltpu.roll(x, sh, x.ndim - 1), 0.0)
        sh *= 2
    return x


def _fox_sample_kernel(q_ref, k_ref, v_ref, ck_ref, cv_ref, clf_ref, o_ref, *, t, past):
    r = lax.broadcasted_iota(jnp.int32, (t, t), 0)
    c = lax.broadcasted_iota(jnp.int32, (t, t), 1)
    cum_c = _prefix_sum_lanes(clf_ref[0])
    suffix = (cum_c[:, past - 1:past] - cum_c) * LOG2E
    for hh in range(FOX_HEADS):
        data = slice(0, HEAD_DIM) if hh % 2 == 0 else slice(HEAD_DIM, LANES)
        ext = HEAD_DIM if hh % 2 == 0 else 0
        qa = q_ref[0, hh]
        qf = qa.astype(F32)
        cum_q = sum(qf[:, ext + 8 * p + hh:ext + 8 * p + hh + 1] for p in range(3))
        s_c = jnp.dot(qa[:, data], ck_ref[0, 0, hh].astype(BF16), preferred_element_type=F32)
        s_c = s_c + cum_q + suffix[hh:hh + 1]
        s_n = lax.dot_general(qa, k_ref[0, hh], _NT, preferred_element_type=F32)
        s_n = jnp.where(c <= r, s_n, NEG_INF)
        m = jnp.maximum(jnp.max(s_c, axis=1, keepdims=True), jnp.max(s_n, axis=1, keepdims=True))
        p_c = jnp.exp2(s_c - m)
        p_n = jnp.exp2(s_n - m)
        denom = jnp.sum(p_c, axis=1, keepdims=True) + jnp.sum(p_n, axis=1, keepdims=True)
        o = lax.dot_general(p_c.astype(BF16), cv_ref[0, 0, hh].astype(BF16), _NT, preferred_element_type=F32)
        o = o + jnp.dot(p_n.astype(BF16), v_ref[0, hh][:, 0:HEAD_DIM], preferred_element_type=F32)
        o_ref[0, :, hh * HEAD_DIM:(hh + 1) * HEAD_DIM] = (o / denom).astype(BF16)


def _fox_sample(fqa, fka, fva, cache_k, cache_v, cache_logf):
    b, nh, t, _ = fqa.shape
    past = cache_k.shape[3]
    cache_k, cache_v = jnp.swapaxes(cache_k, 3, 4), jnp.swapaxes(cache_v, 3, 4)
    new = pl.BlockSpec((1, nh, t, LANES), lambda i: (i, 0, 0, 0))
    cache = pl.BlockSpec((1, 1, nh, HEAD_DIM, past), lambda i: (0, i, 0, 0, 0))
    return pl.pallas_call(
        functools.partial(_fox_sample_kernel, t=t, past=past),
        grid=(b,),
        in_specs=[new, new, new, cache, cache, pl.BlockSpec((1, nh, past), lambda i: (i, 0, 0))],
        out_specs=pl.BlockSpec((1, t, FOX_WIDTH), lambda i: (i, 0, 0)),
        out_shape=jax.ShapeDtypeStruct((b, t, FOX_WIDTH), BF16),
        compiler_params=_params(("arbitrary",)),
        name="fox_sample",
    )(fqa, fka, fva, cache_k, cache_v, cache_logf)


def _diff_sample_kernel(q_ref, k_ref, v_ref, ck_ref, cv_ref, lam_ref, g_ref, o_ref, *, t, past, lam_init):
    lane = lax.broadcasted_iota(jnp.int32, (t, LANES), 1)
    pos_q = past + lax.broadcasted_iota(jnp.int32, (t, past), 0)
    pos_c = lax.broadcasted_iota(jnp.int32, (t, past), 1)
    dist_c = jnp.abs(pos_q - pos_c).astype(F32)
    vis_c = (pos_c // CHUNK) <= (pos_q // CHUNK)
    pq_n = past + lax.broadcasted_iota(jnp.int32, (t, t), 0)
    pk_n = past + lax.broadcasted_iota(jnp.int32, (t, t), 1)
    dist_n = jnp.abs(pq_n - pk_n).astype(F32)
    vis_n = (pk_n // CHUNK) <= (pq_n // CHUNK)
    lam = _lambda(lam_ref, lam_init)
    for hd in range(DIFF_HEADS):
        slope = LOG2E * 2.0 ** (-8.0 * (hd + 1) / DIFF_HEADS)
        ck = ck_ref[0, 0, hd].astype(BF16)
        cv = cv_ref[0, 0, hd].astype(BF16)
        v = v_ref[0, hd]
        outs = []
        for c in range(2):
            half = (lane < HEAD_DIM) if c == 0 else (lane >= HEAD_DIM)
            qz = jnp.where(half, q_ref[0, 2 * hd + c], jnp.zeros((), BF16))
            kz = jnp.where(half, k_ref[0, 2 * hd + c], jnp.zeros((), BF16))
            s_c = jnp.dot(qz, ck, preferred_element_type=F32)
            s_c = jnp.where(vis_c, s_c - slope * dist_c, NEG_INF)
            s_n = lax.dot_general(qz, kz, _NT, preferred_element_type=F32)
            s_n = jnp.where(vis_n, s_n - slope * dist_n, NEG_INF)
            m = jnp.maximum(jnp.max(s_c, axis=1, keepdims=True), jnp.max(s_n, axis=1, keepdims=True))
            p_c = jnp.exp2(s_c - m)
            p_n = jnp.exp2(s_n - m)
            denom = jnp.sum(p_c, axis=1, keepdims=True) + jnp.sum(p_n, axis=1, keepdims=True)
            o = jnp.dot(p_c.astype(BF16), cv, preferred_element_type=F32)
            o = o + jnp.dot(p_n.astype(BF16), v, preferred_element_type=F32)
            outs.append(o / denom)
        normed = _diff_normed(outs[0], outs[1], lam, 1) * g_ref[...] * (1.0 - lam_init)
        o_ref[0, :, hd * DIFF_VDIM:(hd + 1) * DIFF_VDIM] = normed.astype(BF16)


def _diff_sample(dqa, dka, dvb, cache_k, cache_v, lam_vecs, subln_g, lam_init):
    b, nm, t, _ = dqa.shape
    past = cache_k.shape[3]
    ck = jnp.transpose(cache_k, (0, 1, 2, 4, 5, 3)).reshape(cache_k.shape[:3] + (DIFF_VDIM, past))
    new = pl.BlockSpec((1, nm, t, LANES), lambda i: (i, 0, 0, 0))
    return pl.pallas_call(
        functools.partial(_diff_sample_kernel, t=t, past=past, lam_init=lam_init),
        grid=(b,),
        in_specs=[new, new, pl.BlockSpec((1, DIFF_HEADS, t, DIFF_VDIM), lambda i: (i, 0, 0, 0)),
                  pl.BlockSpec((1, 1, DIFF_HEADS, DIFF_VDIM, past), lambda i: (0, i, 0, 0, 0)),
                  pl.BlockSpec((1, 1, DIFF_HEADS, past, DIFF_VDIM), lambda i: (0, i, 0, 0, 0)),
                  pl.BlockSpec(lam_vecs.shape, lambda i: (0, 0)),
                  pl.BlockSpec(subln_g.shape, lambda i: (0, 0))],
        out_specs=pl.BlockSpec((1, t, DIFF_WIDTH), lambda i: (i, 0, 0)),
        out_shape=jax.ShapeDtypeStruct((b, t, DIFF_WIDTH), BF16),
        compiler_params=_params(("arbitrary",)),
        name="diff_sample",
    )(dqa, dka, dvb, ck, cache_v, lam_vecs, subln_g)


_X_SLOTS = 3


def _out_proj_kernel(fo_ref, *refs, alpha, n_diff, tm, tiles_per_batch, n_steps):
    do_refs = refs[:n_diff]
    fg_ref, dg_ref, x_hbm, mod_ref, w_ref, g_ref, b_ref, y_ref, x_buf, x_sem = refs[n_diff:]
    step = pl.program_id(0) * tiles_per_batch + pl.program_id(1)

    def x_copy(k):
        rows = pl.ds((k % tiles_per_batch) * tm, tm)
        return pltpu.make_async_copy(x_hbm.at[k // tiles_per_batch, rows, :], x_buf.at[k % _X_SLOTS],
                                     x_sem.at[k % _X_SLOTS])

    @pl.when(step == 0)
    def _():
        x_copy(step).start()
        if n_steps > 1:
            x_copy(step + 1).start()

    @pl.when(step + 2 < n_steps)
    def _():
        x_copy(step + 2).start()

    x_copy(step).wait()
    x = x_buf[step % _X_SLOTS]
    fox = fo_ref[0] * fg_ref[0]
    diff = jnp.concatenate([r[0] for r in do_refs], axis=1) * dg_ref[0]
    branch = jnp.dot(fox, w_ref[0:FOX_WIDTH, :], preferred_element_type=F32)
    branch = branch + jnp.dot(diff, w_ref[FOX_WIDTH:, :], preferred_element_type=F32)
    r = alpha * x + mod_ref[0, 2:3, :] * branch
    mu = jnp.mean(r, axis=-1, keepdims=True)
    rc = r - mu
    var = jnp.mean(rc * rc, axis=-1, keepdims=True)
    y_ref[0] = rc * lax.rsqrt(var + LN_EPS) * g_ref[...] + b_ref[...]


def _out_proj(fox_o, diff_parts, fg, dg, x, mod, w_out, ln_g, ln_b, alpha):
    b, t, d = x.shape
    tm = 1024 if t % 1024 == 0 else min(t, _TILE)
    rows = pl.BlockSpec((1, tm, FOX_WIDTH), lambda i, j: (i, j, 0))
    wide = pl.BlockSpec((1, tm, d), lambda i, j: (i, j, 0))

    def full(a):
        return pl.BlockSpec(a.shape, lambda i, j: (0,) * a.ndim)

    return pl.pallas_call(
        functools.partial(_out_proj_kernel, alpha=alpha, n_diff=len(diff_parts), tm=tm,
                          tiles_per_batch=t // tm, n_steps=b * (t // tm)),
        grid=(b, t // tm),
        in_specs=([rows] + [pl.BlockSpec((1, tm, p.shape[-1]), lambda i, j: (i, j, 0)) for p in diff_parts]
                  + [rows, rows, pl.BlockSpec(memory_space=pl.ANY),
                     pl.BlockSpec((1, 3, d), lambda i, j: (i, 0, 0)), full(w_out), full(ln_g), full(ln_b)]),
        out_specs=wide,
        out_shape=jax.ShapeDtypeStruct((b, t, d), F32),
        scratch_shapes=[pltpu.VMEM((_X_SLOTS, tm, d), F32), pltpu.SemaphoreType.DMA((_X_SLOTS,))],
        compiler_params=_params(("arbitrary", "arbitrary")),
        name="out_proj",
    )(fox_o, *diff_parts, fg, dg, x, mod, w_out, ln_g, ln_b)


def _pack_w_in(w_in):
    fw = FOX_WIDTH
    o = 0
    fq = w_in[:, o:o + fw]; o += fw
    fk = w_in[:, o:o + fw]; o += fw
    fv = w_in[:, o:o + fw]; o += fw
    ff = w_in[:, o:o + FOX_HEADS]; o += FOX_HEADS
    fg = w_in[:, o:o + fw]; o += fw
    dq = w_in[:, o:o + fw]; o += fw
    dk = w_in[:, o:o + fw]; o += fw
    dv = w_in[:, o:o + fw]; o += fw
    dg = w_in[:, o:o + fw]
    qs = HEAD_DIM ** -0.5 * LOG2E
    return jnp.concatenate([fq * qs, fk, fv, fg, dq * qs, dk, dv, dg,
                            jnp.tile(ff, (1, LANES // FOX_HEADS))], axis=1).astype(BF16)


def kernel(x_prompt, x_sample, cache_fox_k, cache_fox_v, cache_fox_logf, cache_diff_k, cache_diff_v,
           c_prompt, c_sample, w_ada, b_ada, w_in, b_f, lambda_q1, lambda_k1, lambda_q2, lambda_k2,
           subln_g, w_out, ln_g, ln_b):
    depth = w_in.shape[0]
    assert depth == 1, "single-layer step"
    layer = 0
    lam_init = 0.8 - 0.6 * math.exp(-0.3 * layer)
    alpha = (2 * depth) ** 0.25
    bp, bs = c_prompt.shape[0], c_sample.shape[0]
    d = x_prompt.shape[-1]
    past = cache_fox_k.shape[3]

    c_all = jnp.concatenate([c_prompt, c_sample, jnp.zeros((-(bp + bs) % 8, d), F32)], axis=0)
    mod = _ada(c_all, w_ada[layer], b_ada[layer]).reshape(c_all.shape[0], 3, d)
    mod_p, mod_s = mod[:bp], mod[bp:bp + bs]

    w = _pack_w_in(w_in[layer])
    bf16x = jnp.tile(b_f[layer], LANES // FOX_HEADS).reshape(1, LANES)
    seg = np.repeat(np.eye(FOX_WIDTH // HEAD_DIM, LANES, dtype=np.float32), HEAD_DIM, axis=0)
    seg = jnp.asarray(seg, BF16)
    lam_vecs = jnp.stack([lambda_q1[layer], lambda_k1[layer], lambda_q2[layer], lambda_k2[layer]])
    g = subln_g[layer].reshape(1, DIFF_VDIM)
    w_o = w_out[layer].astype(BF16)
    lg, lb = ln_g[layer].reshape(1, d), ln_b[layer].reshape(1, d)

    def run(x, mod_x, pos_offset, attend):
        (fk, fv, dk, dv, logf, fqa, fka, fva, fg, dqa, dka, dvb, dg, stats) = _in_proj(
            x, mod_x, w, bf16x, seg, pos_offset)
        stats = jnp.transpose(stats[..., :FOX_WIDTH // HEAD_DIM], (0, 3, 2, 1))
        fox_o, diff_o = attend(fqa, fka, fva, dqa, dka, dvb, stats)
        y = _out_proj(fox_o, diff_o, fg, dg, x, mod_x, w_o, lg, lb, alpha)
        b, t = x.shape[:2]
        states = (fk[None], fv[None], logf[None],
                  dk.reshape(1, b, DIFF_HEADS, t, 2, HEAD_DIM), dv[None])
        return y, states

    def prompt_attend(fqa, fka, fva, dqa, dka, dvb, stats):
        return (_fox_attention(fqa, fka, fva, stats),
                _diff_attention(dqa, dka, dvb, stats, lam_vecs, g, lam_init))

    def sample_attend(fqa, fka, fva, dqa, dka, dvb, stats):
        del stats
        return (_fox_sample(fqa, fka, fva, cache_fox_k, cache_fox_v, cache_fox_logf[layer]),
                (_diff_sample(dqa, dka, dvb, cache_diff_k, cache_diff_v, lam_vecs, g, lam_init),))

    yp, sp = run(x_prompt, mod_p, 0, prompt_attend)
    ys, ss = run(x_sample, mod_s, past, sample_attend)
    return (yp, ys) + sp + ss
```

```python
import functools
import math

import jax
import jax.numpy as jnp
import numpy as np
from jax import lax
from jax.experimental import pallas as pl
from jax.experimental.pallas import tpu as pltpu

F32 = jnp.float32
BF16 = jnp.bfloat16

HEAD_DIM = 64
FOX_HEADS = 8
DIFF_HEADS = 4
DIFF_VDIM = 2 * HEAD_DIM
FOX_WIDTH = FOX_HEADS * HEAD_DIM
DIFF_WIDTH = DIFF_HEADS * DIFF_VDIM
CHUNK = 64
LN_EPS = 1e-5
RMS_EPS = 1e-5
LANES = 128
NEG_INF = float("-inf")
LOG2E = math.log2(math.e)

_SEC = {name: i * FOX_WIDTH for i, name in enumerate(("fq", "fk", "fv", "fg", "dq", "dk", "dv", "dg"))}
_FF_OFF = 8 * FOX_WIDTH

_VMEM_LIMIT = 56 * 1024 * 1024
_TILE = 512


def _params(sem):
    return pltpu.CompilerParams(dimension_semantics=sem, vmem_limit_bytes=_VMEM_LIMIT)


def _ada_kernel(c_ref, w_ref, b_ref, o_ref):
    c = c_ref[...]
    s = c * jax.nn.sigmoid(c)
    o_ref[...] = jnp.dot(s, w_ref[...], preferred_element_type=F32) + b_ref[...]


def _ada(c_all, w_ada, b_ada):
    rows, d = c_all.shape
    n = w_ada.shape[1]
    tn = 512
    return pl.pallas_call(
        _ada_kernel,
        grid=(n // tn,),
        in_specs=[pl.BlockSpec((rows, d), lambda j: (0, 0)),
                  pl.BlockSpec((d, tn), lambda j: (0, j)),
                  pl.BlockSpec((1, tn), lambda j: (0, j))],
        out_specs=pl.BlockSpec((rows, tn), lambda j: (0, j)),
        out_shape=jax.ShapeDtypeStruct((rows, n), F32),
        compiler_params=_params(("arbitrary",)),
        name="ada",
    )(c_all, w_ada, b_ada.reshape(1, n))


def _split3(x):
    hi = x.astype(BF16)
    r1 = x - hi.astype(F32)
    mid = r1.astype(BF16)
    lo = (r1 - mid.astype(F32)).astype(BF16)
    return hi, mid, lo


def _in_proj_kernel(x_ref, mod_ref, w_ref, bf_ref, seg_ref,
                    fk_ref, fv_ref, dk_ref, dv_ref, logf_ref,
                    fqa_ref, fka_ref, fva_ref, fg_ref, dqa_ref, dka_ref, dvb_ref, dg_ref, stat_ref,
                    carry_ref, *, tm, pos_offset):
    ti = pl.program_id(1)
    x = x_ref[0]
    shift = mod_ref[0, 0:1, :]
    scale = mod_ref[0, 1:2, :]
    h = (x * (1.0 + scale) + shift).astype(BF16)

    def proj(name, width=FOX_WIDTH):
        off = _SEC[name] if name in _SEC else _FF_OFF
        return jnp.dot(h, w_ref[:, off:off + width], preferred_element_type=F32)

    lane = lax.broadcasted_iota(jnp.int32, (tm, LANES), 1)
    row = lax.broadcasted_iota(jnp.int32, (tm, LANES), 0)
    lower = lane < HEAD_DIM
    grp = (lane // 8) % 4

    def by_group(a, b, c):
        return jnp.where(grp == 0, a, jnp.where(grp == 1, b, jnp.where(grp == 2, c, jnp.zeros_like(a))))

    def max_sq_norm(z):
        zr = z.astype(BF16).astype(F32)
        sq = zr * zr
        if tm % 32 == 0:
            sq = jnp.maximum(sq[0:tm // 2], sq[tm // 2:tm])
        sq = jnp.dot(sq.astype(BF16), seg_ref[...], preferred_element_type=F32)
        return jnp.max(sq, axis=0, keepdims=True)

    logf = jax.nn.log_sigmoid(proj("ff", LANES) + bf_ref[...])
    logf_ref[0] = logf.T[0:FOX_HEADS] if tm % LANES == 0 else logf

    dq, dk, dv = proj("dq"), proj("dk"), proj("dv")
    pos = (row + (ti * tm + pos_offset)).astype(F32)
    sub = lane % HEAD_DIM
    for hd in range(DIFF_HEADS):
        slope = 2.0 ** (-8.0 * (hd + 1) / DIFF_HEADS)
        sl = slice(hd * LANES, (hd + 1) * LANES)
        p0, p1, p2 = (a.astype(F32) for a in _split3((slope * LOG2E) * pos))
        alibi = jnp.where(sub % 3 == 0, p0, jnp.where(sub % 3 == 1, p1, p2))
        alibi_q = jnp.where(sub < 3, -alibi, (sub < 6).astype(F32))
        alibi_k = jnp.where(sub < 3, 1.0, jnp.where(sub < 6, alibi, 0.0))
        dqa_ref[0, 2 * hd] = jnp.where(lower, dq[:, sl], alibi_q).astype(BF16)
        dqa_ref[0, 2 * hd + 1] = jnp.where(lower, alibi_q, dq[:, sl]).astype(BF16)
        dka_ref[0, 2 * hd] = jnp.where(lower, dk[:, sl], alibi_k).astype(BF16)
        dka_ref[0, 2 * hd + 1] = jnp.where(lower, alibi_k, dk[:, sl]).astype(BF16)
        if tm % LANES:
            dk_ref[0, hd] = dk[:, sl]
        dv_ref[0, hd] = dv[:, sl]
        dvb_ref[0, hd] = dv[:, sl].astype(BF16)
    if tm % LANES == 0:
        dk_ref[0] = dk.T.reshape(dk_ref.shape[1:])
    dg_ref[0] = jax.nn.silu(proj("dg")).astype(BF16)
    diff_norms = [max_sq_norm(dq), max_sq_norm(dk)]

    tri_r = lax.broadcasted_iota(jnp.int32, (tm, tm), 0)
    tri_c = lax.broadcasted_iota(jnp.int32, (tm, tm), 1)
    tri = (tri_c <= tri_r).astype(BF16)
    part = jnp.dot(tri, by_group(*_split3(logf)), preferred_element_type=F32)
    zq, zk, zv = proj("fq"), proj("fk"), proj("fv")
    local = part + pltpu.roll(part, 8, 1) + pltpu.roll(part, 16, 1) + pltpu.roll(part, 24, 1)

    @pl.when(ti == 0)
    def _():
        carry_ref[...] = jnp.zeros_like(carry_ref)

    cum = local + carry_ref[...]
    carry_ref[...] = cum[tm - 1:tm, :]
    cum2 = cum * LOG2E
    pieces = by_group(*(a.astype(F32) for a in _split3(cum2)))
    behind = pltpu.roll(pieces, 24, 1)
    u = lane % HEAD_DIM
    ext_q = jnp.where(u < 24, pieces, (u < 48).astype(F32))

    def ext_k(h):
        mine = u % 8 == h
        return jnp.where(mine & (u < 24), 1.0, jnp.where(mine & (u < 48), -behind, 0.0))

    fox_norms = [max_sq_norm(zq), max_sq_norm(zk)]
    e_even = (lane == HEAD_DIM).astype(F32)
    for j in range(FOX_HEADS // 2):
        sl = slice(j * LANES, (j + 1) * LANES)
        fqa_ref[0, 2 * j] = jnp.where(lower, zq[:, sl], ext_q).astype(BF16)
        fqa_ref[0, 2 * j + 1] = jnp.where(lower, ext_q, zq[:, sl]).astype(BF16)
        fka_ref[0, 2 * j] = jnp.where(lower, zk[:, sl], ext_k(2 * j)).astype(BF16)
        fka_ref[0, 2 * j + 1] = jnp.where(lower, ext_k(2 * j + 1), zk[:, sl]).astype(BF16)
        fva_ref[0, 2 * j] = jnp.where(lower, zv[:, sl], e_even).astype(BF16)
        fva_ref[0, 2 * j + 1] = jnp.where(lower, pltpu.roll(zv[:, sl], HEAD_DIM, 1), e_even).astype(BF16)
    if tm % LANES == 0:
        fk_ref[0] = zk.T.reshape(fk_ref.shape[1:])
        fv_ref[0] = zv.T.reshape(fv_ref.shape[1:])
    else:
        for hd in range(FOX_HEADS):
            fk_ref[0, hd] = zk[:, hd * HEAD_DIM:(hd + 1) * HEAD_DIM]
            fv_ref[0, hd] = zv[:, hd * HEAD_DIM:(hd + 1) * HEAD_DIM]
    fg_ref[0] = jax.nn.silu(proj("fg")).astype(BF16)
    stat_ref[0, 0] = jnp.concatenate(
        fox_norms + diff_norms + [cum2[0:1], cum2[tm - 1:tm], jnp.zeros((2, LANES), F32)],
        axis=0)


def _in_proj(x, mod, w, bf16x, seg, pos_offset):
    b, t, d = x.shape
    tm = min(t, _TILE)
    nt = t // tm
    head_major = tm % LANES == 0

    def full(a):
        return pl.BlockSpec(a.shape, lambda i, j: (0,) * a.ndim)

    def heads(n, width):
        return pl.BlockSpec((1, n, tm, width), lambda i, j: (i, 0, j, 0))

    def cache_shape(n, width):
        return jax.ShapeDtypeStruct((b, n, width, t) if head_major else (b, n, t, width), F32)

    def cache_spec(n, width):
        return pl.BlockSpec((1, n, width, tm), lambda i, j: (i, 0, 0, j)) if head_major else heads(n, width)

    rows = pl.BlockSpec((1, tm, FOX_WIDTH), lambda i, j: (i, j, 0))
    out_shape = (
        cache_shape(FOX_HEADS, HEAD_DIM),
        cache_shape(FOX_HEADS, HEAD_DIM),
        cache_shape(DIFF_HEADS, DIFF_VDIM),
        jax.ShapeDtypeStruct((b, DIFF_HEADS, t, DIFF_VDIM), F32),
        jax.ShapeDtypeStruct((b, FOX_HEADS, t) if head_major else (b, t, LANES), F32),
        jax.ShapeDtypeStruct((b, FOX_HEADS, t, LANES), BF16),
        jax.ShapeDtypeStruct((b, FOX_HEADS, t, LANES), BF16),
        jax.ShapeDtypeStruct((b, FOX_HEADS, t, LANES), BF16),
        jax.ShapeDtypeStruct((b, t, FOX_WIDTH), BF16),
        jax.ShapeDtypeStruct((b, 2 * DIFF_HEADS, t, LANES), BF16),
        jax.ShapeDtypeStruct((b, 2 * DIFF_HEADS, t, LANES), BF16),
        jax.ShapeDtypeStruct((b, DIFF_HEADS, t, DIFF_VDIM), BF16),
        jax.ShapeDtypeStruct((b, t, DIFF_WIDTH), BF16),
        jax.ShapeDtypeStruct((b, nt, 8, LANES), F32),
    )
    out_specs = (
        cache_spec(FOX_HEADS, HEAD_DIM), cache_spec(FOX_HEADS, HEAD_DIM),
        cache_spec(DIFF_HEADS, DIFF_VDIM), heads(DIFF_HEADS, DIFF_VDIM),
        (pl.BlockSpec((1, FOX_HEADS, tm), lambda i, j: (i, 0, j)) if head_major
         else pl.BlockSpec((1, tm, LANES), lambda i, j: (i, j, 0))),
        heads(FOX_HEADS, LANES), heads(FOX_HEADS, LANES), heads(FOX_HEADS, LANES), rows,
        heads(2 * DIFF_HEADS, LANES), heads(2 * DIFF_HEADS, LANES), heads(DIFF_HEADS, DIFF_VDIM), rows,
        pl.BlockSpec((1, 1, 8, LANES), lambda i, j: (i, j, 0, 0)),
    )
    outs = pl.pallas_call(
        functools.partial(_in_proj_kernel, tm=tm, pos_offset=pos_offset),
        grid=(b, nt),
        in_specs=[pl.BlockSpec((1, tm, d), lambda i, j: (i, j, 0)),
                  pl.BlockSpec((1, 3, d), lambda i, j: (i, 0, 0)),
                  full(w), full(bf16x), full(seg)],
        out_specs=out_specs,
        out_shape=out_shape,
        scratch_shapes=[pltpu.VMEM((1, LANES), F32)],
        compiler_params=_params(("arbitrary", "arbitrary")),
        name="in_proj",
    )(x, mod, w, bf16x, seg)
    if head_major:
        outs = tuple(jnp.swapaxes(o, 2, 3) for o in outs[:3]) + outs[3:]
    else:
        outs = outs[:4] + (jnp.swapaxes(outs[4][:, :, :FOX_HEADS], 1, 2),) + outs[5:]
    return outs


_NT = (((1,), (1,)), ((), ()))
_TN = (((0,), (0,)), ((), ()))
_STAGES = 4
_FOX_V_ROWS = HEAD_DIM + 16
_FOX_GROUP = 4


_SKIP_MARGIN = 136.0
_STAT_FOX_Q, _STAT_FOX_K, _STAT_DIFF_Q, _STAT_DIFF_K, _STAT_CUM_FIRST, _STAT_CUM_LAST = range(6)


def _diagonal_blocks(tq):
    assert tq % (2 * LANES) == 0
    return ((tq // 2, 0), (tq, tq // 2))


def _dot_bound(stat, q_row, k_row, qi):
    tile = lax.broadcasted_iota(jnp.int32, (1, stat.shape[-1]), 1)
    qn2 = jnp.sum(jnp.where(tile == qi, stat[q_row:q_row + 1], 0.0))
    return jnp.sqrt(qn2 * stat[k_row:k_row + 1]) * 1.01 + 1.0


def _first_needed_tile(qi, bounds, m_ref):
    first = qi
    for c, bound in enumerate(bounds):
        tile = lax.broadcasted_iota(jnp.int32, bound.shape, 1)
        needed = (bound >= jnp.min(m_ref[c]) - _SKIP_MARGIN) & (tile < qi)
        first = jnp.minimum(first, jnp.min(jnp.where(needed, tile, qi)))
    return first


def _pipelined_tiles(qi, n_chains, scores, absorb, first_needed, sa_ref, sb_ref):
    chains = range(n_chains)
    bufs = (sa_ref, sb_ref)
    for c in chains:
        scores(qi, sa_ref, c, True)
        scores(jnp.maximum(qi - 1, 0), sb_ref, c)
        absorb(sa_ref, qi, c, True)
    n = qi - first_needed()

    def stages(top, count, more):
        for s in range(count):
            for c in chains:
                if more or s + 1 < count:
                    scores(jnp.maximum(top - s - 1, 0), bufs[s % 2], c)
                absorb(bufs[(s + 1) % 2], top - s, c, False)

    def body(i, carry):
        stages(qi - 1 - _STAGES * i, _STAGES, True)
        return carry

    trips = jnp.maximum(n - 1, 0) // _STAGES
    lax.fori_loop(0, trips, body, 0)
    rem = n - _STAGES * trips
    for count in range(1, _STAGES + 1):
        @pl.when(rem == count)
        def _(count=count):
            stages(qi - 1 - _STAGES * trips, count, False)


def _fox_kernel(q_ref, k_ref, v_ref, stat_ref, o_ref, sa_ref, sb_ref, m_ref, acc_ref, *, tq):
    qi = pl.program_id(2)
    half = tq // 2

    def scores(ki, dst, hh, diagonal=False):
        start = pl.multiple_of(ki * tq, tq)
        if diagonal:
            for keys, q0 in _diagonal_blocks(tq):
                k = k_ref[0, hh, pl.ds(start, keys), :]
                dst[hh, 0:keys, q0:q0 + half] = lax.dot_general(k, q_ref[0, hh, q0:q0 + half, :], _NT,
                                                                preferred_element_type=F32)
        else:
            k = k_ref[0, hh, pl.ds(start, tq), :]
            dst[hh] = lax.dot_general(k, q_ref[0, hh], _NT, preferred_element_type=F32)

    def absorb(src, ki, hh, diagonal):
        start = pl.multiple_of(ki * tq, tq)
        if diagonal:
            for keys, q0 in _diagonal_blocks(tq):
                s = src[hh, 0:keys, q0:q0 + half]
                key = lax.broadcasted_iota(jnp.int32, s.shape, 0)
                qry = lax.broadcasted_iota(jnp.int32, s.shape, 1) + q0
                s = jnp.where(key <= qry, s, NEG_INF)
                m_new = jnp.max(s, axis=0, keepdims=True)
                p = jnp.exp2(s - m_new)
                v = v_ref[0, hh, pl.ds(start, keys), 0:_FOX_V_ROWS]
                acc_ref[hh, :, q0:q0 + half] = lax.dot_general(v, p.astype(BF16), _TN, preferred_element_type=F32)
                m_ref[hh, :, q0:q0 + half] = m_new
            return
        s = src[hh]
        m_prev = m_ref[hh]
        m_new = jnp.maximum(m_prev, jnp.max(s, axis=0, keepdims=True))
        p = jnp.exp2(s - m_new)
        alpha = jnp.exp2(m_prev - m_new)
        v = v_ref[0, hh, pl.ds(start, tq), 0:_FOX_V_ROWS]
        pv = lax.dot_general(v, p.astype(BF16), _TN, preferred_element_type=F32)
        acc_ref[hh] = alpha * acc_ref[hh] + pv
        m_ref[hh] = m_new

    def first_needed():
        bounds = []
        for hh in range(_FOX_GROUP):
            stat = stat_ref[0, hh]
            tile = lax.broadcasted_iota(jnp.int32, (1, stat.shape[-1]), 1)
            cum_q = jnp.sum(jnp.where(tile == qi, stat[_STAT_CUM_FIRST:_STAT_CUM_FIRST + 1], 0.0))
            decay = cum_q - stat[_STAT_CUM_LAST:_STAT_CUM_LAST + 1]
            bounds.append(_dot_bound(stat, _STAT_FOX_Q, _STAT_FOX_K, qi) + decay)
        return _first_needed_tile(qi, bounds, m_ref)

    _pipelined_tiles(qi, _FOX_GROUP, scores, absorb, first_needed, sa_ref, sb_ref)
    outs = [acc_ref[hh, 0:HEAD_DIM] / acc_ref[hh, HEAD_DIM:HEAD_DIM + 1] for hh in range(_FOX_GROUP)]
    o_ref[0] = jnp.concatenate(outs, axis=0).T.astype(BF16)


def _fox_attention(fqa, fka, fva, stats):
    b, nh, t, _ = fqa.shape
    tq = min(t, _TILE)
    g = _FOX_GROUP
    return pl.pallas_call(
        functools.partial(_fox_kernel, tq=tq),
        grid=(b, nh // g, t // tq),
        in_specs=[pl.BlockSpec((1, g, tq, LANES), lambda i, j, q: (i, j, q, 0)),
                  pl.BlockSpec((1, g, t, LANES), lambda i, j, q: (i, j, 0, 0)),
                  pl.BlockSpec((1, g, t, LANES), lambda i, j, q: (i, j, 0, 0)),
                  pl.BlockSpec((1, g) + stats.shape[2:], lambda i, j, q: (i, j, 0, 0))],
        out_specs=pl.BlockSpec((1, tq, g * HEAD_DIM), lambda i, j, q: (i, q, j)),
        out_shape=jax.ShapeDtypeStruct((b, t, FOX_WIDTH), BF16),
        scratch_shapes=[pltpu.VMEM((g, tq, tq), F32), pltpu.VMEM((g, tq, tq), F32),
                        pltpu.VMEM((g, 1, tq), F32), pltpu.VMEM((g, _FOX_V_ROWS, tq), F32)],
        compiler_params=_params(("arbitrary", "arbitrary", "arbitrary")),
        name="fox_attn",
    )(fqa, fka, fva, stats)


def _alibi_slope(head):
    return jnp.exp2(jnp.full((1, 1), -8.0 / DIFF_HEADS, F32) * (head + 1).astype(F32))


def _lambda(lam_ref, lam_init):
    lv = lam_ref[...]
    e1 = jnp.exp(jnp.sum(lv[0:1] * lv[1:2], axis=1, keepdims=True))
    e2 = jnp.exp(jnp.sum(lv[2:3] * lv[3:4], axis=1, keepdims=True))
    return e1 - e2 + lam_init


def _diff_normed(o0, o1, lam, axis):
    df = o0 - lam * o1
    return df * lax.rsqrt(jnp.mean(df * df, axis=axis, keepdims=True) + RMS_EPS)


_ONES_ROWS = 16


def _diff_kernel(q_ref, k_ref, v_ref, stat_ref, lam_ref, g_ref, o_ref, sa_ref, sb_ref, m_ref, acc_ref, *, tq,
                 lam_init, first_head, group):
    qi = pl.program_id(2)
    half = tq // 2
    chains = 2 * group
    slopes = [_alibi_slope(first_head + pl.program_id(1) * group + h) for h in range(group)]

    def scores(ki, dst, c, diagonal=False):
        start = pl.multiple_of(ki * tq, tq)
        if diagonal:
            for keys, q0 in _diagonal_blocks(tq):
                k = k_ref[0, c, pl.ds(start, keys), :]
                dst[c, 0:keys, q0:q0 + half] = lax.dot_general(k, q_ref[0, c, q0:q0 + half, :], _NT,
                                                               preferred_element_type=F32)
        else:
            k = k_ref[0, c, pl.ds(start, tq), :]
            dst[c] = lax.dot_general(k, q_ref[0, c], _NT, preferred_element_type=F32)

    def values_t(c, start, keys):
        v = v_ref[0, c // 2, pl.ds(start, keys), :]
        return jnp.concatenate([v.T, jnp.ones((_ONES_ROWS, keys), BF16)], axis=0)

    def absorb(src, ki, c, diagonal):
        start = pl.multiple_of(ki * tq, tq)
        if diagonal:
            for keys, q0 in _diagonal_blocks(tq):
                s = src[c, 0:keys, q0:q0 + half]
                key = lax.broadcasted_iota(jnp.int32, s.shape, 0)
                qry = lax.broadcasted_iota(jnp.int32, s.shape, 1) + q0
                ahead = jnp.maximum(key - qry, 0).astype(F32)
                s = jnp.where((key // CHUNK) <= (qry // CHUNK), s - (2.0 * LOG2E * slopes[c // 2]) * ahead, NEG_INF)
                m_new = jnp.max(s, axis=0, keepdims=True)
                p = jnp.exp2(s - m_new)
                acc_ref[c, :, q0:q0 + half] = jnp.dot(values_t(c, start, keys), p.astype(BF16),
                                                      preferred_element_type=F32)
                m_ref[c, :, q0:q0 + half] = m_new
            return
        s = src[c]
        m_prev = m_ref[c]
        m_new = jnp.maximum(m_prev, jnp.max(s, axis=0, keepdims=True))
        p = jnp.exp2(s - m_new)
        alpha = jnp.exp2(m_prev - m_new)
        acc_ref[c] = alpha * acc_ref[c] + jnp.dot(values_t(c, start, tq), p.astype(BF16),
                                                  preferred_element_type=F32)
        m_ref[c] = m_new

    def first_needed():
        bounds = []
        for c in range(chains):
            stat = stat_ref[0, c]
            tile = lax.broadcasted_iota(jnp.int32, (1, stat.shape[-1]), 1)
            gap = ((qi - 1 - tile) * tq + 1).astype(F32)
            bounds.append(_dot_bound(stat, _STAT_DIFF_Q, _STAT_DIFF_K, qi) - (LOG2E * slopes[c // 2]) * gap)
        return _first_needed_tile(qi, bounds, m_ref)

    _pipelined_tiles(qi, chains, scores, absorb, first_needed, sa_ref, sb_ref)
    lam = _lambda(lam_ref, lam_init)
    for h in range(group):
        o0, o1 = (acc_ref[2 * h + c, :DIFF_VDIM] / acc_ref[2 * h + c, DIFF_VDIM:DIFF_VDIM + 1] for c in range(2))
        normed = _diff_normed(o0, o1, lam, 0).T * g_ref[...] * (1.0 - lam_init)
        o_ref[0, :, h * DIFF_VDIM:(h + 1) * DIFF_VDIM] = normed.astype(BF16)


_DIFF_GROUPS = ((0, 1, 2), (2, 2, 1))


def _diff_attention(dqa, dka, dvb, stats, lam_vecs, subln_g, lam_init):
    b, _, t, _ = dqa.shape
    tq = min(t, _TILE)
    outs = []
    for first_head, g, n_groups in _DIFF_GROUPS:
        off = first_head // g

        def heads(block, off=off):
            return pl.BlockSpec(block, lambda i, j, q: (i, j + off, 0, 0))

        outs.append(pl.pallas_call(
            functools.partial(_diff_kernel, tq=tq, lam_init=lam_init, first_head=first_head, group=g),
            grid=(b, n_groups, t // tq),
            in_specs=[pl.BlockSpec((1, 2 * g, tq, LANES), lambda i, j, q, off=off: (i, j + off, q, 0)),
                      heads((1, 2 * g, t, LANES)), heads((1, g, t, DIFF_VDIM)),
                      heads((1, 2 * g) + stats.shape[2:]),
                      pl.BlockSpec(lam_vecs.shape, lambda i, j, q: (0, 0)),
                      pl.BlockSpec(subln_g.shape, lambda i, j, q: (0, 0))],
            out_specs=pl.BlockSpec((1, tq, g * DIFF_VDIM), lambda i, j, q: (i, q, j)),
            out_shape=jax.ShapeDtypeStruct((b, t, n_groups * g * DIFF_VDIM), BF16),
            scratch_shapes=[pltpu.VMEM((2 * g, tq, tq), F32), pltpu.VMEM((2 * g, tq, tq), F32),
                            pltpu.VMEM((2 * g, 1, tq), F32),
                            pltpu.VMEM((2 * g, DIFF_VDIM + _ONES_ROWS, tq), F32)],
            compiler_params=_params(("arbitrary", "arbitrary", "arbitrary")),
            name="diff_attn",
        )(dqa, dka, dvb, stats, lam_vecs, subln_g))
    return tuple(outs)


def _prefix_sum_lanes(x):
    n = x.shape[-1]
    lane = lax.broadcasted_iota(jnp.int32, x.shape, x.ndim - 1)
    sh = 1
    while sh < n:
        x = x + jnp.where(lane >= sh, pltpu.roll(x, sh, x.ndim - 1), 0.0)
        sh *= 2
    return x


def _fox_sample_kernel(q_ref, k_ref, v_ref, ck_ref, cv_ref, clf_ref, o_ref, *, t, past):
    r = lax.broadcasted_iota(jnp.int32, (t, t), 0)
    c = lax.broadcasted_iota(jnp.int32, (t, t), 1)
    cum_c = _prefix_sum_lanes(clf_ref[0])
    suffix = (cum_c[:, past - 1:past] - cum_c) * LOG2E
    for hh in range(FOX_HEADS):
        data = slice(0, HEAD_DIM) if hh % 2 == 0 else slice(HEAD_DIM, LANES)
        ext = HEAD_DIM if hh % 2 == 0 else 0
        qa = q_ref[0, hh]
        qf = qa.astype(F32)
        cum_q = sum(qf[:, ext + 8 * p + hh:ext + 8 * p + hh + 1] for p in range(3))
        s_c = jnp.dot(qa[:, data], ck_ref[0, 0, hh].astype(BF16), preferred_element_type=F32)
        s_c = s_c + cum_q + suffix[hh:hh + 1]
        s_n = lax.dot_general(qa, k_ref[0, hh], _NT, preferred_element_type=F32)
        s_n = jnp.where(c <= r, s_n, NEG_INF)
        m = jnp.maximum(jnp.max(s_c, axis=1, keepdims=True), jnp.max(s_n, axis=1, keepdims=True))
        p_c = jnp.exp2(s_c - m)
        p_n = jnp.exp2(s_n - m)
        denom = jnp.sum(p_c, axis=1, keepdims=True) + jnp.sum(p_n, axis=1, keepdims=True)
        o = lax.dot_general(p_c.astype(BF16), cv_ref[0, 0, hh].astype(BF16), _NT, preferred_element_type=F32)
        o = o + jnp.dot(p_n.astype(BF16), v_ref[0, hh][:, 0:HEAD_DIM], preferred_element_type=F32)
        o_ref[0, :, hh * HEAD_DIM:(hh + 1) * HEAD_DIM] = (o / denom).astype(BF16)


def _fox_sample(fqa, fka, fva, cache_k, cache_v, cache_logf):
    b, nh, t, _ = fqa.shape
    past = cache_k.shape[3]
    cache_k, cache_v = jnp.swapaxes(cache_k, 3, 4), jnp.swapaxes(cache_v, 3, 4)
    new = pl.BlockSpec((1, nh, t, LANES), lambda i: (i, 0, 0, 0))
    cache = pl.BlockSpec((1, 1, nh, HEAD_DIM, past), lambda i: (0, i, 0, 0, 0))
    return pl.pallas_call(
        functools.partial(_fox_sample_kernel, t=t, past=past),
        grid=(b,),
        in_specs=[new, new, new, cache, cache, pl.BlockSpec((1, nh, past), lambda i: (i, 0, 0))],
        out_specs=pl.BlockSpec((1, t, FOX_WIDTH), lambda i: (i, 0, 0)),
        out_shape=jax.ShapeDtypeStruct((b, t, FOX_WIDTH), BF16),
        compiler_params=_params(("arbitrary",)),
        name="fox_sample",
    )(fqa, fka, fva, cache_k, cache_v, cache_logf)


def _diff_sample_kernel(q_ref, k_ref, v_ref, ck_ref, cv_ref, lam_ref, g_ref, o_ref, *, t, past, lam_init):
    lane = lax.broadcasted_iota(jnp.int32, (t, LANES), 1)
    pos_q = past + lax.broadcasted_iota(jnp.int32, (t, past), 0)
    pos_c = lax.broadcasted_iota(jnp.int32, (t, past), 1)
    dist_c = jnp.abs(pos_q - pos_c).astype(F32)
    vis_c = (pos_c // CHUNK) <= (pos_q // CHUNK)
    pq_n = past + lax.broadcasted_iota(jnp.int32, (t, t), 0)
    pk_n = past + lax.broadcasted_iota(jnp.int32, (t, t), 1)
    dist_n = jnp.abs(pq_n - pk_n).astype(F32)
    vis_n = (pk_n // CHUNK) <= (pq_n // CHUNK)
    lam = _lambda(lam_ref, lam_init)
    for hd in range(DIFF_HEADS):
        slope = LOG2E * 2.0 ** (-8.0 * (hd + 1) / DIFF_HEADS)
        ck = ck_ref[0, 0, hd].astype(BF16)
        cv = cv_ref[0, 0, hd].astype(BF16)
        v = v_ref[0, hd]
        outs = []
        for c in range(2):
            half = (lane < HEAD_DIM) if c == 0 else (lane >= HEAD_DIM)
            qz = jnp.where(half, q_ref[0, 2 * hd + c], jnp.zeros((), BF16))
            kz = jnp.where(half, k_ref[0, 2 * hd + c], jnp.zeros((), BF16))
            s_c = jnp.dot(qz, ck, preferred_element_type=F32)
            s_c = jnp.where(vis_c, s_c - slope * dist_c, NEG_INF)
            s_n = lax.dot_general(qz, kz, _NT, preferred_element_type=F32)
            s_n = jnp.where(vis_n, s_n - slope * dist_n, NEG_INF)
            m = jnp.maximum(jnp.max(s_c, axis=1, keepdims=True), jnp.max(s_n, axis=1, keepdims=True))
            p_c = jnp.exp2(s_c - m)
            p_n = jnp.exp2(s_n - m)
            denom = jnp.sum(p_c, axis=1, keepdims=True) + jnp.sum(p_n, axis=1, keepdims=True)
            o = jnp.dot(p_c.astype(BF16), cv, preferred_element_type=F32)
            o = o + jnp.dot(p_n.astype(BF16), v, preferred_element_type=F32)
            outs.append(o / denom)
        normed = _diff_normed(outs[0], outs[1], lam, 1) * g_ref[...] * (1.0 - lam_init)
        o_ref[0, :, hd * DIFF_VDIM:(hd + 1) * DIFF_VDIM] = normed.astype(BF16)


def _diff_sample(dqa, dka, dvb, cache_k, cache_v, lam_vecs, subln_g, lam_init):
    b, nm, t, _ = dqa.shape
    past = cache_k.shape[3]
    ck = jnp.transpose(cache_k, (0, 1, 2, 4, 5, 3)).reshape(cache_k.shape[:3] + (DIFF_VDIM, past))
    new = pl.BlockSpec((1, nm, t, LANES), lambda i: (i, 0, 0, 0))
    return pl.pallas_call(
        functools.partial(_diff_sample_kernel, t=t, past=past, lam_init=lam_init),
        grid=(b,),
        in_specs=[new, new, pl.BlockSpec((1, DIFF_HEADS, t, DIFF_VDIM), lambda i: (i, 0, 0, 0)),
                  pl.BlockSpec((1, 1, DIFF_HEADS, DIFF_VDIM, past), lambda i: (0, i, 0, 0, 0)),
                  pl.BlockSpec((1, 1, DIFF_HEADS, past, DIFF_VDIM), lambda i: (0, i, 0, 0, 0)),
                  pl.BlockSpec(lam_vecs.shape, lambda i: (0, 0)),
                  pl.BlockSpec(subln_g.shape, lambda i: (0, 0))],
        out_specs=pl.BlockSpec((1, t, DIFF_WIDTH), lambda i: (i, 0, 0)),
        out_shape=jax.ShapeDtypeStruct((b, t, DIFF_WIDTH), BF16),
        compiler_params=_params(("arbitrary",)),
        name="diff_sample",
    )(dqa, dka, dvb, ck, cache_v, lam_vecs, subln_g)


_OUT_CHUNK = 256
_X_SLOTS = 3


def _out_proj_kernel(fo_ref, *refs, alpha, n_diff, tm, tiles_per_batch, n_steps):
    do_refs = refs[:n_diff]
    fg_ref, dg_ref, x_hbm, mod_ref, w_ref, g_ref, b_ref, y_ref, x_buf, x_sem = refs[n_diff:]
    step = pl.program_id(0) * tiles_per_batch + pl.program_id(1)

    def x_copy(k):
        rows = pl.ds((k % tiles_per_batch) * tm, tm)
        return pltpu.make_async_copy(x_hbm.at[k // tiles_per_batch, rows, :], x_buf.at[k % _X_SLOTS],
                                     x_sem.at[k % _X_SLOTS])

    @pl.when(step == 0)
    def _():
        x_copy(step).start()
        if n_steps > 1:
            x_copy(step + 1).start()

    @pl.when(step + 2 < n_steps)
    def _():
        x_copy(step + 2).start()

    x_copy(step).wait()
    chunk = min(tm, _OUT_CHUNK)
    for r0 in range(0, tm, chunk):
        rows = slice(r0, r0 + chunk)
        x = x_buf[step % _X_SLOTS, rows, :]
        fox = fo_ref[0, rows, :] * fg_ref[0, rows, :]
        diff = jnp.concatenate([r[0, rows, :] for r in do_refs], axis=1) * dg_ref[0, rows, :]
        branch = jnp.dot(fox, w_ref[0:FOX_WIDTH, :], preferred_element_type=F32)
        branch = branch + jnp.dot(diff, w_ref[FOX_WIDTH:, :], preferred_element_type=F32)
        r = alpha * x + mod_ref[0, 2:3, :] * branch
        mu = jnp.mean(r, axis=-1, keepdims=True)
        rc = r - mu
        var = jnp.mean(rc * rc, axis=-1, keepdims=True)
        y_ref[0, rows, :] = rc * lax.rsqrt(var + LN_EPS) * g_ref[...] + b_ref[...]


def _out_proj(fox_o, diff_parts, fg, dg, x, mod, w_out, ln_g, ln_b, alpha):
    b, t, d = x.shape
    tm = 1024 if t % 1024 == 0 else min(t, _TILE)
    rows = pl.BlockSpec((1, tm, FOX_WIDTH), lambda i, j: (i, j, 0))
    wide = pl.BlockSpec((1, tm, d), lambda i, j: (i, j, 0))

    def full(a):
        return pl.BlockSpec(a.shape, lambda i, j: (0,) * a.ndim)

    return pl.pallas_call(
        functools.partial(_out_proj_kernel, alpha=alpha, n_diff=len(diff_parts), tm=tm,
                          tiles_per_batch=t // tm, n_steps=b * (t // tm)),
        grid=(b, t // tm),
        in_specs=([rows] + [pl.BlockSpec((1, tm, p.shape[-1]), lambda i, j: (i, j, 0)) for p in diff_parts]
                  + [rows, rows, pl.BlockSpec(memory_space=pl.ANY),
                     pl.BlockSpec((1, 3, d), lambda i, j: (i, 0, 0)), full(w_out), full(ln_g), full(ln_b)]),
        out_specs=wide,
        out_shape=jax.ShapeDtypeStruct((b, t, d), F32),
        scratch_shapes=[pltpu.VMEM((_X_SLOTS, tm, d), F32), pltpu.SemaphoreType.DMA((_X_SLOTS,))],
        compiler_params=_params(("arbitrary", "arbitrary")),
        name="out_proj",
    )(fox_o, *diff_parts, fg, dg, x, mod, w_out, ln_g, ln_b)


def _pack_w_in(w_in):
    fw = FOX_WIDTH
    o = 0
    fq = w_in[:, o:o + fw]; o += fw
    fk = w_in[:, o:o + fw]; o += fw
    fv = w_in[:, o:o + fw]; o += fw
    ff = w_in[:, o:o + FOX_HEADS]; o += FOX_HEADS
    fg = w_in[:, o:o + fw]; o += fw
    dq = w_in[:, o:o + fw]; o += fw
    dk = w_in[:, o:o + fw]; o += fw
    dv = w_in[:, o:o + fw]; o += fw
    dg = w_in[:, o:o + fw]
    qs = HEAD_DIM ** -0.5 * LOG2E
    return jnp.concatenate([fq * qs, fk, fv, fg, dq * qs, dk, dv, dg,
                            jnp.tile(ff, (1, LANES // FOX_HEADS))], axis=1).astype(BF16)


def kernel(x_prompt, x_sample, cache_fox_k, cache_fox_v, cache_fox_logf, cache_diff_k, cache_diff_v,
           c_prompt, c_sample, w_ada, b_ada, w_in, b_f, lambda_q1, lambda_k1, lambda_q2, lambda_k2,
           subln_g, w_out, ln_g, ln_b):
    depth = w_in.shape[0]
    assert depth == 1, "single-layer step"
    layer = 0
    lam_init = 0.8 - 0.6 * math.exp(-0.3 * layer)
    alpha = (2 * depth) ** 0.25
    bp, bs = c_prompt.shape[0], c_sample.shape[0]
    d = x_prompt.shape[-1]
    past = cache_fox_k.shape[3]

    c_all = jnp.concatenate([c_prompt, c_sample, jnp.zeros((-(bp + bs) % 8, d), F32)], axis=0)
    mod = _ada(c_all, w_ada[layer], b_ada[layer]).reshape(c_all.shape[0], 3, d)
    mod_p, mod_s = mod[:bp], mod[bp:bp + bs]

    w = _pack_w_in(w_in[layer])
    bf16x = jnp.tile(b_f[layer], LANES // FOX_HEADS).reshape(1, LANES)
    seg = np.repeat(np.eye(FOX_WIDTH // HEAD_DIM, LANES, dtype=np.float32), HEAD_DIM, axis=0)
    seg = jnp.asarray(seg, BF16)
    lam_vecs = jnp.stack([lambda_q1[layer], lambda_k1[layer], lambda_q2[layer], lambda_k2[layer]])
    g = subln_g[layer].reshape(1, DIFF_VDIM)
    w_o = w_out[layer].astype(BF16)
    lg, lb = ln_g[layer].reshape(1, d), ln_b[layer].reshape(1, d)

    def run(x, mod_x, pos_offset, attend):
        (fk, fv, dk, dv, logf, fqa, fka, fva, fg, dqa, dka, dvb, dg, stats) = _in_proj(
            x, mod_x, w, bf16x, seg, pos_offset)
        stats = jnp.transpose(stats[..., :FOX_WIDTH // HEAD_DIM], (0, 3, 2, 1))
        fox_o, diff_o = attend(fqa, fka, fva, dqa, dka, dvb, stats)
        y = _out_proj(fox_o, diff_o, fg, dg, x, mod_x, w_o, lg, lb, alpha)
        b, t = x.shape[:2]
        states = (fk[None], fv[None], logf[None],
                  dk.reshape(1, b, DIFF_HEADS, t, 2, HEAD_DIM), dv[None])
        return y, states

    def prompt_attend(fqa, fka, fva, dqa, dka, dvb, stats):
        return (_fox_attention(fqa, fka, fva, stats),
                _diff_attention(dqa, dka, dvb, stats, lam_vecs, g, lam_init))

    def sample_attend(fqa, fka, fva, dqa, dka, dvb, stats):
        del stats
        return (_fox_sample(fqa, fka, fva, cache_fox_k, cache_fox_v, cache_fox_logf[layer]),
                (_diff_sample(dqa, dka, dvb, cache_diff_k, cache_diff_v, lam_vecs, g, lam_init),))

    yp, sp = run(x_prompt, mod_p, 0, prompt_attend)
    ys, ss = run(x_sample, mod_s, past, sample_attend)
    return (yp, ys) + sp + ss
```
